```python
import jax, jax.numpy as jnp
from jax import lax
import numpy as np

D_MODEL = 1024
BATCH = 8
SEQ = 2048
DEPTH = 2

GRID_W = 64
CTX_LEN = 256
HEAD_DIM = 64
NA_HEADS = D_MODEL // 4 // HEAD_DIM
NA_WIN_R = 8
NA_WIN_C = 16
GQA_HEADS = D_MODEL // 2 // HEAD_DIM
GQA_KV_HEADS = GQA_HEADS // 4
GQA_BLOCK = 128
ROPE_THETA = 10000.0
HGRN_HEADS = D_MODEL // 4 // HEAD_DIM
HGRN_DK = HEAD_DIM
HGRN_DV = HEAD_DIM
HGRN_CHUNK = 16
N_GROUPS = 4
EXPERTS_PER_GROUP = 8
N_EXPERTS = N_GROUPS * EXPERTS_PER_GROUP
EXPERT_TOP_K = 2
EXPERT_FF = D_MODEL // 4
N_MOD = 6
EPS = 1e-6
NEG_INF = -1e30
LB_FLOOR = 1e-20

NA_W = NA_HEADS * HEAD_DIM
GQA_QW = GQA_HEADS * HEAD_DIM
GQA_KW = GQA_KV_HEADS * HEAD_DIM
HG_KW = HGRN_HEADS * HGRN_DK
HG_VW = HGRN_HEADS * HGRN_DV
D_MIX = NA_W + GQA_QW + HG_VW
IN_SPLIT_SIZES = (NA_W, NA_W, NA_W, GQA_QW, GQA_KW, GQA_KW, HG_KW, HG_VW, HG_KW, HG_KW, HG_VW)
D_IN = sum(IN_SPLIT_SIZES)

kernel_name = 'hybrid_na_gqa_hgrn2_hmoe_block'


def rms_norm(x, gain):
    xf = x.astype(jnp.float32)
    y = xf * lax.rsqrt(jnp.mean(xf * xf, axis=-1, keepdims=True) + EPS)
    return (y * gain.astype(jnp.float32)).astype(x.dtype)


def modulate(h, shift, scale):
    return h * (1 + scale) + shift


def to_heads(t, n):
    b, s, _ = t.shape
    return jnp.swapaxes(t.reshape(b, s, n, -1), 1, 2)


def from_heads(t):
    b, n, s, hd = t.shape
    return jnp.swapaxes(t, 1, 2).reshape(b, s, n * hd)


def split_cols(p):
    idx = np.cumsum(IN_SPLIT_SIZES)[:-1].tolist()
    return jnp.split(p, idx, axis=-1)


def axial_rope_tables(n_tokens):
    t = jnp.arange(n_tokens)
    row = (t // GRID_W).astype(jnp.float32)
    col = (t % GRID_W).astype(jnp.float32)
    half = HEAD_DIM // 2
    inv = ROPE_THETA ** (-jnp.arange(0, half, 2, dtype=jnp.float32) / half)
    ang = jnp.concatenate([row[:, None] * inv, col[:, None] * inv], axis=-1)
    return jnp.cos(ang), jnp.sin(ang)


def apply_rope(x, cos, sin):
    xf = x.astype(jnp.float32).reshape(*x.shape[:-1], HEAD_DIM // 2, 2)
    x1, x2 = xf[..., 0], xf[..., 1]
    out = jnp.stack([x1 * cos - x2 * sin, x1 * sin + x2 * cos], axis=-1)
    return out.reshape(x.shape).astype(x.dtype)


def softmax_attend(q, k, v):
    s = jnp.matmul(q, jnp.swapaxes(k, -1, -2)).astype(jnp.float32) * (q.shape[-1] ** -0.5)
    p = jax.nn.softmax(s, axis=-1).astype(v.dtype)
    return jnp.matmul(p, v)


def neighbourhood_attention(q, k, v, k_ctx, v_ctx, rpb):
    b, h, s, hd = q.shape
    rows = s // GRID_W
    wr = min(NA_WIN_R, rows)
    r = jnp.arange(rows)
    row_start = jnp.clip(r - wr // 2, 0, rows - wr)
    key_rows = row_start[:, None] + jnp.arange(wr)
    qg = q.reshape(b, h, rows, GRID_W, hd)
    kb = k.reshape(b, h, rows, GRID_W, hd)[:, :, key_rows].reshape(b, h, rows, wr * GRID_W, hd)
    vb = v.reshape(b, h, rows, GRID_W, hd)[:, :, key_rows].reshape(b, h, rows, wr * GRID_W, hd)
    cidx = jnp.arange(GRID_W)
    col_start = jnp.clip(cidx - NA_WIN_C // 2, 0, GRID_W - NA_WIN_C)
    col_ok = (cidx[None, :] >= col_start[:, None]) & (cidx[None, :] < col_start[:, None] + NA_WIN_C)
    dr = key_rows - r[:, None] + (NA_WIN_R - 1)
    dc = jnp.clip(cidx[None, :] - cidx[:, None] + (NA_WIN_C - 1), 0, 2 * NA_WIN_C - 2)
    bias = rpb.astype(jnp.float32)[:, dr[:, None, :, None], dc[None, :, None, :]]
    bias = jnp.where(col_ok[None, None, :, None, :], bias, NEG_INF).reshape(h, rows, GRID_W, wr * GRID_W)
    scale = hd ** -0.5
    s_nb = jnp.einsum('bhrqd,bhrkd->bhrqk', qg, kb).astype(jnp.float32) * scale + bias
    s_cx = jnp.einsum('bhrqd,bhld->bhrql', qg, k_ctx).astype(jnp.float32) * scale
    p = jax.nn.softmax(jnp.concatenate([s_nb, s_cx], axis=-1), axis=-1).astype(v.dtype)
    n_nb = wr * GRID_W
    out = (jnp.einsum('bhrqk,bhrkd->bhrqd', p[..., :n_nb], vb)
           + jnp.einsum('bhrql,bhld->bhrqd', p[..., n_nb:], v_ctx))
    return out.reshape(b, h, s, hd)


def gqa_block_sweep(q, k_all, v_all):
    b, hkv, g, s, hd = q.shape
    nb = s // GQA_BLOCK
    qb = jnp.moveaxis(q.reshape(b, hkv, g, nb, GQA_BLOCK, hd), 3, 0)
    ob = lax.map(lambda qi: softmax_attend(qi, k_all, v_all), qb)
    return jnp.moveaxis(ob, 0, 3).reshape(b, hkv, g, s, hd)


def hgrn_lower_bounds(lb_raw):
    p = jax.nn.softmax(lb_raw.astype(jnp.float32), axis=0)
    return jnp.cumsum(p, axis=0) - p[0]


def log_forget(z, lb):
    return jnp.logaddexp(jnp.log1p(-lb) + jax.nn.log_sigmoid(z), jnp.log(jnp.maximum(lb, LB_FLOOR)))


def hgrn2_chunkwise(q, k, v, logf, s0):
    b, h, t, dk = q.shape
    dv = v.shape[-1]
    c = HGRN_CHUNK
    n = t // c
    q, k, logf = (a.reshape(b, h, n, c, dk) for a in (q, k, logf))
    v = v.reshape(b, h, n, c, dv)
    cum = jnp.cumsum(logf, axis=3)
    lower = jnp.tril(jnp.ones((c, c), dtype=bool))[:, :, None]
    diff = cum[..., :, None, :] - cum[..., None, :, :]
    decay = jnp.where(lower, jnp.exp(jnp.where(lower, diff, 0.0)), 0.0)
    attn = jnp.einsum('bhntd,bhnsd,bhntsd->bhnts', q, k, decay)
    o_intra = jnp.einsum('bhnts,bhnsv->bhntv', attn, v)
    cum_last = cum[..., -1:, :]
    u = jnp.einsum('bhnsd,bhnsv->bhndv', k * jnp.exp(cum_last - cum), v)
    a = jnp.exp(cum_last[..., 0, :])

    def step(s, xs):
        a_n, u_n = xs
        return a_n[..., None] * s + u_n, s

    s_fin, s_enter = lax.scan(step, s0, (jnp.moveaxis(a, 2, 0), jnp.moveaxis(u, 2, 0)))
    s_enter = jnp.moveaxis(s_enter, 0, 2)
    o_inter = jnp.einsum('bhntd,bhndv->bhntv', q * jnp.exp(cum), s_enter)
    return (o_intra + o_inter).reshape(b, h, t, dv), s_fin


def hgrn2_final_state(k, v, logf):
    cum = jnp.cumsum(logf, axis=2)
    w = k * jnp.exp(cum[:, :, -1:, :] - cum)
    return jnp.einsum('bhtd,bhtv->bhdv', w, v)


def hgrn2_mixer(q, i, f_fwd, f_bwd, g, qc, ic, fc_fwd, fc_bwd, gc, lb, o_norm, ctx_out):
    dt = q.dtype
    nh = HGRN_HEADS
    bsz = q.shape[0]
    scale = HGRN_DK ** -0.5
    h32 = lambda t: to_heads(t, nh).astype(jnp.float32)
    rev = lambda t: t[:, :, ::-1]
    qh, vh = h32(q) * scale, h32(i)
    qch, vch = h32(qc) * scale, h32(ic)
    s_zero = jnp.zeros((bsz, nh, HGRN_DK, HGRN_DV), jnp.float32)
    o_lat, o_ctx = [], []
    for d, (f_l, f_c) in enumerate(((f_fwd, fc_fwd), (f_bwd, fc_bwd))):
        lbd = lb[d].reshape(nh, 1, HGRN_DK)
        logf = log_forget(h32(f_l), lbd)
        logfc = log_forget(h32(f_c), lbd)
        seq_l = (qh, -jnp.expm1(logf), vh, logf)
        seq_c = (qch, -jnp.expm1(logfc), vch, logfc)
        if d == 1:
            seq_l = tuple(rev(t) for t in seq_l)
            seq_c = tuple(rev(t) for t in seq_c)
        if ctx_out:
            oc, s_c = hgrn2_chunkwise(seq_c[0], seq_c[1], seq_c[2], seq_c[3], s_zero)
            o_ctx.append(rev(oc) if d == 1 else oc)
        else:
            s_c = hgrn2_final_state(seq_c[1], seq_c[2], seq_c[3])
        o, _ = hgrn2_chunkwise(seq_l[0], seq_l[1], seq_l[2], seq_l[3], s_c)
        o_lat.append(rev(o) if d == 1 else o)

    def readout(o_sum, gate):
        y = from_heads(rms_norm(o_sum, o_norm))
        return (y * jax.nn.silu(gate.astype(jnp.float32))).astype(dt)

    lat = readout(o_lat[0] + o_lat[1], g)
    ctx_o = readout(o_ctx[0] + o_ctx[1], gc) if ctx_out else None
    return lat, ctx_o


def token_mixers(h, hc, w_in, na_qn, na_kn, na_rpb, gq_qn, gq_kn, lb, hg_on, cos, sin, ctx_out):
    b, s, _ = h.shape
    L = hc.shape[1]
    grp = GQA_HEADS // GQA_KV_HEADS
    a_q, a_k, a_v, b_q, b_k, b_v, c_q, c_i, c_ff, c_fb, c_g = split_cols(h @ w_in)
    ac_q, ac_k, ac_v, bc_q, bc_k, bc_v, cc_q, cc_i, cc_ff, cc_fb, cc_g = split_cols(hc @ w_in)
    qa = rms_norm(to_heads(a_q, NA_HEADS), na_qn)
    ka = rms_norm(to_heads(a_k, NA_HEADS), na_kn)
    va = to_heads(a_v, NA_HEADS)
    kac = rms_norm(to_heads(ac_k, NA_HEADS), na_kn)
    vac = to_heads(ac_v, NA_HEADS)
    o_a = neighbourhood_attention(qa, ka, va, kac, vac, na_rpb)
    qb = apply_rope(rms_norm(to_heads(b_q, GQA_HEADS), gq_qn), cos, sin)
    kb = apply_rope(rms_norm(to_heads(b_k, GQA_KV_HEADS), gq_kn), cos, sin)
    vb = to_heads(b_v, GQA_KV_HEADS)
    kbc = rms_norm(to_heads(bc_k, GQA_KV_HEADS), gq_kn)
    vbc = to_heads(bc_v, GQA_KV_HEADS)
    k_all = jnp.concatenate([kb, kbc], axis=2)[:, :, None]
    v_all = jnp.concatenate([vb, vbc], axis=2)[:, :, None]
    o_b = gqa_block_sweep(qb.reshape(b, GQA_KV_HEADS, grp, s, HEAD_DIM), k_all, v_all)
    o_b = o_b.reshape(b, GQA_HEADS, s, HEAD_DIM)
    o_c, o_cc = hgrn2_mixer(c_q, c_i, c_ff, c_fb, c_g, cc_q, cc_i, cc_ff, cc_fb, cc_g, lb, hg_on, ctx_out)
    mix = jnp.concatenate([from_heads(o_a), from_heads(o_b), o_c], axis=-1)
    if not ctx_out:
        return mix, None
    qac = rms_norm(to_heads(ac_q, NA_HEADS), na_qn)
    o_ac = softmax_attend(qac, kac, vac)
    qbc = rms_norm(to_heads(bc_q, GQA_HEADS), gq_qn).reshape(b, GQA_KV_HEADS, grp, L, HEAD_DIM)
    o_bc = softmax_attend(qbc, kbc[:, :, None], vbc[:, :, None]).reshape(b, GQA_HEADS, L, HEAD_DIM)
    mix_c = jnp.concatenate([from_heads(o_ac), from_heads(o_bc), o_cc], axis=-1)
    return mix, mix_c


def hier_moe(t, w_rg, b_rg, w_re, b_re, w_gate, w_up, w_down):
    g_prob = jax.nn.softmax((t @ w_rg).astype(jnp.float32) + b_rg, axis=-1)
    g_top, g_idx = lax.top_k(g_prob, 1)
    e_logits = ((t @ w_re).astype(jnp.float32) + b_re).reshape(-1, N_GROUPS, EXPERTS_PER_GROUP)
    in_group = jnp.take_along_axis(e_logits, g_idx[:, :, None], axis=1)[:, 0]
    e_top, e_idx = lax.top_k(in_group, EXPERT_TOP_K)
    e_w = jax.nn.softmax(e_top, axis=-1) * g_top
    expert_id = g_idx * EXPERTS_PER_GROUP + e_idx
    combine = jnp.sum(jax.nn.one_hot(expert_id, N_EXPERTS, dtype=jnp.float32) * e_w[..., None], axis=1)
    combine = combine.reshape(-1, N_GROUPS, EXPERTS_PER_GROUP).astype(t.dtype)
    wg = w_gate.reshape(N_GROUPS, EXPERTS_PER_GROUP, D_MODEL, EXPERT_FF)
    wu = w_up.reshape(N_GROUPS, EXPERTS_PER_GROUP, D_MODEL, EXPERT_FF)
    wd = w_down.reshape(N_GROUPS, EXPERTS_PER_GROUP, EXPERT_FF, D_MODEL)
    out = jnp.zeros_like(t)
    for gi in range(N_GROUPS):
        hid = jax.nn.silu(jnp.einsum('td,edf->tef', t, wg[gi])) * jnp.einsum('td,edf->tef', t, wu[gi])
        out = out + jnp.einsum('tef,efd->td', hid * combine[:, gi, :, None], wd[gi])
    return out


def setup_inputs(seed: int = 0) -> dict:
    key = jax.random.key(seed)
    ks = jax.random.split(key, 32)
    nrm = lambda k, shape, sc: jax.random.normal(k, shape, jnp.float32) * sc
    D = D_MODEL
    return {
        'x': nrm(ks[0], (BATCH, SEQ, D), 1.0),
        'c': nrm(ks[1], (BATCH, D), 1.0),
        'ctx': nrm(ks[2], (BATCH, CTX_LEN, D), 1.0),
        'c_ctx': nrm(ks[3], (D,), 1.0),
        'w_ada': nrm(ks[4], (DEPTH, D, N_MOD * D), 0.5 * D ** -0.5),
        'b_ada': nrm(ks[5], (DEPTH, N_MOD * D), 0.02),
        'norm1_g': 1.0 + nrm(ks[6], (DEPTH, D), 0.02),
        'w_in': nrm(ks[7], (DEPTH, D, D_IN), D ** -0.5),
        'na_q_norm': 1.0 + nrm(ks[8], (DEPTH, HEAD_DIM), 0.02),
        'na_k_norm': 1.0 + nrm(ks[9], (DEPTH, HEAD_DIM), 0.02),
        'na_rpb': nrm(ks[10], (DEPTH, NA_HEADS, 2 * NA_WIN_R - 1, 2 * NA_WIN_C - 1), 0.5),
        'gqa_q_norm': 1.0 + nrm(ks[11], (DEPTH, HEAD_DIM), 0.02),
        'gqa_k_norm': 1.0 + nrm(ks[12], (DEPTH, HEAD_DIM), 0.02),
        'hgrn_lb': nrm(ks[13], (DEPTH, 2, HG_KW), 1.0),
        'hgrn_o_norm': 1.0 + nrm(ks[14], (DEPTH, HGRN_DV), 0.02),
        'w_out': nrm(ks[15], (DEPTH, D_MIX, D), D_MIX ** -0.5),
        'norm2_g': 1.0 + nrm(ks[16], (DEPTH, D), 0.02),
        'w_route_group': nrm(ks[17], (DEPTH, D, N_GROUPS), D ** -0.5),
        'b_route_group': nrm(ks[18], (DEPTH, N_GROUPS), 0.01),
        'w_route_expert': nrm(ks[19], (DEPTH, D, N_EXPERTS), D ** -0.5),
        'b_route_expert': nrm(ks[20], (DEPTH, N_EXPERTS), 0.01),
        'w_exp_gate': nrm(ks[21], (DEPTH, N_EXPERTS, D, EXPERT_FF), D ** -0.5),
        'w_exp_up': nrm(ks[22], (DEPTH, N_EXPERTS, D, EXPERT_FF), D ** -0.5),
        'w_exp_down': nrm(ks[23], (DEPTH, N_EXPERTS, EXPERT_FF, D), EXPERT_FF ** -0.5),
    }


def reference(x, c, ctx, c_ctx, w_ada, b_ada, norm1_g, w_in, na_q_norm, na_k_norm, na_rpb,
              gqa_q_norm, gqa_k_norm, hgrn_lb, hgrn_o_norm, w_out, norm2_g, w_route_group,
              b_route_group, w_route_expert, b_route_expert, w_exp_gate, w_exp_up, w_exp_down):
    b, s, d = x.shape
    L = ctx.shape[1]
    cos, sin = axial_rope_tables(s)
    lb_all = hgrn_lower_bounds(hgrn_lb)
    xc = ctx
    for layer in range(DEPTH):
        ctx_out = layer < DEPTH - 1
        mod = (jax.nn.silu(c) @ w_ada[layer] + b_ada[layer]).reshape(b, N_MOD, d)
        mod_c = (jax.nn.silu(c_ctx) @ w_ada[layer] + b_ada[layer]).reshape(N_MOD, d)
        h = modulate(rms_norm(x, norm1_g[layer]), mod[:, 0:1], mod[:, 1:2])
        hc = modulate(rms_norm(xc, norm1_g[layer]), mod_c[0], mod_c[1])
        mix, mix_c = token_mixers(h, hc, w_in[layer], na_q_norm[layer], na_k_norm[layer], na_rpb[layer],
                                  gqa_q_norm[layer], gqa_k_norm[layer], lb_all[layer], hgrn_o_norm[layer],
                                  cos, sin, ctx_out)
        x = x + mod[:, 2:3] * (mix @ w_out[layer])
        moe_args = (w_route_group[layer], b_route_group[layer], w_route_expert[layer],
                    b_route_expert[layer], w_exp_gate[layer], w_exp_up[layer], w_exp_down[layer])
        h2 = modulate(rms_norm(x, norm2_g[layer]), mod[:, 3:4], mod[:, 4:5])
        if ctx_out:
            xc = xc + mod_c[2] * (mix_c @ w_out[layer])
            hc2 = modulate(rms_norm(xc, norm2_g[layer]), mod_c[3], mod_c[4])
            ff = hier_moe(jnp.concatenate([h2.reshape(-1, d), hc2.reshape(-1, d)], axis=0), *moe_args)
            x = x + mod[:, 5:6] * ff[: b * s].reshape(b, s, d)
            xc = xc + mod_c[5] * ff[b * s:].reshape(b, L, d)
        else:
            x = x + mod[:, 5:6] * hier_moe(h2.reshape(-1, d), *moe_args).reshape(b, s, d)
    return x
```

```python
import functools

import jax
import jax.numpy as jnp
import numpy as np
from jax import lax
from jax.experimental import pallas as pl
from jax.experimental.pallas import tpu as pltpu

F32 = jnp.float32
BF16 = jnp.bfloat16
HIGHEST = lax.Precision.HIGHEST

HEAD_DIM = 64
GRID_W = 64
NA_WIN_R = 8
NA_WIN_C = 16
ROPE_THETA = 10000.0
HGRN_CHUNK = 16
N_GROUPS = 4
EXPERTS_PER_GROUP = 8
N_EXPERTS = N_GROUPS * EXPERTS_PER_GROUP
N_MOD = 6
EPS = 1e-6
NEG_INF = -1e30
LB_FLOOR = 1e-20
LANES = 128
VMEM_LIMIT = 56 * 1024 * 1024


def _cparams(*sem):
    return pltpu.CompilerParams(dimension_semantics=sem, vmem_limit_bytes=VMEM_LIMIT)


def _block_diag_ones(n, blk, dtype):
    i = np.arange(n)
    return jnp.asarray((i[:, None] // blk) == (i[None, :] // blk), dtype=dtype)


def _ada_kernel(c_ref, w_ref, b_ref, o_ref):
    c = c_ref[...]
    s = c * jax.nn.sigmoid(c)
    o_ref[0] = jnp.dot(s, w_ref[0], precision=HIGHEST, preferred_element_type=F32) + b_ref[0]


def ada_mod(c_all, w_ada, b_ada):
    depth, d, n = w_ada.shape
    tn = 1536
    return pl.pallas_call(
        _ada_kernel,
        grid=(depth, n // tn),
        in_specs=[
            pl.BlockSpec((16, d), lambda l, j: (0, 0)),
            pl.BlockSpec((1, d, tn), lambda l, j: (l, 0, j)),
            pl.BlockSpec((1, 1, tn), lambda l, j: (l, 0, j)),
        ],
        out_specs=pl.BlockSpec((1, 16, tn), lambda l, j: (l, 0, j)),
        out_shape=jax.ShapeDtypeStruct((depth, 16, n), F32),
        compiler_params=_cparams("parallel", "parallel"),
        name="ada_mod",
    )(c_all, w_ada, b_ada.reshape(depth, 1, n))


def _seg_inv_rms(x, bd):
    ss = jnp.dot(x * x, bd, precision=HIGHEST, preferred_element_type=F32)
    return lax.rsqrt(ss * (1.0 / HEAD_DIM) + EPS)


def _pair_swap(x):
    lane = lax.broadcasted_iota(jnp.int32, x.shape, 1)
    return jnp.where((lane & 1) == 0, pltpu.roll(x, LANES - 1, 1), pltpu.roll(x, 1, 1))


def _inproj_kernel(x_ref, g1_ref, shift_ref, scale_ref, w_ref, cos_ref, sin_ref, gains_ref, bd_ref,
                   oa_ref, ob_ref, oc_ref, *, na_w, gq_qw, gq_kw):
    x = x_ref[...]
    ms = jnp.mean(x * x, axis=-1, keepdims=True)
    h = x * lax.rsqrt(ms + EPS) * g1_ref[0]
    h = h * (1.0 + scale_ref[0]) + shift_ref[0]
    p = jnp.dot(h.astype(BF16), w_ref[...], preferred_element_type=F32)
    bd = bd_ref[...]
    cos = cos_ref[...]
    sin = sin_ref[...]
    qscale = HEAD_DIM ** -0.5

    def normed(col, gain_row):
        xb = p[:, col:col + LANES]
        return xb * _seg_inv_rms(xb, bd) * gains_ref[gain_row:gain_row + 1, :]

    def rope(xn):
        return xn * cos + _pair_swap(xn) * sin

    for j in range(na_w // LANES):
        c = j * LANES
        oa_ref[:, c:c + LANES] = (normed(c, 0) * qscale).astype(BF16)
        oa_ref[:, na_w + c:na_w + c + LANES] = normed(na_w + c, 1).astype(BF16)
    oa_ref[:, 2 * na_w:3 * na_w] = p[:, 2 * na_w:3 * na_w].astype(BF16)
    b0 = 3 * na_w
    for j in range(gq_qw // LANES):
        c = j * LANES
        ob_ref[:, c:c + LANES] = (rope(normed(b0 + c, 2)) * qscale).astype(BF16)
    for j in range(gq_kw // LANES):
        c = gq_qw + j * LANES
        ob_ref[:, c:c + LANES] = rope(normed(b0 + c, 3)).astype(BF16)
    ob_ref[:, gq_qw + gq_kw:] = p[:, b0 + gq_qw + gq_kw:b0 + gq_qw + 2 * gq_kw].astype(BF16)
    oc_ref[...] = p[:, b0 + gq_qw + 2 * gq_kw:]


def in_projection(xall, mod3, layer_g1, w_in_bf, cos_t, sin_t, gains, bd, *, n_lat_rows, seq, nseg, tm):
    r, d = xall.shape
    d_in = w_in_bf.shape[1]
    na_w = d // 4
    gq_qw = d // 2
    gq_kw = gq_qw // 4
    c_w = d_in - 3 * na_w - gq_qw - 2 * gq_kw
    lat_tiles = n_lat_rows // tm
    tiles_per_seq = seq // tm

    def seg(i):
        return jnp.minimum(i // tiles_per_seq, nseg - 1)

    def rope_blk(i):
        return jnp.where(i < lat_tiles, i % tiles_per_seq, tiles_per_seq)

    kern = functools.partial(_inproj_kernel, na_w=na_w, gq_qw=gq_qw, gq_kw=gq_kw)
    return pl.pallas_call(
        kern,
        grid=(r // tm,),
        in_specs=[
            pl.BlockSpec((tm, d), lambda i: (i, 0)),
            pl.BlockSpec((1, d), lambda i: (0, 0)),
            pl.BlockSpec((1, 1, d), lambda i: (seg(i) * N_MOD + 0, 0, 0)),
            pl.BlockSpec((1, 1, d), lambda i: (seg(i) * N_MOD + 1, 0, 0)),
            pl.BlockSpec((d, d_in), lambda i: (0, 0)),
            pl.BlockSpec((tm, LANES), lambda i: (rope_blk(i), 0)),
            pl.BlockSpec((tm, LANES), lambda i: (rope_blk(i), 0)),
            pl.BlockSpec((8, LANES), lambda i: (0, 0)),
            pl.BlockSpec((LANES, LANES), lambda i: (0, 0)),
        ],
        out_specs=[
            pl.BlockSpec((tm, 3 * na_w), lambda i: (i, 0)),
            pl.BlockSpec((tm, gq_qw + 2 * gq_kw), lambda i: (i, 0)),
            pl.BlockSpec((tm, c_w), lambda i: (i, 0)),
        ],
        out_shape=[
            jax.ShapeDtypeStruct((r, 3 * na_w), BF16),
            jax.ShapeDtypeStruct((r, gq_qw + 2 * gq_kw), BF16),
            jax.ShapeDtypeStruct((r, c_w), F32),
        ],
        compiler_params=_cparams("parallel"),
        name="in_projection",
    )(xall, layer_g1, mod3, mod3, w_in_bf, cos_t, sin_t, gains, bd)


def _attn_kernel(q_ref, k_ref, v_ref, o_ref):
    s = lax.dot_general(q_ref[0], k_ref[0], (((1,), (1,)), ((), ())), preferred_element_type=F32)
    m = jnp.max(s, axis=-1, keepdims=True)
    p = jnp.exp(s - m)
    l = jnp.sum(p, axis=-1, keepdims=True)
    o = jnp.dot(p.astype(BF16), v_ref[0], preferred_element_type=F32)
    o_ref[0] = (o / l).astype(o_ref.dtype)


def attention(q, k, v, tq):
    g, nq, hd = q.shape
    nk = k.shape[1]
    tq = min(tq, nq)
    return pl.pallas_call(
        _attn_kernel,
        grid=(g, nq // tq),
        in_specs=[
            pl.BlockSpec((1, tq, hd), lambda i, j: (i, j, 0)),
            pl.BlockSpec((1, nk, hd), lambda i, j: (i, 0, 0)),
            pl.BlockSpec((1, nk, hd), lambda i, j: (i, 0, 0)),
        ],
        out_specs=pl.BlockSpec((1, tq, hd), lambda i, j: (i, j, 0)),
        out_shape=jax.ShapeDtypeStruct((g, nq, hd), BF16),
        compiler_params=_cparams("parallel", "parallel"),
        name="attention",
    )(q, k, v)


def _na_kernel(q_ref, k_ref, v_ref, kc_ref, vc_ref, bias_ref, o_ref, *, rows, wr):
    kc = kc_ref[0]
    vc = vc_ref[0]
    nkey = wr * GRID_W

    def body(r, carry):
        rs = jnp.clip(r - wr // 2, 0, rows - wr)
        q = q_ref[0, pl.ds(pl.multiple_of(r * GRID_W, GRID_W), GRID_W), :]
        kb = k_ref[0, pl.ds(pl.multiple_of(rs * GRID_W, GRID_W), nkey), :]
        vb = v_ref[0, pl.ds(pl.multiple_of(rs * GRID_W, GRID_W), nkey), :]
        s_nb = lax.dot_general(q, kb, (((1,), (1,)), ((), ())), preferred_element_type=F32)
        s_nb = s_nb + bias_ref[0, r - rs]
        s_cx = lax.dot_general(q, kc, (((1,), (1,)), ((), ())), preferred_element_type=F32)
        m = jnp.maximum(jnp.max(s_nb, axis=-1, keepdims=True), jnp.max(s_cx, axis=-1, keepdims=True))
        p_nb = jnp.exp(s_nb - m)
        p_cx = jnp.exp(s_cx - m)
        l = jnp.sum(p_nb, axis=-1, keepdims=True) + jnp.sum(p_cx, axis=-1, keepdims=True)
        o = (jnp.dot(p_nb.astype(BF16), vb, preferred_element_type=F32)
             + jnp.dot(p_cx.astype(BF16), vc, preferred_element_type=F32))
        o_ref[0, pl.ds(pl.multiple_of(r * GRID_W, GRID_W), GRID_W), :] = (o / l).astype(o_ref.dtype)
        return carry

    lax.fori_loop(0, rows, body, 0)


def na_attention(q, k, v, kc, vc, bias_tab, n_heads):
    g, s, hd = q.shape
    l = kc.shape[1]
    rows = s // GRID_W
    wr = min(NA_WIN_R, rows)
    kern = functools.partial(_na_kernel, rows=rows, wr=wr)
    seq_spec = pl.BlockSpec((1, s, hd), lambda i: (i, 0, 0))
    ctx_spec = pl.BlockSpec((1, l, hd), lambda i: (i, 0, 0))
    return pl.pallas_call(
        kern,
        grid=(g,),
        in_specs=[seq_spec, seq_spec, seq_spec, ctx_spec, ctx_spec,
                  pl.BlockSpec((1,) + bias_tab.shape[1:], lambda i: (i % n_heads, 0, 0, 0))],
        out_specs=seq_spec,
        out_shape=jax.ShapeDtypeStruct((g, s, hd), BF16),
        compiler_params=_cparams("parallel"),
        name="na_attention",
    )(q, k, v, kc, vc, bias_tab)


def na_bias_table(rpb, rows):
    wr = min(NA_WIN_R, rows)
    h = rpb.shape[0]
    t = np.arange(wr)
    kr = np.arange(wr)
    dr = kr[None, :] - t[:, None] + (NA_WIN_R - 1)
    cidx = np.arange(GRID_W)
    col_start = np.clip(cidx - NA_WIN_C // 2, 0, GRID_W - NA_WIN_C)
    col_ok = (cidx[None, :] >= col_start[:, None]) & (cidx[None, :] < col_start[:, None] + NA_WIN_C)
    dc = np.clip(cidx[None, :] - cidx[:, None] + (NA_WIN_C - 1), 0, 2 * NA_WIN_C - 2)
    bias = rpb.astype(F32)[:, dr[:, None, :, None], dc[None, :, None, :]]
    bias = jnp.where(jnp.asarray(col_ok)[None, None, :, None, :], bias, NEG_INF)
    return bias.reshape(h, wr, GRID_W, wr * GRID_W)


HG_BLOCK = 128


def _hgrn_kernel(q_ref, v_ref, z_ref, lb_ref, bdb_ref, bdf_ref, o_ref, *, n_tok):
    c = HGRN_CHUNK
    ncb = HG_BLOCK // c
    lb = lb_ref[0, 0]
    lbm = jnp.maximum(lb, LB_FLOOR)
    one_m_lb = 1.0 - lb
    bdb = bdb_ref[...]
    bdf = bdf_ref[...]
    scale = HEAD_DIM ** -0.5
    t_idx = lax.broadcasted_iota(jnp.int32, (ncb, c, LANES), 1)

    def bs(x, s):
        return jnp.broadcast_to(x[:, s:s + 1, :], x.shape)

    def body(blk, st):
        r0 = pl.multiple_of(blk * HG_BLOCK, HG_BLOCK)
        z = z_ref[0, 0, pl.ds(r0, HG_BLOCK), :]
        q = q_ref[0, 0, pl.ds(r0, HG_BLOCK), :] * scale
        v = v_ref[0, 0, pl.ds(r0, HG_BLOCK), :]
        f = one_m_lb * jax.nn.sigmoid(z) + lbm
        k = one_m_lb * jax.nn.sigmoid(-z) - (lbm - lb)
        logf = jnp.log(f).reshape(ncb, c, LANES)
        q3 = q.reshape(ncb, c, LANES)
        k3 = k.reshape(ncb, c, LANES)
        v3 = v.reshape(ncb, c, LANES)
        cum = jnp.zeros_like(logf)
        for s in range(c):
            cum = cum + jnp.where(t_idx >= s, bs(logf, s), 0.0)
        o3 = jnp.zeros_like(logf)
        for s in range(c):
            d = jnp.where(t_idx >= s, cum - bs(cum, s), NEG_INF)
            w = q3 * bs(k3, s) * jnp.exp(d)
            a = jnp.dot(w.reshape(HG_BLOCK, LANES).astype(BF16), bdb, preferred_element_type=F32)
            o3 = o3 + a.reshape(ncb, c, LANES) * bs(v3, s)
        cum_last = bs(cum, c - 1)
        qe = (q3 * jnp.exp(cum)).astype(BF16)
        kd = (k3 * jnp.exp(cum_last - cum)).astype(BF16)
        vb = v3.astype(BF16)
        a_all = jnp.exp(cum_last)
        outs = []
        for n in range(ncb):
            o_inter = lax.dot_general(qe[n], st.astype(BF16), (((1,), (1,)), ((), ())),
                                      preferred_element_type=F32)
            outs.append(o3[n] + o_inter)
            u_t = lax.dot_general(vb[n], kd[n], (((0,), (0,)), ((), ())), preferred_element_type=F32)
            st = a_all[n, 0:1, :] * st + u_t * bdf
        o_ref[0, 0, pl.ds(r0, HG_BLOCK), :] = jnp.concatenate(outs, axis=0)
        return st

    lax.fori_loop(0, n_tok // HG_BLOCK, body, jnp.zeros((LANES, LANES), F32))


def hgrn_scan(q, v, z, lb, bdb, bdf):
    nd, b, n, w = q.shape
    kern = functools.partial(_hgrn_kernel, n_tok=n)
    spec = pl.BlockSpec((1, 1, n, LANES), lambda d, i, j: (d, i, 0, j))
    return pl.pallas_call(
        kern,
        grid=(nd, b, w // LANES),
        in_specs=[spec, spec, spec,
                  pl.BlockSpec((1, 1, 1, LANES), lambda d, i, j: (d, j, 0, 0)),
                  pl.BlockSpec((LANES, LANES), lambda d, i, j: (0, 0)),
                  pl.BlockSpec((LANES, LANES), lambda d, i, j: (0, 0))],
        out_specs=spec,
        out_shape=jax.ShapeDtypeStruct((nd, b, n, w), F32),
        compiler_params=_cparams("parallel", "parallel", "parallel"),
        name="hgrn_scan",
    )(q, v, z, lb, bdb, bdf)


def _readout_kernel(of_ref, ob_ref, g_ref, gain_ref, bd_ref, y_ref):
    bd = bd_ref[...]
    for j in range(of_ref.shape[1] // LANES):
        sl = slice(j * LANES, (j + 1) * LANES)
        o = of_ref[:, sl] + ob_ref[:, sl]
        y = o * _seg_inv_rms(o, bd) * gain_ref[...]
        g = g_ref[:, sl]
        y_ref[:, sl] = (y * (g * jax.nn.sigmoid(g))).astype(y_ref.dtype)


def hgrn_readout(o_f, o_b, gate, gain128, bd, tm):
    r, w = o_f.shape
    spec = pl.BlockSpec((tm, w), lambda i: (i, 0))
    return pl.pallas_call(
        _readout_kernel,
        grid=(r // tm,),
        in_specs=[spec, spec, spec,
                  pl.BlockSpec((1, LANES), lambda i: (0, 0)),
                  pl.BlockSpec((LANES, LANES), lambda i: (0, 0))],
        out_specs=spec,
        out_shape=jax.ShapeDtypeStruct((r, w), BF16),
        compiler_params=_cparams("parallel"),
        name="hgrn_readout",
    )(o_f, o_b, gate, gain128, bd)


def _outproj_kernel(x_ref, ma_ref, mb_ref, mc_ref, w_ref, gate_ref, g2_ref, shift_ref, scale_ref,
                    wr_ref, br_ref, xo_ref, h2_ref, lg_ref, *, wa, wb):
    w = w_ref[...]
    y = jnp.dot(ma_ref[...], w[:wa], preferred_element_type=F32)
    y = y + jnp.dot(mb_ref[...], w[wa:wa + wb], preferred_element_type=F32)
    y = y + jnp.dot(mc_ref[...], w[wa + wb:], preferred_element_type=F32)
    x = x_ref[...] + gate_ref[0] * y
    xo_ref[...] = x
    ms = jnp.mean(x * x, axis=-1, keepdims=True)
    h = x * lax.rsqrt(ms + EPS) * g2_ref[0]
    h = h * (1.0 + scale_ref[0]) + shift_ref[0]
    h2_ref[...] = h.astype(BF16)
    lg_ref[...] = jnp.dot(h, wr_ref[...], precision=HIGHEST, preferred_element_type=F32) + br_ref[...]


def out_projection(xall, mix_a, mix_b, mix_c, w_out_bf, mod3, layer_g2, w_route, b_route, *, seq, nseg, tm):
    r, d = xall.shape
    wa, wb, wc = mix_a.shape[1], mix_b.shape[1], mix_c.shape[1]
    tiles_per_seq = seq // tm

    def seg(i):
        return jnp.minimum(i // tiles_per_seq, nseg - 1)

    def modspec(m):
        return pl.BlockSpec((1, 1, d), lambda i: (seg(i) * N_MOD + m, 0, 0))

    row = lambda wdt: pl.BlockSpec((tm, wdt), lambda i: (i, 0))
    kern = functools.partial(_outproj_kernel, wa=wa, wb=wb)
    return pl.pallas_call(
        kern,
        grid=(r // tm,),
        in_specs=[row(d), row(wa), row(wb), row(wc),
                  pl.BlockSpec((wa + wb + wc, d), lambda i: (0, 0)),
                  modspec(2),
                  pl.BlockSpec((1, d), lambda i: (0, 0)),
                  modspec(3), modspec(4),
                  pl.BlockSpec((d, LANES), lambda i: (0, 0)),
                  pl.BlockSpec((1, LANES), lambda i: (0, 0))],
        out_specs=[row(d), row(d), row(LANES)],
        out_shape=[jax.ShapeDtypeStruct((r, d), F32),
                   jax.ShapeDtypeStruct((r, d), BF16),
                   jax.ShapeDtypeStruct((r, LANES), F32)],
        compiler_params=_cparams("parallel"),
        name="out_projection",
    )(xall, mix_a, mix_b, mix_c, w_out_bf, mod3, layer_g2, mod3, mod3, w_route, b_route)


def _route_weights(logits):
    lane = lax.broadcasted_iota(jnp.int32, logits.shape, 1).astype(F32)
    is_g = lane < N_GROUPS
    gl = jnp.where(is_g, logits, -jnp.inf)
    gmax = jnp.max(gl, axis=-1, keepdims=True)
    g_idx = jnp.min(jnp.where(gl == gmax, lane, LANES), axis=-1, keepdims=True)
    gsum = jnp.sum(jnp.where(is_g, jnp.exp(gl - gmax), 0.0), axis=-1, keepdims=True)
    g_top = 1.0 / gsum
    lo = N_GROUPS + g_idx * EXPERTS_PER_GROUP
    in_grp = (lane >= lo) & (lane < lo + EXPERTS_PER_GROUP)
    el = jnp.where(in_grp, logits, -jnp.inf)
    m1 = jnp.max(el, axis=-1, keepdims=True)
    i1 = jnp.min(jnp.where(el == m1, lane, LANES), axis=-1, keepdims=True)
    el2 = jnp.where(lane == i1, -jnp.inf, el)
    m2 = jnp.max(el2, axis=-1, keepdims=True)
    i2 = jnp.min(jnp.where(el2 == m2, lane, LANES), axis=-1, keepdims=True)
    e21 = jnp.exp(m2 - m1)
    w1 = 1.0 / (1.0 + e21)
    w2 = e21 * w1
    return jnp.where(lane == i1, w1 * g_top, 0.0) + jnp.where(lane == i2, w2 * g_top, 0.0)


def _moe_kernel(x_ref, h_ref, lg_ref, gate_ref, wg_ref, wu_ref, wd_ref, o_ref, cw_ref, acc_ref):
    e = pl.program_id(1)

    @pl.when(e == 0)
    def _():
        cw_ref[...] = _route_weights(lg_ref[...])
        acc_ref[...] = jnp.zeros_like(acc_ref)

    h = h_ref[...]
    lane = lax.broadcasted_iota(jnp.int32, cw_ref.shape, 1)
    w_e = jnp.sum(jnp.where(lane == e + N_GROUPS, cw_ref[...], 0.0), axis=-1, keepdims=True)
    hg = jnp.dot(h, wg_ref[0].astype(BF16), preferred_element_type=F32)
    hu = jnp.dot(h, wu_ref[0].astype(BF16), preferred_element_type=F32)
    hid = (hg * jax.nn.sigmoid(hg)) * hu * w_e
    acc_ref[...] += jnp.dot(hid.astype(BF16), wd_ref[0].astype(BF16), preferred_element_type=F32)

    @pl.when(e == pl.num_programs(1) - 1)
    def _():
        o_ref[...] = x_ref[...] + gate_ref[0] * acc_ref[...]


def moe_block(xall, h2, logits, mod3, w_gate, w_up, w_down, *, seq, nseg, tm):
    r, d = xall.shape
    ne, _, ff = w_gate.shape
    tiles_per_seq = seq // tm

    def seg(i):
        return jnp.minimum(i // tiles_per_seq, nseg - 1)

    row = lambda wdt: pl.BlockSpec((tm, wdt), lambda i, e: (i, 0))
    return pl.pallas_call(
        _moe_kernel,
        grid=(r // tm, ne),
        in_specs=[row(d), row(d), row(LANES),
                  pl.BlockSpec((1, 1, d), lambda i, e: (seg(i) * N_MOD + 5, 0, 0)),
                  pl.BlockSpec((1, d, ff), lambda i, e: (e, 0, 0)),
                  pl.BlockSpec((1, d, ff), lambda i, e: (e, 0, 0)),
                  pl.BlockSpec((1, ff, d), lambda i, e: (e, 0, 0))],
        out_specs=row(d),
        out_shape=jax.ShapeDtypeStruct((r, d), F32),
        scratch_shapes=[pltpu.VMEM((tm, LANES), F32), pltpu.VMEM((tm, d), F32)],
        compiler_params=_cparams("parallel", "arbitrary"),
        name="moe_block",
    )(xall, h2, logits, mod3, w_gate, w_up, w_down)


def _rope_tables(seq, tm):
    t = np.arange(seq)
    row = (t // GRID_W).astype(np.float32)
    col = (t % GRID_W).astype(np.float32)
    half = HEAD_DIM // 2
    inv = jnp.asarray(ROPE_THETA, F32) ** (-jnp.arange(0, half, 2, dtype=F32) / half)
    ang = jnp.concatenate([jnp.asarray(row)[:, None] * inv, jnp.asarray(col)[:, None] * inv], axis=-1)
    cos = jnp.repeat(jnp.cos(ang), 2, axis=-1)
    sin = jnp.repeat(jnp.sin(ang), 2, axis=-1) * jnp.asarray(np.tile([-1.0, 1.0], half), F32)
    cos = jnp.tile(cos, (1, LANES // HEAD_DIM))
    sin = jnp.tile(sin, (1, LANES // HEAD_DIM))
    cos = jnp.concatenate([cos, jnp.ones((tm, LANES), F32)], axis=0)
    sin = jnp.concatenate([sin, jnp.zeros((tm, LANES), F32)], axis=0)
    return cos, sin


def _heads(t, b, n, nh):
    return jnp.swapaxes(t.reshape(b, n, nh, HEAD_DIM), 1, 2).reshape(b * nh, n, HEAD_DIM)


def _unheads(t, b, n, nh):
    return jnp.swapaxes(t.reshape(b, nh, n, HEAD_DIM), 1, 2).reshape(b * n, nh * HEAD_DIM)


def kernel(x, c, ctx, c_ctx, w_ada, b_ada, norm1_g, w_in, na_q_norm, na_k_norm, na_rpb, gqa_q_norm, gqa_k_norm, hgrn_lb, hgrn_o_norm, w_out, norm2_g, w_route_group, b_route_group, w_route_expert, b_route_expert, w_exp_gate, w_exp_up, w_exp_down):
    b, s, d = x.shape
    l = ctx.shape[1]
    depth = w_ada.shape[0]
    assert s % 512 == 0 and (b * l) % 512 == 0 and s // GRID_W >= NA_WIN_R
    nseg = b + 1
    n_lat = b * s
    n_ctx = b * l
    na_w, gq_qw = d // 4, d // 2
    gq_kw = gq_qw // 4
    na_h, gq_h, gq_kvh = na_w // HEAD_DIM, gq_qw // HEAD_DIM, gq_kw // HEAD_DIM
    grp = gq_h // gq_kvh
    hg_w = d // 4
    tm = 512

    c_all = jnp.zeros((16, d), F32).at[:b].set(c).at[b].set(c_ctx)
    mod = ada_mod(c_all, w_ada, b_ada)
    cos_t, sin_t = _rope_tables(s, tm)
    bd_f = _block_diag_ones(LANES, HEAD_DIM, F32)
    bd_b = _block_diag_ones(LANES, HEAD_DIM, BF16)
    p_lb = jax.nn.softmax(hgrn_lb.astype(F32), axis=0)
    lb_all = jnp.cumsum(p_lb, axis=0) - p_lb[0]
    tile2 = lambda g: jnp.tile(g, LANES // HEAD_DIM)

    xall = jnp.concatenate([x.reshape(n_lat, d), ctx.reshape(n_ctx, d)], axis=0)
    for layer in range(depth):
        ctx_out = layer < depth - 1
        mod3 = mod[layer].reshape(16 * N_MOD, 1, d)
        gains = jnp.zeros((8, LANES), F32)
        gains = gains.at[0].set(tile2(na_q_norm[layer])).at[1].set(tile2(na_k_norm[layer]))
        gains = gains.at[2].set(tile2(gqa_q_norm[layer])).at[3].set(tile2(gqa_k_norm[layer]))
        pa, pb, pc = in_projection(xall, mod3, norm1_g[layer][None], w_in[layer].astype(BF16), cos_t, sin_t,
                                   gains, bd_f, n_lat_rows=n_lat, seq=s, nseg=nseg, tm=tm)
        qa, ka, va = pa[:, :na_w], pa[:, na_w:2 * na_w], pa[:, 2 * na_w:]
        kac = _heads(ka[n_lat:], b, l, na_h)
        vac = _heads(va[n_lat:], b, l, na_h)
        bias_tab = na_bias_table(na_rpb[layer], s // GRID_W)
        o_a = na_attention(_heads(qa[:n_lat], b, s, na_h), _heads(ka[:n_lat], b, s, na_h),
                           _heads(va[:n_lat], b, s, na_h), kac, vac, bias_tab, na_h)
        o_a = _unheads(o_a, b, s, na_h)
        qb, kb, vb = pb[:, :gq_qw], pb[:, gq_qw:gq_qw + gq_kw], pb[:, gq_qw + gq_kw:]
        kbl, kbc = _heads(kb[:n_lat], b, s, gq_kvh), _heads(kb[n_lat:], b, l, gq_kvh)
        vbl, vbc = _heads(vb[:n_lat], b, s, gq_kvh), _heads(vb[n_lat:], b, l, gq_kvh)
        k_all = jnp.concatenate([kbl, kbc], axis=1)
        v_all = jnp.concatenate([vbl, vbc], axis=1)
        qbl = _heads(qb[:n_lat], b, s, gq_h).reshape(b * gq_kvh, grp * s, HEAD_DIM)
        o_b = attention(qbl, k_all, v_all, 512)
        o_b = _unheads(o_b.reshape(b * gq_h, s, HEAD_DIM), b, s, gq_h)
        lat_c = pc[:n_lat].reshape(b, s, -1)
        ctx_c = pc[n_lat:].reshape(b, l, -1)
        seq_f = jnp.concatenate([ctx_c, lat_c], axis=1)
        seq_r = jnp.concatenate([ctx_c[:, ::-1], lat_c[:, ::-1]], axis=1)
        hq = jnp.stack([seq_f[..., :hg_w], seq_r[..., :hg_w]])
        hv = jnp.stack([seq_f[..., hg_w:2 * hg_w], seq_r[..., hg_w:2 * hg_w]])
        hz = jnp.stack([seq_f[..., 2 * hg_w:3 * hg_w], seq_r[..., 3 * hg_w:4 * hg_w]])
        lb4 = lb_all[layer].reshape(2, hg_w // LANES, 1, LANES)
        o_h = hgrn_scan(hq, hv, hz, lb4, bd_b, bd_f)
        o_f = o_h[0]
        o_r = jnp.concatenate([o_h[1][:, :l][:, ::-1], o_h[1][:, l:][:, ::-1]], axis=1)
        to_rows = lambda t: jnp.concatenate([t[:, l:].reshape(n_lat, hg_w), t[:, :l].reshape(n_ctx, hg_w)], axis=0)
        y_c = hgrn_readout(to_rows(o_f), to_rows(o_r), pc[:, 4 * hg_w:], tile2(hgrn_o_norm[layer])[None], bd_f, tm)

        w_route = jnp.zeros((d, LANES), F32).at[:, :N_GROUPS].set(w_route_group[layer])
        w_route = w_route.at[:, N_GROUPS:N_GROUPS + N_EXPERTS].set(w_route_expert[layer])
        b_route = jnp.zeros((1, LANES), F32).at[0, :N_GROUPS].set(b_route_group[layer])
        b_route = b_route.at[0, N_GROUPS:N_GROUPS + N_EXPERTS].set(b_route_expert[layer])
        if ctx_out:
            o_ac = _unheads(attention(_heads(qa[n_lat:], b, l, na_h), kac, vac, 512), b, l, na_h)
            qbc = _heads(qb[n_lat:], b, l, gq_h).reshape(b * gq_kvh, grp * l, HEAD_DIM)
            o_bc = attention(qbc, kbc, vbc, 512)
            o_bc = _unheads(o_bc.reshape(b * gq_h, l, HEAD_DIM), b, l, gq_h)
            mix_a = jnp.concatenate([o_a, o_ac], axis=0)
            mix_b = jnp.concatenate([o_b, o_bc], axis=0)
            x_in = xall
        else:
            mix_a, mix_b, y_c, x_in = o_a, o_b, y_c[:n_lat], xall[:n_lat]
        x_mid, h2, logits = out_projection(x_in, mix_a, mix_b, y_c, w_out[layer].astype(BF16), mod3,
                                            norm2_g[layer][None], w_route, b_route, seq=s, nseg=nseg, tm=tm)
        xall = moe_block(x_mid, h2, logits, mod3, w_exp_gate[layer], w_exp_up[layer], w_exp_down[layer],
                         seq=s, nseg=nseg, tm=1024 if x_mid.shape[0] % 1024 == 0 and s % 1024 == 0 else tm)
    return xall[:n_lat].reshape(b, s, d)
```

```python
import functools

import jax
import jax.numpy as jnp
import numpy as np
from jax import lax
from jax.experimental import pallas as pl
from jax.experimental.pallas import tpu as pltpu

F32 = jnp.float32
BF16 = jnp.bfloat16
HIGHEST = lax.Precision.HIGHEST

HEAD_DIM = 64
GRID_W = 64
NA_WIN_R = 8
NA_WIN_C = 16
ROPE_THETA = 10000.0
HGRN_CHUNK = 16
N_GROUPS = 4
EXPERTS_PER_GROUP = 8
N_EXPERTS = N_GROUPS * EXPERTS_PER_GROUP
N_MOD = 6
EPS = 1e-6
NEG_INF = -1e30
LB_FLOOR = 1e-20
LANES = 128
VMEM_LIMIT = 56 * 1024 * 1024


def _cparams(*sem):
    return pltpu.CompilerParams(dimension_semantics=sem, vmem_limit_bytes=VMEM_LIMIT)


def _block_diag_ones(n, blk, dtype):
    i = np.arange(n)
    return jnp.asarray((i[:, None] // blk) == (i[None, :] // blk), dtype=dtype)


def _ada_kernel(c_ref, w_ref, b_ref, o_ref):
    c = c_ref[...]
    s = c * jax.nn.sigmoid(c)
    o_ref[0] = jnp.dot(s, w_ref[0], precision=HIGHEST, preferred_element_type=F32) + b_ref[0]


def ada_mod(c_all, w_ada, b_ada):
    depth, d, n = w_ada.shape
    tn = 1536
    return pl.pallas_call(
        _ada_kernel,
        grid=(depth, n // tn),
        in_specs=[
            pl.BlockSpec((16, d), lambda l, j: (0, 0)),
            pl.BlockSpec((1, d, tn), lambda l, j: (l, 0, j)),
            pl.BlockSpec((1, 1, tn), lambda l, j: (l, 0, j)),
        ],
        out_specs=pl.BlockSpec((1, 16, tn), lambda l, j: (l, 0, j)),
        out_shape=jax.ShapeDtypeStruct((depth, 16, n), F32),
        compiler_params=_cparams("parallel", "parallel"),
        name="ada_mod",
    )(c_all, w_ada, b_ada.reshape(depth, 1, n))


def _seg_inv_rms(x, bd):
    ss = jnp.dot(x * x, bd, precision=HIGHEST, preferred_element_type=F32)
    return lax.rsqrt(ss * (1.0 / HEAD_DIM) + EPS)


def _pair_swap(x):
    lane = lax.broadcasted_iota(jnp.int32, x.shape, 1)
    return jnp.where((lane & 1) == 0, pltpu.roll(x, LANES - 1, 1), pltpu.roll(x, 1, 1))


def _inproj_kernel(x_ref, g1_ref, shift_ref, scale_ref, w_ref, cos_ref, sin_ref, gains_ref, bd_ref,
                   oa_ref, ob_ref, oc_ref, *, na_w, gq_qw, gq_kw):
    x = x_ref[...]
    ms = jnp.mean(x * x, axis=-1, keepdims=True)
    h = x * lax.rsqrt(ms + EPS) * g1_ref[0]
    h = h * (1.0 + scale_ref[0]) + shift_ref[0]
    p = jnp.dot(h.astype(BF16), w_ref[...], preferred_element_type=F32)
    bd = bd_ref[...]
    cos = cos_ref[...]
    sin = sin_ref[...]
    qscale = HEAD_DIM ** -0.5

    def normed(col, gain_row):
        xb = p[:, col:col + LANES]
        return xb * _seg_inv_rms(xb, bd) * gains_ref[gain_row:gain_row + 1, :]

    def rope(xn):
        return xn * cos + _pair_swap(xn) * sin

    for j in range(na_w // LANES):
        c = j * LANES
        oa_ref[:, c:c + LANES] = (normed(c, 0) * qscale).astype(BF16)
        oa_ref[:, na_w + c:na_w + c + LANES] = normed(na_w + c, 1).astype(BF16)
    oa_ref[:, 2 * na_w:3 * na_w] = p[:, 2 * na_w:3 * na_w].astype(BF16)
    b0 = 3 * na_w
    for j in range(gq_qw // LANES):
        c = j * LANES
        ob_ref[:, c:c + LANES] = (rope(normed(b0 + c, 2)) * qscale).astype(BF16)
    for j in range(gq_kw // LANES):
        c = gq_qw + j * LANES
        ob_ref[:, c:c + LANES] = rope(normed(b0 + c, 3)).astype(BF16)
    ob_ref[:, gq_qw + gq_kw:] = p[:, b0 + gq_qw + gq_kw:b0 + gq_qw + 2 * gq_kw].astype(BF16)
    oc_ref[...] = p[:, b0 + gq_qw + 2 * gq_kw:]


def in_projection(xall, mod3, layer_g1, w_in_bf, cos_t, sin_t, gains, bd, *, n_lat_rows, seq, nseg, tm):
    r, d = xall.shape
    d_in = w_in_bf.shape[1]
    na_w = d // 4
    gq_qw = d // 2
    gq_kw = gq_qw // 4
    c_w = d_in - 3 * na_w - gq_qw - 2 * gq_kw
    lat_tiles = n_lat_rows // tm
    tiles_per_seq = seq // tm

    def seg(i):
        return jnp.minimum(i // tiles_per_seq, nseg - 1)

    def rope_blk(i):
        return jnp.where(i < lat_tiles, i % tiles_per_seq, tiles_per_seq)

    kern = functools.partial(_inproj_kernel, na_w=na_w, gq_qw=gq_qw, gq_kw=gq_kw)
    return pl.pallas_call(
        kern,
        grid=(r // tm,),
        in_specs=[
            pl.BlockSpec((tm, d), lambda i: (i, 0)),
            pl.BlockSpec((1, d), lambda i: (0, 0)),
            pl.BlockSpec((1, 1, d), lambda i: (seg(i) * N_MOD + 0, 0, 0)),
            pl.BlockSpec((1, 1, d), lambda i: (seg(i) * N_MOD + 1, 0, 0)),
            pl.BlockSpec((d, d_in), lambda i: (0, 0)),
            pl.BlockSpec((tm, LANES), lambda i: (rope_blk(i), 0)),
            pl.BlockSpec((tm, LANES), lambda i: (rope_blk(i), 0)),
            pl.BlockSpec((8, LANES), lambda i: (0, 0)),
            pl.BlockSpec((LANES, LANES), lambda i: (0, 0)),
        ],
        out_specs=[
            pl.BlockSpec((tm, 3 * na_w), lambda i: (i, 0)),
            pl.BlockSpec((tm, gq_qw + 2 * gq_kw), lambda i: (i, 0)),
            pl.BlockSpec((tm, c_w), lambda i: (i, 0)),
        ],
        out_shape=[
            jax.ShapeDtypeStruct((r, 3 * na_w), BF16),
            jax.ShapeDtypeStruct((r, gq_qw + 2 * gq_kw), BF16),
            jax.ShapeDtypeStruct((r, c_w), F32),
        ],
        compiler_params=_cparams("parallel"),
        name="in_projection",
    )(xall, layer_g1, mod3, mod3, w_in_bf, cos_t, sin_t, gains, bd)


def _attn_kernel(q_ref, k_ref, v_ref, o_ref):
    s = lax.dot_general(q_ref[0], k_ref[0], (((1,), (1,)), ((), ())), preferred_element_type=F32)
    m = jnp.max(s, axis=-1, keepdims=True)
    p = jnp.exp(s - m)
    l = jnp.sum(p, axis=-1, keepdims=True)
    o = jnp.dot(p.astype(BF16), v_ref[0], preferred_element_type=F32)
    o_ref[0] = (o / l).astype(o_ref.dtype)


def attention(q, k, v, tq):
    g, nq, hd = q.shape
    nk = k.shape[1]
    tq = min(tq, nq)
    return pl.pallas_call(
        _attn_kernel,
        grid=(g, nq // tq),
        in_specs=[
            pl.BlockSpec((1, tq, hd), lambda i, j: (i, j, 0)),
            pl.BlockSpec((1, nk, hd), lambda i, j: (i, 0, 0)),
            pl.BlockSpec((1, nk, hd), lambda i, j: (i, 0, 0)),
        ],
        out_specs=pl.BlockSpec((1, tq, hd), lambda i, j: (i, j, 0)),
        out_shape=jax.ShapeDtypeStruct((g, nq, hd), BF16),
        compiler_params=_cparams("parallel", "parallel"),
        name="attention",
    )(q, k, v)


def _na_kernel(q_ref, k_ref, v_ref, kc_ref, vc_ref, bias_ref, o_ref, *, rows, wr):
    kc = kc_ref[0]
    vc = vc_ref[0]
    nkey = wr * GRID_W

    def body(r, carry):
        rs = jnp.clip(r - wr // 2, 0, rows - wr)
        q = q_ref[0, pl.ds(pl.multiple_of(r * GRID_W, GRID_W), GRID_W), :]
        kb = k_ref[0, pl.ds(pl.multiple_of(rs * GRID_W, GRID_W), nkey), :]
        vb = v_ref[0, pl.ds(pl.multiple_of(rs * GRID_W, GRID_W), nkey), :]
        s_nb = lax.dot_general(q, kb, (((1,), (1,)), ((), ())), preferred_element_type=F32)
        s_nb = s_nb + bias_ref[0, r - rs]
        s_cx = lax.dot_general(q, kc, (((1,), (1,)), ((), ())), preferred_element_type=F32)
        m = jnp.maximum(jnp.max(s_nb, axis=-1, keepdims=True), jnp.max(s_cx, axis=-1, keepdims=True))
        p_nb = jnp.exp(s_nb - m)
        p_cx = jnp.exp(s_cx - m)
        l = jnp.sum(p_nb, axis=-1, keepdims=True) + jnp.sum(p_cx, axis=-1, keepdims=True)
        o = (jnp.dot(p_nb.astype(BF16), vb, preferred_element_type=F32)
             + jnp.dot(p_cx.astype(BF16), vc, preferred_element_type=F32))
        o_ref[0, pl.ds(pl.multiple_of(r * GRID_W, GRID_W), GRID_W), :] = (o / l).astype(o_ref.dtype)
        return carry

    lax.fori_loop(0, rows, body, 0)


def na_attention(q, k, v, kc, vc, bias_tab, n_heads):
    g, s, hd = q.shape
    l = kc.shape[1]
    rows = s // GRID_W
    wr = min(NA_WIN_R, rows)
    kern = functools.partial(_na_kernel, rows=rows, wr=wr)
    seq_spec = pl.BlockSpec((1, s, hd), lambda i: (i, 0, 0))
    ctx_spec = pl.BlockSpec((1, l, hd), lambda i: (i, 0, 0))
    return pl.pallas_call(
        kern,
        grid=(g,),
        in_specs=[seq_spec, seq_spec, seq_spec, ctx_spec, ctx_spec,
                  pl.BlockSpec((1,) + bias_tab.shape[1:], lambda i: (i % n_heads, 0, 0, 0))],
        out_specs=seq_spec,
        out_shape=jax.ShapeDtypeStruct((g, s, hd), BF16),
        compiler_params=_cparams("parallel"),
        name="na_attention",
    )(q, k, v, kc, vc, bias_tab)


def na_bias_table(rpb, rows):
    wr = min(NA_WIN_R, rows)
    h = rpb.shape[0]
    t = np.arange(wr)
    kr = np.arange(wr)
    dr = kr[None, :] - t[:, None] + (NA_WIN_R - 1)
    cidx = np.arange(GRID_W)
    col_start = np.clip(cidx - NA_WIN_C // 2, 0, GRID_W - NA_WIN_C)
    col_ok = (cidx[None, :] >= col_start[:, None]) & (cidx[None, :] < col_start[:, None] + NA_WIN_C)
    dc = np.clip(cidx[None, :] - cidx[:, None] + (NA_WIN_C - 1), 0, 2 * NA_WIN_C - 2)
    sel_r = jnp.asarray(dr[:, :, None] == np.arange(2 * NA_WIN_R - 1), F32)
    sel_c = jnp.asarray(dc[:, :, None] == np.arange(2 * NA_WIN_C - 1), F32)
    bias = jnp.einsum("tki,hij->htkj", sel_r, rpb.astype(F32), precision=HIGHEST)
    bias = jnp.einsum("htkj,qcj->htqkc", bias, sel_c, precision=HIGHEST)
    bias = jnp.where(jnp.asarray(col_ok)[None, None, :, None, :], bias, NEG_INF)
    return bias.reshape(h, wr, GRID_W, wr * GRID_W)


HG_BLOCK = 128


def _hgrn_pass(q_ref, v_ref, z_ref, o_acc, lb, bdb, bdf, st, *, reverse, first):
    c = HGRN_CHUNK
    ncb = HG_BLOCK // c
    nblk = q_ref.shape[0] // HG_BLOCK
    lbm = jnp.maximum(lb, LB_FLOOR)
    one_m_lb = 1.0 - lb
    scale = HEAD_DIM ** -0.5
    t_idx = lax.broadcasted_iota(jnp.int32, (ncb, c, LANES), 1)
    edge = 0 if reverse else c - 1

    def bs(x, s):
        return jnp.broadcast_to(x[:, s:s + 1, :], x.shape)

    def seen(s):
        return (t_idx <= s) if reverse else (t_idx >= s)

    def body(i, st):
        blk = (nblk - 1 - i) if reverse else i
        r0 = pl.multiple_of(blk * HG_BLOCK, HG_BLOCK)
        z = z_ref[pl.ds(r0, HG_BLOCK), :]
        q = q_ref[pl.ds(r0, HG_BLOCK), :] * scale
        v = v_ref[pl.ds(r0, HG_BLOCK), :]
        f = one_m_lb * jax.nn.sigmoid(z) + lbm
        k = one_m_lb * jax.nn.sigmoid(-z) - (lbm - lb)
        logf = jnp.log(f).reshape(ncb, c, LANES)
        q3 = q.reshape(ncb, c, LANES)
        k3 = k.reshape(ncb, c, LANES)
        v3 = v.reshape(ncb, c, LANES)
        cum = jnp.zeros_like(logf)
        for s in range(c):
            cum = cum + jnp.where(seen(s), bs(logf, s), 0.0)
        o3 = jnp.zeros_like(logf)
        for s in range(c):
            d = jnp.where(seen(s), cum - bs(cum, s), NEG_INF)
            w = q3 * bs(k3, s) * jnp.exp(d)
            a = jnp.dot(w.reshape(HG_BLOCK, LANES).astype(BF16), bdb, preferred_element_type=F32)
            o3 = o3 + a.reshape(ncb, c, LANES) * bs(v3, s)
        cum_edge = bs(cum, edge)
        qe = (q3 * jnp.exp(cum)).astype(BF16)
        kd = (k3 * jnp.exp(cum_edge - cum)).astype(BF16)
        vb = v3.astype(BF16)
        a_all = jnp.exp(cum_edge)
        outs = [None] * ncb
        for n in (range(ncb - 1, -1, -1) if reverse else range(ncb)):
            o_inter = lax.dot_general(qe[n], st.astype(BF16), (((1,), (1,)), ((), ())),
                                      preferred_element_type=F32)
            outs[n] = o3[n] + o_inter
            u_t = lax.dot_general(vb[n], kd[n], (((0,), (0,)), ((), ())), preferred_element_type=F32)
            st = a_all[n, 0:1, :] * st + u_t * bdf
        val = jnp.concatenate(outs, axis=0)
        if first:
            o_acc[pl.ds(r0, HG_BLOCK), :] = val
        else:
            o_acc[pl.ds(r0, HG_BLOCK), :] += val
        return st

    return lax.fori_loop(0, nblk, body, st)


def _hgrn_kernel(ql_ref, qc_ref, vl_ref, vc_ref, zfl_ref, zfc_ref, zbl_ref, zbc_ref, gl_ref, gc_ref,
                 lb_ref, gain_ref, bdb_ref, bdf_ref, yl_ref, yc_ref, ol_acc, oc_acc):
    bdb = bdb_ref[...]
    bdf = bdf_ref[...]
    zero = jnp.zeros((LANES, LANES), F32)
    run = functools.partial(_hgrn_pass, bdb=bdb, bdf=bdf)
    st = run(qc_ref, vc_ref, zfc_ref, oc_acc, lb_ref[0, 0], st=zero, reverse=False, first=True)
    run(ql_ref, vl_ref, zfl_ref, ol_acc, lb_ref[0, 0], st=st, reverse=False, first=True)
    st = run(qc_ref, vc_ref, zbc_ref, oc_acc, lb_ref[1, 0], st=zero, reverse=True, first=False)
    run(ql_ref, vl_ref, zbl_ref, ol_acc, lb_ref[1, 0], st=st, reverse=True, first=False)
    for acc, g_ref, y_ref in ((ol_acc, gl_ref, yl_ref), (oc_acc, gc_ref, yc_ref)):
        o = acc[...]
        g = g_ref[...]
        y = o * _seg_inv_rms(o, bdf) * gain_ref[...]
        y_ref[...] = (y * (g * jax.nn.sigmoid(g))).astype(y_ref.dtype)


def hgrn_mixer(pc, lb, gain128, bdb, bdf, *, b, s, l):
    w = pc.shape[1] // 5
    nj = w // LANES
    n_lat = b * s
    lat = lambda m: pl.BlockSpec((s, LANES), lambda i, j: (i, m * nj + j))
    cx = lambda m: pl.BlockSpec((l, LANES), lambda i, j: (n_lat // l + i, m * nj + j))
    in_specs = []
    for m in (0, 1, 2, 3, 4):
        in_specs += [lat(m), cx(m)]
    in_specs += [pl.BlockSpec((2, 1, 1, LANES), lambda i, j: (0, j, 0, 0)),
                 pl.BlockSpec((1, LANES), lambda i, j: (0, 0)),
                 pl.BlockSpec((LANES, LANES), lambda i, j: (0, 0)),
                 pl.BlockSpec((LANES, LANES), lambda i, j: (0, 0))]
    return pl.pallas_call(
        _hgrn_kernel,
        grid=(b, nj),
        in_specs=in_specs,
        out_specs=[pl.BlockSpec((s, LANES), lambda i, j: (i, j)),
                   pl.BlockSpec((l, LANES), lambda i, j: (i, j))],
        out_shape=[jax.ShapeDtypeStruct((n_lat, w), BF16), jax.ShapeDtypeStruct((b * l, w), BF16)],
        scratch_shapes=[pltpu.VMEM((s, LANES), F32), pltpu.VMEM((l, LANES), F32)],
        compiler_params=_cparams("parallel", "parallel"),
        name="hgrn_mixer",
    )(*([pc] * 10), lb, gain128, bdb, bdf)


def _outproj_kernel(x_ref, ma_ref, mb_ref, mc_ref, w_ref, gate_ref, g2_ref, shift_ref, scale_ref,
                    wr_ref, br_ref, xo_ref, h2_ref, lg_ref, *, wa, wb):
    w = w_ref[...]
    y = jnp.dot(ma_ref[...], w[:wa], preferred_element_type=F32)
    y = y + jnp.dot(mb_ref[...], w[wa:wa + wb], preferred_element_type=F32)
    y = y + jnp.dot(mc_ref[...], w[wa + wb:], preferred_element_type=F32)
    x = x_ref[...] + gate_ref[0] * y
    xo_ref[...] = x
    ms = jnp.mean(x * x, axis=-1, keepdims=True)
    h = x * lax.rsqrt(ms + EPS) * g2_ref[0]
    h = h * (1.0 + scale_ref[0]) + shift_ref[0]
    h2_ref[...] = h.astype(BF16)
    lg_ref[...] = jnp.dot(h, wr_ref[...], precision=HIGHEST, preferred_element_type=F32) + br_ref[...]


def out_projection(xall, mix_a, mix_b, mix_c, w_out_bf, mod3, layer_g2, w_route, b_route, *, seq, nseg, tm):
    r, d = xall.shape
    wa, wb, wc = mix_a.shape[1], mix_b.shape[1], mix_c.shape[1]
    tiles_per_seq = seq // tm

    def seg(i):
        return jnp.minimum(i // tiles_per_seq, nseg - 1)

    def modspec(m):
        return pl.BlockSpec((1, 1, d), lambda i: (seg(i) * N_MOD + m, 0, 0))

    row = lambda wdt: pl.BlockSpec((tm, wdt), lambda i: (i, 0))
    kern = functools.partial(_outproj_kernel, wa=wa, wb=wb)
    return pl.pallas_call(
        kern,
        grid=(r // tm,),
        in_specs=[row(d), row(wa), row(wb), row(wc),
                  pl.BlockSpec((wa + wb + wc, d), lambda i: (0, 0)),
                  modspec(2),
                  pl.BlockSpec((1, d), lambda i: (0, 0)),
                  modspec(3), modspec(4),
                  pl.BlockSpec((d, LANES), lambda i: (0, 0)),
                  pl.BlockSpec((1, LANES), lambda i: (0, 0))],
        out_specs=[row(d), row(d), row(LANES)],
        out_shape=[jax.ShapeDtypeStruct((r, d), F32),
                   jax.ShapeDtypeStruct((r, d), BF16),
                   jax.ShapeDtypeStruct((r, LANES), F32)],
        compiler_params=_cparams("parallel"),
        name="out_projection",
    )(xall, mix_a, mix_b, mix_c, w_out_bf, mod3, layer_g2, mod3, mod3, w_route, b_route)


def _route_weights(logits):
    lane = lax.broadcasted_iota(jnp.int32, logits.shape, 1).astype(F32)
    is_g = lane < N_GROUPS
    gl = jnp.where(is_g, logits, -jnp.inf)
    gmax = jnp.max(gl, axis=-1, keepdims=True)
    g_idx = jnp.min(jnp.where(gl == gmax, lane, LANES), axis=-1, keepdims=True)
    gsum = jnp.sum(jnp.where(is_g, jnp.exp(gl - gmax), 0.0), axis=-1, keepdims=True)
    g_top = 1.0 / gsum
    lo = N_GROUPS + g_idx * EXPERTS_PER_GROUP
    in_grp = (lane >= lo) & (lane < lo + EXPERTS_PER_GROUP)
    el = jnp.where(in_grp, logits, -jnp.inf)
    m1 = jnp.max(el, axis=-1, keepdims=True)
    i1 = jnp.min(jnp.where(el == m1, lane, LANES), axis=-1, keepdims=True)
    el2 = jnp.where(lane == i1, -jnp.inf, el)
    m2 = jnp.max(el2, axis=-1, keepdims=True)
    i2 = jnp.min(jnp.where(el2 == m2, lane, LANES), axis=-1, keepdims=True)
    e21 = jnp.exp(m2 - m1)
    w1 = 1.0 / (1.0 + e21)
    w2 = e21 * w1
    return jnp.where(lane == i1, w1 * g_top, 0.0) + jnp.where(lane == i2, w2 * g_top, 0.0)


def _moe_kernel(x_ref, h_ref, lg_ref, gate_ref, wg_ref, wu_ref, wd_ref, o_ref, cw_ref, acc_ref):
    e = pl.program_id(1)

    @pl.when(e == 0)
    def _():
        cw_ref[...] = _route_weights(lg_ref[...])
        acc_ref[...] = jnp.zeros_like(acc_ref)

    h = h_ref[...]
    lane = lax.broadcasted_iota(jnp.int32, cw_ref.shape, 1)
    w_e = jnp.sum(jnp.where(lane == e + N_GROUPS, cw_ref[...], 0.0), axis=-1, keepdims=True)
    hg = jnp.dot(h, wg_ref[0].astype(BF16), preferred_element_type=F32)
    hu = jnp.dot(h, wu_ref[0].astype(BF16), preferred_element_type=F32)
    hid = (hg * jax.nn.sigmoid(hg)) * hu * w_e
    acc_ref[...] += jnp.dot(hid.astype(BF16), wd_ref[0].astype(BF16), preferred_element_type=F32)

    @pl.when(e == pl.num_programs(1) - 1)
    def _():
        o_ref[...] = x_ref[...] + gate_ref[0] * acc_ref[...]


def moe_block(xall, h2, logits, mod3, w_gate, w_up, w_down, *, seq, nseg, tm):
    r, d = xall.shape
    ne, _, ff = w_gate.shape
    tiles_per_seq = seq // tm

    def seg(i):
        return jnp.minimum(i // tiles_per_seq, nseg - 1)

    row = lambda wdt: pl.BlockSpec((tm, wdt), lambda i, e: (i, 0))
    return pl.pallas_call(
        _moe_kernel,
        grid=(r // tm, ne),
        in_specs=[row(d), row(d), row(LANES),
                  pl.BlockSpec((1, 1, d), lambda i, e: (seg(i) * N_MOD + 5, 0, 0)),
                  pl.BlockSpec((1, d, ff), lambda i, e: (e, 0, 0)),
                  pl.BlockSpec((1, d, ff), lambda i, e: (e, 0, 0)),
                  pl.BlockSpec((1, ff, d), lambda i, e: (e, 0, 0))],
        out_specs=row(d),
        out_shape=jax.ShapeDtypeStruct((r, d), F32),
        scratch_shapes=[pltpu.VMEM((tm, LANES), F32), pltpu.VMEM((tm, d), F32)],
        compiler_params=_cparams("parallel", "arbitrary"),
        name="moe_block",
    )(xall, h2, logits, mod3, w_gate, w_up, w_down)


def _rope_tables(seq, tm):
    t = np.arange(seq)
    row = (t // GRID_W).astype(np.float32)
    col = (t % GRID_W).astype(np.float32)
    half = HEAD_DIM // 2
    inv = jnp.asarray(ROPE_THETA, F32) ** (-jnp.arange(0, half, 2, dtype=F32) / half)
    ang = jnp.concatenate([jnp.asarray(row)[:, None] * inv, jnp.asarray(col)[:, None] * inv], axis=-1)
    cos = jnp.repeat(jnp.cos(ang), 2, axis=-1)
    sin = jnp.repeat(jnp.sin(ang), 2, axis=-1) * jnp.asarray(np.tile([-1.0, 1.0], half), F32)
    cos = jnp.tile(cos, (1, LANES // HEAD_DIM))
    sin = jnp.tile(sin, (1, LANES // HEAD_DIM))
    cos = jnp.concatenate([cos, jnp.ones((tm, LANES), F32)], axis=0)
    sin = jnp.concatenate([sin, jnp.zeros((tm, LANES), F32)], axis=0)
    return cos, sin


def _heads(t, b, n, nh):
    return jnp.swapaxes(t.reshape(b, n, nh, HEAD_DIM), 1, 2).reshape(b * nh, n, HEAD_DIM)


def _unheads(t, b, n, nh):
    return jnp.swapaxes(t.reshape(b, nh, n, HEAD_DIM), 1, 2).reshape(b * n, nh * HEAD_DIM)


def kernel(x, c, ctx, c_ctx, w_ada, b_ada, norm1_g, w_in, na_q_norm, na_k_norm, na_rpb, gqa_q_norm, gqa_k_norm, hgrn_lb, hgrn_o_norm, w_out, norm2_g, w_route_group, b_route_group, w_route_expert, b_route_expert, w_exp_gate, w_exp_up, w_exp_down):
    b, s, d = x.shape
    l = ctx.shape[1]
    depth = w_ada.shape[0]
    assert s % 512 == 0 and (b * l) % 512 == 0 and s // GRID_W >= NA_WIN_R
    assert s % l == 0 and l % HG_BLOCK == 0
    nseg = b + 1
    n_lat = b * s
    n_ctx = b * l
    na_w, gq_qw = d // 4, d // 2
    gq_kw = gq_qw // 4
    na_h, gq_h, gq_kvh = na_w // HEAD_DIM, gq_qw // HEAD_DIM, gq_kw // HEAD_DIM
    grp = gq_h // gq_kvh
    hg_w = d // 4
    tm = 512

    c_all = jnp.zeros((16, d), F32).at[:b].set(c).at[b].set(c_ctx)
    mod = ada_mod(c_all, w_ada, b_ada)
    cos_t, sin_t = _rope_tables(s, tm)
    bd_f = _block_diag_ones(LANES, HEAD_DIM, F32)
    bd_b = _block_diag_ones(LANES, HEAD_DIM, BF16)
    p_lb = jax.nn.softmax(hgrn_lb.astype(F32), axis=0)
    lb_all = jnp.cumsum(p_lb, axis=0) - p_lb[0]
    tile2 = lambda g: jnp.tile(g, LANES // HEAD_DIM)

    xall = jnp.concatenate([x.reshape(n_lat, d), ctx.reshape(n_ctx, d)], axis=0)
    for layer in range(depth):
        ctx_out = layer < depth - 1
        mod3 = mod[layer].reshape(16 * N_MOD, 1, d)
        gains = jnp.zeros((8, LANES), F32)
        gains = gains.at[0].set(tile2(na_q_norm[layer])).at[1].set(tile2(na_k_norm[layer]))
        gains = gains.at[2].set(tile2(gqa_q_norm[layer])).at[3].set(tile2(gqa_k_norm[layer]))
        pa, pb, pc = in_projection(xall, mod3, norm1_g[layer][None], w_in[layer].astype(BF16), cos_t, sin_t,
                                   gains, bd_f, n_lat_rows=n_lat, seq=s, nseg=nseg, tm=tm)
        qa, ka, va = pa[:, :na_w], pa[:, na_w:2 * na_w], pa[:, 2 * na_w:]
        kac = _heads(ka[n_lat:], b, l, na_h)
        vac = _heads(va[n_lat:], b, l, na_h)
        bias_tab = na_bias_table(na_rpb[layer], s // GRID_W)
        o_a = na_attention(_heads(qa[:n_lat], b, s, na_h), _heads(ka[:n_lat], b, s, na_h),
                           _heads(va[:n_lat], b, s, na_h), kac, vac, bias_tab, na_h)
        o_a = _unheads(o_a, b, s, na_h)
        qb, kb, vb = pb[:, :gq_qw], pb[:, gq_qw:gq_qw + gq_kw], pb[:, gq_qw + gq_kw:]
        kbl, kbc = _heads(kb[:n_lat], b, s, gq_kvh), _heads(kb[n_lat:], b, l, gq_kvh)
        vbl, vbc = _heads(vb[:n_lat], b, s, gq_kvh), _heads(vb[n_lat:], b, l, gq_kvh)
        k_all = jnp.concatenate([kbl, kbc], axis=1)
        v_all = jnp.concatenate([vbl, vbc], axis=1)
        qbl = _heads(qb[:n_lat], b, s, gq_h).reshape(b * gq_kvh, grp * s, HEAD_DIM)
        o_b = attention(qbl, k_all, v_all, 512)
        o_b = _unheads(o_b.reshape(b * gq_h, s, HEAD_DIM), b, s, gq_h)
        lb4 = lb_all[layer].reshape(2, hg_w // LANES, 1, LANES)
        y_lat, y_ctx = hgrn_mixer(pc, lb4, tile2(hgrn_o_norm[layer])[None], bd_b, bd_f, b=b, s=s, l=l)

        w_route = jnp.zeros((d, LANES), F32).at[:, :N_GROUPS].set(w_route_group[layer])
        w_route = w_route.at[:, N_GROUPS:N_GROUPS + N_EXPERTS].set(w_route_expert[layer])
        b_route = jnp.zeros((1, LANES), F32).at[0, :N_GROUPS].set(b_route_group[layer])
        b_route = b_route.at[0, N_GROUPS:N_GROUPS + N_EXPERTS].set(b_route_expert[layer])
        if ctx_out:
            o_ac = _unheads(attention(_heads(qa[n_lat:], b, l, na_h), kac, vac, 512), b, l, na_h)
            qbc = _heads(qb[n_lat:], b, l, gq_h).reshape(b * gq_kvh, grp * l, HEAD_DIM)
            o_bc = attention(qbc, kbc, vbc, 512)
            o_bc = _unheads(o_bc.reshape(b * gq_h, l, HEAD_DIM), b, l, gq_h)
            mix_a = jnp.concatenate([o_a, o_ac], axis=0)
            mix_b = jnp.concatenate([o_b, o_bc], axis=0)
            y_c = jnp.concatenate([y_lat, y_ctx], axis=0)
            x_in = xall
        else:
            mix_a, mix_b, y_c, x_in = o_a, o_b, y_lat, xall[:n_lat]
        x_mid, h2, logits = out_projection(x_in, mix_a, mix_b, y_c, w_out[layer].astype(BF16), mod3,
                                            norm2_g[layer][None], w_route, b_route, seq=s, nseg=nseg, tm=tm)
        xall = moe_block(x_mid, h2, logits, mod3, w_exp_gate[layer], w_exp_up[layer], w_exp_down[layer],
                         seq=s, nseg=nseg, tm=1024 if x_mid.shape[0] % 1024 == 0 and s % 1024 == 0 else tm)
    return xall[:n_lat].reshape(b, s, d)
```

```python
import functools

import jax
import jax.numpy as jnp
import numpy as np
from jax import lax
from jax.experimental import pallas as pl
from jax.experimental.pallas import tpu as pltpu

F32 = jnp.float32
BF16 = jnp.bfloat16
HIGHEST = lax.Precision.HIGHEST

HEAD_DIM = 64
GRID_W = 64
NA_WIN_R = 8
NA_WIN_C = 16
ROPE_THETA = 10000.0
HGRN_CHUNK = 16
N_GROUPS = 4
EXPERTS_PER_GROUP = 8
N_EXPERTS = N_GROUPS * EXPERTS_PER_GROUP
N_MOD = 6
EPS = 1e-6
NEG_INF = -1e30
LB_FLOOR = 1e-20
LANES = 128
VMEM_LIMIT = 56 * 1024 * 1024


def _cparams(*sem):
    return pltpu.CompilerParams(dimension_semantics=sem, vmem_limit_bytes=VMEM_LIMIT)


def _block_diag_ones(n, blk, dtype):
    i = np.arange(n)
    return jnp.asarray((i[:, None] // blk) == (i[None, :] // blk), dtype=dtype)


def _ada_kernel(c_ref, w_ref, b_ref, o_ref):
    c = c_ref[...]
    s = c * jax.nn.sigmoid(c)
    o_ref[0] = jnp.dot(s, w_ref[0], precision=HIGHEST, preferred_element_type=F32) + b_ref[0]


def ada_mod(c_all, w_ada, b_ada):
    depth, d, n = w_ada.shape
    tn = 1536
    return pl.pallas_call(
        _ada_kernel,
        grid=(depth, n // tn),
        in_specs=[
            pl.BlockSpec((16, d), lambda l, j: (0, 0)),
            pl.BlockSpec((1, d, tn), lambda l, j: (l, 0, j)),
            pl.BlockSpec((1, 1, tn), lambda l, j: (l, 0, j)),
        ],
        out_specs=pl.BlockSpec((1, 16, tn), lambda l, j: (l, 0, j)),
        out_shape=jax.ShapeDtypeStruct((depth, 16, n), F32),
        compiler_params=_cparams("parallel", "parallel"),
        name="ada_mod",
    )(c_all, w_ada, b_ada.reshape(depth, 1, n))


def _seg_inv_rms(x, bd):
    ss = jnp.dot(x * x, bd, precision=HIGHEST, preferred_element_type=F32)
    return lax.rsqrt(ss * (1.0 / HEAD_DIM) + EPS)


def _pair_swap(x):
    lane = lax.broadcasted_iota(jnp.int32, x.shape, 1)
    return jnp.where((lane & 1) == 0, pltpu.roll(x, LANES - 1, 1), pltpu.roll(x, 1, 1))


def _inproj_kernel(x_ref, g1_ref, shift_ref, scale_ref, w_ref, cos_ref, sin_ref, gains_ref, bd_ref,
                   oa_ref, ob_ref, oc_ref, *, na_w, gq_qw, gq_kw):
    x = x_ref[...]
    ms = jnp.mean(x * x, axis=-1, keepdims=True)
    h = x * lax.rsqrt(ms + EPS) * g1_ref[0]
    h = h * (1.0 + scale_ref[0]) + shift_ref[0]
    p = jnp.dot(h.astype(BF16), w_ref[...], preferred_element_type=F32)
    bd = bd_ref[...]
    cos = cos_ref[...]
    sin = sin_ref[...]
    qscale = HEAD_DIM ** -0.5

    def normed(col, gain_row):
        xb = p[:, col:col + LANES]
        return xb * _seg_inv_rms(xb, bd) * gains_ref[gain_row:gain_row + 1, :]

    def rope(xn):
        return xn * cos + _pair_swap(xn) * sin

    for j in range(na_w // LANES):
        c = j * LANES
        oa_ref[:, c:c + LANES] = (normed(c, 0) * qscale).astype(BF16)
        oa_ref[:, na_w + c:na_w + c + LANES] = normed(na_w + c, 1).astype(BF16)
    oa_ref[:, 2 * na_w:3 * na_w] = p[:, 2 * na_w:3 * na_w].astype(BF16)
    b0 = 3 * na_w
    for j in range(gq_qw // LANES):
        c = j * LANES
        ob_ref[:, c:c + LANES] = (rope(normed(b0 + c, 2)) * qscale).astype(BF16)
    for j in range(gq_kw // LANES):
        c = gq_qw + j * LANES
        ob_ref[:, c:c + LANES] = rope(normed(b0 + c, 3)).astype(BF16)
    ob_ref[:, gq_qw + gq_kw:] = p[:, b0 + gq_qw + gq_kw:b0 + gq_qw + 2 * gq_kw].astype(BF16)
    oc_ref[...] = p[:, b0 + gq_qw + 2 * gq_kw:]


def in_projection(xall, mod3, layer_g1, w_in_bf, cos_t, sin_t, gains, bd, *, n_lat_rows, seq, nseg, tm):
    r, d = xall.shape
    d_in = w_in_bf.shape[1]
    na_w = d // 4
    gq_qw = d // 2
    gq_kw = gq_qw // 4
    c_w = d_in - 3 * na_w - gq_qw - 2 * gq_kw
    lat_tiles = n_lat_rows // tm
    tiles_per_seq = seq // tm

    def seg(i):
        return jnp.minimum(i // tiles_per_seq, nseg - 1)

    def rope_blk(i):
        return jnp.where(i < lat_tiles, i % tiles_per_seq, tiles_per_seq)

    kern = functools.partial(_inproj_kernel, na_w=na_w, gq_qw=gq_qw, gq_kw=gq_kw)
    return pl.pallas_call(
        kern,
        grid=(r // tm,),
        in_specs=[
            pl.BlockSpec((tm, d), lambda i: (i, 0)),
            pl.BlockSpec((1, d), lambda i: (0, 0)),
            pl.BlockSpec((1, 1, d), lambda i: (seg(i) * N_MOD + 0, 0, 0)),
            pl.BlockSpec((1, 1, d), lambda i: (seg(i) * N_MOD + 1, 0, 0)),
            pl.BlockSpec((d, d_in), lambda i: (0, 0)),
            pl.BlockSpec((tm, LANES), lambda i: (rope_blk(i), 0)),
            pl.BlockSpec((tm, LANES), lambda i: (rope_blk(i), 0)),
            pl.BlockSpec((8, LANES), lambda i: (0, 0)),
            pl.BlockSpec((LANES, LANES), lambda i: (0, 0)),
        ],
        out_specs=[
            pl.BlockSpec((tm, 3 * na_w), lambda i: (i, 0)),
            pl.BlockSpec((tm, gq_qw + 2 * gq_kw), lambda i: (i, 0)),
            pl.BlockSpec((tm, c_w), lambda i: (i, 0)),
        ],
        out_shape=[
            jax.ShapeDtypeStruct((r, 3 * na_w), BF16),
            jax.ShapeDtypeStruct((r, gq_qw + 2 * gq_kw), BF16),
            jax.ShapeDtypeStruct((r, c_w), F32),
        ],
        compiler_params=_cparams("parallel"),
        name="in_projection",
    )(xall, layer_g1, mod3, mod3, w_in_bf, cos_t, sin_t, gains, bd)


def _head(j):
    return slice(j * HEAD_DIM, (j + 1) * HEAD_DIM)


def _softmax_attend(q, k, v):
    s = lax.dot_general(q, k, (((1,), (1,)), ((), ())), preferred_element_type=F32)
    m = jnp.max(s, axis=-1, keepdims=True)
    p = jnp.exp(s - m)
    l = jnp.sum(p, axis=-1, keepdims=True)
    return jnp.dot(p.astype(BF16), v, preferred_element_type=F32) / l


def _grouped_attend(q_ref, k_of, v_of, n_kv, grp):
    t = q_ref.shape[0]
    outs = []
    for j in range(n_kv):
        q4 = jnp.concatenate([q_ref[:, _head(j * grp + g)] for g in range(grp)], axis=0)
        o = _softmax_attend(q4, k_of(j), v_of(j))
        outs += [o[g * t:(g + 1) * t] for g in range(grp)]
    return jnp.concatenate(outs, axis=1)


def _gqa_kernel(q_ref, kl_ref, vl_ref, kc_ref, vc_ref, o_ref, k_s, v_s, *, n_kv, grp):
    s_len = kl_ref.shape[0]

    @pl.when(pl.program_id(1) == 0)
    def _():
        for j in range(n_kv):
            k_s[j, :s_len, :] = kl_ref[:, _head(j)]
            k_s[j, s_len:, :] = kc_ref[:, _head(j)]
            v_s[j, :s_len, :] = vl_ref[:, _head(j)]
            v_s[j, s_len:, :] = vc_ref[:, _head(j)]

    o = _grouped_attend(q_ref, lambda j: k_s[j], lambda j: v_s[j], n_kv, grp)
    o_ref[...] = o.astype(o_ref.dtype)


def gqa_attention(pb, *, b, s, l, qw, kw, tq):
    n_lat = b * s
    n_kv = kw // HEAD_DIM
    grp = qw // kw
    assert kw == LANES and qw % kw == 0
    kcol, vcol = qw // kw, qw // kw + 1
    kern = functools.partial(_gqa_kernel, n_kv=n_kv, grp=grp)
    return pl.pallas_call(
        kern,
        grid=(b, s // tq),
        in_specs=[
            pl.BlockSpec((tq, qw), lambda i, j: (i * (s // tq) + j, 0)),
            pl.BlockSpec((s, kw), lambda i, j: (i, kcol)),
            pl.BlockSpec((s, kw), lambda i, j: (i, vcol)),
            pl.BlockSpec((l, kw), lambda i, j: (n_lat // l + i, kcol)),
            pl.BlockSpec((l, kw), lambda i, j: (n_lat // l + i, vcol)),
        ],
        out_specs=pl.BlockSpec((tq, qw), lambda i, j: (i * (s // tq) + j, 0)),
        out_shape=jax.ShapeDtypeStruct((n_lat, qw), BF16),
        scratch_shapes=[pltpu.VMEM((n_kv, s + l, HEAD_DIM), BF16), pltpu.VMEM((n_kv, s + l, HEAD_DIM), BF16)],
        compiler_params=_cparams("parallel", "arbitrary"),
        name="gqa_attention",
    )(pb, pb, pb, pb, pb)


def _ctx_attn_kernel(qa_ref, ka_ref, va_ref, qb_ref, kb_ref, vb_ref, oa_ref, ob_ref, *, n_kv, grp):
    na_h = qa_ref.shape[1] // HEAD_DIM
    oa = [_softmax_attend(qa_ref[:, _head(h)], ka_ref[:, _head(h)], va_ref[:, _head(h)]) for h in range(na_h)]
    oa_ref[...] = jnp.concatenate(oa, axis=1).astype(oa_ref.dtype)
    ob = _grouped_attend(qb_ref, lambda j: kb_ref[:, _head(j)], lambda j: vb_ref[:, _head(j)], n_kv, grp)
    ob_ref[...] = ob.astype(ob_ref.dtype)


def ctx_attention(pa, pb, *, b, s, l, na_w, qw, kw):
    r0 = (b * s) // l
    grp = qw // kw
    kern = functools.partial(_ctx_attn_kernel, n_kv=kw // HEAD_DIM, grp=grp)
    a_spec = lambda m: pl.BlockSpec((l, na_w), lambda i: (r0 + i, m))
    return pl.pallas_call(
        kern,
        grid=(b,),
        in_specs=[a_spec(0), a_spec(1), a_spec(2),
                  pl.BlockSpec((l, qw), lambda i: (r0 + i, 0)),
                  pl.BlockSpec((l, kw), lambda i: (r0 + i, grp)),
                  pl.BlockSpec((l, kw), lambda i: (r0 + i, grp + 1))],
        out_specs=[pl.BlockSpec((l, na_w), lambda i: (i, 0)), pl.BlockSpec((l, qw), lambda i: (i, 0))],
        out_shape=[jax.ShapeDtypeStruct((b * l, na_w), BF16), jax.ShapeDtypeStruct((b * l, qw), BF16)],
        compiler_params=_cparams("parallel"),
        name="ctx_attention",
    )(pa, pa, pa, pb, pb, pb)


def _na_kernel(q_ref, k_ref, v_ref, kc_ref, vc_ref, bias_ref, o_ref, k_s, v_s, kc_s, vc_s, *, rows, wr):
    nh = q_ref.shape[1] // HEAD_DIM
    nkey = wr * GRID_W
    for h in range(nh):
        k_s[h] = k_ref[:, _head(h)]
        v_s[h] = v_ref[:, _head(h)]
        kc_s[h] = kc_ref[:, _head(h)]
        vc_s[h] = vc_ref[:, _head(h)]

    def body(r, carry):
        rs = jnp.clip(r - wr // 2, 0, rows - wr)
        q_rows = pl.ds(pl.multiple_of(r * GRID_W, GRID_W), GRID_W)
        k_rows = pl.ds(pl.multiple_of(rs * GRID_W, GRID_W), nkey)
        outs = []
        for h in range(nh):
            q = q_ref[q_rows, _head(h)]
            s_nb = lax.dot_general(q, k_s[h, k_rows, :], (((1,), (1,)), ((), ())), preferred_element_type=F32)
            s_nb = s_nb + bias_ref[h, r - rs]
            s_cx = lax.dot_general(q, kc_s[h], (((1,), (1,)), ((), ())), preferred_element_type=F32)
            m = jnp.maximum(jnp.max(s_nb, axis=-1, keepdims=True), jnp.max(s_cx, axis=-1, keepdims=True))
            p_nb = jnp.exp(s_nb - m)
            p_cx = jnp.exp(s_cx - m)
            l = jnp.sum(p_nb, axis=-1, keepdims=True) + jnp.sum(p_cx, axis=-1, keepdims=True)
            o = (jnp.dot(p_nb.astype(BF16), v_s[h, k_rows, :], preferred_element_type=F32)
                 + jnp.dot(p_cx.astype(BF16), vc_s[h], preferred_element_type=F32))
            outs.append(o / l)
        o_ref[q_rows, :] = jnp.concatenate(outs, axis=1).astype(o_ref.dtype)
        return carry

    lax.fori_loop(0, rows, body, 0)


def na_attention(pa, bias_tab, *, b, s, l, na_w):
    n_lat = b * s
    nh = na_w // HEAD_DIM
    rows = s // GRID_W
    wr = min(NA_WIN_R, rows)
    kern = functools.partial(_na_kernel, rows=rows, wr=wr)
    lat = lambda m: pl.BlockSpec((s, na_w), lambda i: (i, m))
    cx = lambda m: pl.BlockSpec((l, na_w), lambda i: (n_lat // l + i, m))
    return pl.pallas_call(
        kern,
        grid=(b,),
        in_specs=[lat(0), lat(1), lat(2), cx(1), cx(2),
                  pl.BlockSpec(bias_tab.shape, lambda i: (0, 0, 0, 0))],
        out_specs=pl.BlockSpec((s, na_w), lambda i: (i, 0)),
        out_shape=jax.ShapeDtypeStruct((n_lat, na_w), BF16),
        scratch_shapes=[pltpu.VMEM((nh, s, HEAD_DIM), BF16), pltpu.VMEM((nh, s, HEAD_DIM), BF16),
                        pltpu.VMEM((nh, l, HEAD_DIM), BF16), pltpu.VMEM((nh, l, HEAD_DIM), BF16)],
        compiler_params=_cparams("parallel"),
        name="na_attention",
    )(pa, pa, pa, pa, pa, bias_tab)


def na_bias_table(rpb, rows):
    wr = min(NA_WIN_R, rows)
    h = rpb.shape[0]
    t = np.arange(wr)
    kr = np.arange(wr)
    dr = kr[None, :] - t[:, None] + (NA_WIN_R - 1)
    cidx = np.arange(GRID_W)
    col_start = np.clip(cidx - NA_WIN_C // 2, 0, GRID_W - NA_WIN_C)
    col_ok = (cidx[None, :] >= col_start[:, None]) & (cidx[None, :] < col_start[:, None] + NA_WIN_C)
    dc = np.clip(cidx[None, :] - cidx[:, None] + (NA_WIN_C - 1), 0, 2 * NA_WIN_C - 2)
    sel_r = jnp.asarray(dr[:, :, None] == np.arange(2 * NA_WIN_R - 1), F32)
    sel_c = jnp.asarray(dc[:, :, None] == np.arange(2 * NA_WIN_C - 1), F32)
    bias = jnp.einsum("tki,hij->htkj", sel_r, rpb.astype(F32), precision=HIGHEST)
    bias = jnp.einsum("htkj,qcj->htqkc", bias, sel_c, precision=HIGHEST)
    bias = jnp.where(jnp.asarray(col_ok)[None, None, :, None, :], bias, NEG_INF)
    return bias.reshape(h, wr, GRID_W, wr * GRID_W)


HG_BLOCK = 128


def _hgrn_pass(q_ref, v_ref, z_ref, o_acc, lb, bdb, bdf, st, *, reverse, first):
    c = HGRN_CHUNK
    ncb = HG_BLOCK // c
    nblk = q_ref.shape[0] // HG_BLOCK
    lbm = jnp.maximum(lb, LB_FLOOR)
    one_m_lb = 1.0 - lb
    scale = HEAD_DIM ** -0.5
    t_idx = lax.broadcasted_iota(jnp.int32, (ncb, c, LANES), 1)
    edge = 0 if reverse else c - 1

    def bs(x, s):
        return jnp.broadcast_to(x[:, s:s + 1, :], x.shape)

    def seen(s):
        return (t_idx <= s) if reverse else (t_idx >= s)

    def body(i, st):
        blk = (nblk - 1 - i) if reverse else i
        r0 = pl.multiple_of(blk * HG_BLOCK, HG_BLOCK)
        z = z_ref[pl.ds(r0, HG_BLOCK), :]
        q = q_ref[pl.ds(r0, HG_BLOCK), :] * scale
        v = v_ref[pl.ds(r0, HG_BLOCK), :]
        f = one_m_lb * jax.nn.sigmoid(z) + lbm
        k = one_m_lb * jax.nn.sigmoid(-z) - (lbm - lb)
        logf = jnp.log(f).reshape(ncb, c, LANES)
        q3 = q.reshape(ncb, c, LANES)
        k3 = k.reshape(ncb, c, LANES)
        v3 = v.reshape(ncb, c, LANES)
        cum = jnp.zeros_like(logf)
        for s in range(c):
            cum = cum + jnp.where(seen(s), bs(logf, s), 0.0)
        o3 = jnp.zeros_like(logf)
        for s in range(c):
            d = jnp.where(seen(s), cum - bs(cum, s), NEG_INF)
            w = q3 * bs(k3, s) * jnp.exp(d)
            a = jnp.dot(w.reshape(HG_BLOCK, LANES).astype(BF16), bdb, preferred_element_type=F32)
            o3 = o3 + a.reshape(ncb, c, LANES) * bs(v3, s)
        cum_edge = bs(cum, edge)
        qe = (q3 * jnp.exp(cum)).astype(BF16)
        kd = (k3 * jnp.exp(cum_edge - cum)).astype(BF16)
        vb = v3.astype(BF16)
        a_all = jnp.exp(cum_edge)
        outs = [None] * ncb
        for n in (range(ncb - 1, -1, -1) if reverse else range(ncb)):
            o_inter = lax.dot_general(qe[n], st.astype(BF16), (((1,), (1,)), ((), ())),
                                      preferred_element_type=F32)
            outs[n] = o3[n] + o_inter
            u_t = lax.dot_general(vb[n], kd[n], (((0,), (0,)), ((), ())), preferred_element_type=F32)
            st = a_all[n, 0:1, :] * st + u_t * bdf
        val = jnp.concatenate(outs, axis=0)
        if first:
            o_acc[pl.ds(r0, HG_BLOCK), :] = val
        else:
            o_acc[pl.ds(r0, HG_BLOCK), :] += val
        return st

    return lax.fori_loop(0, nblk, body, st)


def _hgrn_kernel(ql_ref, qc_ref, vl_ref, vc_ref, zfl_ref, zfc_ref, zbl_ref, zbc_ref, gl_ref, gc_ref,
                 lb_ref, gain_ref, bdb_ref, bdf_ref, yl_ref, yc_ref, ol_acc, oc_acc):
    bdb = bdb_ref[...]
    bdf = bdf_ref[...]
    zero = jnp.zeros((LANES, LANES), F32)
    run = functools.partial(_hgrn_pass, bdb=bdb, bdf=bdf)
    st = run(qc_ref, vc_ref, zfc_ref, oc_acc, lb_ref[0, 0], st=zero, reverse=False, first=True)
    run(ql_ref, vl_ref, zfl_ref, ol_acc, lb_ref[0, 0], st=st, reverse=False, first=True)
    st = run(qc_ref, vc_ref, zbc_ref, oc_acc, lb_ref[1, 0], st=zero, reverse=True, first=False)
    run(ql_ref, vl_ref, zbl_ref, ol_acc, lb_ref[1, 0], st=st, reverse=True, first=False)
    for acc, g_ref, y_ref in ((ol_acc, gl_ref, yl_ref), (oc_acc, gc_ref, yc_ref)):
        o = acc[...]
        g = g_ref[...]
        y = o * _seg_inv_rms(o, bdf) * gain_ref[...]
        y_ref[...] = (y * (g * jax.nn.sigmoid(g))).astype(y_ref.dtype)


def hgrn_mixer(pc, lb, gain128, bdb, bdf, *, b, s, l):
    w = pc.shape[1] // 5
    nj = w // LANES
    n_lat = b * s
    lat = lambda m: pl.BlockSpec((s, LANES), lambda i, j: (i, m * nj + j))
    cx = lambda m: pl.BlockSpec((l, LANES), lambda i, j: (n_lat // l + i, m * nj + j))
    in_specs = []
    for m in (0, 1, 2, 3, 4):
        in_specs += [lat(m), cx(m)]
    in_specs += [pl.BlockSpec((2, 1, 1, LANES), lambda i, j: (0, j, 0, 0)),
                 pl.BlockSpec((1, LANES), lambda i, j: (0, 0)),
                 pl.BlockSpec((LANES, LANES), lambda i, j: (0, 0)),
                 pl.BlockSpec((LANES, LANES), lambda i, j: (0, 0))]
    return pl.pallas_call(
        _hgrn_kernel,
        grid=(b, nj),
        in_specs=in_specs,
        out_specs=[pl.BlockSpec((s, LANES), lambda i, j: (i, j)),
                   pl.BlockSpec((l, LANES), lambda i, j: (i, j))],
        out_shape=[jax.ShapeDtypeStruct((n_lat, w), BF16), jax.ShapeDtypeStruct((b * l, w), BF16)],
        scratch_shapes=[pltpu.VMEM((s, LANES), F32), pltpu.VMEM((l, LANES), F32)],
        compiler_params=_cparams("parallel", "parallel"),
        name="hgrn_mixer",
    )(*([pc] * 10), lb, gain128, bdb, bdf)


def _outproj_kernel(x_ref, ma_ref, mb_ref, mc_ref, w_ref, gate_ref, g2_ref, shift_ref, scale_ref,
                    wr_ref, br_ref, xo_ref, h2_ref, lg_ref, *, wa, wb):
    w = w_ref[...]
    y = jnp.dot(ma_ref[...], w[:wa], preferred_element_type=F32)
    y = y + jnp.dot(mb_ref[...], w[wa:wa + wb], preferred_element_type=F32)
    y = y + jnp.dot(mc_ref[...], w[wa + wb:], preferred_element_type=F32)
    x = x_ref[...] + gate_ref[0] * y
    xo_ref[...] = x
    ms = jnp.mean(x * x, axis=-1, keepdims=True)
    h = x * lax.rsqrt(ms + EPS) * g2_ref[0]
    h = h * (1.0 + scale_ref[0]) + shift_ref[0]
    h2_ref[...] = h.astype(BF16)
    lg_ref[...] = jnp.dot(h, wr_ref[...], precision=HIGHEST, preferred_element_type=F32) + br_ref[...]


def out_projection(xall, mix_a, mix_b, mix_c, w_out_bf, mod3, layer_g2, w_route, b_route, *, seq, nseg, tm):
    r, d = xall.shape
    wa, wb, wc = mix_a.shape[1], mix_b.shape[1], mix_c.shape[1]
    tiles_per_seq = seq // tm

    def seg(i):
        return jnp.minimum(i // tiles_per_seq, nseg - 1)

    def modspec(m):
        return pl.BlockSpec((1, 1, d), lambda i: (seg(i) * N_MOD + m, 0, 0))

    row = lambda wdt: pl.BlockSpec((tm, wdt), lambda i: (i, 0))
    kern = functools.partial(_outproj_kernel, wa=wa, wb=wb)
    return pl.pallas_call(
        kern,
        grid=(r // tm,),
        in_specs=[row(d), row(wa), row(wb), row(wc),
                  pl.BlockSpec((wa + wb + wc, d), lambda i: (0, 0)),
                  modspec(2),
                  pl.BlockSpec((1, d), lambda i: (0, 0)),
                  modspec(3), modspec(4),
                  pl.BlockSpec((d, LANES), lambda i: (0, 0)),
                  pl.BlockSpec((1, LANES), lambda i: (0, 0))],
        out_specs=[row(d), row(d), row(LANES)],
        out_shape=[jax.ShapeDtypeStruct((r, d), F32),
                   jax.ShapeDtypeStruct((r, d), BF16),
                   jax.ShapeDtypeStruct((r, LANES), F32)],
        compiler_params=_cparams("parallel"),
        name="out_projection",
    )(xall, mix_a, mix_b, mix_c, w_out_bf, mod3, layer_g2, mod3, mod3, w_route, b_route)


def _route_weights(logits):
    lane = lax.broadcasted_iota(jnp.int32, logits.shape, 1).astype(F32)
    is_g = lane < N_GROUPS
    gl = jnp.where(is_g, logits, -jnp.inf)
    gmax = jnp.max(gl, axis=-1, keepdims=True)
    g_idx = jnp.min(jnp.where(gl == gmax, lane, LANES), axis=-1, keepdims=True)
    gsum = jnp.sum(jnp.where(is_g, jnp.exp(gl - gmax), 0.0), axis=-1, keepdims=True)
    g_top = 1.0 / gsum
    lo = N_GROUPS + g_idx * EXPERTS_PER_GROUP
    in_grp = (lane >= lo) & (lane < lo + EXPERTS_PER_GROUP)
    el = jnp.where(in_grp, logits, -jnp.inf)
    m1 = jnp.max(el, axis=-1, keepdims=True)
    i1 = jnp.min(jnp.where(el == m1, lane, LANES), axis=-1, keepdims=True)
    el2 = jnp.where(lane == i1, -jnp.inf, el)
    m2 = jnp.max(el2, axis=-1, keepdims=True)
    i2 = jnp.min(jnp.where(el2 == m2, lane, LANES), axis=-1, keepdims=True)
    e21 = jnp.exp(m2 - m1)
    w1 = 1.0 / (1.0 + e21)
    w2 = e21 * w1
    return jnp.where(lane == i1, w1 * g_top, 0.0) + jnp.where(lane == i2, w2 * g_top, 0.0)


def _moe_kernel(x_ref, h_ref, lg_ref, gate_ref, wg_ref, wu_ref, wd_ref, o_ref, cw_ref, acc_ref):
    e = pl.program_id(1)

    @pl.when(e == 0)
    def _():
        cw_ref[...] = _route_weights(lg_ref[...])
        acc_ref[...] = jnp.zeros_like(acc_ref)

    h = h_ref[...]
    lane = lax.broadcasted_iota(jnp.int32, cw_ref.shape, 1)
    w_e = jnp.sum(jnp.where(lane == e + N_GROUPS, cw_ref[...], 0.0), axis=-1, keepdims=True)
    hg = jnp.dot(h, wg_ref[0].astype(BF16), preferred_element_type=F32)
    hu = jnp.dot(h, wu_ref[0].astype(BF16), preferred_element_type=F32)
    hid = (hg * jax.nn.sigmoid(hg)) * hu * w_e
    acc_ref[...] += jnp.dot(hid.astype(BF16), wd_ref[0].astype(BF16), preferred_element_type=F32)

    @pl.when(e == pl.num_programs(1) - 1)
    def _():
        o_ref[...] = x_ref[...] + gate_ref[0] * acc_ref[...]


def moe_block(xall, h2, logits, mod3, w_gate, w_up, w_down, *, seq, nseg, tm):
    r, d = xall.shape
    ne, _, ff = w_gate.shape
    tiles_per_seq = seq // tm

    def seg(i):
        return jnp.minimum(i // tiles_per_seq, nseg - 1)

    row = lambda wdt: pl.BlockSpec((tm, wdt), lambda i, e: (i, 0))
    return pl.pallas_call(
        _moe_kernel,
        grid=(r // tm, ne),
        in_specs=[row(d), row(d), row(LANES),
                  pl.BlockSpec((1, 1, d), lambda i, e: (seg(i) * N_MOD + 5, 0, 0)),
                  pl.BlockSpec((1, d, ff), lambda i, e: (e, 0, 0)),
                  pl.BlockSpec((1, d, ff), lambda i, e: (e, 0, 0)),
                  pl.BlockSpec((1, ff, d), lambda i, e: (e, 0, 0))],
        out_specs=row(d),
        out_shape=jax.ShapeDtypeStruct((r, d), F32),
        scratch_shapes=[pltpu.VMEM((tm, LANES), F32), pltpu.VMEM((tm, d), F32)],
        compiler_params=_cparams("parallel", "arbitrary"),
        name="moe_block",
    )(xall, h2, logits, mod3, w_gate, w_up, w_down)


def _rope_tables(seq, tm):
    t = np.arange(seq)
    row = (t // GRID_W).astype(np.float32)
    col = (t % GRID_W).astype(np.float32)
    half = HEAD_DIM // 2
    inv = jnp.asarray(ROPE_THETA, F32) ** (-jnp.arange(0, half, 2, dtype=F32) / half)
    ang = jnp.concatenate([jnp.asarray(row)[:, None] * inv, jnp.asarray(col)[:, None] * inv], axis=-1)
    cos = jnp.repeat(jnp.cos(ang), 2, axis=-1)
    sin = jnp.repeat(jnp.sin(ang), 2, axis=-1) * jnp.asarray(np.tile([-1.0, 1.0], half), F32)
    cos = jnp.tile(cos, (1, LANES // HEAD_DIM))
    sin = jnp.tile(sin, (1, LANES // HEAD_DIM))
    cos = jnp.concatenate([cos, jnp.ones((tm, LANES), F32)], axis=0)
    sin = jnp.concatenate([sin, jnp.zeros((tm, LANES), F32)], axis=0)
    return cos, sin


def kernel(x, c, ctx, c_ctx, w_ada, b_ada, norm1_g, w_in, na_q_norm, na_k_norm, na_rpb, gqa_q_norm, gqa_k_norm, hgrn_lb, hgrn_o_norm, w_out, norm2_g, w_route_group, b_route_group, w_route_expert, b_route_expert, w_exp_gate, w_exp_up, w_exp_down):
    b, s, d = x.shape
    l = ctx.shape[1]
    depth = w_ada.shape[0]
    assert s % 512 == 0 and (b * l) % 512 == 0 and s // GRID_W >= NA_WIN_R
    assert s % l == 0 and l % HG_BLOCK == 0
    nseg = b + 1
    n_lat = b * s
    n_ctx = b * l
    na_w, gq_qw = d // 4, d // 2
    gq_kw = gq_qw // 4
    hg_w = d // 4
    tm = 512

    c_all = jnp.zeros((16, d), F32).at[:b].set(c).at[b].set(c_ctx)
    mod = ada_mod(c_all, w_ada, b_ada)
    cos_t, sin_t = _rope_tables(s, tm)
    bd_f = _block_diag_ones(LANES, HEAD_DIM, F32)
    bd_b = _block_diag_ones(LANES, HEAD_DIM, BF16)
    p_lb = jax.nn.softmax(hgrn_lb.astype(F32), axis=0)
    lb_all = jnp.cumsum(p_lb, axis=0) - p_lb[0]
    tile2 = lambda g: jnp.tile(g, LANES // HEAD_DIM)

    xall = jnp.concatenate([x.reshape(n_lat, d), ctx.reshape(n_ctx, d)], axis=0)
    for layer in range(depth):
        ctx_out = layer < depth - 1
        mod3 = mod[layer].reshape(16 * N_MOD, 1, d)
        gains = jnp.zeros((8, LANES), F32)
        gains = gains.at[0].set(tile2(na_q_norm[layer])).at[1].set(tile2(na_k_norm[layer]))
        gains = gains.at[2].set(tile2(gqa_q_norm[layer])).at[3].set(tile2(gqa_k_norm[layer]))
        pa, pb, pc = in_projection(xall, mod3, norm1_g[layer][None], w_in[layer].astype(BF16), cos_t, sin_t,
                                   gains, bd_f, n_lat_rows=n_lat, seq=s, nseg=nseg, tm=tm)
        bias_tab = na_bias_table(na_rpb[layer], s // GRID_W)
        o_a = na_attention(pa, bias_tab, b=b, s=s, l=l, na_w=na_w)
        o_b = gqa_attention(pb, b=b, s=s, l=l, qw=gq_qw, kw=gq_kw, tq=128)
        lb4 = lb_all[layer].reshape(2, hg_w // LANES, 1, LANES)
        y_lat, y_ctx = hgrn_mixer(pc, lb4, tile2(hgrn_o_norm[layer])[None], bd_b, bd_f, b=b, s=s, l=l)

        w_route = jnp.zeros((d, LANES), F32).at[:, :N_GROUPS].set(w_route_group[layer])
        w_route = w_route.at[:, N_GROUPS:N_GROUPS + N_EXPERTS].set(w_route_expert[layer])
        b_route = jnp.zeros((1, LANES), F32).at[0, :N_GROUPS].set(b_route_group[layer])
        b_route = b_route.at[0, N_GROUPS:N_GROUPS + N_EXPERTS].set(b_route_expert[layer])
        if ctx_out:
            o_ac, o_bc = ctx_attention(pa, pb, b=b, s=s, l=l, na_w=na_w, qw=gq_qw, kw=gq_kw)
            mix_a = jnp.concatenate([o_a, o_ac], axis=0)
            mix_b = jnp.concatenate([o_b, o_bc], axis=0)
            y_c = jnp.concatenate([y_lat, y_ctx], axis=0)
            x_in = xall
        else:
            mix_a, mix_b, y_c, x_in = o_a, o_b, y_lat, xall[:n_lat]
        x_mid, h2, logits = out_projection(x_in, mix_a, mix_b, y_c, w_out[layer].astype(BF16), mod3,
                                            norm2_g[layer][None], w_route, b_route, seq=s, nseg=nseg, tm=tm)
        xall = moe_block(x_mid, h2, logits, mod3, w_exp_gate[layer], w_exp_up[layer], w_exp_down[layer],
                         seq=s, nseg=nseg, tm=1024 if x_mid.shape[0] % 1024 == 0 and s % 1024 == 0 else tm)
    return xall[:n_lat].reshape(b, s, d)
```

```python
import functools

import jax
import jax.numpy as jnp
import numpy as np
from jax import lax
from jax.experimental import pallas as pl
from jax.experimental.pallas import tpu as pltpu

F32 = jnp.float32
BF16 = jnp.bfloat16
HIGHEST = lax.Precision.HIGHEST

HEAD_DIM = 64
GRID_W = 64
NA_WIN_R = 8
NA_WIN_C = 16
ROPE_THETA = 10000.0
HGRN_CHUNK = 16
N_GROUPS = 4
EXPERTS_PER_GROUP = 8
N_EXPERTS = N_GROUPS * EXPERTS_PER_GROUP
N_MOD = 6
EPS = 1e-6
NEG_INF = -1e30
LB_FLOOR = 1e-20
LANES = 128
VMEM_LIMIT = 56 * 1024 * 1024


def _cparams(*sem):
    return pltpu.CompilerParams(dimension_semantics=sem, vmem_limit_bytes=VMEM_LIMIT)


def _block_diag_ones(n, blk, dtype):
    i = np.arange(n)
    return jnp.asarray((i[:, None] // blk) == (i[None, :] // blk), dtype=dtype)


def _ada_kernel(c_ref, w_ref, b_ref, o_ref):
    c = c_ref[...]
    s = c * jax.nn.sigmoid(c)
    o_ref[0] = jnp.dot(s, w_ref[0], precision=HIGHEST, preferred_element_type=F32) + b_ref[0]


def ada_mod(c_all, w_ada, b_ada):
    depth, d, n = w_ada.shape
    tn = 1536
    return pl.pallas_call(
        _ada_kernel,
        grid=(depth, n // tn),
        in_specs=[
            pl.BlockSpec((16, d), lambda l, j: (0, 0)),
            pl.BlockSpec((1, d, tn), lambda l, j: (l, 0, j)),
            pl.BlockSpec((1, 1, tn), lambda l, j: (l, 0, j)),
        ],
        out_specs=pl.BlockSpec((1, 16, tn), lambda l, j: (l, 0, j)),
        out_shape=jax.ShapeDtypeStruct((depth, 16, n), F32),
        compiler_params=_cparams("parallel", "parallel"),
        name="ada_mod",
    )(c_all, w_ada, b_ada.reshape(depth, 1, n))


def _seg_inv_rms(x, bd):
    ss = jnp.dot(x * x, bd, precision=HIGHEST, preferred_element_type=F32)
    return lax.rsqrt(ss * (1.0 / HEAD_DIM) + EPS)


def _pair_swap(x):
    lane = lax.broadcasted_iota(jnp.int32, x.shape, 1)
    return jnp.where((lane & 1) == 0, pltpu.roll(x, LANES - 1, 1), pltpu.roll(x, 1, 1))


def _inproj_kernel(x_ref, g1_ref, shift_ref, scale_ref, w_ref, cos_ref, sin_ref, gains_ref, bd_ref,
                   oa_ref, ob_ref, oc_ref, *, na_w, gq_qw, gq_kw):
    x = x_ref[...]
    ms = jnp.mean(x * x, axis=-1, keepdims=True)
    h = x * lax.rsqrt(ms + EPS) * g1_ref[0]
    h = h * (1.0 + scale_ref[0]) + shift_ref[0]
    p = jnp.dot(h.astype(BF16), w_ref[...], preferred_element_type=F32)
    bd = bd_ref[...]
    cos = cos_ref[...]
    sin = sin_ref[...]
    qscale = HEAD_DIM ** -0.5

    def normed(col, gain_row):
        xb = p[:, col:col + LANES]
        return xb * _seg_inv_rms(xb, bd) * gains_ref[gain_row:gain_row + 1, :]

    def rope(xn):
        return xn * cos + _pair_swap(xn) * sin

    for j in range(na_w // LANES):
        c = j * LANES
        oa_ref[:, c:c + LANES] = (normed(c, 0) * qscale).astype(BF16)
        oa_ref[:, na_w + c:na_w + c + LANES] = normed(na_w + c, 1).astype(BF16)
    oa_ref[:, 2 * na_w:3 * na_w] = p[:, 2 * na_w:3 * na_w].astype(BF16)
    b0 = 3 * na_w
    for j in range(gq_qw // LANES):
        c = j * LANES
        ob_ref[:, c:c + LANES] = (rope(normed(b0 + c, 2)) * qscale).astype(BF16)
    for j in range(gq_kw // LANES):
        c = gq_qw + j * LANES
        ob_ref[:, c:c + LANES] = rope(normed(b0 + c, 3)).astype(BF16)
    ob_ref[:, gq_qw + gq_kw:] = p[:, b0 + gq_qw + gq_kw:b0 + gq_qw + 2 * gq_kw].astype(BF16)
    oc_ref[...] = p[:, b0 + gq_qw + 2 * gq_kw:]


def in_projection(xall, mod3, layer_g1, w_in_bf, cos_t, sin_t, gains, bd, *, n_lat_rows, seq, nseg, tm):
    r, d = xall.shape
    d_in = w_in_bf.shape[1]
    na_w = d // 4
    gq_qw = d // 2
    gq_kw = gq_qw // 4
    c_w = d_in - 3 * na_w - gq_qw - 2 * gq_kw
    lat_tiles = n_lat_rows // tm
    tiles_per_seq = seq // tm

    def seg(i):
        return jnp.minimum(i // tiles_per_seq, nseg - 1)

    def rope_blk(i):
        return jnp.where(i < lat_tiles, i % tiles_per_seq, tiles_per_seq)

    kern = functools.partial(_inproj_kernel, na_w=na_w, gq_qw=gq_qw, gq_kw=gq_kw)
    return pl.pallas_call(
        kern,
        grid=(r // tm,),
        in_specs=[
            pl.BlockSpec((tm, d), lambda i: (i, 0)),
            pl.BlockSpec((1, d), lambda i: (0, 0)),
            pl.BlockSpec((1, 1, d), lambda i: (seg(i) * N_MOD + 0, 0, 0)),
            pl.BlockSpec((1, 1, d), lambda i: (seg(i) * N_MOD + 1, 0, 0)),
            pl.BlockSpec((d, d_in), lambda i: (0, 0)),
            pl.BlockSpec((tm, LANES), lambda i: (rope_blk(i), 0)),
            pl.BlockSpec((tm, LANES), lambda i: (rope_blk(i), 0)),
            pl.BlockSpec((8, LANES), lambda i: (0, 0)),
            pl.BlockSpec((LANES, LANES), lambda i: (0, 0)),
        ],
        out_specs=[
            pl.BlockSpec((tm, 3 * na_w), lambda i: (i, 0)),
            pl.BlockSpec((tm, gq_qw + 2 * gq_kw), lambda i: (i, 0)),
            pl.BlockSpec((tm, c_w), lambda i: (i, 0)),
        ],
        out_shape=[
            jax.ShapeDtypeStruct((r, 3 * na_w), BF16),
            jax.ShapeDtypeStruct((r, gq_qw + 2 * gq_kw), BF16),
            jax.ShapeDtypeStruct((r, c_w), F32),
        ],
        compiler_params=_cparams("parallel"),
        name="in_projection",
    )(xall, layer_g1, mod3, mod3, w_in_bf, cos_t, sin_t, gains, bd)


def _head(j):
    return slice(j * HEAD_DIM, (j + 1) * HEAD_DIM)


def _softmax_attend(q, k, v):
    s = lax.dot_general(q, k, (((1,), (1,)), ((), ())), preferred_element_type=F32)
    m = jnp.max(s, axis=-1, keepdims=True)
    p = jnp.exp(s - m)
    l = jnp.sum(p, axis=-1, keepdims=True)
    return jnp.dot(p.astype(BF16), v, preferred_element_type=F32) / l


def _grouped_attend(q_ref, k_of, v_of, n_kv, grp):
    t = q_ref.shape[0]
    outs = []
    for j in range(n_kv):
        q4 = jnp.concatenate([q_ref[:, _head(j * grp + g)] for g in range(grp)], axis=0)
        o = _softmax_attend(q4, k_of(j), v_of(j))
        outs += [o[g * t:(g + 1) * t] for g in range(grp)]
    return jnp.concatenate(outs, axis=1)


def _gqa_kernel(q_ref, kl_ref, vl_ref, kc_ref, vc_ref, o_ref, k_s, v_s, *, n_kv, grp):
    s_len = kl_ref.shape[0]

    @pl.when(pl.program_id(1) == 0)
    def _():
        for j in range(n_kv):
            k_s[j, :s_len, :] = kl_ref[:, _head(j)]
            k_s[j, s_len:, :] = kc_ref[:, _head(j)]
            v_s[j, :s_len, :] = vl_ref[:, _head(j)]
            v_s[j, s_len:, :] = vc_ref[:, _head(j)]

    o = _grouped_attend(q_ref, lambda j: k_s[j], lambda j: v_s[j], n_kv, grp)
    o_ref[...] = o.astype(o_ref.dtype)


def gqa_attention(pb, *, b, s, l, qw, kw, tq):
    n_lat = b * s
    n_kv = kw // HEAD_DIM
    grp = qw // kw
    assert kw == LANES and qw % kw == 0
    kcol, vcol = qw // kw, qw // kw + 1
    kern = functools.partial(_gqa_kernel, n_kv=n_kv, grp=grp)
    return pl.pallas_call(
        kern,
        grid=(b, s // tq),
        in_specs=[
            pl.BlockSpec((tq, qw), lambda i, j: (i * (s // tq) + j, 0)),
            pl.BlockSpec((s, kw), lambda i, j: (i, kcol)),
            pl.BlockSpec((s, kw), lambda i, j: (i, vcol)),
            pl.BlockSpec((l, kw), lambda i, j: (n_lat // l + i, kcol)),
            pl.BlockSpec((l, kw), lambda i, j: (n_lat // l + i, vcol)),
        ],
        out_specs=pl.BlockSpec((tq, qw), lambda i, j: (i * (s // tq) + j, 0)),
        out_shape=jax.ShapeDtypeStruct((n_lat, qw), BF16),
        scratch_shapes=[pltpu.VMEM((n_kv, s + l, HEAD_DIM), BF16), pltpu.VMEM((n_kv, s + l, HEAD_DIM), BF16)],
        compiler_params=_cparams("parallel", "arbitrary"),
        name="gqa_attention",
    )(pb, pb, pb, pb, pb)


def _ctx_attn_kernel(qa_ref, ka_ref, va_ref, qb_ref, kb_ref, vb_ref, oa_ref, ob_ref, *, n_kv, grp):
    na_h = qa_ref.shape[1] // HEAD_DIM
    oa = [_softmax_attend(qa_ref[:, _head(h)], ka_ref[:, _head(h)], va_ref[:, _head(h)]) for h in range(na_h)]
    oa_ref[...] = jnp.concatenate(oa, axis=1).astype(oa_ref.dtype)
    ob = _grouped_attend(qb_ref, lambda j: kb_ref[:, _head(j)], lambda j: vb_ref[:, _head(j)], n_kv, grp)
    ob_ref[...] = ob.astype(ob_ref.dtype)


def ctx_attention(pa, pb, *, b, s, l, na_w, qw, kw):
    r0 = (b * s) // l
    grp = qw // kw
    kern = functools.partial(_ctx_attn_kernel, n_kv=kw // HEAD_DIM, grp=grp)
    a_spec = lambda m: pl.BlockSpec((l, na_w), lambda i: (r0 + i, m))
    return pl.pallas_call(
        kern,
        grid=(b,),
        in_specs=[a_spec(0), a_spec(1), a_spec(2),
                  pl.BlockSpec((l, qw), lambda i: (r0 + i, 0)),
                  pl.BlockSpec((l, kw), lambda i: (r0 + i, grp)),
                  pl.BlockSpec((l, kw), lambda i: (r0 + i, grp + 1))],
        out_specs=[pl.BlockSpec((l, na_w), lambda i: (i, 0)), pl.BlockSpec((l, qw), lambda i: (i, 0))],
        out_shape=[jax.ShapeDtypeStruct((b * l, na_w), BF16), jax.ShapeDtypeStruct((b * l, qw), BF16)],
        compiler_params=_cparams("parallel"),
        name="ctx_attention",
    )(pa, pa, pa, pb, pb, pb)


def _na_kernel(q_ref, k_ref, v_ref, kc_ref, vc_ref, bias_ref, o_ref, k_s, v_s, kc_s, vc_s, *, rows, wr):
    nh = q_ref.shape[1] // HEAD_DIM
    nkey = wr * GRID_W
    for h in range(nh):
        k_s[h] = k_ref[:, _head(h)]
        v_s[h] = v_ref[:, _head(h)]
        kc_s[h] = kc_ref[:, _head(h)]
        vc_s[h] = vc_ref[:, _head(h)]

    def body(r, carry):
        rs = jnp.clip(r - wr // 2, 0, rows - wr)
        q_rows = pl.ds(pl.multiple_of(r * GRID_W, GRID_W), GRID_W)
        k_rows = pl.ds(pl.multiple_of(rs * GRID_W, GRID_W), nkey)
        outs = []
        for h in range(nh):
            q = q_ref[q_rows, _head(h)]
            s_nb = lax.dot_general(q, k_s[h, k_rows, :], (((1,), (1,)), ((), ())), preferred_element_type=F32)
            s_nb = s_nb + bias_ref[h, r - rs]
            s_cx = lax.dot_general(q, kc_s[h], (((1,), (1,)), ((), ())), preferred_element_type=F32)
            m = jnp.maximum(jnp.max(s_nb, axis=-1, keepdims=True), jnp.max(s_cx, axis=-1, keepdims=True))
            p_nb = jnp.exp(s_nb - m)
            p_cx = jnp.exp(s_cx - m)
            l = jnp.sum(p_nb, axis=-1, keepdims=True) + jnp.sum(p_cx, axis=-1, keepdims=True)
            o = (jnp.dot(p_nb.astype(BF16), v_s[h, k_rows, :], preferred_element_type=F32)
                 + jnp.dot(p_cx.astype(BF16), vc_s[h], preferred_element_type=F32))
            outs.append(o / l)
        o_ref[q_rows, :] = jnp.concatenate(outs, axis=1).astype(o_ref.dtype)
        return carry

    lax.fori_loop(0, rows, body, 0)


def na_attention(pa, bias_tab, *, b, s, l, na_w):
    n_lat = b * s
    nh = na_w // HEAD_DIM
    rows = s // GRID_W
    wr = min(NA_WIN_R, rows)
    kern = functools.partial(_na_kernel, rows=rows, wr=wr)
    lat = lambda m: pl.BlockSpec((s, na_w), lambda i: (i, m))
    cx = lambda m: pl.BlockSpec((l, na_w), lambda i: (n_lat // l + i, m))
    return pl.pallas_call(
        kern,
        grid=(b,),
        in_specs=[lat(0), lat(1), lat(2), cx(1), cx(2),
                  pl.BlockSpec(bias_tab.shape, lambda i: (0, 0, 0, 0))],
        out_specs=pl.BlockSpec((s, na_w), lambda i: (i, 0)),
        out_shape=jax.ShapeDtypeStruct((n_lat, na_w), BF16),
        scratch_shapes=[pltpu.VMEM((nh, s, HEAD_DIM), BF16), pltpu.VMEM((nh, s, HEAD_DIM), BF16),
                        pltpu.VMEM((nh, l, HEAD_DIM), BF16), pltpu.VMEM((nh, l, HEAD_DIM), BF16)],
        compiler_params=_cparams("parallel"),
        name="na_attention",
    )(pa, pa, pa, pa, pa, bias_tab)


def na_bias_table(rpb, rows):
    wr = min(NA_WIN_R, rows)
    h = rpb.shape[0]
    t = np.arange(wr)
    kr = np.arange(wr)
    dr = kr[None, :] - t[:, None] + (NA_WIN_R - 1)
    cidx = np.arange(GRID_W)
    col_start = np.clip(cidx - NA_WIN_C // 2, 0, GRID_W - NA_WIN_C)
    col_ok = (cidx[None, :] >= col_start[:, None]) & (cidx[None, :] < col_start[:, None] + NA_WIN_C)
    dc = np.clip(cidx[None, :] - cidx[:, None] + (NA_WIN_C - 1), 0, 2 * NA_WIN_C - 2)
    sel_r = jnp.asarray(dr[:, :, None] == np.arange(2 * NA_WIN_R - 1), F32)
    sel_c = jnp.asarray(dc[:, :, None] == np.arange(2 * NA_WIN_C - 1), F32)
    bias = jnp.einsum("tki,hij->htkj", sel_r, rpb.astype(F32), precision=HIGHEST)
    bias = jnp.einsum("htkj,qcj->htqkc", bias, sel_c, precision=HIGHEST)
    bias = jnp.where(jnp.asarray(col_ok)[None, None, :, None, :], bias, NEG_INF)
    return bias.reshape(h, wr, GRID_W, wr * GRID_W)


HG_BLOCK = 128


def _hgrn_pass(q_ref, v_ref, z_ref, o_acc, lb, bdb, bdf, st, *, reverse, first):
    c = HGRN_CHUNK
    ncb = HG_BLOCK // c
    nblk = q_ref.shape[0] // HG_BLOCK
    lbm = jnp.maximum(lb, LB_FLOOR)
    one_m_lb = 1.0 - lb
    scale = HEAD_DIM ** -0.5
    t_idx = lax.broadcasted_iota(jnp.int32, (ncb, c, LANES), 1)
    edge = 0 if reverse else c - 1

    def bs(x, s):
        return jnp.broadcast_to(x[:, s:s + 1, :], x.shape)

    def seen(s):
        return (t_idx <= s) if reverse else (t_idx >= s)

    def body(i, st):
        blk = (nblk - 1 - i) if reverse else i
        r0 = pl.multiple_of(blk * HG_BLOCK, HG_BLOCK)
        z = z_ref[pl.ds(r0, HG_BLOCK), :]
        q = q_ref[pl.ds(r0, HG_BLOCK), :] * scale
        v = v_ref[pl.ds(r0, HG_BLOCK), :]
        f = one_m_lb * jax.nn.sigmoid(z) + lbm
        k = one_m_lb * jax.nn.sigmoid(-z) - (lbm - lb)
        logf = jnp.log(f).reshape(ncb, c, LANES)
        q3 = q.reshape(ncb, c, LANES)
        k3 = k.reshape(ncb, c, LANES)
        v3 = v.reshape(ncb, c, LANES)
        cum = jnp.zeros_like(logf)
        for s in range(c):
            cum = cum + jnp.where(seen(s), bs(logf, s), 0.0)
        o3 = jnp.zeros_like(logf)
        for s in range(c):
            d = jnp.where(seen(s), cum - bs(cum, s), NEG_INF)
            w = q3 * bs(k3, s) * jnp.exp(d)
            a = jnp.dot(w.reshape(HG_BLOCK, LANES).astype(BF16), bdb, preferred_element_type=F32)
            o3 = o3 + a.reshape(ncb, c, LANES) * bs(v3, s)
        cum_edge = bs(cum, edge)
        qe = (q3 * jnp.exp(cum)).astype(BF16)
        kd = (k3 * jnp.exp(cum_edge - cum)).astype(BF16)
        vb = v3.astype(BF16)
        a_all = jnp.exp(cum_edge)
        outs = [None] * ncb
        for n in (range(ncb - 1, -1, -1) if reverse else range(ncb)):
            o_inter = lax.dot_general(qe[n], st.astype(BF16), (((1,), (1,)), ((), ())),
                                      preferred_element_type=F32)
            outs[n] = o3[n] + o_inter
            u_t = lax.dot_general(vb[n], kd[n], (((0,), (0,)), ((), ())), preferred_element_type=F32)
            st = a_all[n, 0:1, :] * st + u_t * bdf
        val = jnp.concatenate(outs, axis=0)
        if first:
            o_acc[pl.ds(r0, HG_BLOCK), :] = val
        else:
            o_acc[pl.ds(r0, HG_BLOCK), :] += val
        return st

    return lax.fori_loop(0, nblk, body, st)


def _hgrn_kernel(ql_ref, qc_ref, vl_ref, vc_ref, zfl_ref, zfc_ref, zbl_ref, zbc_ref, gl_ref, gc_ref,
                 lb_ref, gain_ref, bdb_ref, bdf_ref, yl_ref, yc_ref, ol_acc, oc_acc):
    bdb = bdb_ref[...]
    bdf = bdf_ref[...]
    zero = jnp.zeros((LANES, LANES), F32)
    run = functools.partial(_hgrn_pass, bdb=bdb, bdf=bdf)
    st = run(qc_ref, vc_ref, zfc_ref, oc_acc, lb_ref[0, 0], st=zero, reverse=False, first=True)
    run(ql_ref, vl_ref, zfl_ref, ol_acc, lb_ref[0, 0], st=st, reverse=False, first=True)
    st = run(qc_ref, vc_ref, zbc_ref, oc_acc, lb_ref[1, 0], st=zero, reverse=True, first=False)
    run(ql_ref, vl_ref, zbl_ref, ol_acc, lb_ref[1, 0], st=st, reverse=True, first=False)
    for acc, g_ref, y_ref in ((ol_acc, gl_ref, yl_ref), (oc_acc, gc_ref, yc_ref)):
        o = acc[...]
        g = g_ref[...]
        y = o * _seg_inv_rms(o, bdf) * gain_ref[...]
        y_ref[...] = (y * (g * jax.nn.sigmoid(g))).astype(y_ref.dtype)


def hgrn_mixer(pc, lb, gain128, bdb, bdf, *, b, s, l):
    w = pc.shape[1] // 5
    nj = w // LANES
    n_lat = b * s
    lat = lambda m: pl.BlockSpec((s, LANES), lambda i, j: (i, m * nj + j))
    cx = lambda m: pl.BlockSpec((l, LANES), lambda i, j: (n_lat // l + i, m * nj + j))
    in_specs = []
    for m in (0, 1, 2, 3, 4):
        in_specs += [lat(m), cx(m)]
    in_specs += [pl.BlockSpec((2, 1, 1, LANES), lambda i, j: (0, j, 0, 0)),
                 pl.BlockSpec((1, LANES), lambda i, j: (0, 0)),
                 pl.BlockSpec((LANES, LANES), lambda i, j: (0, 0)),
                 pl.BlockSpec((LANES, LANES), lambda i, j: (0, 0))]
    return pl.pallas_call(
        _hgrn_kernel,
        grid=(b, nj),
        in_specs=in_specs,
        out_specs=[pl.BlockSpec((s, LANES), lambda i, j: (i, j)),
                   pl.BlockSpec((l, LANES), lambda i, j: (i, j))],
        out_shape=[jax.ShapeDtypeStruct((n_lat, w), BF16), jax.ShapeDtypeStruct((b * l, w), BF16)],
        scratch_shapes=[pltpu.VMEM((s, LANES), F32), pltpu.VMEM((l, LANES), F32)],
        compiler_params=_cparams("parallel", "parallel"),
        name="hgrn_mixer",
    )(*([pc] * 10), lb, gain128, bdb, bdf)


def _outproj_kernel(x_ref, ma_ref, mb_ref, mc_ref, w_ref, gate_ref, g2_ref, shift_ref, scale_ref,
                    wr_ref, br_ref, xo_ref, h2_ref, lg_ref, *, wa, wb):
    w = w_ref[...]
    y = jnp.dot(ma_ref[...], w[:wa], preferred_element_type=F32)
    y = y + jnp.dot(mb_ref[...], w[wa:wa + wb], preferred_element_type=F32)
    y = y + jnp.dot(mc_ref[...], w[wa + wb:], preferred_element_type=F32)
    x = x_ref[...] + gate_ref[0] * y
    xo_ref[...] = x
    ms = jnp.mean(x * x, axis=-1, keepdims=True)
    h = x * lax.rsqrt(ms + EPS) * g2_ref[0]
    h = h * (1.0 + scale_ref[0]) + shift_ref[0]
    h2_ref[...] = h
    logits = jnp.dot(h, wr_ref[...], precision=HIGHEST, preferred_element_type=F32) + br_ref[...]
    lg_ref[...] = _route_meta(logits)


def out_projection(xall, mix_a, mix_b, mix_c, w_out_bf, mod3, layer_g2, w_route, b_route, *, seq, nseg, tm):
    r, d = xall.shape
    wa, wb, wc = mix_a.shape[1], mix_b.shape[1], mix_c.shape[1]
    tiles_per_seq = seq // tm

    def seg(i):
        return jnp.minimum(i // tiles_per_seq, nseg - 1)

    def modspec(m):
        return pl.BlockSpec((1, 1, d), lambda i: (seg(i) * N_MOD + m, 0, 0))

    row = lambda wdt: pl.BlockSpec((tm, wdt), lambda i: (i, 0))
    kern = functools.partial(_outproj_kernel, wa=wa, wb=wb)
    return pl.pallas_call(
        kern,
        grid=(r // tm,),
        in_specs=[row(d), row(wa), row(wb), row(wc),
                  pl.BlockSpec((wa + wb + wc, d), lambda i: (0, 0)),
                  modspec(2),
                  pl.BlockSpec((1, d), lambda i: (0, 0)),
                  modspec(3), modspec(4),
                  pl.BlockSpec((d, LANES), lambda i: (0, 0)),
                  pl.BlockSpec((1, LANES), lambda i: (0, 0))],
        out_specs=[row(d), row(d), row(LANES)],
        out_shape=[jax.ShapeDtypeStruct((r, d), F32),
                   jax.ShapeDtypeStruct((r, d), F32),
                   jax.ShapeDtypeStruct((r, LANES), F32)],
        compiler_params=_cparams("parallel"),
        name="out_projection",
    )(xall, mix_a, mix_b, mix_c, w_out_bf, mod3, layer_g2, mod3, mod3, w_route, b_route)


PAIRS_PER_GROUP = EXPERTS_PER_GROUP * (EXPERTS_PER_GROUP - 1) // 2
N_BUCKETS = N_GROUPS * PAIRS_PER_GROUP
ROW_TILE = 128
META_BUCKET, META_WA, META_WB = 0, 1, 2


def _bucket_experts():
    ea = np.zeros((LANES,), np.int32)
    eb = np.zeros((LANES,), np.int32)
    for g in range(N_GROUPS):
        k = g * PAIRS_PER_GROUP
        for a in range(EXPERTS_PER_GROUP):
            for b in range(a + 1, EXPERTS_PER_GROUP):
                ea[k], eb[k] = g * EXPERTS_PER_GROUP + a, g * EXPERTS_PER_GROUP + b
                k += 1
    return ea, eb


def _route_meta(logits):
    lane = lax.broadcasted_iota(jnp.int32, logits.shape, 1).astype(F32)
    is_g = lane < N_GROUPS
    gl = jnp.where(is_g, logits, -jnp.inf)
    gmax = jnp.max(gl, axis=-1, keepdims=True)
    g_idx = jnp.min(jnp.where(gl == gmax, lane, LANES), axis=-1, keepdims=True)
    gsum = jnp.sum(jnp.where(is_g, jnp.exp(gl - gmax), 0.0), axis=-1, keepdims=True)
    g_top = 1.0 / gsum
    lo = N_GROUPS + g_idx * EXPERTS_PER_GROUP
    in_grp = (lane >= lo) & (lane < lo + EXPERTS_PER_GROUP)
    el = jnp.where(in_grp, logits, -jnp.inf)
    m1 = jnp.max(el, axis=-1, keepdims=True)
    i1 = jnp.min(jnp.where(el == m1, lane, LANES), axis=-1, keepdims=True)
    el2 = jnp.where(lane == i1, -jnp.inf, el)
    m2 = jnp.max(el2, axis=-1, keepdims=True)
    i2 = jnp.min(jnp.where(el2 == m2, lane, LANES), axis=-1, keepdims=True)
    e21 = jnp.exp(m2 - m1)
    w1 = g_top / (1.0 + e21)
    w2 = e21 * w1
    first_low = i1 < i2
    la = jnp.minimum(i1, i2) - lo
    lb = jnp.maximum(i1, i2) - lo
    pair = la * (2 * EXPERTS_PER_GROUP - 1 - la) * 0.5 + (lb - la - 1.0)
    bucket = g_idx * PAIRS_PER_GROUP + pair
    wa = jnp.where(first_low, w1, w2)
    wb = jnp.where(first_low, w2, w1)
    return jnp.where(lane == META_BUCKET, bucket,
                     jnp.where(lane == META_WA, wa, jnp.where(lane == META_WB, wb, 0.0)))


def _plan_kernel(meta_ref, tri_ref, ids_ref, rank_ref, cnt_ref, carry):
    @pl.when(pl.program_id(0) == 0)
    def _():
        carry[...] = jnp.zeros_like(carry)

    ids = meta_ref[...].T[META_BUCKET:META_BUCKET + 1, :]
    sub = lax.broadcasted_iota(jnp.int32, (LANES, ids.shape[1]), 0).astype(F32)
    onehot = (sub == ids).astype(F32)
    before = jnp.dot(onehot.astype(BF16), tri_ref[...], preferred_element_type=F32)
    rank = jnp.sum(onehot * (before + carry[...]), axis=0, keepdims=True)
    ids_ref[0] = ids.astype(jnp.int32)
    rank_ref[0] = rank.astype(jnp.int32)
    total = carry[...] + jnp.sum(onehot, axis=1, keepdims=True)
    carry[...] = total
    cnt_ref[...] = total.astype(jnp.int32)


def route_plan(meta, tm):
    r = meta.shape[0]
    nt = r // tm
    i = np.arange(tm)
    tri = jnp.asarray(i[:, None] < i[None, :], BF16)
    ids, rank, cnt = pl.pallas_call(
        _plan_kernel,
        grid=(nt,),
        in_specs=[pl.BlockSpec((tm, LANES), lambda i: (i, 0)),
                  pl.BlockSpec((tm, tm), lambda i: (0, 0))],
        out_specs=[pl.BlockSpec((1, 1, tm), lambda i: (i, 0, 0)),
                   pl.BlockSpec((1, 1, tm), lambda i: (i, 0, 0)),
                   pl.BlockSpec((LANES, 1), lambda i: (0, 0))],
        out_shape=[jax.ShapeDtypeStruct((nt, 1, tm), jnp.int32),
                   jax.ShapeDtypeStruct((nt, 1, tm), jnp.int32),
                   jax.ShapeDtypeStruct((LANES, 1), jnp.int32)],
        scratch_shapes=[pltpu.VMEM((LANES, 1), F32)],
        compiler_params=_cparams("arbitrary"),
        name="route_plan",
    )(meta, tri)
    return ids.reshape(r), rank.reshape(r), cnt.reshape(LANES)


def _row_copy(src, dst, i, j, sem):
    return pltpu.make_async_copy(src.at[pl.ds(i, 1)], dst.at[pl.ds(j, 1)], sem)


def _dispatch_kernel(pos_ref, h_hbm, init_hbm, o_hbm, sem, *, ch):
    del init_hbm
    base = pl.program_id(0) * ch

    def issue(i, c):
        _row_copy(h_hbm, o_hbm, base + i, pos_ref[base + i], sem).start()
        return c

    def drain(i, c):
        _row_copy(h_hbm, o_hbm, 0, 0, sem).wait()
        return c

    lax.fori_loop(0, ch, issue, 0, unroll=8)
    lax.fori_loop(0, ch, drain, 0, unroll=8)


def dispatch_rows(pos, h2, n_rows, ch):
    r, d = h2.shape
    return pl.pallas_call(
        functools.partial(_dispatch_kernel, ch=ch),
        grid_spec=pltpu.PrefetchScalarGridSpec(
            num_scalar_prefetch=1,
            grid=(r // ch,),
            in_specs=[pl.BlockSpec(memory_space=pl.ANY), pl.BlockSpec(memory_space=pl.ANY)],
            out_specs=pl.BlockSpec(memory_space=pl.ANY),
            scratch_shapes=[pltpu.SemaphoreType.DMA],
        ),
        out_shape=jax.ShapeDtypeStruct((n_rows, d), h2.dtype),
        input_output_aliases={2: 0},
        compiler_params=_cparams("arbitrary"),
        name="dispatch_rows",
    )(pos, h2, jnp.zeros((n_rows, d), h2.dtype))


def _expert_kernel(ta_ref, tb_ref, nu_ref, hs_ref, wga_ref, wua_ref, wda_ref, wgb_ref, wub_ref, wdb_ref,
                   y_ref, gu_a, dn_a, gu_b, dn_b):
    j = pl.program_id(0)
    d = hs_ref.shape[1]
    ff = wga_ref.shape[2]
    prev = jnp.maximum(j - 1, 0)

    def refresh(gu, dn, wg_ref, wu_ref, wd_ref):
        gu[:, :ff] = wg_ref[0].astype(BF16)
        gu[:, ff:] = wu_ref[0].astype(BF16)
        dn[...] = wd_ref[0].astype(BF16)

    @pl.when((j == 0) | (ta_ref[j] != ta_ref[prev]))
    def _():
        refresh(gu_a, dn_a, wga_ref, wua_ref, wda_ref)

    @pl.when((j == 0) | (tb_ref[j] != tb_ref[prev]))
    def _():
        refresh(gu_b, dn_b, wgb_ref, wub_ref, wdb_ref)

    @pl.when(j < nu_ref[0])
    def _():
        h = hs_ref[...].astype(BF16)
        for n, (gu, dn) in enumerate(((gu_a, dn_a), (gu_b, dn_b))):
            hgu = jnp.dot(h, gu[...], preferred_element_type=F32)
            hg = hgu[:, :ff]
            hid = (hg * jax.nn.sigmoid(hg)) * hgu[:, ff:]
            y_ref[:, n * d:(n + 1) * d] = jnp.dot(hid.astype(BF16), dn[...], preferred_element_type=F32)

    @pl.when(j >= nu_ref[0])
    def _():
        y_ref[...] = jnp.zeros_like(y_ref)


def expert_pairs(tile_a, tile_b, n_used, hs, w_gate, w_up, w_down):
    rows, d = hs.shape
    _, _, ff = w_gate.shape
    nt = rows // ROW_TILE
    blk = lambda i, ta, tb, nu: (jnp.minimum(i, nu[0]), 0)
    wsel = lambda which, shape: pl.BlockSpec(
        shape, (lambda i, ta, tb, nu: (ta[i], 0, 0)) if which == 0 else (lambda i, ta, tb, nu: (tb[i], 0, 0)))
    return pl.pallas_call(
        _expert_kernel,
        grid_spec=pltpu.PrefetchScalarGridSpec(
            num_scalar_prefetch=3,
            grid=(nt,),
            in_specs=[pl.BlockSpec((ROW_TILE, d), blk),
                      wsel(0, (1, d, ff)), wsel(0, (1, d, ff)), wsel(0, (1, ff, d)),
                      wsel(1, (1, d, ff)), wsel(1, (1, d, ff)), wsel(1, (1, ff, d))],
            out_specs=pl.BlockSpec((ROW_TILE, 2 * d), lambda i, ta, tb, nu: (i, 0)),
            scratch_shapes=[pltpu.VMEM((d, 2 * ff), BF16), pltpu.VMEM((ff, d), BF16),
                            pltpu.VMEM((d, 2 * ff), BF16), pltpu.VMEM((ff, d), BF16)],
        ),
        out_shape=jax.ShapeDtypeStruct((rows, 2 * d), F32),
        compiler_params=_cparams("arbitrary"),
        name="expert_pairs",
    )(tile_a, tile_b, n_used, hs, w_gate, w_up, w_down, w_gate, w_up, w_down)


def _combine_kernel(pos_ref, x_ref, meta_ref, gate_ref, y_hbm, o_ref, ybuf, sem):
    tm = x_ref.shape[0]
    d = x_ref.shape[1]
    base = pl.program_id(0) * tm

    def issue(i, c):
        _row_copy(y_hbm, ybuf, pos_ref[base + i], i, sem).start()
        return c

    def drain(i, c):
        _row_copy(y_hbm, ybuf, 0, i, sem).wait()
        return c

    lax.fori_loop(0, tm, issue, 0, unroll=8)
    lax.fori_loop(0, tm, drain, 0, unroll=8)
    meta = meta_ref[...]
    ff = (meta[:, META_WA:META_WA + 1] * ybuf[:, :d] + meta[:, META_WB:META_WB + 1] * ybuf[:, d:])
    o_ref[...] = x_ref[...] + gate_ref[0] * ff


def combine_rows(pos, x_mid, meta, mod3, y, *, seq, nseg, tm):
    r, d = x_mid.shape
    tiles_per_seq = seq // tm
    seg = lambda i: jnp.minimum(i // tiles_per_seq, nseg - 1)
    return pl.pallas_call(
        _combine_kernel,
        grid_spec=pltpu.PrefetchScalarGridSpec(
            num_scalar_prefetch=1,
            grid=(r // tm,),
            in_specs=[pl.BlockSpec((tm, d), lambda i, p: (i, 0)),
                      pl.BlockSpec((tm, LANES), lambda i, p: (i, 0)),
                      pl.BlockSpec((1, 1, d), lambda i, p: (seg(i) * N_MOD + 5, 0, 0)),
                      pl.BlockSpec(memory_space=pl.ANY)],
            out_specs=pl.BlockSpec((tm, d), lambda i, p: (i, 0)),
            scratch_shapes=[pltpu.VMEM((tm, 2 * d), F32), pltpu.SemaphoreType.DMA],
        ),
        out_shape=jax.ShapeDtypeStruct((r, d), F32),
        compiler_params=_cparams("arbitrary"),
        name="combine_rows",
    )(pos, x_mid, meta, mod3, y)


def moe_block(x_mid, h2, meta, mod3, w_gate, w_up, w_down, *, seq, nseg):
    r, d = x_mid.shape
    ids, rank, cnt = route_plan(meta, 512)
    tiles = (cnt + (ROW_TILE - 1)) // ROW_TILE
    incl = jnp.cumsum(tiles)
    n_used = incl[-1]
    pos = (incl - tiles)[ids] * ROW_TILE + rank
    nt = r // ROW_TILE + N_BUCKETS
    tile_bucket = jnp.searchsorted(incl, jnp.minimum(jnp.arange(nt), n_used - 1), side="right")
    ea, eb = _bucket_experts()
    tile_a = jnp.asarray(ea)[tile_bucket]
    tile_b = jnp.asarray(eb)[tile_bucket]
    hs = dispatch_rows(pos, h2, nt * ROW_TILE, 512)
    y = expert_pairs(tile_a, tile_b, n_used.reshape(1).astype(jnp.int32), hs, w_gate, w_up, w_down)
    return combine_rows(pos, x_mid, meta, mod3, y, seq=seq, nseg=nseg, tm=256)


def _rope_tables(seq, tm):
    t = np.arange(seq)
    row = (t // GRID_W).astype(np.float32)
    col = (t % GRID_W).astype(np.float32)
    half = HEAD_DIM // 2
    inv = jnp.asarray(ROPE_THETA, F32) ** (-jnp.arange(0, half, 2, dtype=F32) / half)
    ang = jnp.concatenate([jnp.asarray(row)[:, None] * inv, jnp.asarray(col)[:, None] * inv], axis=-1)
    cos = jnp.repeat(jnp.cos(ang), 2, axis=-1)
    sin = jnp.repeat(jnp.sin(ang), 2, axis=-1) * jnp.asarray(np.tile([-1.0, 1.0], half), F32)
    cos = jnp.tile(cos, (1, LANES // HEAD_DIM))
    sin = jnp.tile(sin, (1, LANES // HEAD_DIM))
    cos = jnp.concatenate([cos, jnp.ones((tm, LANES), F32)], axis=0)
    sin = jnp.concatenate([sin, jnp.zeros((tm, LANES), F32)], axis=0)
    return cos, sin


def kernel(x, c, ctx, c_ctx, w_ada, b_ada, norm1_g, w_in, na_q_norm, na_k_norm, na_rpb, gqa_q_norm, gqa_k_norm, hgrn_lb, hgrn_o_norm, w_out, norm2_g, w_route_group, b_route_group, w_route_expert, b_route_expert, w_exp_gate, w_exp_up, w_exp_down):
    b, s, d = x.shape
    l = ctx.shape[1]
    depth = w_ada.shape[0]
    assert s % 512 == 0 and (b * l) % 512 == 0 and s // GRID_W >= NA_WIN_R
    assert s % l == 0 and l % HG_BLOCK == 0
    nseg = b + 1
    n_lat = b * s
    n_ctx = b * l
    na_w, gq_qw = d // 4, d // 2
    gq_kw = gq_qw // 4
    hg_w = d // 4
    tm = 512

    c_all = jnp.zeros((16, d), F32).at[:b].set(c).at[b].set(c_ctx)
    mod = ada_mod(c_all, w_ada, b_ada)
    cos_t, sin_t = _rope_tables(s, tm)
    bd_f = _block_diag_ones(LANES, HEAD_DIM, F32)
    bd_b = _block_diag_ones(LANES, HEAD_DIM, BF16)
    p_lb = jax.nn.softmax(hgrn_lb.astype(F32), axis=0)
    lb_all = jnp.cumsum(p_lb, axis=0) - p_lb[0]
    tile2 = lambda g: jnp.tile(g, LANES // HEAD_DIM)

    xall = jnp.concatenate([x.reshape(n_lat, d), ctx.reshape(n_ctx, d)], axis=0)
    for layer in range(depth):
        ctx_out = layer < depth - 1
        mod3 = mod[layer].reshape(16 * N_MOD, 1, d)
        gains = jnp.zeros((8, LANES), F32)
        gains = gains.at[0].set(tile2(na_q_norm[layer])).at[1].set(tile2(na_k_norm[layer]))
        gains = gains.at[2].set(tile2(gqa_q_norm[layer])).at[3].set(tile2(gqa_k_norm[layer]))
        pa, pb, pc = in_projection(xall, mod3, norm1_g[layer][None], w_in[layer].astype(BF16), cos_t, sin_t,
                                   gains, bd_f, n_lat_rows=n_lat, seq=s, nseg=nseg, tm=tm)
        bias_tab = na_bias_table(na_rpb[layer], s // GRID_W)
        o_a = na_attention(pa, bias_tab, b=b, s=s, l=l, na_w=na_w)
        o_b = gqa_attention(pb, b=b, s=s, l=l, qw=gq_qw, kw=gq_kw, tq=128)
        lb4 = lb_all[layer].reshape(2, hg_w // LANES, 1, LANES)
        y_lat, y_ctx = hgrn_mixer(pc, lb4, tile2(hgrn_o_norm[layer])[None], bd_b, bd_f, b=b, s=s, l=l)

        w_route = jnp.zeros((d, LANES), F32).at[:, :N_GROUPS].set(w_route_group[layer])
        w_route = w_route.at[:, N_GROUPS:N_GROUPS + N_EXPERTS].set(w_route_expert[layer])
        b_route = jnp.zeros((1, LANES), F32).at[0, :N_GROUPS].set(b_route_group[layer])
        b_route = b_route.at[0, N_GROUPS:N_GROUPS + N_EXPERTS].set(b_route_expert[layer])
        if ctx_out:
            o_ac, o_bc = ctx_attention(pa, pb, b=b, s=s, l=l, na_w=na_w, qw=gq_qw, kw=gq_kw)
            mix_a = jnp.concatenate([o_a, o_ac], axis=0)
            mix_b = jnp.concatenate([o_b, o_bc], axis=0)
            y_c = jnp.concatenate([y_lat, y_ctx], axis=0)
            x_in = xall
        else:
            mix_a, mix_b, y_c, x_in = o_a, o_b, y_lat, xall[:n_lat]
        x_mid, h2, meta = out_projection(x_in, mix_a, mix_b, y_c, w_out[layer].astype(BF16), mod3,
                                          norm2_g[layer][None], w_route, b_route, seq=s, nseg=nseg, tm=tm)
        xall = moe_block(x_mid, h2, meta, mod3, w_exp_gate[layer], w_exp_up[layer], w_exp_down[layer],
                         seq=s, nseg=nseg)
    return xall[:n_lat].reshape(b, s, d)
```

```python
import functools

import jax
import jax.numpy as jnp
import numpy as np
from jax import lax
from jax.experimental import pallas as pl
from jax.experimental.pallas import tpu as pltpu

F32 = jnp.float32
BF16 = jnp.bfloat16
HIGHEST = lax.Precision.HIGHEST

HEAD_DIM = 64
GRID_W = 64
NA_WIN_R = 8
NA_WIN_C = 16
ROPE_THETA = 10000.0
HGRN_CHUNK = 16
N_GROUPS = 4
EXPERTS_PER_GROUP = 8
N_EXPERTS = N_GROUPS * EXPERTS_PER_GROUP
N_MOD = 6
EPS = 1e-6
NEG_INF = -1e30
LB_FLOOR = 1e-20
LANES = 128
VMEM_LIMIT = 56 * 1024 * 1024


def _cparams(*sem):
    return pltpu.CompilerParams(dimension_semantics=sem, vmem_limit_bytes=VMEM_LIMIT)


def _block_diag_ones(n, blk, dtype):
    i = np.arange(n)
    return jnp.asarray((i[:, None] // blk) == (i[None, :] // blk), dtype=dtype)


def _ada_kernel(c_ref, w_ref, b_ref, o_ref):
    c = c_ref[...]
    s = c * jax.nn.sigmoid(c)
    o_ref[0] = jnp.dot(s, w_ref[0], precision=HIGHEST, preferred_element_type=F32) + b_ref[0]


def ada_mod(c_all, w_ada, b_ada):
    depth, d, n = w_ada.shape
    tn = 1536
    return pl.pallas_call(
        _ada_kernel,
        grid=(depth, n // tn),
        in_specs=[
            pl.BlockSpec((16, d), lambda l, j: (0, 0)),
            pl.BlockSpec((1, d, tn), lambda l, j: (l, 0, j)),
            pl.BlockSpec((1, 1, tn), lambda l, j: (l, 0, j)),
        ],
        out_specs=pl.BlockSpec((1, 16, tn), lambda l, j: (l, 0, j)),
        out_shape=jax.ShapeDtypeStruct((depth, 16, n), F32),
        compiler_params=_cparams("parallel", "parallel"),
        name="ada_mod",
    )(c_all, w_ada, b_ada.reshape(depth, 1, n))


def _seg_inv_rms(x, bd):
    ss = jnp.dot(x * x, bd, precision=HIGHEST, preferred_element_type=F32)
    return lax.rsqrt(ss * (1.0 / HEAD_DIM) + EPS)


def _pair_swap(x):
    lane = lax.broadcasted_iota(jnp.int32, x.shape, 1)
    return jnp.where((lane & 1) == 0, pltpu.roll(x, LANES - 1, 1), pltpu.roll(x, 1, 1))


def _inproj_kernel(x_ref, g1_ref, shift_ref, scale_ref, w_ref, cos_ref, sin_ref, gains_ref, bd_ref,
                   oa_ref, ob_ref, oc_ref, *, na_w, gq_qw, gq_kw):
    x = x_ref[...]
    ms = jnp.mean(x * x, axis=-1, keepdims=True)
    h = x * lax.rsqrt(ms + EPS) * g1_ref[0]
    h = h * (1.0 + scale_ref[0]) + shift_ref[0]
    p = jnp.dot(h.astype(BF16), w_ref[...], preferred_element_type=F32)
    bd = bd_ref[...]
    cos = cos_ref[...]
    sin = sin_ref[...]
    qscale = HEAD_DIM ** -0.5

    def normed(col, gain_row):
        xb = p[:, col:col + LANES]
        return xb * _seg_inv_rms(xb, bd) * gains_ref[gain_row:gain_row + 1, :]

    def rope(xn):
        return xn * cos + _pair_swap(xn) * sin

    for j in range(na_w // LANES):
        c = j * LANES
        oa_ref[:, c:c + LANES] = (normed(c, 0) * qscale).astype(BF16)
        oa_ref[:, na_w + c:na_w + c + LANES] = normed(na_w + c, 1).astype(BF16)
    oa_ref[:, 2 * na_w:3 * na_w] = p[:, 2 * na_w:3 * na_w].astype(BF16)
    b0 = 3 * na_w
    for j in range(gq_qw // LANES):
        c = j * LANES
        ob_ref[:, c:c + LANES] = (rope(normed(b0 + c, 2)) * qscale).astype(BF16)
    for j in range(gq_kw // LANES):
        c = gq_qw + j * LANES
        ob_ref[:, c:c + LANES] = rope(normed(b0 + c, 3)).astype(BF16)
    ob_ref[:, gq_qw + gq_kw:] = p[:, b0 + gq_qw + gq_kw:b0 + gq_qw + 2 * gq_kw].astype(BF16)
    oc_ref[...] = p[:, b0 + gq_qw + 2 * gq_kw:]


def in_projection(xall, mod3, layer_g1, w_in_bf, cos_t, sin_t, gains, bd, *, n_lat_rows, seq, nseg, tm):
    r, d = xall.shape
    d_in = w_in_bf.shape[1]
    na_w = d // 4
    gq_qw = d // 2
    gq_kw = gq_qw // 4
    c_w = d_in - 3 * na_w - gq_qw - 2 * gq_kw
    lat_tiles = n_lat_rows // tm
    tiles_per_seq = seq // tm

    def seg(i):
        return jnp.minimum(i // tiles_per_seq, nseg - 1)

    def rope_blk(i):
        return jnp.where(i < lat_tiles, i % tiles_per_seq, tiles_per_seq)

    kern = functools.partial(_inproj_kernel, na_w=na_w, gq_qw=gq_qw, gq_kw=gq_kw)
    return pl.pallas_call(
        kern,
        grid=(r // tm,),
        in_specs=[
            pl.BlockSpec((tm, d), lambda i: (i, 0)),
            pl.BlockSpec((1, d), lambda i: (0, 0)),
            pl.BlockSpec((1, 1, d), lambda i: (seg(i) * N_MOD + 0, 0, 0)),
            pl.BlockSpec((1, 1, d), lambda i: (seg(i) * N_MOD + 1, 0, 0)),
            pl.BlockSpec((d, d_in), lambda i: (0, 0)),
            pl.BlockSpec((tm, LANES), lambda i: (rope_blk(i), 0)),
            pl.BlockSpec((tm, LANES), lambda i: (rope_blk(i), 0)),
            pl.BlockSpec((8, LANES), lambda i: (0, 0)),
            pl.BlockSpec((LANES, LANES), lambda i: (0, 0)),
        ],
        out_specs=[
            pl.BlockSpec((tm, 3 * na_w), lambda i: (i, 0)),
            pl.BlockSpec((tm, gq_qw + 2 * gq_kw), lambda i: (i, 0)),
            pl.BlockSpec((tm, c_w), lambda i: (i, 0)),
        ],
        out_shape=[
            jax.ShapeDtypeStruct((r, 3 * na_w), BF16),
            jax.ShapeDtypeStruct((r, gq_qw + 2 * gq_kw), BF16),
            jax.ShapeDtypeStruct((r, c_w), F32),
        ],
        compiler_params=_cparams("parallel"),
        name="in_projection",
    )(xall, layer_g1, mod3, mod3, w_in_bf, cos_t, sin_t, gains, bd)


def _head(j):
    return slice(j * HEAD_DIM, (j + 1) * HEAD_DIM)


def _softmax_attend(q, k, v):
    s = lax.dot_general(q, k, (((1,), (1,)), ((), ())), preferred_element_type=F32)
    m = jnp.max(s, axis=-1, keepdims=True)
    p = jnp.exp(s - m)
    l = jnp.sum(p, axis=-1, keepdims=True)
    return jnp.dot(p.astype(BF16), v, preferred_element_type=F32) / l


def _grouped_attend(q_ref, k_of, v_of, n_kv, grp):
    t = q_ref.shape[0]
    outs = []
    for j in range(n_kv):
        q4 = jnp.concatenate([q_ref[:, _head(j * grp + g)] for g in range(grp)], axis=0)
        o = _softmax_attend(q4, k_of(j), v_of(j))
        outs += [o[g * t:(g + 1) * t] for g in range(grp)]
    return jnp.concatenate(outs, axis=1)


def _gqa_kernel(q_ref, kl_ref, vl_ref, kc_ref, vc_ref, o_ref, k_s, v_s, *, n_kv, grp):
    s_len = kl_ref.shape[0]

    @pl.when(pl.program_id(1) == 0)
    def _():
        for j in range(n_kv):
            k_s[j, :s_len, :] = kl_ref[:, _head(j)]
            k_s[j, s_len:, :] = kc_ref[:, _head(j)]
            v_s[j, :s_len, :] = vl_ref[:, _head(j)]
            v_s[j, s_len:, :] = vc_ref[:, _head(j)]

    o = _grouped_attend(q_ref, lambda j: k_s[j], lambda j: v_s[j], n_kv, grp)
    o_ref[...] = o.astype(o_ref.dtype)


def gqa_attention(pb, *, b, s, l, qw, kw, tq):
    n_lat = b * s
    n_kv = kw // HEAD_DIM
    grp = qw // kw
    assert kw == LANES and qw % kw == 0
    kcol, vcol = qw // kw, qw // kw + 1
    kern = functools.partial(_gqa_kernel, n_kv=n_kv, grp=grp)
    return pl.pallas_call(
        kern,
        grid=(b, s // tq),
        in_specs=[
            pl.BlockSpec((tq, qw), lambda i, j: (i * (s // tq) + j, 0)),
            pl.BlockSpec((s, kw), lambda i, j: (i, kcol)),
            pl.BlockSpec((s, kw), lambda i, j: (i, vcol)),
            pl.BlockSpec((l, kw), lambda i, j: (n_lat // l + i, kcol)),
            pl.BlockSpec((l, kw), lambda i, j: (n_lat // l + i, vcol)),
        ],
        out_specs=pl.BlockSpec((tq, qw), lambda i, j: (i * (s // tq) + j, 0)),
        out_shape=jax.ShapeDtypeStruct((n_lat, qw), BF16),
        scratch_shapes=[pltpu.VMEM((n_kv, s + l, HEAD_DIM), BF16), pltpu.VMEM((n_kv, s + l, HEAD_DIM), BF16)],
        compiler_params=_cparams("parallel", "arbitrary"),
        name="gqa_attention",
    )(pb, pb, pb, pb, pb)


def _ctx_attn_kernel(qa_ref, ka_ref, va_ref, qb_ref, kb_ref, vb_ref, oa_ref, ob_ref, *, n_kv, grp):
    na_h = qa_ref.shape[1] // HEAD_DIM
    oa = [_softmax_attend(qa_ref[:, _head(h)], ka_ref[:, _head(h)], va_ref[:, _head(h)]) for h in range(na_h)]
    oa_ref[...] = jnp.concatenate(oa, axis=1).astype(oa_ref.dtype)
    ob = _grouped_attend(qb_ref, lambda j: kb_ref[:, _head(j)], lambda j: vb_ref[:, _head(j)], n_kv, grp)
    ob_ref[...] = ob.astype(ob_ref.dtype)


def ctx_attention(pa, pb, *, b, s, l, na_w, qw, kw):
    r0 = (b * s) // l
    grp = qw // kw
    kern = functools.partial(_ctx_attn_kernel, n_kv=kw // HEAD_DIM, grp=grp)
    a_spec = lambda m: pl.BlockSpec((l, na_w), lambda i: (r0 + i, m))
    return pl.pallas_call(
        kern,
        grid=(b,),
        in_specs=[a_spec(0), a_spec(1), a_spec(2),
                  pl.BlockSpec((l, qw), lambda i: (r0 + i, 0)),
                  pl.BlockSpec((l, kw), lambda i: (r0 + i, grp)),
                  pl.BlockSpec((l, kw), lambda i: (r0 + i, grp + 1))],
        out_specs=[pl.BlockSpec((l, na_w), lambda i: (i, 0)), pl.BlockSpec((l, qw), lambda i: (i, 0))],
        out_shape=[jax.ShapeDtypeStruct((b * l, na_w), BF16), jax.ShapeDtypeStruct((b * l, qw), BF16)],
        compiler_params=_cparams("parallel"),
        name="ctx_attention",
    )(pa, pa, pa, pb, pb, pb)


def _na_kernel(q_ref, k_ref, v_ref, kc_ref, vc_ref, bias_ref, o_ref, k_s, v_s, kc_s, vc_s, *, rows, wr):
    nh = q_ref.shape[1] // HEAD_DIM
    nkey = wr * GRID_W
    for h in range(nh):
        k_s[h] = k_ref[:, _head(h)]
        v_s[h] = v_ref[:, _head(h)]
        kc_s[h] = kc_ref[:, _head(h)]
        vc_s[h] = vc_ref[:, _head(h)]

    def body(r, carry):
        rs = jnp.clip(r - wr // 2, 0, rows - wr)
        q_rows = pl.ds(pl.multiple_of(r * GRID_W, GRID_W), GRID_W)
        k_rows = pl.ds(pl.multiple_of(rs * GRID_W, GRID_W), nkey)
        outs = []
        for h in range(nh):
            q = q_ref[q_rows, _head(h)]
            s_nb = lax.dot_general(q, k_s[h, k_rows, :], (((1,), (1,)), ((), ())), preferred_element_type=F32)
            s_nb = s_nb + bias_ref[h, r - rs]
            s_cx = lax.dot_general(q, kc_s[h], (((1,), (1,)), ((), ())), preferred_element_type=F32)
            m = jnp.maximum(jnp.max(s_nb, axis=-1, keepdims=True), jnp.max(s_cx, axis=-1, keepdims=True))
            p_nb = jnp.exp(s_nb - m)
            p_cx = jnp.exp(s_cx - m)
            l = jnp.sum(p_nb, axis=-1, keepdims=True) + jnp.sum(p_cx, axis=-1, keepdims=True)
            o = (jnp.dot(p_nb.astype(BF16), v_s[h, k_rows, :], preferred_element_type=F32)
                 + jnp.dot(p_cx.astype(BF16), vc_s[h], preferred_element_type=F32))
            outs.append(o / l)
        o_ref[q_rows, :] = jnp.concatenate(outs, axis=1).astype(o_ref.dtype)
        return carry

    lax.fori_loop(0, rows, body, 0)


def na_attention(pa, bias_tab, *, b, s, l, na_w):
    n_lat = b * s
    nh = na_w // HEAD_DIM
    rows = s // GRID_W
    wr = min(NA_WIN_R, rows)
    kern = functools.partial(_na_kernel, rows=rows, wr=wr)
    lat = lambda m: pl.BlockSpec((s, na_w), lambda i: (i, m))
    cx = lambda m: pl.BlockSpec((l, na_w), lambda i: (n_lat // l + i, m))
    return pl.pallas_call(
        kern,
        grid=(b,),
        in_specs=[lat(0), lat(1), lat(2), cx(1), cx(2),
                  pl.BlockSpec(bias_tab.shape, lambda i: (0, 0, 0, 0))],
        out_specs=pl.BlockSpec((s, na_w), lambda i: (i, 0)),
        out_shape=jax.ShapeDtypeStruct((n_lat, na_w), BF16),
        scratch_shapes=[pltpu.VMEM((nh, s, HEAD_DIM), BF16), pltpu.VMEM((nh, s, HEAD_DIM), BF16),
                        pltpu.VMEM((nh, l, HEAD_DIM), BF16), pltpu.VMEM((nh, l, HEAD_DIM), BF16)],
        compiler_params=_cparams("parallel"),
        name="na_attention",
    )(pa, pa, pa, pa, pa, bias_tab)


def na_bias_table(rpb, rows):
    wr = min(NA_WIN_R, rows)
    h = rpb.shape[0]
    t = np.arange(wr)
    kr = np.arange(wr)
    dr = kr[None, :] - t[:, None] + (NA_WIN_R - 1)
    cidx = np.arange(GRID_W)
    col_start = np.clip(cidx - NA_WIN_C // 2, 0, GRID_W - NA_WIN_C)
    col_ok = (cidx[None, :] >= col_start[:, None]) & (cidx[None, :] < col_start[:, None] + NA_WIN_C)
    dc = np.clip(cidx[None, :] - cidx[:, None] + (NA_WIN_C - 1), 0, 2 * NA_WIN_C - 2)
    sel_r = jnp.asarray(dr[:, :, None] == np.arange(2 * NA_WIN_R - 1), F32)
    sel_c = jnp.asarray(dc[:, :, None] == np.arange(2 * NA_WIN_C - 1), F32)
    bias = jnp.einsum("tki,hij->htkj", sel_r, rpb.astype(F32), precision=HIGHEST)
    bias = jnp.einsum("htkj,qcj->htqkc", bias, sel_c, precision=HIGHEST)
    bias = jnp.where(jnp.asarray(col_ok)[None, None, :, None, :], bias, NEG_INF)
    return bias.reshape(h, wr, GRID_W, wr * GRID_W)


HG_BLOCK = 128


def _hgrn_pass(q_ref, v_ref, z_ref, o_acc, lb, bdb, bdf, st, *, reverse, first):
    c = HGRN_CHUNK
    ncb = HG_BLOCK // c
    nblk = q_ref.shape[0] // HG_BLOCK
    lbm = jnp.maximum(lb, LB_FLOOR)
    one_m_lb = 1.0 - lb
    scale = HEAD_DIM ** -0.5
    t_idx = lax.broadcasted_iota(jnp.int32, (ncb, c, LANES), 1)
    edge = 0 if reverse else c - 1

    def bs(x, s):
        return jnp.broadcast_to(x[:, s:s + 1, :], x.shape)

    def seen(s):
        return (t_idx <= s) if reverse else (t_idx >= s)

    def body(i, st):
        blk = (nblk - 1 - i) if reverse else i
        r0 = pl.multiple_of(blk * HG_BLOCK, HG_BLOCK)
        z = z_ref[pl.ds(r0, HG_BLOCK), :]
        q = q_ref[pl.ds(r0, HG_BLOCK), :] * scale
        v = v_ref[pl.ds(r0, HG_BLOCK), :]
        f = one_m_lb * jax.nn.sigmoid(z) + lbm
        k = one_m_lb * jax.nn.sigmoid(-z) - (lbm - lb)
        logf = jnp.log(f).reshape(ncb, c, LANES)
        q3 = q.reshape(ncb, c, LANES)
        k3 = k.reshape(ncb, c, LANES)
        v3 = v.reshape(ncb, c, LANES)
        cum = jnp.zeros_like(logf)
        for s in range(c):
            cum = cum + jnp.where(seen(s), bs(logf, s), 0.0)
        o3 = jnp.zeros_like(logf)
        for s in range(c):
            d = jnp.where(seen(s), cum - bs(cum, s), NEG_INF)
            w = q3 * bs(k3, s) * jnp.exp(d)
            a = jnp.dot(w.reshape(HG_BLOCK, LANES).astype(BF16), bdb, preferred_element_type=F32)
            o3 = o3 + a.reshape(ncb, c, LANES) * bs(v3, s)
        cum_edge = bs(cum, edge)
        qe = (q3 * jnp.exp(cum)).astype(BF16)
        kd = (k3 * jnp.exp(cum_edge - cum)).astype(BF16)
        vb = v3.astype(BF16)
        a_all = jnp.exp(cum_edge)
        outs = [None] * ncb
        for n in (range(ncb - 1, -1, -1) if reverse else range(ncb)):
            o_inter = lax.dot_general(qe[n], st.astype(BF16), (((1,), (1,)), ((), ())),
                                      preferred_element_type=F32)
            outs[n] = o3[n] + o_inter
            u_t = lax.dot_general(vb[n], kd[n], (((0,), (0,)), ((), ())), preferred_element_type=F32)
            st = a_all[n, 0:1, :] * st + u_t * bdf
        val = jnp.concatenate(outs, axis=0)
        if first:
            o_acc[pl.ds(r0, HG_BLOCK), :] = val
        else:
            o_acc[pl.ds(r0, HG_BLOCK), :] += val
        return st

    return lax.fori_loop(0, nblk, body, st)


def _hgrn_kernel(ql_ref, qc_ref, vl_ref, vc_ref, zfl_ref, zfc_ref, zbl_ref, zbc_ref, gl_ref, gc_ref,
                 lb_ref, gain_ref, bdb_ref, bdf_ref, yl_ref, yc_ref, ol_acc, oc_acc):
    bdb = bdb_ref[...]
    bdf = bdf_ref[...]
    zero = jnp.zeros((LANES, LANES), F32)
    run = functools.partial(_hgrn_pass, bdb=bdb, bdf=bdf)
    st = run(qc_ref, vc_ref, zfc_ref, oc_acc, lb_ref[0, 0], st=zero, reverse=False, first=True)
    run(ql_ref, vl_ref, zfl_ref, ol_acc, lb_ref[0, 0], st=st, reverse=False, first=True)
    st = run(qc_ref, vc_ref, zbc_ref, oc_acc, lb_ref[1, 0], st=zero, reverse=True, first=False)
    run(ql_ref, vl_ref, zbl_ref, ol_acc, lb_ref[1, 0], st=st, reverse=True, first=False)
    for acc, g_ref, y_ref in ((ol_acc, gl_ref, yl_ref), (oc_acc, gc_ref, yc_ref)):
        o = acc[...]
        g = g_ref[...]
        y = o * _seg_inv_rms(o, bdf) * gain_ref[...]
        y_ref[...] = (y * (g * jax.nn.sigmoid(g))).astype(y_ref.dtype)


def hgrn_mixer(pc, lb, gain128, bdb, bdf, *, b, s, l):
    w = pc.shape[1] // 5
    nj = w // LANES
    n_lat = b * s
    lat = lambda m: pl.BlockSpec((s, LANES), lambda i, j: (i, m * nj + j))
    cx = lambda m: pl.BlockSpec((l, LANES), lambda i, j: (n_lat // l + i, m * nj + j))
    in_specs = []
    for m in (0, 1, 2, 3, 4):
        in_specs += [lat(m), cx(m)]
    in_specs += [pl.BlockSpec((2, 1, 1, LANES), lambda i, j: (0, j, 0, 0)),
                 pl.BlockSpec((1, LANES), lambda i, j: (0, 0)),
                 pl.BlockSpec((LANES, LANES), lambda i, j: (0, 0)),
                 pl.BlockSpec((LANES, LANES), lambda i, j: (0, 0))]
    return pl.pallas_call(
        _hgrn_kernel,
        grid=(b, nj),
        in_specs=in_specs,
        out_specs=[pl.BlockSpec((s, LANES), lambda i, j: (i, j)),
                   pl.BlockSpec((l, LANES), lambda i, j: (i, j))],
        out_shape=[jax.ShapeDtypeStruct((n_lat, w), BF16), jax.ShapeDtypeStruct((b * l, w), BF16)],
        scratch_shapes=[pltpu.VMEM((s, LANES), F32), pltpu.VMEM((l, LANES), F32)],
        compiler_params=_cparams("parallel", "parallel"),
        name="hgrn_mixer",
    )(*([pc] * 10), lb, gain128, bdb, bdf)


def _outproj_kernel(x_ref, ma_ref, mb_ref, mc_ref, w_ref, gate_ref, g2_ref, shift_ref, scale_ref,
                    wr_ref, br_ref, xo_ref, h2_ref, lg_ref, *, wa, wb):
    w = w_ref[...]
    y = jnp.dot(ma_ref[...], w[:wa], preferred_element_type=F32)
    y = y + jnp.dot(mb_ref[...], w[wa:wa + wb], preferred_element_type=F32)
    y = y + jnp.dot(mc_ref[...], w[wa + wb:], preferred_element_type=F32)
    x = x_ref[...] + gate_ref[0] * y
    xo_ref[...] = x
    ms = jnp.mean(x * x, axis=-1, keepdims=True)
    h = x * lax.rsqrt(ms + EPS) * g2_ref[0]
    h = h * (1.0 + scale_ref[0]) + shift_ref[0]
    h2_ref[...] = h
    logits = jnp.dot(h, wr_ref[...], precision=HIGHEST, preferred_element_type=F32) + br_ref[...]
    lg_ref[...] = _route_meta(logits)


def out_projection(xall, mix_a, mix_b, mix_c, w_out_bf, mod3, layer_g2, w_route, b_route, *, seq, nseg, tm):
    r, d = xall.shape
    wa, wb, wc = mix_a.shape[1], mix_b.shape[1], mix_c.shape[1]
    tiles_per_seq = seq // tm

    def seg(i):
        return jnp.minimum(i // tiles_per_seq, nseg - 1)

    def modspec(m):
        return pl.BlockSpec((1, 1, d), lambda i: (seg(i) * N_MOD + m, 0, 0))

    row = lambda wdt: pl.BlockSpec((tm, wdt), lambda i: (i, 0))
    kern = functools.partial(_outproj_kernel, wa=wa, wb=wb)
    return pl.pallas_call(
        kern,
        grid=(r // tm,),
        in_specs=[row(d), row(wa), row(wb), row(wc),
                  pl.BlockSpec((wa + wb + wc, d), lambda i: (0, 0)),
                  modspec(2),
                  pl.BlockSpec((1, d), lambda i: (0, 0)),
                  modspec(3), modspec(4),
                  pl.BlockSpec((d, LANES), lambda i: (0, 0)),
                  pl.BlockSpec((1, LANES), lambda i: (0, 0))],
        out_specs=[row(d), row(d), row(LANES)],
        out_shape=[jax.ShapeDtypeStruct((r, d), F32),
                   jax.ShapeDtypeStruct((r, d), F32),
                   jax.ShapeDtypeStruct((r, LANES), F32)],
        compiler_params=_cparams("parallel"),
        name="out_projection",
    )(xall, mix_a, mix_b, mix_c, w_out_bf, mod3, layer_g2, mod3, mod3, w_route, b_route)


PAIRS_PER_GROUP = EXPERTS_PER_GROUP * (EXPERTS_PER_GROUP - 1) // 2
N_BUCKETS = N_GROUPS * PAIRS_PER_GROUP
ROW_TILE = 256
META_BUCKET, META_WA, META_WB = 0, 1, 2


def _bucket_experts():
    ea = np.zeros((LANES,), np.int32)
    eb = np.zeros((LANES,), np.int32)
    for g in range(N_GROUPS):
        k = g * PAIRS_PER_GROUP
        for a in range(EXPERTS_PER_GROUP):
            for b in range(a + 1, EXPERTS_PER_GROUP):
                ea[k], eb[k] = g * EXPERTS_PER_GROUP + a, g * EXPERTS_PER_GROUP + b
                k += 1
    return ea, eb


def _route_meta(logits):
    lane = lax.broadcasted_iota(jnp.int32, logits.shape, 1).astype(F32)
    is_g = lane < N_GROUPS
    gl = jnp.where(is_g, logits, -jnp.inf)
    gmax = jnp.max(gl, axis=-1, keepdims=True)
    g_idx = jnp.min(jnp.where(gl == gmax, lane, LANES), axis=-1, keepdims=True)
    gsum = jnp.sum(jnp.where(is_g, jnp.exp(gl - gmax), 0.0), axis=-1, keepdims=True)
    g_top = 1.0 / gsum
    lo = N_GROUPS + g_idx * EXPERTS_PER_GROUP
    in_grp = (lane >= lo) & (lane < lo + EXPERTS_PER_GROUP)
    el = jnp.where(in_grp, logits, -jnp.inf)
    m1 = jnp.max(el, axis=-1, keepdims=True)
    i1 = jnp.min(jnp.where(el == m1, lane, LANES), axis=-1, keepdims=True)
    el2 = jnp.where(lane == i1, -jnp.inf, el)
    m2 = jnp.max(el2, axis=-1, keepdims=True)
    i2 = jnp.min(jnp.where(el2 == m2, lane, LANES), axis=-1, keepdims=True)
    e21 = jnp.exp(m2 - m1)
    w1 = g_top / (1.0 + e21)
    w2 = e21 * w1
    first_low = i1 < i2
    la = jnp.minimum(i1, i2) - lo
    lb = jnp.maximum(i1, i2) - lo
    pair = la * (2 * EXPERTS_PER_GROUP - 1 - la) * 0.5 + (lb - la - 1.0)
    bucket = g_idx * PAIRS_PER_GROUP + pair
    wa = jnp.where(first_low, w1, w2)
    wb = jnp.where(first_low, w2, w1)
    return jnp.where(lane == META_BUCKET, bucket,
                     jnp.where(lane == META_WA, wa, jnp.where(lane == META_WB, wb, 0.0)))


def _plan_kernel(meta_ref, tri_ref, ids_ref, rank_ref, cnt_ref, carry):
    @pl.when(pl.program_id(0) == 0)
    def _():
        carry[...] = jnp.zeros_like(carry)

    ids = meta_ref[...].T[META_BUCKET:META_BUCKET + 1, :]
    sub = lax.broadcasted_iota(jnp.int32, (LANES, ids.shape[1]), 0).astype(F32)
    onehot = (sub == ids).astype(F32)
    before = jnp.dot(onehot.astype(BF16), tri_ref[...], preferred_element_type=F32)
    rank = jnp.sum(onehot * (before + carry[...]), axis=0, keepdims=True)
    ids_ref[0] = ids.astype(jnp.int32)
    rank_ref[0] = rank.astype(jnp.int32)
    total = carry[...] + jnp.sum(onehot, axis=1, keepdims=True)
    carry[...] = total
    cnt_ref[...] = total.astype(jnp.int32)


def route_plan(meta, tm):
    r = meta.shape[0]
    nt = r // tm
    i = np.arange(tm)
    tri = jnp.asarray(i[:, None] < i[None, :], BF16)
    ids, rank, cnt = pl.pallas_call(
        _plan_kernel,
        grid=(nt,),
        in_specs=[pl.BlockSpec((tm, LANES), lambda i: (i, 0)),
                  pl.BlockSpec((tm, tm), lambda i: (0, 0))],
        out_specs=[pl.BlockSpec((1, 1, tm), lambda i: (i, 0, 0)),
                   pl.BlockSpec((1, 1, tm), lambda i: (i, 0, 0)),
                   pl.BlockSpec((LANES, 1), lambda i: (0, 0))],
        out_shape=[jax.ShapeDtypeStruct((nt, 1, tm), jnp.int32),
                   jax.ShapeDtypeStruct((nt, 1, tm), jnp.int32),
                   jax.ShapeDtypeStruct((LANES, 1), jnp.int32)],
        scratch_shapes=[pltpu.VMEM((LANES, 1), F32)],
        compiler_params=_cparams("arbitrary"),
        name="route_plan",
    )(meta, tri)
    return ids.reshape(r), rank.reshape(r), cnt.reshape(LANES)


def _row_copy(src, dst, i, j, sem):
    return pltpu.make_async_copy(src.at[pl.ds(i, 1)], dst.at[pl.ds(j, 1)], sem)


def _dispatch_kernel(pos_ref, h_ref, init_hbm, o_hbm, sem):
    del init_hbm
    ch = h_ref.shape[0]
    base = pl.program_id(0) * ch

    def issue(i, c):
        _row_copy(h_ref, o_hbm, i, pos_ref[base + i], sem).start()
        return c

    def drain(i, c):
        _row_copy(h_ref, o_hbm, i, 0, sem).wait()
        return c

    lax.fori_loop(0, ch, issue, 0, unroll=8)
    lax.fori_loop(0, ch, drain, 0, unroll=8)


def dispatch_rows(pos, h2, n_rows, ch):
    r, d = h2.shape
    return pl.pallas_call(
        _dispatch_kernel,
        grid_spec=pltpu.PrefetchScalarGridSpec(
            num_scalar_prefetch=1,
            grid=(r // ch,),
            in_specs=[pl.BlockSpec((ch, d), lambda i, p: (i, 0)), pl.BlockSpec(memory_space=pl.ANY)],
            out_specs=pl.BlockSpec(memory_space=pl.ANY),
            scratch_shapes=[pltpu.SemaphoreType.DMA],
        ),
        out_shape=jax.ShapeDtypeStruct((n_rows, d), h2.dtype),
        input_output_aliases={2: 0},
        compiler_params=_cparams("arbitrary"),
        name="dispatch_rows",
    )(pos, h2, jnp.zeros((n_rows, d), h2.dtype))


def _expert_kernel(ta_ref, tb_ref, nu_ref, hs_ref, wga_ref, wua_ref, wda_ref, wgb_ref, wub_ref, wdb_ref,
                   y_ref, gu_a, dn_a, gu_b, dn_b):
    j = pl.program_id(0)
    d = hs_ref.shape[1]
    ff = wga_ref.shape[2]
    prev = jnp.maximum(j - 1, 0)

    def refresh(gu, dn, wg_ref, wu_ref, wd_ref):
        gu[:, :ff] = wg_ref[0].astype(BF16)
        gu[:, ff:] = wu_ref[0].astype(BF16)
        dn[...] = wd_ref[0].astype(BF16)

    @pl.when((j == 0) | (ta_ref[j] != ta_ref[prev]))
    def _():
        refresh(gu_a, dn_a, wga_ref, wua_ref, wda_ref)

    @pl.when((j == 0) | (tb_ref[j] != tb_ref[prev]))
    def _():
        refresh(gu_b, dn_b, wgb_ref, wub_ref, wdb_ref)

    @pl.when(j < nu_ref[0])
    def _():
        h = hs_ref[...].astype(BF16)
        for n, (gu, dn) in enumerate(((gu_a, dn_a), (gu_b, dn_b))):
            hgu = jnp.dot(h, gu[...], preferred_element_type=F32)
            hg = hgu[:, :ff]
            hid = (hg * jax.nn.sigmoid(hg)) * hgu[:, ff:]
            y_ref[:, n * d:(n + 1) * d] = jnp.dot(hid.astype(BF16), dn[...], preferred_element_type=F32)

    @pl.when(j >= nu_ref[0])
    def _():
        y_ref[...] = jnp.zeros_like(y_ref)


def expert_pairs(tile_a, tile_b, n_used, hs, w_gate, w_up, w_down):
    rows, d = hs.shape
    _, _, ff = w_gate.shape
    nt = rows // ROW_TILE
    blk = lambda i, ta, tb, nu: (jnp.minimum(i, nu[0]), 0)
    wsel = lambda which, shape: pl.BlockSpec(
        shape, (lambda i, ta, tb, nu: (ta[i], 0, 0)) if which == 0 else (lambda i, ta, tb, nu: (tb[i], 0, 0)))
    return pl.pallas_call(
        _expert_kernel,
        grid_spec=pltpu.PrefetchScalarGridSpec(
            num_scalar_prefetch=3,
            grid=(nt,),
            in_specs=[pl.BlockSpec((ROW_TILE, d), blk),
                      wsel(0, (1, d, ff)), wsel(0, (1, d, ff)), wsel(0, (1, ff, d)),
                      wsel(1, (1, d, ff)), wsel(1, (1, d, ff)), wsel(1, (1, ff, d))],
            out_specs=pl.BlockSpec((ROW_TILE, 2 * d), lambda i, ta, tb, nu: (i, 0)),
            scratch_shapes=[pltpu.VMEM((d, 2 * ff), BF16), pltpu.VMEM((ff, d), BF16),
                            pltpu.VMEM((d, 2 * ff), BF16), pltpu.VMEM((ff, d), BF16)],
        ),
        out_shape=jax.ShapeDtypeStruct((rows, 2 * d), F32),
        compiler_params=_cparams("arbitrary"),
        name="expert_pairs",
    )(tile_a, tile_b, n_used, hs, w_gate, w_up, w_down, w_gate, w_up, w_down)


def _combine_kernel(pos_ref, x_ref, meta_ref, gate_ref, y_hbm, o_ref, ybuf, sem):
    tm = x_ref.shape[0]
    d = x_ref.shape[1]
    base = pl.program_id(0) * tm

    def issue(i, c):
        _row_copy(y_hbm, ybuf, pos_ref[base + i], i, sem).start()
        return c

    def drain(i, c):
        _row_copy(y_hbm, ybuf, 0, i, sem).wait()
        return c

    lax.fori_loop(0, tm, issue, 0, unroll=8)
    lax.fori_loop(0, tm, drain, 0, unroll=8)
    meta = meta_ref[...]
    ff = (meta[:, META_WA:META_WA + 1] * ybuf[:, :d] + meta[:, META_WB:META_WB + 1] * ybuf[:, d:])
    o_ref[...] = x_ref[...] + gate_ref[0] * ff


def combine_rows(pos, x_mid, meta, mod3, y, *, seq, nseg, tm):
    r, d = x_mid.shape
    tiles_per_seq = seq // tm
    seg = lambda i: jnp.minimum(i // tiles_per_seq, nseg - 1)
    return pl.pallas_call(
        _combine_kernel,
        grid_spec=pltpu.PrefetchScalarGridSpec(
            num_scalar_prefetch=1,
            grid=(r // tm,),
            in_specs=[pl.BlockSpec((tm, d), lambda i, p: (i, 0)),
                      pl.BlockSpec((tm, LANES), lambda i, p: (i, 0)),
                      pl.BlockSpec((1, 1, d), lambda i, p: (seg(i) * N_MOD + 5, 0, 0)),
                      pl.BlockSpec(memory_space=pl.ANY)],
            out_specs=pl.BlockSpec((tm, d), lambda i, p: (i, 0)),
            scratch_shapes=[pltpu.VMEM((tm, 2 * d), F32), pltpu.SemaphoreType.DMA],
        ),
        out_shape=jax.ShapeDtypeStruct((r, d), F32),
        compiler_params=_cparams("arbitrary"),
        name="combine_rows",
    )(pos, x_mid, meta, mod3, y)


def moe_block(x_mid, h2, meta, mod3, w_gate, w_up, w_down, *, seq, nseg):
    r, d = x_mid.shape
    ids, rank, cnt = route_plan(meta, 512)
    tiles = (cnt + (ROW_TILE - 1)) // ROW_TILE
    incl = jnp.cumsum(tiles)
    n_used = incl[-1]
    lookup = lambda table, idx: jnp.sum(jnp.where(idx[:, None] == jnp.arange(LANES)[None, :], table[None, :], 0), axis=1)
    pos = lookup(incl - tiles, ids) * ROW_TILE + rank
    nt = r // ROW_TILE + N_BUCKETS
    last = jnp.minimum(jnp.arange(nt), n_used - 1)
    tile_bucket = jnp.sum((incl[None, :] <= last[:, None]).astype(jnp.int32), axis=1)
    ea, eb = _bucket_experts()
    tile_a = lookup(jnp.asarray(ea), tile_bucket)
    tile_b = lookup(jnp.asarray(eb), tile_bucket)
    hs = dispatch_rows(pos, h2, nt * ROW_TILE, 512)
    y = expert_pairs(tile_a, tile_b, n_used.reshape(1).astype(jnp.int32), hs, w_gate, w_up, w_down)
    return combine_rows(pos, x_mid, meta, mod3, y, seq=seq, nseg=nseg, tm=256)


def _rope_tables(seq, tm):
    t = np.arange(seq)
    row = (t // GRID_W).astype(np.float32)
    col = (t % GRID_W).astype(np.float32)
    half = HEAD_DIM // 2
    inv = jnp.asarray(ROPE_THETA, F32) ** (-jnp.arange(0, half, 2, dtype=F32) / half)
    ang = jnp.concatenate([jnp.asarray(row)[:, None] * inv, jnp.asarray(col)[:, None] * inv], axis=-1)
    cos = jnp.repeat(jnp.cos(ang), 2, axis=-1)
    sin = jnp.repeat(jnp.sin(ang), 2, axis=-1) * jnp.asarray(np.tile([-1.0, 1.0], half), F32)
    cos = jnp.tile(cos, (1, LANES // HEAD_DIM))
    sin = jnp.tile(sin, (1, LANES // HEAD_DIM))
    cos = jnp.concatenate([cos, jnp.ones((tm, LANES), F32)], axis=0)
    sin = jnp.concatenate([sin, jnp.zeros((tm, LANES), F32)], axis=0)
    return cos, sin


def kernel(x, c, ctx, c_ctx, w_ada, b_ada, norm1_g, w_in, na_q_norm, na_k_norm, na_rpb, gqa_q_norm, gqa_k_norm, hgrn_lb, hgrn_o_norm, w_out, norm2_g, w_route_group, b_route_group, w_route_expert, b_route_expert, w_exp_gate, w_exp_up, w_exp_down):
    b, s, d = x.shape
    l = ctx.shape[1]
    depth = w_ada.shape[0]
    assert s % 512 == 0 and (b * l) % 512 == 0 and s // GRID_W >= NA_WIN_R
    assert s % l == 0 and l % HG_BLOCK == 0
    nseg = b + 1
    n_lat = b * s
    n_ctx = b * l
    na_w, gq_qw = d // 4, d // 2
    gq_kw = gq_qw // 4
    hg_w = d // 4
    tm = 512

    c_all = jnp.zeros((16, d), F32).at[:b].set(c).at[b].set(c_ctx)
    mod = ada_mod(c_all, w_ada, b_ada)
    cos_t, sin_t = _rope_tables(s, tm)
    bd_f = _block_diag_ones(LANES, HEAD_DIM, F32)
    bd_b = _block_diag_ones(LANES, HEAD_DIM, BF16)
    p_lb = jax.nn.softmax(hgrn_lb.astype(F32), axis=0)
    lb_all = jnp.cumsum(p_lb, axis=0) - p_lb[0]
    tile2 = lambda g: jnp.tile(g, LANES // HEAD_DIM)

    xall = jnp.concatenate([x.reshape(n_lat, d), ctx.reshape(n_ctx, d)], axis=0)
    for layer in range(depth):
        ctx_out = layer < depth - 1
        mod3 = mod[layer].reshape(16 * N_MOD, 1, d)
        gains = jnp.zeros((8, LANES), F32)
        gains = gains.at[0].set(tile2(na_q_norm[layer])).at[1].set(tile2(na_k_norm[layer]))
        gains = gains.at[2].set(tile2(gqa_q_norm[layer])).at[3].set(tile2(gqa_k_norm[layer]))
        pa, pb, pc = in_projection(xall, mod3, norm1_g[layer][None], w_in[layer].astype(BF16), cos_t, sin_t,
                                   gains, bd_f, n_lat_rows=n_lat, seq=s, nseg=nseg, tm=tm)
        bias_tab = na_bias_table(na_rpb[layer], s // GRID_W)
        o_a = na_attention(pa, bias_tab, b=b, s=s, l=l, na_w=na_w)
        o_b = gqa_attention(pb, b=b, s=s, l=l, qw=gq_qw, kw=gq_kw, tq=128)
        lb4 = lb_all[layer].reshape(2, hg_w // LANES, 1, LANES)
        y_lat, y_ctx = hgrn_mixer(pc, lb4, tile2(hgrn_o_norm[layer])[None], bd_b, bd_f, b=b, s=s, l=l)

        w_route = jnp.zeros((d, LANES), F32).at[:, :N_GROUPS].set(w_route_group[layer])
        w_route = w_route.at[:, N_GROUPS:N_GROUPS + N_EXPERTS].set(w_route_expert[layer])
        b_route = jnp.zeros((1, LANES), F32).at[0, :N_GROUPS].set(b_route_group[layer])
        b_route = b_route.at[0, N_GROUPS:N_GROUPS + N_EXPERTS].set(b_route_expert[layer])
        if ctx_out:
            o_ac, o_bc = ctx_attention(pa, pb, b=b, s=s, l=l, na_w=na_w, qw=gq_qw, kw=gq_kw)
            mix_a = jnp.concatenate([o_a, o_ac], axis=0)
            mix_b = jnp.concatenate([o_b, o_bc], axis=0)
            y_c = jnp.concatenate([y_lat, y_ctx], axis=0)
            x_in = xall
        else:
            mix_a, mix_b, y_c, x_in = o_a, o_b, y_lat, xall[:n_lat]
        x_mid, h2, meta = out_projection(x_in, mix_a, mix_b, y_c, w_out[layer].astype(BF16), mod3,
                                          norm2_g[layer][None], w_route, b_route, seq=s, nseg=nseg, tm=tm)
        xall = moe_block(x_mid, h2, meta, mod3, w_exp_gate[layer], w_exp_up[layer], w_exp_down[layer],
                         seq=s, nseg=nseg)
    return xall[:n_lat].reshape(b, s, d)
```

```python
import functools

import jax
import jax.numpy as jnp
import numpy as np
from jax import lax
from jax.experimental import pallas as pl
from jax.experimental.pallas import tpu as pltpu

F32 = jnp.float32
BF16 = jnp.bfloat16
HIGHEST = lax.Precision.HIGHEST

HEAD_DIM = 64
GRID_W = 64
NA_WIN_R = 8
NA_WIN_C = 16
ROPE_THETA = 10000.0
HGRN_CHUNK = 16
N_GROUPS = 4
EXPERTS_PER_GROUP = 8
N_EXPERTS = N_GROUPS * EXPERTS_PER_GROUP
N_MOD = 6
EPS = 1e-6
NEG_INF = -1e30
LB_FLOOR = 1e-20
LANES = 128
VMEM_LIMIT = 56 * 1024 * 1024


def _cparams(*sem):
    return pltpu.CompilerParams(dimension_semantics=sem, vmem_limit_bytes=VMEM_LIMIT)


def _block_diag_ones(n, blk, dtype):
    i = np.arange(n)
    return jnp.asarray((i[:, None] // blk) == (i[None, :] // blk), dtype=dtype)


def _ada_kernel(c_ref, w_ref, b_ref, o_ref):
    c = c_ref[...]
    s = c * jax.nn.sigmoid(c)
    o_ref[0] = jnp.dot(s, w_ref[0], precision=HIGHEST, preferred_element_type=F32) + b_ref[0]


def ada_mod(c_all, w_ada, b_ada):
    depth, d, n = w_ada.shape
    tn = 1536
    return pl.pallas_call(
        _ada_kernel,
        grid=(depth, n // tn),
        in_specs=[
            pl.BlockSpec((16, d), lambda l, j: (0, 0)),
            pl.BlockSpec((1, d, tn), lambda l, j: (l, 0, j)),
            pl.BlockSpec((1, 1, tn), lambda l, j: (l, 0, j)),
        ],
        out_specs=pl.BlockSpec((1, 16, tn), lambda l, j: (l, 0, j)),
        out_shape=jax.ShapeDtypeStruct((depth, 16, n), F32),
        compiler_params=_cparams("parallel", "parallel"),
        name="ada_mod",
    )(c_all, w_ada, b_ada.reshape(depth, 1, n))


def _seg_inv_rms(x, bd):
    ss = jnp.dot(x * x, bd, precision=HIGHEST, preferred_element_type=F32)
    return lax.rsqrt(ss * (1.0 / HEAD_DIM) + EPS)


def _pair_swap(x):
    lane = lax.broadcasted_iota(jnp.int32, x.shape, 1)
    return jnp.where((lane & 1) == 0, pltpu.roll(x, LANES - 1, 1), pltpu.roll(x, 1, 1))


def _inproj_kernel(x_ref, g1_ref, shift_ref, scale_ref, w_ref, cos_ref, sin_ref, gains_ref, bd_ref,
                   oa_ref, ob_ref, oc_ref, *, na_w, gq_qw, gq_kw):
    x = x_ref[...]
    ms = jnp.mean(x * x, axis=-1, keepdims=True)
    h = x * lax.rsqrt(ms + EPS) * g1_ref[0]
    h = h * (1.0 + scale_ref[0]) + shift_ref[0]
    p = jnp.dot(h.astype(BF16), w_ref[...], preferred_element_type=F32)
    bd = bd_ref[...]
    cos = cos_ref[...]
    sin = sin_ref[...]
    qscale = HEAD_DIM ** -0.5

    def normed(col, gain_row):
        xb = p[:, col:col + LANES]
        return xb * _seg_inv_rms(xb, bd) * gains_ref[gain_row:gain_row + 1, :]

    def rope(xn):
        return xn * cos + _pair_swap(xn) * sin

    for j in range(na_w // LANES):
        c = j * LANES
        oa_ref[:, c:c + LANES] = (normed(c, 0) * qscale).astype(BF16)
        oa_ref[:, na_w + c:na_w + c + LANES] = normed(na_w + c, 1).astype(BF16)
    oa_ref[:, 2 * na_w:3 * na_w] = p[:, 2 * na_w:3 * na_w].astype(BF16)
    b0 = 3 * na_w
    for j in range(gq_qw // LANES):
        c = j * LANES
        ob_ref[:, c:c + LANES] = (rope(normed(b0 + c, 2)) * qscale).astype(BF16)
    for j in range(gq_kw // LANES):
        c = gq_qw + j * LANES
        ob_ref[:, c:c + LANES] = rope(normed(b0 + c, 3)).astype(BF16)
    ob_ref[:, gq_qw + gq_kw:] = p[:, b0 + gq_qw + gq_kw:b0 + gq_qw + 2 * gq_kw].astype(BF16)
    oc_ref[...] = p[:, b0 + gq_qw + 2 * gq_kw:]


def in_projection(xall, mod3, layer_g1, w_in_bf, cos_t, sin_t, gains, bd, *, n_lat_rows, seq, nseg, tm):
    r, d = xall.shape
    d_in = w_in_bf.shape[1]
    na_w = d // 4
    gq_qw = d // 2
    gq_kw = gq_qw // 4
    c_w = d_in - 3 * na_w - gq_qw - 2 * gq_kw
    lat_tiles = n_lat_rows // tm
    tiles_per_seq = seq // tm

    def seg(i):
        return jnp.minimum(i // tiles_per_seq, nseg - 1)

    def rope_blk(i):
        return jnp.where(i < lat_tiles, i % tiles_per_seq, tiles_per_seq)

    kern = functools.partial(_inproj_kernel, na_w=na_w, gq_qw=gq_qw, gq_kw=gq_kw)
    return pl.pallas_call(
        kern,
        grid=(r // tm,),
        in_specs=[
            pl.BlockSpec((tm, d), lambda i: (i, 0)),
            pl.BlockSpec((1, d), lambda i: (0, 0)),
            pl.BlockSpec((1, 1, d), lambda i: (seg(i) * N_MOD + 0, 0, 0)),
            pl.BlockSpec((1, 1, d), lambda i: (seg(i) * N_MOD + 1, 0, 0)),
            pl.BlockSpec((d, d_in), lambda i: (0, 0)),
            pl.BlockSpec((tm, LANES), lambda i: (rope_blk(i), 0)),
            pl.BlockSpec((tm, LANES), lambda i: (rope_blk(i), 0)),
            pl.BlockSpec((8, LANES), lambda i: (0, 0)),
            pl.BlockSpec((LANES, LANES), lambda i: (0, 0)),
        ],
        out_specs=[
            pl.BlockSpec((tm, 3 * na_w), lambda i: (i, 0)),
            pl.BlockSpec((tm, gq_qw + 2 * gq_kw), lambda i: (i, 0)),
            pl.BlockSpec((tm, c_w), lambda i: (i, 0)),
        ],
        out_shape=[
            jax.ShapeDtypeStruct((r, 3 * na_w), BF16),
            jax.ShapeDtypeStruct((r, gq_qw + 2 * gq_kw), BF16),
            jax.ShapeDtypeStruct((r, c_w), F32),
        ],
        compiler_params=_cparams("parallel"),
        name="in_projection",
    )(xall, layer_g1, mod3, mod3, w_in_bf, cos_t, sin_t, gains, bd)


def _head(j):
    return slice(j * HEAD_DIM, (j + 1) * HEAD_DIM)


def _with_ones(v):
    return jnp.concatenate([v, jnp.ones_like(v)], axis=1)


def _scores(q, k):
    return lax.dot_general(q, k, (((1,), (1,)), ((), ())), preferred_element_type=F32)


def _normalise(o):
    return o[:, :HEAD_DIM] / o[:, HEAD_DIM:HEAD_DIM + 1]


def _softmax_attend(q, k, v1):
    s = _scores(q, k)
    p = jnp.exp((s - jnp.max(s, axis=-1, keepdims=True)).astype(BF16))
    return _normalise(jnp.dot(p, v1, preferred_element_type=F32))


def _grouped_attend(q_ref, k_of, v_of, n_kv, grp):
    t = q_ref.shape[0]
    outs = []
    for j in range(n_kv):
        q4 = jnp.concatenate([q_ref[:, _head(j * grp + g)] for g in range(grp)], axis=0)
        o = _softmax_attend(q4, k_of(j), v_of(j))
        outs += [o[g * t:(g + 1) * t] for g in range(grp)]
    return jnp.concatenate(outs, axis=1)


def _gqa_kernel(q_ref, kl_ref, vl_ref, kc_ref, vc_ref, o_ref, k_s, v_s, *, n_kv, grp):
    s_len = kl_ref.shape[0]

    @pl.when(pl.program_id(1) == 0)
    def _():
        for j in range(n_kv):
            k_s[j, :s_len, :] = kl_ref[:, _head(j)]
            k_s[j, s_len:, :] = kc_ref[:, _head(j)]
            v_s[j, :s_len, :] = _with_ones(vl_ref[:, _head(j)])
            v_s[j, s_len:, :] = _with_ones(vc_ref[:, _head(j)])

    o = _grouped_attend(q_ref, lambda j: k_s[j], lambda j: v_s[j], n_kv, grp)
    o_ref[...] = o.astype(o_ref.dtype)


def gqa_attention(pb, *, b, s, l, qw, kw, tq):
    n_lat = b * s
    n_kv = kw // HEAD_DIM
    grp = qw // kw
    assert kw == LANES and qw % kw == 0
    kcol, vcol = qw // kw, qw // kw + 1
    kern = functools.partial(_gqa_kernel, n_kv=n_kv, grp=grp)
    return pl.pallas_call(
        kern,
        grid=(b, s // tq),
        in_specs=[
            pl.BlockSpec((tq, qw), lambda i, j: (i * (s // tq) + j, 0)),
            pl.BlockSpec((s, kw), lambda i, j: (i, kcol)),
            pl.BlockSpec((s, kw), lambda i, j: (i, vcol)),
            pl.BlockSpec((l, kw), lambda i, j: (n_lat // l + i, kcol)),
            pl.BlockSpec((l, kw), lambda i, j: (n_lat // l + i, vcol)),
        ],
        out_specs=pl.BlockSpec((tq, qw), lambda i, j: (i * (s // tq) + j, 0)),
        out_shape=jax.ShapeDtypeStruct((n_lat, qw), BF16),
        scratch_shapes=[pltpu.VMEM((n_kv, s + l, HEAD_DIM), BF16), pltpu.VMEM((n_kv, s + l, 2 * HEAD_DIM), BF16)],
        compiler_params=_cparams("parallel", "arbitrary"),
        name="gqa_attention",
    )(pb, pb, pb, pb, pb)


def _ctx_attn_kernel(qa_ref, ka_ref, va_ref, qb_ref, kb_ref, vb_ref, oa_ref, ob_ref, *, n_kv, grp):
    na_h = qa_ref.shape[1] // HEAD_DIM
    oa = [_softmax_attend(qa_ref[:, _head(h)], ka_ref[:, _head(h)], _with_ones(va_ref[:, _head(h)]))
          for h in range(na_h)]
    oa_ref[...] = jnp.concatenate(oa, axis=1).astype(oa_ref.dtype)
    ob = _grouped_attend(qb_ref, lambda j: kb_ref[:, _head(j)], lambda j: _with_ones(vb_ref[:, _head(j)]),
                         n_kv, grp)
    ob_ref[...] = ob.astype(ob_ref.dtype)


def ctx_attention(pa, pb, *, b, s, l, na_w, qw, kw):
    r0 = (b * s) // l
    grp = qw // kw
    kern = functools.partial(_ctx_attn_kernel, n_kv=kw // HEAD_DIM, grp=grp)
    a_spec = lambda m: pl.BlockSpec((l, na_w), lambda i: (r0 + i, m))
    return pl.pallas_call(
        kern,
        grid=(b,),
        in_specs=[a_spec(0), a_spec(1), a_spec(2),
                  pl.BlockSpec((l, qw), lambda i: (r0 + i, 0)),
                  pl.BlockSpec((l, kw), lambda i: (r0 + i, grp)),
                  pl.BlockSpec((l, kw), lambda i: (r0 + i, grp + 1))],
        out_specs=[pl.BlockSpec((l, na_w), lambda i: (i, 0)), pl.BlockSpec((l, qw), lambda i: (i, 0))],
        out_shape=[jax.ShapeDtypeStruct((b * l, na_w), BF16), jax.ShapeDtypeStruct((b * l, qw), BF16)],
        compiler_params=_cparams("parallel"),
        name="ctx_attention",
    )(pa, pa, pa, pb, pb, pb)


NA_QROWS = 4
NA_UNION = NA_WIN_R + NA_QROWS


def _na_block_geometry(rows):
    wu = min(rows, NA_UNION)
    wr = min(NA_WIN_R, rows)
    nblk = rows // NA_QROWS
    sig, u0s = [], []
    for blk in range(nblk):
        r0 = blk * NA_QROWS
        u0 = int(np.clip(r0 - wr // 2, 0, rows - wu))
        u0s.append(u0)
        sig.append(tuple((r0 + j - u0, int(np.clip(r0 + j - wr // 2, 0, rows - wr)) - u0) for j in range(NA_QROWS)))
    cls = [int(blk > 0) + int(blk == nblk - 1) for blk in range(nblk)]
    reps = {}
    for blk in range(nblk):
        assert reps.setdefault(cls[blk], sig[blk]) == sig[blk]
    return wu, wr, [reps.get(c, reps[0]) for c in range(3)]


def _na_kernel(q_ref, k_ref, v_ref, kc_ref, vc_ref, bias_ref, o_ref, k_s, v_s, kc_s, vc_s, *, rows, wu, wr):
    nh = q_ref.shape[1] // HEAD_DIM
    nblk = rows // NA_QROWS
    nq = NA_QROWS * GRID_W
    for h in range(nh):
        k_s[h] = k_ref[:, _head(h)]
        v_s[h] = _with_ones(v_ref[:, _head(h)])
        kc_s[h] = kc_ref[:, _head(h)]
        vc_s[h] = _with_ones(vc_ref[:, _head(h)])

    def body(blk, carry):
        u0 = jnp.clip(blk * NA_QROWS - wr // 2, 0, rows - wu)
        cls = jnp.minimum(blk, 1) + jnp.maximum(blk - (nblk - 2), 0)
        q_rows = pl.ds(pl.multiple_of(blk * nq, nq), nq)
        k_rows = pl.ds(pl.multiple_of(u0 * GRID_W, GRID_W), wu * GRID_W)
        outs = []
        for h in range(nh):
            q = q_ref[q_rows, _head(h)]
            s_nb = _scores(q, k_s[h, k_rows, :]) + bias_ref[h, cls]
            s_cx = _scores(q, kc_s[h])
            m = jnp.maximum(jnp.max(s_nb, axis=-1, keepdims=True), jnp.max(s_cx, axis=-1, keepdims=True))
            p_nb = jnp.exp((s_nb - m).astype(BF16))
            p_cx = jnp.exp((s_cx - m).astype(BF16))
            outs.append(_normalise(jnp.dot(p_nb, v_s[h, k_rows, :], preferred_element_type=F32)
                                   + jnp.dot(p_cx, vc_s[h], preferred_element_type=F32)))
        o_ref[q_rows, :] = jnp.concatenate(outs, axis=1).astype(o_ref.dtype)
        return carry

    lax.fori_loop(0, nblk, body, 0)


def na_attention(pa, bias_tab, *, b, s, l, na_w):
    n_lat = b * s
    nh = na_w // HEAD_DIM
    rows = s // GRID_W
    wu, wr, _ = _na_block_geometry(rows)
    kern = functools.partial(_na_kernel, rows=rows, wu=wu, wr=wr)
    lat = lambda m: pl.BlockSpec((s, na_w), lambda i: (i, m))
    cx = lambda m: pl.BlockSpec((l, na_w), lambda i: (n_lat // l + i, m))
    return pl.pallas_call(
        kern,
        grid=(b,),
        in_specs=[lat(0), lat(1), lat(2), cx(1), cx(2),
                  pl.BlockSpec(bias_tab.shape, lambda i: (0, 0, 0, 0))],
        out_specs=pl.BlockSpec((s, na_w), lambda i: (i, 0)),
        out_shape=jax.ShapeDtypeStruct((n_lat, na_w), BF16),
        scratch_shapes=[pltpu.VMEM((nh, s, HEAD_DIM), BF16), pltpu.VMEM((nh, s, 2 * HEAD_DIM), BF16),
                        pltpu.VMEM((nh, l, HEAD_DIM), BF16), pltpu.VMEM((nh, l, 2 * HEAD_DIM), BF16)],
        compiler_params=_cparams("parallel"),
        name="na_attention",
    )(pa, pa, pa, pa, pa, bias_tab)


def na_bias_table(rpb, rows):
    wu, wr, reps = _na_block_geometry(rows)
    h = rpb.shape[0]
    r_off = np.array([[rj for rj, _ in rep] for rep in reps])
    s_off = np.array([[sj for _, sj in rep] for rep in reps])
    kr = np.arange(wu)
    row_ok = (kr >= s_off[..., None]) & (kr < s_off[..., None] + wr)
    dr = kr - r_off[..., None] + (NA_WIN_R - 1)
    cidx = np.arange(GRID_W)
    col_start = np.clip(cidx - NA_WIN_C // 2, 0, GRID_W - NA_WIN_C)
    col_ok = (cidx[None, :] >= col_start[:, None]) & (cidx[None, :] < col_start[:, None] + NA_WIN_C)
    dc = np.clip(cidx[None, :] - cidx[:, None] + (NA_WIN_C - 1), 0, 2 * NA_WIN_C - 2)
    sel_r = jnp.asarray((dr[..., None] == np.arange(2 * NA_WIN_R - 1)) & row_ok[..., None], F32)
    sel_c = jnp.asarray(dc[:, :, None] == np.arange(2 * NA_WIN_C - 1), F32)
    bias = jnp.einsum("cjki,hid->hcjkd", sel_r, rpb.astype(F32), precision=HIGHEST)
    bias = jnp.einsum("hcjkd,qxd->hcjqkx", bias, sel_c, precision=HIGHEST)
    ok = row_ok[None, :, :, None, :, None] & col_ok[None, None, None, :, None, :]
    bias = jnp.where(jnp.asarray(ok), bias, NEG_INF)
    return bias.reshape(h, len(reps), NA_QROWS * GRID_W, wu * GRID_W)


HG_BLOCK = 128


def _hgrn_pass(q_ref, v_ref, z_ref, o_acc, lb, bdb, bdf, st, *, reverse, first):
    c = HGRN_CHUNK
    ncb = HG_BLOCK // c
    nblk = q_ref.shape[0] // HG_BLOCK
    lbm = jnp.maximum(lb, LB_FLOOR)
    one_m_lb = 1.0 - lb
    scale = HEAD_DIM ** -0.5
    t_idx = lax.broadcasted_iota(jnp.int32, (ncb, c, LANES), 1)
    edge = 0 if reverse else c - 1

    def bs(x, s):
        return jnp.broadcast_to(x[:, s:s + 1, :], x.shape)

    def seen(s):
        return (t_idx <= s) if reverse else (t_idx >= s)

    def body(i, st):
        blk = (nblk - 1 - i) if reverse else i
        r0 = pl.multiple_of(blk * HG_BLOCK, HG_BLOCK)
        z = z_ref[pl.ds(r0, HG_BLOCK), :]
        q = q_ref[pl.ds(r0, HG_BLOCK), :] * scale
        v = v_ref[pl.ds(r0, HG_BLOCK), :]
        f = one_m_lb * jax.nn.sigmoid(z) + lbm
        k = one_m_lb * jax.nn.sigmoid(-z) - (lbm - lb)
        logf = jnp.log(f).reshape(ncb, c, LANES)
        q3 = q.reshape(ncb, c, LANES)
        k3 = k.reshape(ncb, c, LANES)
        v3 = v.reshape(ncb, c, LANES)
        cum = jnp.zeros_like(logf)
        for s in range(c):
            cum = cum + jnp.where(seen(s), bs(logf, s), 0.0)
        o3 = jnp.zeros_like(logf)
        for s in range(c):
            d = jnp.where(seen(s), cum - bs(cum, s), NEG_INF)
            w = q3 * bs(k3, s) * jnp.exp(d)
            a = jnp.dot(w.reshape(HG_BLOCK, LANES).astype(BF16), bdb, preferred_element_type=F32)
            o3 = o3 + a.reshape(ncb, c, LANES) * bs(v3, s)
        cum_edge = bs(cum, edge)
        qe = (q3 * jnp.exp(cum)).astype(BF16)
        kd = (k3 * jnp.exp(cum_edge - cum)).astype(BF16)
        vb = v3.astype(BF16)
        a_all = jnp.exp(cum_edge)
        outs = [None] * ncb
        for n in (range(ncb - 1, -1, -1) if reverse else range(ncb)):
            o_inter = lax.dot_general(qe[n], st.astype(BF16), (((1,), (1,)), ((), ())),
                                      preferred_element_type=F32)
            outs[n] = o3[n] + o_inter
            u_t = lax.dot_general(vb[n], kd[n], (((0,), (0,)), ((), ())), preferred_element_type=F32)
            st = a_all[n, 0:1, :] * st + u_t * bdf
        val = jnp.concatenate(outs, axis=0)
        if first:
            o_acc[pl.ds(r0, HG_BLOCK), :] = val
        else:
            o_acc[pl.ds(r0, HG_BLOCK), :] += val
        return st

    return lax.fori_loop(0, nblk, body, st)


def _hgrn_kernel(ql_ref, qc_ref, vl_ref, vc_ref, zfl_ref, zfc_ref, zbl_ref, zbc_ref, gl_ref, gc_ref,
                 lb_ref, gain_ref, bdb_ref, bdf_ref, yl_ref, yc_ref, ol_acc, oc_acc):
    bdb = bdb_ref[...]
    bdf = bdf_ref[...]
    zero = jnp.zeros((LANES, LANES), F32)
    run = functools.partial(_hgrn_pass, bdb=bdb, bdf=bdf)
    st = run(qc_ref, vc_ref, zfc_ref, oc_acc, lb_ref[0, 0], st=zero, reverse=False, first=True)
    run(ql_ref, vl_ref, zfl_ref, ol_acc, lb_ref[0, 0], st=st, reverse=False, first=True)
    st = run(qc_ref, vc_ref, zbc_ref, oc_acc, lb_ref[1, 0], st=zero, reverse=True, first=False)
    run(ql_ref, vl_ref, zbl_ref, ol_acc, lb_ref[1, 0], st=st, reverse=True, first=False)
    for acc, g_ref, y_ref in ((ol_acc, gl_ref, yl_ref), (oc_acc, gc_ref, yc_ref)):
        o = acc[...]
        g = g_ref[...]
        y = o * _seg_inv_rms(o, bdf) * gain_ref[...]
        y_ref[...] = (y * (g * jax.nn.sigmoid(g))).astype(y_ref.dtype)


def hgrn_mixer(pc, lb, gain128, bdb, bdf, *, b, s, l):
    w = pc.shape[1] // 5
    nj = w // LANES
    n_lat = b * s
    lat = lambda m: pl.BlockSpec((s, LANES), lambda i, j: (i, m * nj + j))
    cx = lambda m: pl.BlockSpec((l, LANES), lambda i, j: (n_lat // l + i, m * nj + j))
    in_specs = []
    for m in (0, 1, 2, 3, 4):
        in_specs += [lat(m), cx(m)]
    in_specs += [pl.BlockSpec((2, 1, 1, LANES), lambda i, j: (0, j, 0, 0)),
                 pl.BlockSpec((1, LANES), lambda i, j: (0, 0)),
                 pl.BlockSpec((LANES, LANES), lambda i, j: (0, 0)),
                 pl.BlockSpec((LANES, LANES), lambda i, j: (0, 0))]
    return pl.pallas_call(
        _hgrn_kernel,
        grid=(b, nj),
        in_specs=in_specs,
        out_specs=[pl.BlockSpec((s, LANES), lambda i, j: (i, j)),
                   pl.BlockSpec((l, LANES), lambda i, j: (i, j))],
        out_shape=[jax.ShapeDtypeStruct((n_lat, w), BF16), jax.ShapeDtypeStruct((b * l, w), BF16)],
        scratch_shapes=[pltpu.VMEM((s, LANES), F32), pltpu.VMEM((l, LANES), F32)],
        compiler_params=_cparams("parallel", "parallel"),
        name="hgrn_mixer",
    )(*([pc] * 10), lb, gain128, bdb, bdf)


def _outproj_kernel(x_ref, ma_ref, mb_ref, mc_ref, w_ref, gate_ref, g2_ref, shift_ref, scale_ref,
                    wr_ref, br_ref, xo_ref, h2_ref, lg_ref, *, wa, wb):
    w = w_ref[...]
    y = jnp.dot(ma_ref[...], w[:wa], preferred_element_type=F32)
    y = y + jnp.dot(mb_ref[...], w[wa:wa + wb], preferred_element_type=F32)
    y = y + jnp.dot(mc_ref[...], w[wa + wb:], preferred_element_type=F32)
    x = x_ref[...] + gate_ref[0] * y
    xo_ref[...] = x
    ms = jnp.mean(x * x, axis=-1, keepdims=True)
    h = x * lax.rsqrt(ms + EPS) * g2_ref[0]
    h = h * (1.0 + scale_ref[0]) + shift_ref[0]
    h2_ref[...] = h
    logits = jnp.dot(h, wr_ref[...], precision=HIGHEST, preferred_element_type=F32) + br_ref[...]
    lg_ref[...] = _route_meta(logits)


def out_projection(xall, mix_a, mix_b, mix_c, w_out_bf, mod3, layer_g2, w_route, b_route, *, seq, nseg, tm):
    r, d = xall.shape
    wa, wb, wc = mix_a.shape[1], mix_b.shape[1], mix_c.shape[1]
    tiles_per_seq = seq // tm

    def seg(i):
        return jnp.minimum(i // tiles_per_seq, nseg - 1)

    def modspec(m):
        return pl.BlockSpec((1, 1, d), lambda i: (seg(i) * N_MOD + m, 0, 0))

    row = lambda wdt: pl.BlockSpec((tm, wdt), lambda i: (i, 0))
    kern = functools.partial(_outproj_kernel, wa=wa, wb=wb)
    return pl.pallas_call(
        kern,
        grid=(r // tm,),
        in_specs=[row(d), row(wa), row(wb), row(wc),
                  pl.BlockSpec((wa + wb + wc, d), lambda i: (0, 0)),
                  modspec(2),
                  pl.BlockSpec((1, d), lambda i: (0, 0)),
                  modspec(3), modspec(4),
                  pl.BlockSpec((d, LANES), lambda i: (0, 0)),
                  pl.BlockSpec((1, LANES), lambda i: (0, 0))],
        out_specs=[row(d), row(d), row(LANES)],
        out_shape=[jax.ShapeDtypeStruct((r, d), F32),
                   jax.ShapeDtypeStruct((r, d), F32),
                   jax.ShapeDtypeStruct((r, LANES), F32)],
        compiler_params=_cparams("parallel"),
        name="out_projection",
    )(xall, mix_a, mix_b, mix_c, w_out_bf, mod3, layer_g2, mod3, mod3, w_route, b_route)


PAIRS_PER_GROUP = EXPERTS_PER_GROUP * (EXPERTS_PER_GROUP - 1) // 2
N_BUCKETS = N_GROUPS * PAIRS_PER_GROUP
ROW_TILE = 256
META_BUCKET, META_WA, META_WB = 0, 1, 2


def _bucket_experts():
    ea = np.zeros((LANES,), np.int32)
    eb = np.zeros((LANES,), np.int32)
    for g in range(N_GROUPS):
        k = g * PAIRS_PER_GROUP
        for a in range(EXPERTS_PER_GROUP):
            for b in range(a + 1, EXPERTS_PER_GROUP):
                ea[k], eb[k] = g * EXPERTS_PER_GROUP + a, g * EXPERTS_PER_GROUP + b
                k += 1
    return ea, eb


def _route_meta(logits):
    lane = lax.broadcasted_iota(jnp.int32, logits.shape, 1).astype(F32)
    is_g = lane < N_GROUPS
    gl = jnp.where(is_g, logits, -jnp.inf)
    gmax = jnp.max(gl, axis=-1, keepdims=True)
    g_idx = jnp.min(jnp.where(gl == gmax, lane, LANES), axis=-1, keepdims=True)
    gsum = jnp.sum(jnp.where(is_g, jnp.exp(gl - gmax), 0.0), axis=-1, keepdims=True)
    g_top = 1.0 / gsum
    lo = N_GROUPS + g_idx * EXPERTS_PER_GROUP
    in_grp = (lane >= lo) & (lane < lo + EXPERTS_PER_GROUP)
    el = jnp.where(in_grp, logits, -jnp.inf)
    m1 = jnp.max(el, axis=-1, keepdims=True)
    i1 = jnp.min(jnp.where(el == m1, lane, LANES), axis=-1, keepdims=True)
    el2 = jnp.where(lane == i1, -jnp.inf, el)
    m2 = jnp.max(el2, axis=-1, keepdims=True)
    i2 = jnp.min(jnp.where(el2 == m2, lane, LANES), axis=-1, keepdims=True)
    e21 = jnp.exp(m2 - m1)
    w1 = g_top / (1.0 + e21)
    w2 = e21 * w1
    first_low = i1 < i2
    la = jnp.minimum(i1, i2) - lo
    lb = jnp.maximum(i1, i2) - lo
    pair = la * (2 * EXPERTS_PER_GROUP - 1 - la) * 0.5 + (lb - la - 1.0)
    bucket = g_idx * PAIRS_PER_GROUP + pair
    wa = jnp.where(first_low, w1, w2)
    wb = jnp.where(first_low, w2, w1)
    return jnp.where(lane == META_BUCKET, bucket,
                     jnp.where(lane == META_WA, wa, jnp.where(lane == META_WB, wb, 0.0)))


def _plan_kernel(meta_ref, tri_ref, ids_ref, rank_ref, cnt_ref, carry):
    @pl.when(pl.program_id(0) == 0)
    def _():
        carry[...] = jnp.zeros_like(carry)

    ids = meta_ref[...].T[META_BUCKET:META_BUCKET + 1, :]
    sub = lax.broadcasted_iota(jnp.int32, (LANES, ids.shape[1]), 0).astype(F32)
    onehot = (sub == ids).astype(F32)
    before = jnp.dot(onehot.astype(BF16), tri_ref[...], preferred_element_type=F32)
    rank = jnp.sum(onehot * (before + carry[...]), axis=0, keepdims=True)
    ids_ref[0] = ids.astype(jnp.int32)
    rank_ref[0] = rank.astype(jnp.int32)
    total = carry[...] + jnp.sum(onehot, axis=1, keepdims=True)
    carry[...] = total
    cnt_ref[...] = total.astype(jnp.int32)


def route_plan(meta, tm):
    r = meta.shape[0]
    nt = r // tm
    i = np.arange(tm)
    tri = jnp.asarray(i[:, None] < i[None, :], BF16)
    ids, rank, cnt = pl.pallas_call(
        _plan_kernel,
        grid=(nt,),
        in_specs=[pl.BlockSpec((tm, LANES), lambda i: (i, 0)),
                  pl.BlockSpec((tm, tm), lambda i: (0, 0))],
        out_specs=[pl.BlockSpec((1, 1, tm), lambda i: (i, 0, 0)),
                   pl.BlockSpec((1, 1, tm), lambda i: (i, 0, 0)),
                   pl.BlockSpec((LANES, 1), lambda i: (0, 0))],
        out_shape=[jax.ShapeDtypeStruct((nt, 1, tm), jnp.int32),
                   jax.ShapeDtypeStruct((nt, 1, tm), jnp.int32),
                   jax.ShapeDtypeStruct((LANES, 1), jnp.int32)],
        scratch_shapes=[pltpu.VMEM((LANES, 1), F32)],
        compiler_params=_cparams("arbitrary"),
        name="route_plan",
    )(meta, tri)
    return ids.reshape(r), rank.reshape(r), cnt.reshape(LANES)


def _row_copy(src, dst, i, j, sem):
    return pltpu.make_async_copy(src.at[pl.ds(i, 1)], dst.at[pl.ds(j, 1)], sem)


def _dispatch_kernel(pos_ref, h_ref, init_hbm, o_hbm, sem):
    del init_hbm
    ch = h_ref.shape[0]
    base = pl.program_id(0) * ch

    def issue(i, c):
        _row_copy(h_ref, o_hbm, i, pos_ref[base + i], sem).start()
        return c

    def drain(i, c):
        _row_copy(h_ref, o_hbm, i, 0, sem).wait()
        return c

    lax.fori_loop(0, ch, issue, 0, unroll=8)
    lax.fori_loop(0, ch, drain, 0, unroll=8)


def dispatch_rows(pos, h2, n_rows, ch):
    r, d = h2.shape
    return pl.pallas_call(
        _dispatch_kernel,
        grid_spec=pltpu.PrefetchScalarGridSpec(
            num_scalar_prefetch=1,
            grid=(r // ch,),
            in_specs=[pl.BlockSpec((ch, d), lambda i, p: (i, 0)), pl.BlockSpec(memory_space=pl.ANY)],
            out_specs=pl.BlockSpec(memory_space=pl.ANY),
            scratch_shapes=[pltpu.SemaphoreType.DMA],
        ),
        out_shape=jax.ShapeDtypeStruct((n_rows, d), h2.dtype),
        input_output_aliases={2: 0},
        compiler_params=_cparams("arbitrary"),
        name="dispatch_rows",
    )(pos, h2, jnp.zeros((n_rows, d), h2.dtype))


def _expert_kernel(ta_ref, tb_ref, nu_ref, hs_ref, wga_ref, wua_ref, wda_ref, wgb_ref, wub_ref, wdb_ref,
                   y_ref, gu_a, dn_a, gu_b, dn_b):
    j = pl.program_id(0)
    d = hs_ref.shape[1]
    ff = wga_ref.shape[2]
    prev = jnp.maximum(j - 1, 0)

    def refresh(gu, dn, wg_ref, wu_ref, wd_ref):
        gu[:, :ff] = wg_ref[0].astype(BF16)
        gu[:, ff:] = wu_ref[0].astype(BF16)
        dn[...] = wd_ref[0].astype(BF16)

    @pl.when((j == 0) | (ta_ref[j] != ta_ref[prev]))
    def _():
        refresh(gu_a, dn_a, wga_ref, wua_ref, wda_ref)

    @pl.when((j == 0) | (tb_ref[j] != tb_ref[prev]))
    def _():
        refresh(gu_b, dn_b, wgb_ref, wub_ref, wdb_ref)

    @pl.when(j < nu_ref[0])
    def _():
        h = hs_ref[...].astype(BF16)
        for n, (gu, dn) in enumerate(((gu_a, dn_a), (gu_b, dn_b))):
            hgu = jnp.dot(h, gu[...], preferred_element_type=F32)
            hg = hgu[:, :ff]
            hid = (hg * jax.nn.sigmoid(hg)) * hgu[:, ff:]
            y_ref[:, n * d:(n + 1) * d] = jnp.dot(hid.astype(BF16), dn[...], preferred_element_type=F32)

    @pl.when(j >= nu_ref[0])
    def _():
        y_ref[...] = jnp.zeros_like(y_ref)


def expert_pairs(tile_a, tile_b, n_used, hs, w_gate, w_up, w_down):
    rows, d = hs.shape
    _, _, ff = w_gate.shape
    nt = rows // ROW_TILE
    blk = lambda i, ta, tb, nu: (jnp.minimum(i, nu[0]), 0)
    wsel = lambda which, shape: pl.BlockSpec(
        shape, (lambda i, ta, tb, nu: (ta[i], 0, 0)) if which == 0 else (lambda i, ta, tb, nu: (tb[i], 0, 0)))
    return pl.pallas_call(
        _expert_kernel,
        grid_spec=pltpu.PrefetchScalarGridSpec(
            num_scalar_prefetch=3,
            grid=(nt,),
            in_specs=[pl.BlockSpec((ROW_TILE, d), blk),
                      wsel(0, (1, d, ff)), wsel(0, (1, d, ff)), wsel(0, (1, ff, d)),
                      wsel(1, (1, d, ff)), wsel(1, (1, d, ff)), wsel(1, (1, ff, d))],
            out_specs=pl.BlockSpec((ROW_TILE, 2 * d), lambda i, ta, tb, nu: (i, 0)),
            scratch_shapes=[pltpu.VMEM((d, 2 * ff), BF16), pltpu.VMEM((ff, d), BF16),
                            pltpu.VMEM((d, 2 * ff), BF16), pltpu.VMEM((ff, d), BF16)],
        ),
        out_shape=jax.ShapeDtypeStruct((rows, 2 * d), F32),
        compiler_params=_cparams("arbitrary"),
        name="expert_pairs",
    )(tile_a, tile_b, n_used, hs, w_gate, w_up, w_down, w_gate, w_up, w_down)


def _combine_kernel(pos_ref, x_ref, meta_ref, gate_ref, y_hbm, o_ref, ybuf, sem):
    tm = x_ref.shape[0]
    d = x_ref.shape[1]
    base = pl.program_id(0) * tm

    def issue(i, c):
        _row_copy(y_hbm, ybuf, pos_ref[base + i], i, sem).start()
        return c

    def drain(i, c):
        _row_copy(y_hbm, ybuf, 0, i, sem).wait()
        return c

    lax.fori_loop(0, tm, issue, 0, unroll=8)
    lax.fori_loop(0, tm, drain, 0, unroll=8)
    meta = meta_ref[...]
    ff = (meta[:, META_WA:META_WA + 1] * ybuf[:, :d] + meta[:, META_WB:META_WB + 1] * ybuf[:, d:])
    o_ref[...] = x_ref[...] + gate_ref[0] * ff


def combine_rows(pos, x_mid, meta, mod3, y, *, seq, nseg, tm):
    r, d = x_mid.shape
    tiles_per_seq = seq // tm
    seg = lambda i: jnp.minimum(i // tiles_per_seq, nseg - 1)
    return pl.pallas_call(
        _combine_kernel,
        grid_spec=pltpu.PrefetchScalarGridSpec(
            num_scalar_prefetch=1,
            grid=(r // tm,),
            in_specs=[pl.BlockSpec((tm, d), lambda i, p: (i, 0)),
                      pl.BlockSpec((tm, LANES), lambda i, p: (i, 0)),
                      pl.BlockSpec((1, 1, d), lambda i, p: (seg(i) * N_MOD + 5, 0, 0)),
                      pl.BlockSpec(memory_space=pl.ANY)],
            out_specs=pl.BlockSpec((tm, d), lambda i, p: (i, 0)),
            scratch_shapes=[pltpu.VMEM((tm, 2 * d), F32), pltpu.SemaphoreType.DMA],
        ),
        out_shape=jax.ShapeDtypeStruct((r, d), F32),
        compiler_params=_cparams("arbitrary"),
        name="combine_rows",
    )(pos, x_mid, meta, mod3, y)


def moe_block(x_mid, h2, meta, mod3, w_gate, w_up, w_down, *, seq, nseg):
    r, d = x_mid.shape
    ids, rank, cnt = route_plan(meta, 512)
    tiles = (cnt + (ROW_TILE - 1)) // ROW_TILE
    incl = jnp.cumsum(tiles)
    n_used = incl[-1]
    lookup = lambda table, idx: jnp.sum(jnp.where(idx[:, None] == jnp.arange(LANES)[None, :], table[None, :], 0), axis=1)
    pos = lookup(incl - tiles, ids) * ROW_TILE + rank
    nt = r // ROW_TILE + N_BUCKETS
    last = jnp.minimum(jnp.arange(nt), n_used - 1)
    tile_bucket = jnp.sum((incl[None, :] <= last[:, None]).astype(jnp.int32), axis=1)
    ea, eb = _bucket_experts()
    tile_a = lookup(jnp.asarray(ea), tile_bucket)
    tile_b = lookup(jnp.asarray(eb), tile_bucket)
    hs = dispatch_rows(pos, h2, nt * ROW_TILE, 512)
    y = expert_pairs(tile_a, tile_b, n_used.reshape(1).astype(jnp.int32), hs, w_gate, w_up, w_down)
    return combine_rows(pos, x_mid, meta, mod3, y, seq=seq, nseg=nseg, tm=256)


def _rope_tables(seq, tm):
    t = np.arange(seq)
    row = (t // GRID_W).astype(np.float32)
    col = (t % GRID_W).astype(np.float32)
    half = HEAD_DIM // 2
    inv = jnp.asarray(ROPE_THETA, F32) ** (-jnp.arange(0, half, 2, dtype=F32) / half)
    ang = jnp.concatenate([jnp.asarray(row)[:, None] * inv, jnp.asarray(col)[:, None] * inv], axis=-1)
    cos = jnp.repeat(jnp.cos(ang), 2, axis=-1)
    sin = jnp.repeat(jnp.sin(ang), 2, axis=-1) * jnp.asarray(np.tile([-1.0, 1.0], half), F32)
    cos = jnp.tile(cos, (1, LANES // HEAD_DIM))
    sin = jnp.tile(sin, (1, LANES // HEAD_DIM))
    cos = jnp.concatenate([cos, jnp.ones((tm, LANES), F32)], axis=0)
    sin = jnp.concatenate([sin, jnp.zeros((tm, LANES), F32)], axis=0)
    return cos, sin


def kernel(x, c, ctx, c_ctx, w_ada, b_ada, norm1_g, w_in, na_q_norm, na_k_norm, na_rpb, gqa_q_norm, gqa_k_norm, hgrn_lb, hgrn_o_norm, w_out, norm2_g, w_route_group, b_route_group, w_route_expert, b_route_expert, w_exp_gate, w_exp_up, w_exp_down):
    b, s, d = x.shape
    l = ctx.shape[1]
    depth = w_ada.shape[0]
    assert s % 512 == 0 and (b * l) % 512 == 0 and s // GRID_W >= NA_WIN_R
    assert s % l == 0 and l % HG_BLOCK == 0
    nseg = b + 1
    n_lat = b * s
    n_ctx = b * l
    na_w, gq_qw = d // 4, d // 2
    gq_kw = gq_qw // 4
    hg_w = d // 4
    tm = 512

    c_all = jnp.zeros((16, d), F32).at[:b].set(c).at[b].set(c_ctx)
    mod = ada_mod(c_all, w_ada, b_ada)
    cos_t, sin_t = _rope_tables(s, tm)
    bd_f = _block_diag_ones(LANES, HEAD_DIM, F32)
    bd_b = _block_diag_ones(LANES, HEAD_DIM, BF16)
    p_lb = jax.nn.softmax(hgrn_lb.astype(F32), axis=0)
    lb_all = jnp.cumsum(p_lb, axis=0) - p_lb[0]
    tile2 = lambda g: jnp.tile(g, LANES // HEAD_DIM)

    xall = jnp.concatenate([x.reshape(n_lat, d), ctx.reshape(n_ctx, d)], axis=0)
    for layer in range(depth):
        ctx_out = layer < depth - 1
        mod3 = mod[layer].reshape(16 * N_MOD, 1, d)
        gains = jnp.zeros((8, LANES), F32)
        gains = gains.at[0].set(tile2(na_q_norm[layer])).at[1].set(tile2(na_k_norm[layer]))
        gains = gains.at[2].set(tile2(gqa_q_norm[layer])).at[3].set(tile2(gqa_k_norm[layer]))
        pa, pb, pc = in_projection(xall, mod3, norm1_g[layer][None], w_in[layer].astype(BF16), cos_t, sin_t,
                                   gains, bd_f, n_lat_rows=n_lat, seq=s, nseg=nseg, tm=tm)
        bias_tab = na_bias_table(na_rpb[layer], s // GRID_W)
        o_a = na_attention(pa, bias_tab, b=b, s=s, l=l, na_w=na_w)
        o_b = gqa_attention(pb, b=b, s=s, l=l, qw=gq_qw, kw=gq_kw, tq=128)
        lb4 = lb_all[layer].reshape(2, hg_w // LANES, 1, LANES)
        y_lat, y_ctx = hgrn_mixer(pc, lb4, tile2(hgrn_o_norm[layer])[None], bd_b, bd_f, b=b, s=s, l=l)

        w_route = jnp.zeros((d, LANES), F32).at[:, :N_GROUPS].set(w_route_group[layer])
        w_route = w_route.at[:, N_GROUPS:N_GROUPS + N_EXPERTS].set(w_route_expert[layer])
        b_route = jnp.zeros((1, LANES), F32).at[0, :N_GROUPS].set(b_route_group[layer])
        b_route = b_route.at[0, N_GROUPS:N_GROUPS + N_EXPERTS].set(b_route_expert[layer])
        if ctx_out:
            o_ac, o_bc = ctx_attention(pa, pb, b=b, s=s, l=l, na_w=na_w, qw=gq_qw, kw=gq_kw)
            mix_a = jnp.concatenate([o_a, o_ac], axis=0)
            mix_b = jnp.concatenate([o_b, o_bc], axis=0)
            y_c = jnp.concatenate([y_lat, y_ctx], axis=0)
            x_in = xall
        else:
            mix_a, mix_b, y_c, x_in = o_a, o_b, y_lat, xall[:n_lat]
        x_mid, h2, meta = out_projection(x_in, mix_a, mix_b, y_c, w_out[layer].astype(BF16), mod3,
                                          norm2_g[layer][None], w_route, b_route, seq=s, nseg=nseg, tm=tm)
        xall = moe_block(x_mid, h2, meta, mod3, w_exp_gate[layer], w_exp_up[layer], w_exp_down[layer],
                         seq=s, nseg=nseg)
    return xall[:n_lat].reshape(b, s, d)
```

```python
import functools

import jax
import jax.numpy as jnp
import numpy as np
from jax import lax
from jax.experimental import pallas as pl
from jax.experimental.pallas import tpu as pltpu

F32 = jnp.float32
BF16 = jnp.bfloat16
HIGHEST = lax.Precision.HIGHEST

HEAD_DIM = 64
GRID_W = 64
NA_WIN_R = 8
NA_WIN_C = 16
ROPE_THETA = 10000.0
HGRN_CHUNK = 16
N_GROUPS = 4
EXPERTS_PER_GROUP = 8
N_EXPERTS = N_GROUPS * EXPERTS_PER_GROUP
N_MOD = 6
EPS = 1e-6
NEG_INF = -1e30
LB_FLOOR = 1e-20
LANES = 128
VMEM_LIMIT = 56 * 1024 * 1024


def _cparams(*sem):
    return pltpu.CompilerParams(dimension_semantics=sem, vmem_limit_bytes=VMEM_LIMIT)


def _block_diag_ones(n, blk, dtype):
    i = np.arange(n)
    return jnp.asarray((i[:, None] // blk) == (i[None, :] // blk), dtype=dtype)


def _ada_kernel(c_ref, w_ref, b_ref, o_ref):
    c = c_ref[...]
    s = c * jax.nn.sigmoid(c)
    o_ref[0] = jnp.dot(s, w_ref[0], precision=HIGHEST, preferred_element_type=F32) + b_ref[0]


def ada_mod(c_all, w_ada, b_ada):
    depth, d, n = w_ada.shape
    tn = 1536
    return pl.pallas_call(
        _ada_kernel,
        grid=(depth, n // tn),
        in_specs=[
            pl.BlockSpec((16, d), lambda l, j: (0, 0)),
            pl.BlockSpec((1, d, tn), lambda l, j: (l, 0, j)),
            pl.BlockSpec((1, 1, tn), lambda l, j: (l, 0, j)),
        ],
        out_specs=pl.BlockSpec((1, 16, tn), lambda l, j: (l, 0, j)),
        out_shape=jax.ShapeDtypeStruct((depth, 16, n), F32),
        compiler_params=_cparams("parallel", "parallel"),
        name="ada_mod",
    )(c_all, w_ada, b_ada.reshape(depth, 1, n))


def _seg_inv_rms(x, bd):
    xs = x * x
    hi = xs.astype(BF16)
    lo = (xs - hi.astype(F32)).astype(BF16)
    ss = jnp.dot(hi, bd, preferred_element_type=F32) + jnp.dot(lo, bd, preferred_element_type=F32)
    return lax.rsqrt(ss * (1.0 / HEAD_DIM) + EPS)


def _pair_swap(x):
    lane = lax.broadcasted_iota(jnp.int32, x.shape, 1)
    return jnp.where((lane & 1) == 0, pltpu.roll(x, LANES - 1, 1), pltpu.roll(x, 1, 1))


def _inproj_kernel(x_ref, g1_ref, shift_ref, scale_ref, w_ref, cos_ref, sin_ref, gains_ref, bd_ref,
                   oa_ref, ob_ref, oc_ref, *, na_w, gq_qw, gq_kw):
    x = x_ref[...]
    ms = jnp.mean(x * x, axis=-1, keepdims=True)
    h = x * lax.rsqrt(ms + EPS) * g1_ref[0]
    h = h * (1.0 + scale_ref[0]) + shift_ref[0]
    p = jnp.dot(h.astype(BF16), w_ref[...], preferred_element_type=F32)
    bd = bd_ref[...]
    cos = cos_ref[...]
    sin = sin_ref[...]
    qscale = HEAD_DIM ** -0.5

    def normed(col, gain_row):
        xb = p[:, col:col + LANES]
        return xb * _seg_inv_rms(xb, bd) * gains_ref[gain_row:gain_row + 1, :]

    def rope(xn):
        return xn * cos + _pair_swap(xn) * sin

    for j in range(na_w // LANES):
        c = j * LANES
        oa_ref[:, c:c + LANES] = (normed(c, 0) * qscale).astype(BF16)
        oa_ref[:, na_w + c:na_w + c + LANES] = normed(na_w + c, 1).astype(BF16)
    oa_ref[:, 2 * na_w:3 * na_w] = p[:, 2 * na_w:3 * na_w].astype(BF16)
    b0 = 3 * na_w
    for j in range(gq_qw // LANES):
        c = j * LANES
        ob_ref[:, c:c + LANES] = (rope(normed(b0 + c, 2)) * qscale).astype(BF16)
    for j in range(gq_kw // LANES):
        c = gq_qw + j * LANES
        ob_ref[:, c:c + LANES] = rope(normed(b0 + c, 3)).astype(BF16)
    ob_ref[:, gq_qw + gq_kw:] = p[:, b0 + gq_qw + gq_kw:b0 + gq_qw + 2 * gq_kw].astype(BF16)
    oc_ref[...] = p[:, b0 + gq_qw + 2 * gq_kw:]


def in_projection(xall, mod3, layer_g1, w_in_bf, cos_t, sin_t, gains, bd, *, n_lat_rows, seq, nseg, tm):
    r, d = xall.shape
    d_in = w_in_bf.shape[1]
    na_w = d // 4
    gq_qw = d // 2
    gq_kw = gq_qw // 4
    c_w = d_in - 3 * na_w - gq_qw - 2 * gq_kw
    lat_tiles = n_lat_rows // tm
    tiles_per_seq = seq // tm

    def seg(i):
        return jnp.minimum(i // tiles_per_seq, nseg - 1)

    def rope_blk(i):
        return jnp.where(i < lat_tiles, i % tiles_per_seq, tiles_per_seq)

    kern = functools.partial(_inproj_kernel, na_w=na_w, gq_qw=gq_qw, gq_kw=gq_kw)
    return pl.pallas_call(
        kern,
        grid=(r // tm,),
        in_specs=[
            pl.BlockSpec((tm, d), lambda i: (i, 0)),
            pl.BlockSpec((1, d), lambda i: (0, 0)),
            pl.BlockSpec((1, 1, d), lambda i: (seg(i) * N_MOD + 0, 0, 0)),
            pl.BlockSpec((1, 1, d), lambda i: (seg(i) * N_MOD + 1, 0, 0)),
            pl.BlockSpec((d, d_in), lambda i: (0, 0)),
            pl.BlockSpec((tm, LANES), lambda i: (rope_blk(i), 0)),
            pl.BlockSpec((tm, LANES), lambda i: (rope_blk(i), 0)),
            pl.BlockSpec((8, LANES), lambda i: (0, 0)),
            pl.BlockSpec((LANES, LANES), lambda i: (0, 0)),
        ],
        out_specs=[
            pl.BlockSpec((tm, 3 * na_w), lambda i: (i, 0)),
            pl.BlockSpec((tm, gq_qw + 2 * gq_kw), lambda i: (i, 0)),
            pl.BlockSpec((tm, c_w), lambda i: (i, 0)),
        ],
        out_shape=[
            jax.ShapeDtypeStruct((r, 3 * na_w), BF16),
            jax.ShapeDtypeStruct((r, gq_qw + 2 * gq_kw), BF16),
            jax.ShapeDtypeStruct((r, c_w), F32),
        ],
        compiler_params=_cparams("parallel"),
        name="in_projection",
    )(xall, layer_g1, mod3, mod3, w_in_bf, cos_t, sin_t, gains, bd)


def _head(j):
    return slice(j * HEAD_DIM, (j + 1) * HEAD_DIM)


def _with_ones(v):
    return jnp.concatenate([v, jnp.ones_like(v)], axis=1)


def _scores(q, k):
    return lax.dot_general(q, k, (((1,), (1,)), ((), ())), preferred_element_type=F32)


def _normalise(o):
    return o[:, :HEAD_DIM] / o[:, HEAD_DIM:HEAD_DIM + 1]


def _softmax_attend(q, k, v1):
    s = _scores(q, k)
    p = jnp.exp((s - jnp.max(s, axis=-1, keepdims=True)).astype(BF16))
    return _normalise(jnp.dot(p, v1, preferred_element_type=F32))


def _grouped_attend(q_ref, k_of, v_of, n_kv, grp):
    t = q_ref.shape[0]
    outs = []
    for j in range(n_kv):
        q4 = jnp.concatenate([q_ref[:, _head(j * grp + g)] for g in range(grp)], axis=0)
        o = _softmax_attend(q4, k_of(j), v_of(j))
        outs += [o[g * t:(g + 1) * t] for g in range(grp)]
    return jnp.concatenate(outs, axis=1)


def _gqa_kernel(q_ref, kl_ref, vl_ref, kc_ref, vc_ref, o_ref, k_s, v_s, *, n_kv, grp):
    s_len = kl_ref.shape[0]

    @pl.when(pl.program_id(1) == 0)
    def _():
        for j in range(n_kv):
            k_s[j, :s_len, :] = kl_ref[:, _head(j)]
            k_s[j, s_len:, :] = kc_ref[:, _head(j)]
            v_s[j, :s_len, :] = _with_ones(vl_ref[:, _head(j)])
            v_s[j, s_len:, :] = _with_ones(vc_ref[:, _head(j)])

    o = _grouped_attend(q_ref, lambda j: k_s[j], lambda j: v_s[j], n_kv, grp)
    o_ref[...] = o.astype(o_ref.dtype)


def gqa_attention(pb, *, b, s, l, qw, kw, tq):
    n_lat = b * s
    n_kv = kw // HEAD_DIM
    grp = qw // kw
    assert kw == LANES and qw % kw == 0
    kcol, vcol = qw // kw, qw // kw + 1
    kern = functools.partial(_gqa_kernel, n_kv=n_kv, grp=grp)
    return pl.pallas_call(
        kern,
        grid=(b, s // tq),
        in_specs=[
            pl.BlockSpec((tq, qw), lambda i, j: (i * (s // tq) + j, 0)),
            pl.BlockSpec((s, kw), lambda i, j: (i, kcol)),
            pl.BlockSpec((s, kw), lambda i, j: (i, vcol)),
            pl.BlockSpec((l, kw), lambda i, j: (n_lat // l + i, kcol)),
            pl.BlockSpec((l, kw), lambda i, j: (n_lat // l + i, vcol)),
        ],
        out_specs=pl.BlockSpec((tq, qw), lambda i, j: (i * (s // tq) + j, 0)),
        out_shape=jax.ShapeDtypeStruct((n_lat, qw), BF16),
        scratch_shapes=[pltpu.VMEM((n_kv, s + l, HEAD_DIM), BF16), pltpu.VMEM((n_kv, s + l, 2 * HEAD_DIM), BF16)],
        compiler_params=_cparams("parallel", "arbitrary"),
        name="gqa_attention",
    )(pb, pb, pb, pb, pb)


def _ctx_attn_kernel(qa_ref, ka_ref, va_ref, qb_ref, kb_ref, vb_ref, oa_ref, ob_ref, *, n_kv, grp):
    na_h = qa_ref.shape[1] // HEAD_DIM
    oa = [_softmax_attend(qa_ref[:, _head(h)], ka_ref[:, _head(h)], _with_ones(va_ref[:, _head(h)]))
          for h in range(na_h)]
    oa_ref[...] = jnp.concatenate(oa, axis=1).astype(oa_ref.dtype)
    ob = _grouped_attend(qb_ref, lambda j: kb_ref[:, _head(j)], lambda j: _with_ones(vb_ref[:, _head(j)]),
                         n_kv, grp)
    ob_ref[...] = ob.astype(ob_ref.dtype)


def ctx_attention(pa, pb, *, b, s, l, na_w, qw, kw):
    r0 = (b * s) // l
    grp = qw // kw
    kern = functools.partial(_ctx_attn_kernel, n_kv=kw // HEAD_DIM, grp=grp)
    a_spec = lambda m: pl.BlockSpec((l, na_w), lambda i: (r0 + i, m))
    return pl.pallas_call(
        kern,
        grid=(b,),
        in_specs=[a_spec(0), a_spec(1), a_spec(2),
                  pl.BlockSpec((l, qw), lambda i: (r0 + i, 0)),
                  pl.BlockSpec((l, kw), lambda i: (r0 + i, grp)),
                  pl.BlockSpec((l, kw), lambda i: (r0 + i, grp + 1))],
        out_specs=[pl.BlockSpec((l, na_w), lambda i: (i, 0)), pl.BlockSpec((l, qw), lambda i: (i, 0))],
        out_shape=[jax.ShapeDtypeStruct((b * l, na_w), BF16), jax.ShapeDtypeStruct((b * l, qw), BF16)],
        compiler_params=_cparams("parallel"),
        name="ctx_attention",
    )(pa, pa, pa, pb, pb, pb)


NA_QROWS = 4
NA_UNION = NA_WIN_R + NA_QROWS


def _na_block_geometry(rows):
    wu = min(rows, NA_UNION)
    wr = min(NA_WIN_R, rows)
    nblk = rows // NA_QROWS
    sig, u0s = [], []
    for blk in range(nblk):
        r0 = blk * NA_QROWS
        u0 = int(np.clip(r0 - wr // 2, 0, rows - wu))
        u0s.append(u0)
        sig.append(tuple((r0 + j - u0, int(np.clip(r0 + j - wr // 2, 0, rows - wr)) - u0) for j in range(NA_QROWS)))
    cls = [int(blk > 0) + int(blk == nblk - 1) for blk in range(nblk)]
    reps = {}
    for blk in range(nblk):
        assert reps.setdefault(cls[blk], sig[blk]) == sig[blk]
    return wu, wr, [reps.get(c, reps[0]) for c in range(3)]


def _na_kernel(q_ref, k_ref, v_ref, kc_ref, vc_ref, bias_ref, o_ref, k_s, v_s, kc_s, vc_s, *, rows, wu, wr):
    nh = q_ref.shape[1] // HEAD_DIM
    nblk = rows // NA_QROWS
    nq = NA_QROWS * GRID_W
    for h in range(nh):
        k_s[h] = k_ref[:, _head(h)]
        v_s[h] = _with_ones(v_ref[:, _head(h)])
        kc_s[h] = kc_ref[:, _head(h)]
        vc_s[h] = _with_ones(vc_ref[:, _head(h)])

    def body(blk, carry):
        u0 = jnp.clip(blk * NA_QROWS - wr // 2, 0, rows - wu)
        cls = jnp.minimum(blk, 1) + jnp.maximum(blk - (nblk - 2), 0)
        q_rows = pl.ds(pl.multiple_of(blk * nq, nq), nq)
        k_rows = pl.ds(pl.multiple_of(u0 * GRID_W, GRID_W), wu * GRID_W)
        outs = []
        for h in range(nh):
            q = q_ref[q_rows, _head(h)]
            s_nb = _scores(q, k_s[h, k_rows, :]) + bias_ref[h, cls]
            s_cx = _scores(q, kc_s[h])
            m = jnp.maximum(jnp.max(s_nb, axis=-1, keepdims=True), jnp.max(s_cx, axis=-1, keepdims=True))
            p_nb = jnp.exp((s_nb - m).astype(BF16))
            p_cx = jnp.exp((s_cx - m).astype(BF16))
            outs.append(_normalise(jnp.dot(p_nb, v_s[h, k_rows, :], preferred_element_type=F32)
                                   + jnp.dot(p_cx, vc_s[h], preferred_element_type=F32)))
        o_ref[q_rows, :] = jnp.concatenate(outs, axis=1).astype(o_ref.dtype)
        return carry

    lax.fori_loop(0, nblk, body, 0)


def na_attention(pa, bias_tab, *, b, s, l, na_w):
    n_lat = b * s
    nh = na_w // HEAD_DIM
    rows = s // GRID_W
    wu, wr, _ = _na_block_geometry(rows)
    kern = functools.partial(_na_kernel, rows=rows, wu=wu, wr=wr)
    lat = lambda m: pl.BlockSpec((s, na_w), lambda i: (i, m))
    cx = lambda m: pl.BlockSpec((l, na_w), lambda i: (n_lat // l + i, m))
    return pl.pallas_call(
        kern,
        grid=(b,),
        in_specs=[lat(0), lat(1), lat(2), cx(1), cx(2),
                  pl.BlockSpec(bias_tab.shape, lambda i: (0, 0, 0, 0))],
        out_specs=pl.BlockSpec((s, na_w), lambda i: (i, 0)),
        out_shape=jax.ShapeDtypeStruct((n_lat, na_w), BF16),
        scratch_shapes=[pltpu.VMEM((nh, s, HEAD_DIM), BF16), pltpu.VMEM((nh, s, 2 * HEAD_DIM), BF16),
                        pltpu.VMEM((nh, l, HEAD_DIM), BF16), pltpu.VMEM((nh, l, 2 * HEAD_DIM), BF16)],
        compiler_params=_cparams("parallel"),
        name="na_attention",
    )(pa, pa, pa, pa, pa, bias_tab)


def na_bias_table(rpb, rows):
    wu, wr, reps = _na_block_geometry(rows)
    h = rpb.shape[0]
    r_off = np.array([[rj for rj, _ in rep] for rep in reps])
    s_off = np.array([[sj for _, sj in rep] for rep in reps])
    kr = np.arange(wu)
    row_ok = (kr >= s_off[..., None]) & (kr < s_off[..., None] + wr)
    dr = kr - r_off[..., None] + (NA_WIN_R - 1)
    cidx = np.arange(GRID_W)
    col_start = np.clip(cidx - NA_WIN_C // 2, 0, GRID_W - NA_WIN_C)
    col_ok = (cidx[None, :] >= col_start[:, None]) & (cidx[None, :] < col_start[:, None] + NA_WIN_C)
    dc = np.clip(cidx[None, :] - cidx[:, None] + (NA_WIN_C - 1), 0, 2 * NA_WIN_C - 2)
    sel_r = jnp.asarray((dr[..., None] == np.arange(2 * NA_WIN_R - 1)) & row_ok[..., None], F32)
    sel_c = jnp.asarray(dc[:, :, None] == np.arange(2 * NA_WIN_C - 1), F32)
    bias = jnp.einsum("cjki,hid->hcjkd", sel_r, rpb.astype(F32), precision=HIGHEST)
    bias = jnp.einsum("hcjkd,qxd->hcjqkx", bias, sel_c, precision=HIGHEST)
    ok = row_ok[None, :, :, None, :, None] & col_ok[None, None, None, :, None, :]
    bias = jnp.where(jnp.asarray(ok), bias, NEG_INF)
    return bias.reshape(h, len(reps), NA_QROWS * GRID_W, wu * GRID_W)


HG_BLOCK = 128


def _hgrn_block(q_ref, v_ref, z_ref, o_acc, lb, bdb, bdf, st, blk, *, reverse):
    c = HGRN_CHUNK
    ncb = HG_BLOCK // c
    lbm = jnp.maximum(lb, LB_FLOOR)
    one_m_lb = 1.0 - lb
    scale = HEAD_DIM ** -0.5
    t_idx = lax.broadcasted_iota(jnp.int32, (ncb, c, LANES), 1)
    edge = 0 if reverse else c - 1

    def bs(x, s):
        return jnp.broadcast_to(x[:, s:s + 1, :], x.shape)

    def seen(s):
        return (t_idx <= s) if reverse else (t_idx >= s)

    if True:
        r0 = pl.multiple_of(blk * HG_BLOCK, HG_BLOCK)
        z = z_ref[pl.ds(r0, HG_BLOCK), :]
        q = q_ref[pl.ds(r0, HG_BLOCK), :] * scale
        v = v_ref[pl.ds(r0, HG_BLOCK), :]
        f = one_m_lb * jax.nn.sigmoid(z) + lbm
        k = one_m_lb * jax.nn.sigmoid(-z) - (lbm - lb)
        logf = jnp.log2(f).reshape(ncb, c, LANES)
        cum = jnp.zeros_like(logf)
        for s in range(c):
            cum = cum + jnp.where(seen(s), bs(logf, s), 0.0)
        q3 = q.reshape(ncb, c, LANES)
        k3 = k.reshape(ncb, c, LANES)
        v3 = v.reshape(ncb, c, LANES)
        o3 = jnp.zeros_like(cum)
        for s in range(c):
            d = jnp.where(seen(s), cum - bs(cum, s), NEG_INF)
            w = q3 * bs(k3, s) * jnp.exp2(d)
            a = jnp.dot(w.reshape(HG_BLOCK, LANES).astype(BF16), bdb, preferred_element_type=F32)
            o3 = o3 + a.reshape(ncb, c, LANES) * bs(v3, s)
        cum_edge = bs(cum, edge)
        qe = (q3 * jnp.exp2(cum)).astype(BF16)
        kd = (k3 * jnp.exp2(cum_edge - cum)).astype(BF16)
        vb = v3.astype(BF16)
        chunk_decay = jnp.exp2(cum_edge)
        outs = [None] * ncb
        for n in (range(ncb - 1, -1, -1) if reverse else range(ncb)):
            o_inter = lax.dot_general(qe[n], st.astype(BF16), (((1,), (1,)), ((), ())),
                                      preferred_element_type=F32)
            outs[n] = o3[n] + o_inter
            u_t = lax.dot_general(vb[n], kd[n], (((0,), (0,)), ((), ())), preferred_element_type=F32)
            st = chunk_decay[n, 0:1, :] * st + u_t * bdf
        o_acc[pl.ds(r0, HG_BLOCK), :] = jnp.concatenate(outs, axis=0)
        return st


def _hgrn_segment(q_ref, v_ref, zf_ref, zb_ref, of_acc, ob_acc, lb_ref, bdb, bdf, st_f, st_b):
    nblk = q_ref.shape[0] // HG_BLOCK

    def body(i, carry):
        st_f, st_b = carry
        st_f = _hgrn_block(q_ref, v_ref, zf_ref, of_acc, lb_ref[0, 0], bdb, bdf, st_f, i, reverse=False)
        st_b = _hgrn_block(q_ref, v_ref, zb_ref, ob_acc, lb_ref[1, 0], bdb, bdf, st_b, nblk - 1 - i, reverse=True)
        return st_f, st_b

    return lax.fori_loop(0, nblk, body, (st_f, st_b))


def _hgrn_kernel(ql_ref, qc_ref, vl_ref, vc_ref, zfl_ref, zfc_ref, zbl_ref, zbc_ref, gl_ref, gc_ref,
                 lb_ref, gain_ref, bdb_ref, bdf_ref, yl_ref, yc_ref, olf_acc, olb_acc, ocf_acc, ocb_acc):
    bdb = bdb_ref[...]
    bdf = bdf_ref[...]
    zero = jnp.zeros((LANES, LANES), F32)
    st_f, st_b = _hgrn_segment(qc_ref, vc_ref, zfc_ref, zbc_ref, ocf_acc, ocb_acc, lb_ref, bdb, bdf, zero, zero)
    _hgrn_segment(ql_ref, vl_ref, zfl_ref, zbl_ref, olf_acc, olb_acc, lb_ref, bdb, bdf, st_f, st_b)
    for acc_f, acc_b, g_ref, y_ref in ((olf_acc, olb_acc, gl_ref, yl_ref), (ocf_acc, ocb_acc, gc_ref, yc_ref)):
        o = acc_f[...] + acc_b[...]
        g = g_ref[...]
        y = o * _seg_inv_rms(o, bdb) * gain_ref[...]
        y_ref[...] = (y * (g * jax.nn.sigmoid(g))).astype(y_ref.dtype)


def hgrn_mixer(pc, lb, gain128, bdb, bdf, *, b, s, l):
    w = pc.shape[1] // 5
    nj = w // LANES
    n_lat = b * s
    lat = lambda m: pl.BlockSpec((s, LANES), lambda i, j: (i, m * nj + j))
    cx = lambda m: pl.BlockSpec((l, LANES), lambda i, j: (n_lat // l + i, m * nj + j))
    in_specs = []
    for m in (0, 1, 2, 3, 4):
        in_specs += [lat(m), cx(m)]
    in_specs += [pl.BlockSpec((2, 1, 1, LANES), lambda i, j: (0, j, 0, 0)),
                 pl.BlockSpec((1, LANES), lambda i, j: (0, 0)),
                 pl.BlockSpec((LANES, LANES), lambda i, j: (0, 0)),
                 pl.BlockSpec((LANES, LANES), lambda i, j: (0, 0))]
    return pl.pallas_call(
        _hgrn_kernel,
        grid=(b, nj),
        in_specs=in_specs,
        out_specs=[pl.BlockSpec((s, LANES), lambda i, j: (i, j)),
                   pl.BlockSpec((l, LANES), lambda i, j: (i, j))],
        out_shape=[jax.ShapeDtypeStruct((n_lat, w), BF16), jax.ShapeDtypeStruct((b * l, w), BF16)],
        scratch_shapes=[pltpu.VMEM((s, LANES), F32), pltpu.VMEM((s, LANES), F32),
                        pltpu.VMEM((l, LANES), F32), pltpu.VMEM((l, LANES), F32)],
        compiler_params=_cparams("parallel", "parallel"),
        name="hgrn_mixer",
    )(*([pc] * 10), lb, gain128, bdb, bdf)


def _outproj_kernel(x_ref, ma_ref, mb_ref, mc_ref, w_ref, gate_ref, g2_ref, shift_ref, scale_ref,
                    wr_ref, br_ref, xo_ref, h2_ref, lg_ref, *, wa, wb):
    w = w_ref[...]
    y = jnp.dot(ma_ref[...], w[:wa], preferred_element_type=F32)
    y = y + jnp.dot(mb_ref[...], w[wa:wa + wb], preferred_element_type=F32)
    y = y + jnp.dot(mc_ref[...], w[wa + wb:], preferred_element_type=F32)
    x = x_ref[...] + gate_ref[0] * y
    xo_ref[...] = x
    ms = jnp.mean(x * x, axis=-1, keepdims=True)
    h = x * lax.rsqrt(ms + EPS) * g2_ref[0]
    h = h * (1.0 + scale_ref[0]) + shift_ref[0]
    h2_ref[...] = h
    h_hi = h.astype(BF16)
    h_lo = (h - h_hi.astype(F32)).astype(BF16)
    logits = (jnp.dot(h_hi, wr_ref[0], preferred_element_type=F32)
              + jnp.dot(h_lo, wr_ref[0], preferred_element_type=F32)
              + jnp.dot(h_hi, wr_ref[1], preferred_element_type=F32)) + br_ref[...]
    lg_ref[...] = _route_meta(logits)


def out_projection(xall, mix_a, mix_b, mix_c, w_out_bf, mod3, layer_g2, w_route, b_route, *, n_rows, seq, nseg, tm):
    r, d = n_rows, xall.shape[1]
    w_hi = w_route.astype(BF16)
    w_route = jnp.stack([w_hi, (w_route - w_hi.astype(F32)).astype(BF16)])
    wa, wb, wc = mix_a.shape[1], mix_b.shape[1], mix_c.shape[1]
    tiles_per_seq = seq // tm

    def seg(i):
        return jnp.minimum(i // tiles_per_seq, nseg - 1)

    def modspec(m):
        return pl.BlockSpec((1, 1, d), lambda i: (seg(i) * N_MOD + m, 0, 0))

    row = lambda wdt: pl.BlockSpec((tm, wdt), lambda i: (i, 0))
    kern = functools.partial(_outproj_kernel, wa=wa, wb=wb)
    return pl.pallas_call(
        kern,
        grid=(r // tm,),
        in_specs=[row(d), row(wa), row(wb), row(wc),
                  pl.BlockSpec((wa + wb + wc, d), lambda i: (0, 0)),
                  modspec(2),
                  pl.BlockSpec((1, d), lambda i: (0, 0)),
                  modspec(3), modspec(4),
                  pl.BlockSpec((2, d, LANES), lambda i: (0, 0, 0)),
                  pl.BlockSpec((1, LANES), lambda i: (0, 0))],
        out_specs=[row(d), row(d), row(LANES)],
        out_shape=[jax.ShapeDtypeStruct((r, d), F32),
                   jax.ShapeDtypeStruct((r, d), F32),
                   jax.ShapeDtypeStruct((r, LANES), F32)],
        compiler_params=_cparams("parallel"),
        name="out_projection",
    )(xall, mix_a, mix_b, mix_c, w_out_bf, mod3, layer_g2, mod3, mod3, w_route, b_route)


PAIRS_PER_GROUP = EXPERTS_PER_GROUP * (EXPERTS_PER_GROUP - 1) // 2
N_BUCKETS = N_GROUPS * PAIRS_PER_GROUP
ROW_TILE = 256
META_BUCKET, META_WA, META_WB = 0, 1, 2


def _bucket_experts():
    ea = np.zeros((LANES,), np.int32)
    eb = np.zeros((LANES,), np.int32)
    for g in range(N_GROUPS):
        k = g * PAIRS_PER_GROUP
        for a in range(EXPERTS_PER_GROUP):
            for b in range(a + 1, EXPERTS_PER_GROUP):
                ea[k], eb[k] = g * EXPERTS_PER_GROUP + a, g * EXPERTS_PER_GROUP + b
                k += 1
    return ea, eb


def _route_meta(logits):
    lane = lax.broadcasted_iota(jnp.int32, logits.shape, 1).astype(F32)
    is_g = lane < N_GROUPS
    gl = jnp.where(is_g, logits, -jnp.inf)
    gmax = jnp.max(gl, axis=-1, keepdims=True)
    g_idx = jnp.min(jnp.where(gl == gmax, lane, LANES), axis=-1, keepdims=True)
    gsum = jnp.sum(jnp.where(is_g, jnp.exp(gl - gmax), 0.0), axis=-1, keepdims=True)
    g_top = 1.0 / gsum
    lo = N_GROUPS + g_idx * EXPERTS_PER_GROUP
    in_grp = (lane >= lo) & (lane < lo + EXPERTS_PER_GROUP)
    el = jnp.where(in_grp, logits, -jnp.inf)
    m1 = jnp.max(el, axis=-1, keepdims=True)
    i1 = jnp.min(jnp.where(el == m1, lane, LANES), axis=-1, keepdims=True)
    el2 = jnp.where(lane == i1, -jnp.inf, el)
    m2 = jnp.max(el2, axis=-1, keepdims=True)
    i2 = jnp.min(jnp.where(el2 == m2, lane, LANES), axis=-1, keepdims=True)
    e21 = jnp.exp(m2 - m1)
    w1 = g_top / (1.0 + e21)
    w2 = e21 * w1
    first_low = i1 < i2
    la = jnp.minimum(i1, i2) - lo
    lb = jnp.maximum(i1, i2) - lo
    pair = la * (2 * EXPERTS_PER_GROUP - 1 - la) * 0.5 + (lb - la - 1.0)
    bucket = g_idx * PAIRS_PER_GROUP + pair
    wa = jnp.where(first_low, w1, w2)
    wb = jnp.where(first_low, w2, w1)
    return jnp.where(lane == META_BUCKET, bucket,
                     jnp.where(lane == META_WA, wa, jnp.where(lane == META_WB, wb, 0.0)))


def _plan_kernel(meta_ref, tri_ref, ids_ref, rank_ref, cnt_ref, carry):
    @pl.when(pl.program_id(0) == 0)
    def _():
        carry[...] = jnp.zeros_like(carry)

    ids = meta_ref[...].T[META_BUCKET:META_BUCKET + 1, :]
    sub = lax.broadcasted_iota(jnp.int32, (LANES, ids.shape[1]), 0).astype(F32)
    onehot = (sub == ids).astype(F32)
    before = jnp.dot(onehot.astype(BF16), tri_ref[...], preferred_element_type=F32)
    rank = jnp.sum(onehot * (before + carry[...]), axis=0, keepdims=True)
    ids_ref[0] = ids.astype(jnp.int32)
    rank_ref[0] = rank.astype(jnp.int32)
    total = carry[...] + jnp.sum(onehot, axis=1, keepdims=True)
    carry[...] = total
    cnt_ref[...] = total.astype(jnp.int32)


def route_plan(meta, tm):
    r = meta.shape[0]
    nt = r // tm
    i = np.arange(tm)
    tri = jnp.asarray(i[:, None] < i[None, :], BF16)
    ids, rank, cnt = pl.pallas_call(
        _plan_kernel,
        grid=(nt,),
        in_specs=[pl.BlockSpec((tm, LANES), lambda i: (i, 0)),
                  pl.BlockSpec((tm, tm), lambda i: (0, 0))],
        out_specs=[pl.BlockSpec((1, 1, tm), lambda i: (i, 0, 0)),
                   pl.BlockSpec((1, 1, tm), lambda i: (i, 0, 0)),
                   pl.BlockSpec((LANES, 1), lambda i: (0, 0))],
        out_shape=[jax.ShapeDtypeStruct((nt, 1, tm), jnp.int32),
                   jax.ShapeDtypeStruct((nt, 1, tm), jnp.int32),
                   jax.ShapeDtypeStruct((LANES, 1), jnp.int32)],
        scratch_shapes=[pltpu.VMEM((LANES, 1), F32)],
        compiler_params=_cparams("arbitrary"),
        name="route_plan",
    )(meta, tri)
    return ids.reshape(r), rank.reshape(r), cnt.reshape(LANES)


def _row_copy(src, dst, i, j, sem):
    return pltpu.make_async_copy(src.at[pl.ds(i, 1)], dst.at[pl.ds(j, 1)], sem)


def _dispatch_kernel(pos_ref, h_ref, init_hbm, o_hbm, sem):
    del init_hbm
    ch = h_ref.shape[0]
    base = pl.program_id(0) * ch

    def issue(i, c):
        _row_copy(h_ref, o_hbm, i, pos_ref[base + i], sem).start()
        return c

    def drain(i, c):
        _row_copy(h_ref, o_hbm, i, 0, sem).wait()
        return c

    lax.fori_loop(0, ch, issue, 0, unroll=8)
    lax.fori_loop(0, ch, drain, 0, unroll=8)


def dispatch_rows(pos, h2, n_rows, ch):
    r, d = h2.shape
    return pl.pallas_call(
        _dispatch_kernel,
        grid_spec=pltpu.PrefetchScalarGridSpec(
            num_scalar_prefetch=1,
            grid=(r // ch,),
            in_specs=[pl.BlockSpec((ch, d), lambda i, p: (i, 0)), pl.BlockSpec(memory_space=pl.ANY)],
            out_specs=pl.BlockSpec(memory_space=pl.ANY),
            scratch_shapes=[pltpu.SemaphoreType.DMA],
        ),
        out_shape=jax.ShapeDtypeStruct((n_rows, d), h2.dtype),
        input_output_aliases={2: 0},
        compiler_params=_cparams("arbitrary"),
        name="dispatch_rows",
    )(pos, h2, jnp.zeros((n_rows, d), h2.dtype))


def _expert_kernel(ta_ref, tb_ref, nu_ref, hs_ref, wg_ref, wu_ref, wd_ref, y_ref, gu_s, dn_s):
    j = pl.program_id(0)
    d = hs_ref.shape[1]
    ff = wg_ref.shape[3]
    epg = wg_ref.shape[1]
    prev = jnp.maximum(j - 1, 0)

    @pl.when((j == 0) | (ta_ref[j] // epg != ta_ref[prev] // epg))
    def _():
        for e in range(epg):
            gu_s[e, :, :ff] = wg_ref[0, e].astype(BF16)
            gu_s[e, :, ff:] = wu_ref[0, e].astype(BF16)
            dn_s[e] = wd_ref[0, e].astype(BF16)

    @pl.when(j < nu_ref[0])
    def _():
        h = hs_ref[...].astype(BF16)
        for n, t_ref in enumerate((ta_ref, tb_ref)):
            e = t_ref[j] % epg
            hgu = jnp.dot(h, gu_s[e], preferred_element_type=F32)
            hg = hgu[:, :ff]
            hid = (hg * jax.nn.sigmoid(hg)) * hgu[:, ff:]
            y_ref[:, n * d:(n + 1) * d] = jnp.dot(hid.astype(BF16), dn_s[e], preferred_element_type=F32)

    @pl.when(j >= nu_ref[0])
    def _():
        y_ref[...] = jnp.zeros_like(y_ref)


def expert_pairs(tile_a, tile_b, n_used, hs, w_gate, w_up, w_down):
    rows, d = hs.shape
    ne, _, ff = w_gate.shape
    epg = EXPERTS_PER_GROUP
    nt = rows // ROW_TILE
    blk = lambda i, ta, tb, nu: (jnp.minimum(i, nu[0]), 0)
    grp = lambda shape: pl.BlockSpec((1, epg) + shape, lambda i, ta, tb, nu: (ta[i] // epg, 0, 0, 0),
                                     pipeline_mode=pl.Buffered(1))
    by_group = lambda w: w.reshape((ne // epg, epg) + w.shape[1:])
    return pl.pallas_call(
        _expert_kernel,
        grid_spec=pltpu.PrefetchScalarGridSpec(
            num_scalar_prefetch=3,
            grid=(nt,),
            in_specs=[pl.BlockSpec((ROW_TILE, d), blk), grp((d, ff)), grp((d, ff)), grp((ff, d))],
            out_specs=pl.BlockSpec((ROW_TILE, 2 * d), lambda i, ta, tb, nu: (i, 0)),
            scratch_shapes=[pltpu.VMEM((epg, d, 2 * ff), BF16), pltpu.VMEM((epg, ff, d), BF16)],
        ),
        out_shape=jax.ShapeDtypeStruct((rows, 2 * d), F32),
        compiler_params=_cparams("arbitrary"),
        name="expert_pairs",
    )(tile_a, tile_b, n_used, hs, by_group(w_gate), by_group(w_up), by_group(w_down))


def _combine_kernel(pos_ref, x_ref, meta_ref, gate_ref, y_hbm, o_ref, ybuf, sem):
    tm = x_ref.shape[0]
    d = x_ref.shape[1]
    base = pl.program_id(0) * tm

    def issue(i, c):
        _row_copy(y_hbm, ybuf, pos_ref[base + i], i, sem).start()
        return c

    def drain(i, c):
        _row_copy(y_hbm, ybuf, 0, i, sem).wait()
        return c

    lax.fori_loop(0, tm, issue, 0, unroll=8)
    lax.fori_loop(0, tm, drain, 0, unroll=8)
    meta = meta_ref[...]
    ff = (meta[:, META_WA:META_WA + 1] * ybuf[:, :d] + meta[:, META_WB:META_WB + 1] * ybuf[:, d:])
    o_ref[...] = x_ref[...] + gate_ref[0] * ff


def combine_rows(pos, x_mid, meta, mod3, y, *, seq, nseg, tm):
    r, d = x_mid.shape
    tiles_per_seq = seq // tm
    seg = lambda i: jnp.minimum(i // tiles_per_seq, nseg - 1)
    return pl.pallas_call(
        _combine_kernel,
        grid_spec=pltpu.PrefetchScalarGridSpec(
            num_scalar_prefetch=1,
            grid=(r // tm,),
            in_specs=[pl.BlockSpec((tm, d), lambda i, p: (i, 0)),
                      pl.BlockSpec((tm, LANES), lambda i, p: (i, 0)),
                      pl.BlockSpec((1, 1, d), lambda i, p: (seg(i) * N_MOD + 5, 0, 0)),
                      pl.BlockSpec(memory_space=pl.ANY)],
            out_specs=pl.BlockSpec((tm, d), lambda i, p: (i, 0)),
            scratch_shapes=[pltpu.VMEM((tm, 2 * d), F32), pltpu.SemaphoreType.DMA],
        ),
        out_shape=jax.ShapeDtypeStruct((r, d), F32),
        compiler_params=_cparams("arbitrary"),
        name="combine_rows",
    )(pos, x_mid, meta, mod3, y)


def moe_block(x_mid, h2, meta, mod3, w_gate, w_up, w_down, *, seq, nseg):
    r, d = x_mid.shape
    ids, rank, cnt = route_plan(meta, 512)
    tiles = (cnt + (ROW_TILE - 1)) // ROW_TILE
    incl = jnp.cumsum(tiles)
    n_used = incl[-1]
    lookup = lambda table, idx: jnp.sum(jnp.where(idx[:, None] == jnp.arange(LANES)[None, :], table[None, :], 0), axis=1)
    pos = lookup(incl - tiles, ids) * ROW_TILE + rank
    nt = r // ROW_TILE + N_BUCKETS
    last = jnp.minimum(jnp.arange(nt), n_used - 1)
    tile_bucket = jnp.sum((incl[None, :] <= last[:, None]).astype(jnp.int32), axis=1)
    ea, eb = _bucket_experts()
    tile_a = lookup(jnp.asarray(ea), tile_bucket)
    tile_b = lookup(jnp.asarray(eb), tile_bucket)
    hs = dispatch_rows(pos, h2, nt * ROW_TILE, 512)
    y = expert_pairs(tile_a, tile_b, n_used.reshape(1).astype(jnp.int32), hs, w_gate, w_up, w_down)
    return combine_rows(pos, x_mid, meta, mod3, y, seq=seq, nseg=nseg, tm=256)


def _rope_tables(seq, tm):
    t = np.arange(seq)
    row = (t // GRID_W).astype(np.float32)
    col = (t % GRID_W).astype(np.float32)
    half = HEAD_DIM // 2
    inv = jnp.asarray(ROPE_THETA, F32) ** (-jnp.arange(0, half, 2, dtype=F32) / half)
    ang = jnp.concatenate([jnp.asarray(row)[:, None] * inv, jnp.asarray(col)[:, None] * inv], axis=-1)
    cos = jnp.repeat(jnp.cos(ang), 2, axis=-1)
    sin = jnp.repeat(jnp.sin(ang), 2, axis=-1) * jnp.asarray(np.tile([-1.0, 1.0], half), F32)
    cos = jnp.tile(cos, (1, LANES // HEAD_DIM))
    sin = jnp.tile(sin, (1, LANES // HEAD_DIM))
    cos = jnp.concatenate([cos, jnp.ones((tm, LANES), F32)], axis=0)
    sin = jnp.concatenate([sin, jnp.zeros((tm, LANES), F32)], axis=0)
    return cos, sin


def kernel(x, c, ctx, c_ctx, w_ada, b_ada, norm1_g, w_in, na_q_norm, na_k_norm, na_rpb, gqa_q_norm, gqa_k_norm, hgrn_lb, hgrn_o_norm, w_out, norm2_g, w_route_group, b_route_group, w_route_expert, b_route_expert, w_exp_gate, w_exp_up, w_exp_down):
    b, s, d = x.shape
    l = ctx.shape[1]
    depth = w_ada.shape[0]
    assert s % 512 == 0 and (b * l) % 512 == 0 and s // GRID_W >= NA_WIN_R
    assert s % l == 0 and l % HG_BLOCK == 0
    nseg = b + 1
    n_lat = b * s
    n_ctx = b * l
    na_w, gq_qw = d // 4, d // 2
    gq_kw = gq_qw // 4
    hg_w = d // 4
    tm = 512

    c_all = jnp.zeros((16, d), F32).at[:b].set(c).at[b].set(c_ctx)
    mod = ada_mod(c_all, w_ada, b_ada)
    cos_t, sin_t = _rope_tables(s, tm)
    bd_f = _block_diag_ones(LANES, HEAD_DIM, F32)
    bd_b = _block_diag_ones(LANES, HEAD_DIM, BF16)
    p_lb = jax.nn.softmax(hgrn_lb.astype(F32), axis=0)
    lb_all = jnp.cumsum(p_lb, axis=0) - p_lb[0]
    tile2 = lambda g: jnp.tile(g, LANES // HEAD_DIM)

    xall = jnp.concatenate([x.reshape(n_lat, d), ctx.reshape(n_ctx, d)], axis=0)
    for layer in range(depth):
        ctx_out = layer < depth - 1
        mod3 = mod[layer].reshape(16 * N_MOD, 1, d)
        gains = jnp.zeros((8, LANES), F32)
        gains = gains.at[0].set(tile2(na_q_norm[layer])).at[1].set(tile2(na_k_norm[layer]))
        gains = gains.at[2].set(tile2(gqa_q_norm[layer])).at[3].set(tile2(gqa_k_norm[layer]))
        pa, pb, pc = in_projection(xall, mod3, norm1_g[layer][None], w_in[layer].astype(BF16), cos_t, sin_t,
                                   gains, bd_b, n_lat_rows=n_lat, seq=s, nseg=nseg, tm=tm)
        bias_tab = na_bias_table(na_rpb[layer], s // GRID_W)
        o_a = na_attention(pa, bias_tab, b=b, s=s, l=l, na_w=na_w)
        o_b = gqa_attention(pb, b=b, s=s, l=l, qw=gq_qw, kw=gq_kw, tq=128)
        lb4 = lb_all[layer].reshape(2, hg_w // LANES, 1, LANES)
        y_lat, y_ctx = hgrn_mixer(pc, lb4, tile2(hgrn_o_norm[layer])[None], bd_b, bd_f, b=b, s=s, l=l)

        w_route = jnp.zeros((d, LANES), F32).at[:, :N_GROUPS].set(w_route_group[layer])
        w_route = w_route.at[:, N_GROUPS:N_GROUPS + N_EXPERTS].set(w_route_expert[layer])
        b_route = jnp.zeros((1, LANES), F32).at[0, :N_GROUPS].set(b_route_group[layer])
        b_route = b_route.at[0, N_GROUPS:N_GROUPS + N_EXPERTS].set(b_route_expert[layer])
        if ctx_out:
            o_ac, o_bc = ctx_attention(pa, pb, b=b, s=s, l=l, na_w=na_w, qw=gq_qw, kw=gq_kw)
            mix_a = jnp.concatenate([o_a, o_ac], axis=0)
            mix_b = jnp.concatenate([o_b, o_bc], axis=0)
            y_c = jnp.concatenate([y_lat, y_ctx], axis=0)
            n_rows = n_lat + n_ctx
        else:
            mix_a, mix_b, y_c, n_rows = o_a, o_b, y_lat, n_lat
        x_mid, h2, meta = out_projection(xall, mix_a, mix_b, y_c, w_out[layer].astype(BF16), mod3,
                                          norm2_g[layer][None], w_route, b_route, n_rows=n_rows, seq=s,
                                          nseg=nseg, tm=tm)
        xall = moe_block(x_mid, h2, meta, mod3, w_exp_gate[layer], w_exp_up[layer], w_exp_down[layer],
                         seq=s, nseg=nseg)
    return xall[:n_lat].reshape(b, s, d)
```

```python
import functools

import jax
import jax.numpy as jnp
import numpy as np
from jax import lax
from jax.experimental import pallas as pl
from jax.experimental.pallas import tpu as pltpu

F32 = jnp.float32
BF16 = jnp.bfloat16
HIGHEST = lax.Precision.HIGHEST

HEAD_DIM = 64
GRID_W = 64
NA_WIN_R = 8
NA_WIN_C = 16
ROPE_THETA = 10000.0
HGRN_CHUNK = 16
N_GROUPS = 4
EXPERTS_PER_GROUP = 8
N_EXPERTS = N_GROUPS * EXPERTS_PER_GROUP
N_MOD = 6
EPS = 1e-6
NEG_INF = -1e30
LB_FLOOR = 1e-20
LANES = 128
VMEM_LIMIT = 56 * 1024 * 1024


def _cparams(*sem):
    return pltpu.CompilerParams(dimension_semantics=sem, vmem_limit_bytes=VMEM_LIMIT)


def _block_diag_ones(n, blk, dtype):
    i = np.arange(n)
    return jnp.asarray((i[:, None] // blk) == (i[None, :] // blk), dtype=dtype)


def _ada_kernel(c_ref, w_ref, b_ref, o_ref):
    c = c_ref[...]
    s = c * jax.nn.sigmoid(c)
    o_ref[0] = jnp.dot(s, w_ref[0], precision=HIGHEST, preferred_element_type=F32) + b_ref[0]


def ada_mod(c_all, w_ada, b_ada):
    depth, d, n = w_ada.shape
    tn = 1536
    return pl.pallas_call(
        _ada_kernel,
        grid=(depth, n // tn),
        in_specs=[
            pl.BlockSpec((16, d), lambda l, j: (0, 0)),
            pl.BlockSpec((1, d, tn), lambda l, j: (l, 0, j)),
            pl.BlockSpec((1, 1, tn), lambda l, j: (l, 0, j)),
        ],
        out_specs=pl.BlockSpec((1, 16, tn), lambda l, j: (l, 0, j)),
        out_shape=jax.ShapeDtypeStruct((depth, 16, n), F32),
        compiler_params=_cparams("parallel", "parallel"),
        name="ada_mod",
    )(c_all, w_ada, b_ada.reshape(depth, 1, n))


def _seg_inv_rms(x, bd):
    xs = x * x
    hi = xs.astype(BF16)
    lo = (xs - hi.astype(F32)).astype(BF16)
    ss = jnp.dot(hi, bd, preferred_element_type=F32) + jnp.dot(lo, bd, preferred_element_type=F32)
    return lax.rsqrt(ss * (1.0 / HEAD_DIM) + EPS)


def _pair_swap(x):
    lane = lax.broadcasted_iota(jnp.int32, x.shape, 1)
    return jnp.where((lane & 1) == 0, pltpu.roll(x, LANES - 1, 1), pltpu.roll(x, 1, 1))


def _inproj_kernel(x_ref, g1_ref, shift_ref, scale_ref, w_ref, cos_ref, sin_ref, gains_ref, bd_ref,
                   oa_ref, ob_ref, oc_ref, *, na_w, gq_qw, gq_kw):
    x = x_ref[...]
    ms = jnp.mean(x * x, axis=-1, keepdims=True)
    h = x * lax.rsqrt(ms + EPS) * g1_ref[0]
    h = h * (1.0 + scale_ref[0]) + shift_ref[0]
    p = jnp.dot(h.astype(BF16), w_ref[...], preferred_element_type=F32)
    bd = bd_ref[...]
    cos = cos_ref[...]
    sin = sin_ref[...]
    qscale = HEAD_DIM ** -0.5

    def normed(col, gain_row):
        xb = p[:, col:col + LANES]
        return xb * _seg_inv_rms(xb, bd) * gains_ref[gain_row:gain_row + 1, :]

    def rope(xn):
        return xn * cos + _pair_swap(xn) * sin

    for j in range(na_w // LANES):
        c = j * LANES
        oa_ref[:, c:c + LANES] = (normed(c, 0) * qscale).astype(BF16)
        oa_ref[:, na_w + c:na_w + c + LANES] = normed(na_w + c, 1).astype(BF16)
    oa_ref[:, 2 * na_w:3 * na_w] = p[:, 2 * na_w:3 * na_w].astype(BF16)
    b0 = 3 * na_w
    for j in range(gq_qw // LANES):
        c = j * LANES
        ob_ref[:, c:c + LANES] = (rope(normed(b0 + c, 2)) * qscale).astype(BF16)
    for j in range(gq_kw // LANES):
        c = gq_qw + j * LANES
        ob_ref[:, c:c + LANES] = rope(normed(b0 + c, 3)).astype(BF16)
    ob_ref[:, gq_qw + gq_kw:] = p[:, b0 + gq_qw + gq_kw:b0 + gq_qw + 2 * gq_kw].astype(BF16)
    oc_ref[...] = p[:, b0 + gq_qw + 2 * gq_kw:]


def in_projection(xall, mod3, layer_g1, w_in_bf, cos_t, sin_t, gains, bd, *, n_lat_rows, seq, nseg, tm):
    r, d = xall.shape
    d_in = w_in_bf.shape[1]
    na_w = d // 4
    gq_qw = d // 2
    gq_kw = gq_qw // 4
    c_w = d_in - 3 * na_w - gq_qw - 2 * gq_kw
    lat_tiles = n_lat_rows // tm
    tiles_per_seq = seq // tm

    def seg(i):
        return jnp.minimum(i // tiles_per_seq, nseg - 1)

    def rope_blk(i):
        return jnp.where(i < lat_tiles, i % tiles_per_seq, tiles_per_seq)

    kern = functools.partial(_inproj_kernel, na_w=na_w, gq_qw=gq_qw, gq_kw=gq_kw)
    return pl.pallas_call(
        kern,
        grid=(r // tm,),
        in_specs=[
            pl.BlockSpec((tm, d), lambda i: (i, 0)),
            pl.BlockSpec((1, d), lambda i: (0, 0)),
            pl.BlockSpec((1, 1, d), lambda i: (seg(i) * N_MOD + 0, 0, 0)),
            pl.BlockSpec((1, 1, d), lambda i: (seg(i) * N_MOD + 1, 0, 0)),
            pl.BlockSpec((d, d_in), lambda i: (0, 0)),
            pl.BlockSpec((tm, LANES), lambda i: (rope_blk(i), 0)),
            pl.BlockSpec((tm, LANES), lambda i: (rope_blk(i), 0)),
            pl.BlockSpec((8, LANES), lambda i: (0, 0)),
            pl.BlockSpec((LANES, LANES), lambda i: (0, 0)),
        ],
        out_specs=[
            pl.BlockSpec((tm, 3 * na_w), lambda i: (i, 0)),
            pl.BlockSpec((tm, gq_qw + 2 * gq_kw), lambda i: (i, 0)),
            pl.BlockSpec((tm, c_w), lambda i: (i, 0)),
        ],
        out_shape=[
            jax.ShapeDtypeStruct((r, 3 * na_w), BF16),
            jax.ShapeDtypeStruct((r, gq_qw + 2 * gq_kw), BF16),
            jax.ShapeDtypeStruct((r, c_w), F32),
        ],
        compiler_params=_cparams("parallel"),
        name="in_projection",
    )(xall, layer_g1, mod3, mod3, w_in_bf, cos_t, sin_t, gains, bd)


def _head(j):
    return slice(j * HEAD_DIM, (j + 1) * HEAD_DIM)


def _with_ones(v):
    return jnp.concatenate([v, jnp.ones_like(v)], axis=1)


def _scores(q, k):
    return lax.dot_general(q, k, (((1,), (1,)), ((), ())), preferred_element_type=F32)


def _normalise(o):
    return o[:, :HEAD_DIM] / o[:, HEAD_DIM:HEAD_DIM + 1]


def _softmax_attend(q, k, v1):
    s = _scores(q, k)
    p = jnp.exp((s - jnp.max(s, axis=-1, keepdims=True)).astype(BF16))
    return _normalise(jnp.dot(p, v1, preferred_element_type=F32))


def _grouped_attend(q_ref, k_of, v_of, n_kv, grp):
    t = q_ref.shape[0]
    outs = []
    for j in range(n_kv):
        q4 = jnp.concatenate([q_ref[:, _head(j * grp + g)] for g in range(grp)], axis=0)
        o = _softmax_attend(q4, k_of(j), v_of(j))
        outs += [o[g * t:(g + 1) * t] for g in range(grp)]
    return jnp.concatenate(outs, axis=1)


def _gqa_kernel(q_ref, kl_ref, vl_ref, kc_ref, vc_ref, o_ref, k_s, v_s, *, n_kv, grp):
    s_len = kl_ref.shape[0]

    @pl.when(pl.program_id(1) == 0)
    def _():
        for j in range(n_kv):
            k_s[j, :s_len, :] = kl_ref[:, _head(j)]
            k_s[j, s_len:, :] = kc_ref[:, _head(j)]
            v_s[j, :s_len, :] = _with_ones(vl_ref[:, _head(j)])
            v_s[j, s_len:, :] = _with_ones(vc_ref[:, _head(j)])

    o = _grouped_attend(q_ref, lambda j: k_s[j], lambda j: v_s[j], n_kv, grp)
    o_ref[...] = o.astype(o_ref.dtype)


def gqa_attention(pb, *, b, s, l, qw, kw, tq):
    n_lat = b * s
    n_kv = kw // HEAD_DIM
    grp = qw // kw
    assert kw == LANES and qw % kw == 0
    kcol, vcol = qw // kw, qw // kw + 1
    kern = functools.partial(_gqa_kernel, n_kv=n_kv, grp=grp)
    return pl.pallas_call(
        kern,
        grid=(b, s // tq),
        in_specs=[
            pl.BlockSpec((tq, qw), lambda i, j: (i * (s // tq) + j, 0)),
            pl.BlockSpec((s, kw), lambda i, j: (i, kcol)),
            pl.BlockSpec((s, kw), lambda i, j: (i, vcol)),
            pl.BlockSpec((l, kw), lambda i, j: (n_lat // l + i, kcol)),
            pl.BlockSpec((l, kw), lambda i, j: (n_lat // l + i, vcol)),
        ],
        out_specs=pl.BlockSpec((tq, qw), lambda i, j: (i * (s // tq) + j, 0)),
        out_shape=jax.ShapeDtypeStruct((n_lat, qw), BF16),
        scratch_shapes=[pltpu.VMEM((n_kv, s + l, HEAD_DIM), BF16), pltpu.VMEM((n_kv, s + l, 2 * HEAD_DIM), BF16)],
        compiler_params=_cparams("parallel", "arbitrary"),
        name="gqa_attention",
    )(pb, pb, pb, pb, pb)


def _ctx_attn_kernel(qa_ref, ka_ref, va_ref, qb_ref, kb_ref, vb_ref, oa_ref, ob_ref, *, n_kv, grp):
    na_h = qa_ref.shape[1] // HEAD_DIM
    oa = [_softmax_attend(qa_ref[:, _head(h)], ka_ref[:, _head(h)], _with_ones(va_ref[:, _head(h)]))
          for h in range(na_h)]
    oa_ref[...] = jnp.concatenate(oa, axis=1).astype(oa_ref.dtype)
    ob = _grouped_attend(qb_ref, lambda j: kb_ref[:, _head(j)], lambda j: _with_ones(vb_ref[:, _head(j)]),
                         n_kv, grp)
    ob_ref[...] = ob.astype(ob_ref.dtype)


def ctx_attention(pa, pb, *, b, s, l, na_w, qw, kw):
    r0 = (b * s) // l
    grp = qw // kw
    kern = functools.partial(_ctx_attn_kernel, n_kv=kw // HEAD_DIM, grp=grp)
    a_spec = lambda m: pl.BlockSpec((l, na_w), lambda i: (r0 + i, m))
    return pl.pallas_call(
        kern,
        grid=(b,),
        in_specs=[a_spec(0), a_spec(1), a_spec(2),
                  pl.BlockSpec((l, qw), lambda i: (r0 + i, 0)),
                  pl.BlockSpec((l, kw), lambda i: (r0 + i, grp)),
                  pl.BlockSpec((l, kw), lambda i: (r0 + i, grp + 1))],
        out_specs=[pl.BlockSpec((l, na_w), lambda i: (i, 0)), pl.BlockSpec((l, qw), lambda i: (i, 0))],
        out_shape=[jax.ShapeDtypeStruct((b * l, na_w), BF16), jax.ShapeDtypeStruct((b * l, qw), BF16)],
        compiler_params=_cparams("parallel"),
        name="ctx_attention",
    )(pa, pa, pa, pb, pb, pb)


NA_QROWS = 4
NA_UNION = NA_WIN_R + NA_QROWS


def _na_block_geometry(rows):
    wu = min(rows, NA_UNION)
    wr = min(NA_WIN_R, rows)
    nblk = rows // NA_QROWS
    sig, u0s = [], []
    for blk in range(nblk):
        r0 = blk * NA_QROWS
        u0 = int(np.clip(r0 - wr // 2, 0, rows - wu))
        u0s.append(u0)
        sig.append(tuple((r0 + j - u0, int(np.clip(r0 + j - wr // 2, 0, rows - wr)) - u0) for j in range(NA_QROWS)))
    cls = [int(blk > 0) + int(blk == nblk - 1) for blk in range(nblk)]
    reps = {}
    for blk in range(nblk):
        assert reps.setdefault(cls[blk], sig[blk]) == sig[blk]
    return wu, wr, [reps.get(c, reps[0]) for c in range(3)]


def _na_kernel(q_ref, k_ref, v_ref, kc_ref, vc_ref, bias_ref, o_ref, k_s, v_s, kc_s, vc_s, *, rows, wu, wr):
    nh = q_ref.shape[1] // HEAD_DIM
    nblk = rows // NA_QROWS
    nq = NA_QROWS * GRID_W
    for h in range(nh):
        k_s[h] = k_ref[:, _head(h)]
        v_s[h] = _with_ones(v_ref[:, _head(h)])
        kc_s[h] = kc_ref[:, _head(h)]
        vc_s[h] = _with_ones(vc_ref[:, _head(h)])

    def body(blk, carry):
        u0 = jnp.clip(blk * NA_QROWS - wr // 2, 0, rows - wu)
        cls = jnp.minimum(blk, 1) + jnp.maximum(blk - (nblk - 2), 0)
        q_rows = pl.ds(pl.multiple_of(blk * nq, nq), nq)
        k_rows = pl.ds(pl.multiple_of(u0 * GRID_W, GRID_W), wu * GRID_W)
        outs = []
        for h in range(nh):
            q = q_ref[q_rows, _head(h)]
            s_nb = _scores(q, k_s[h, k_rows, :]) + bias_ref[h, cls]
            s_cx = _scores(q, kc_s[h])
            m = jnp.maximum(jnp.max(s_nb, axis=-1, keepdims=True), jnp.max(s_cx, axis=-1, keepdims=True))
            p_nb = jnp.exp((s_nb - m).astype(BF16))
            p_cx = jnp.exp((s_cx - m).astype(BF16))
            outs.append(_normalise(jnp.dot(p_nb, v_s[h, k_rows, :], preferred_element_type=F32)
                                   + jnp.dot(p_cx, vc_s[h], preferred_element_type=F32)))
        o_ref[q_rows, :] = jnp.concatenate(outs, axis=1).astype(o_ref.dtype)
        return carry

    lax.fori_loop(0, nblk, body, 0)


def na_attention(pa, bias_tab, *, b, s, l, na_w):
    n_lat = b * s
    nh = na_w // HEAD_DIM
    rows = s // GRID_W
    wu, wr, _ = _na_block_geometry(rows)
    kern = functools.partial(_na_kernel, rows=rows, wu=wu, wr=wr)
    lat = lambda m: pl.BlockSpec((s, na_w), lambda i: (i, m))
    cx = lambda m: pl.BlockSpec((l, na_w), lambda i: (n_lat // l + i, m))
    return pl.pallas_call(
        kern,
        grid=(b,),
        in_specs=[lat(0), lat(1), lat(2), cx(1), cx(2),
                  pl.BlockSpec(bias_tab.shape, lambda i: (0, 0, 0, 0))],
        out_specs=pl.BlockSpec((s, na_w), lambda i: (i, 0)),
        out_shape=jax.ShapeDtypeStruct((n_lat, na_w), BF16),
        scratch_shapes=[pltpu.VMEM((nh, s, HEAD_DIM), BF16), pltpu.VMEM((nh, s, 2 * HEAD_DIM), BF16),
                        pltpu.VMEM((nh, l, HEAD_DIM), BF16), pltpu.VMEM((nh, l, 2 * HEAD_DIM), BF16)],
        compiler_params=_cparams("parallel"),
        name="na_attention",
    )(pa, pa, pa, pa, pa, bias_tab)


def na_bias_table(rpb, rows):
    wu, wr, reps = _na_block_geometry(rows)
    h = rpb.shape[0]
    r_off = np.array([[rj for rj, _ in rep] for rep in reps])
    s_off = np.array([[sj for _, sj in rep] for rep in reps])
    kr = np.arange(wu)
    row_ok = (kr >= s_off[..., None]) & (kr < s_off[..., None] + wr)
    dr = kr - r_off[..., None] + (NA_WIN_R - 1)
    cidx = np.arange(GRID_W)
    col_start = np.clip(cidx - NA_WIN_C // 2, 0, GRID_W - NA_WIN_C)
    col_ok = (cidx[None, :] >= col_start[:, None]) & (cidx[None, :] < col_start[:, None] + NA_WIN_C)
    dc = np.clip(cidx[None, :] - cidx[:, None] + (NA_WIN_C - 1), 0, 2 * NA_WIN_C - 2)
    sel_r = jnp.asarray((dr[..., None] == np.arange(2 * NA_WIN_R - 1)) & row_ok[..., None], F32)
    sel_c = jnp.asarray(dc[:, :, None] == np.arange(2 * NA_WIN_C - 1), F32)
    bias = jnp.einsum("cjki,hid->hcjkd", sel_r, rpb.astype(F32), precision=HIGHEST)
    bias = jnp.einsum("hcjkd,qxd->hcjqkx", bias, sel_c, precision=HIGHEST)
    ok = row_ok[None, :, :, None, :, None] & col_ok[None, None, None, :, None, :]
    bias = jnp.where(jnp.asarray(ok), bias, NEG_INF)
    return bias.reshape(h, len(reps), NA_QROWS * GRID_W, wu * GRID_W)


HG_BLOCK = 128


def _hgrn_block(q_ref, v_ref, z_ref, o_acc, lb, bdb, bdf, st, blk, *, reverse):
    c = HGRN_CHUNK
    ncb = HG_BLOCK // c
    lbm = jnp.maximum(lb, LB_FLOOR)
    one_m_lb = 1.0 - lb
    scale = HEAD_DIM ** -0.5
    t_idx = lax.broadcasted_iota(jnp.int32, (ncb, c, LANES), 1)
    edge = 0 if reverse else c - 1

    def bs(x, s):
        return jnp.broadcast_to(x[:, s:s + 1, :], x.shape)

    def seen(s):
        return (t_idx <= s) if reverse else (t_idx >= s)

    if True:
        r0 = pl.multiple_of(blk * HG_BLOCK, HG_BLOCK)
        z = z_ref[pl.ds(r0, HG_BLOCK), :]
        q = q_ref[pl.ds(r0, HG_BLOCK), :] * scale
        v = v_ref[pl.ds(r0, HG_BLOCK), :]
        f = one_m_lb * jax.nn.sigmoid(z) + lbm
        k = one_m_lb * jax.nn.sigmoid(-z) - (lbm - lb)
        logf = jnp.log2(f).reshape(ncb, c, LANES)
        cum = jnp.zeros_like(logf)
        for s in range(c):
            cum = cum + jnp.where(seen(s), bs(logf, s), 0.0)
        q3 = q.reshape(ncb, c, LANES)
        k3 = k.reshape(ncb, c, LANES)
        v3 = v.reshape(ncb, c, LANES)
        o3 = jnp.zeros_like(cum)
        for s in range(c):
            d = jnp.where(seen(s), cum - bs(cum, s), NEG_INF)
            w = q3 * bs(k3, s) * jnp.exp2(d)
            a = jnp.dot(w.reshape(HG_BLOCK, LANES).astype(BF16), bdb, preferred_element_type=F32)
            o3 = o3 + a.reshape(ncb, c, LANES) * bs(v3, s)
        cum_edge = bs(cum, edge)
        qe = (q3 * jnp.exp2(cum)).astype(BF16)
        kd = (k3 * jnp.exp2(cum_edge - cum)).astype(BF16)
        vb = v3.astype(BF16)
        chunk_decay = jnp.exp2(cum_edge)
        outs = [None] * ncb
        for n in (range(ncb - 1, -1, -1) if reverse else range(ncb)):
            o_inter = lax.dot_general(qe[n], st.astype(BF16), (((1,), (1,)), ((), ())),
                                      preferred_element_type=F32)
            outs[n] = o3[n] + o_inter
            u_t = lax.dot_general(vb[n], kd[n], (((0,), (0,)), ((), ())), preferred_element_type=F32)
            st = chunk_decay[n, 0:1, :] * st + u_t * bdf
        o_acc[pl.ds(r0, HG_BLOCK), :] = jnp.concatenate(outs, axis=0)
        return st


def _hgrn_segment(q_ref, v_ref, zf_ref, zb_ref, of_acc, ob_acc, lb_ref, bdb, bdf, st_f, st_b):
    nblk = q_ref.shape[0] // HG_BLOCK

    def body(i, carry):
        st_f, st_b = carry
        st_f = _hgrn_block(q_ref, v_ref, zf_ref, of_acc, lb_ref[0, 0], bdb, bdf, st_f, i, reverse=False)
        st_b = _hgrn_block(q_ref, v_ref, zb_ref, ob_acc, lb_ref[1, 0], bdb, bdf, st_b, nblk - 1 - i, reverse=True)
        return st_f, st_b

    return lax.fori_loop(0, nblk, body, (st_f, st_b))


def _hgrn_kernel(ql_ref, qc_ref, vl_ref, vc_ref, zfl_ref, zfc_ref, zbl_ref, zbc_ref, gl_ref, gc_ref,
                 lb_ref, gain_ref, bdb_ref, bdf_ref, yl_ref, yc_ref, olf_acc, olb_acc, ocf_acc, ocb_acc):
    bdb = bdb_ref[...]
    bdf = bdf_ref[...]
    zero = jnp.zeros((LANES, LANES), F32)
    st_f, st_b = _hgrn_segment(qc_ref, vc_ref, zfc_ref, zbc_ref, ocf_acc, ocb_acc, lb_ref, bdb, bdf, zero, zero)
    _hgrn_segment(ql_ref, vl_ref, zfl_ref, zbl_ref, olf_acc, olb_acc, lb_ref, bdb, bdf, st_f, st_b)
    for acc_f, acc_b, g_ref, y_ref in ((olf_acc, olb_acc, gl_ref, yl_ref), (ocf_acc, ocb_acc, gc_ref, yc_ref)):
        o = acc_f[...] + acc_b[...]
        g = g_ref[...]
        y = o * _seg_inv_rms(o, bdb) * gain_ref[...]
        y_ref[...] = (y * (g * jax.nn.sigmoid(g))).astype(y_ref.dtype)


def hgrn_mixer(pc, lb, gain128, bdb, bdf, *, b, s, l):
    w = pc.shape[1] // 5
    nj = w // LANES
    n_lat = b * s
    lat = lambda m: pl.BlockSpec((s, LANES), lambda i, j: (i, m * nj + j))
    cx = lambda m: pl.BlockSpec((l, LANES), lambda i, j: (n_lat // l + i, m * nj + j))
    in_specs = []
    for m in (0, 1, 2, 3, 4):
        in_specs += [lat(m), cx(m)]
    in_specs += [pl.BlockSpec((2, 1, 1, LANES), lambda i, j: (0, j, 0, 0)),
                 pl.BlockSpec((1, LANES), lambda i, j: (0, 0)),
                 pl.BlockSpec((LANES, LANES), lambda i, j: (0, 0)),
                 pl.BlockSpec((LANES, LANES), lambda i, j: (0, 0))]
    return pl.pallas_call(
        _hgrn_kernel,
        grid=(b, nj),
        in_specs=in_specs,
        out_specs=[pl.BlockSpec((s, LANES), lambda i, j: (i, j)),
                   pl.BlockSpec((l, LANES), lambda i, j: (i, j))],
        out_shape=[jax.ShapeDtypeStruct((n_lat, w), BF16), jax.ShapeDtypeStruct((b * l, w), BF16)],
        scratch_shapes=[pltpu.VMEM((s, LANES), F32), pltpu.VMEM((s, LANES), F32),
                        pltpu.VMEM((l, LANES), F32), pltpu.VMEM((l, LANES), F32)],
        compiler_params=_cparams("parallel", "parallel"),
        name="hgrn_mixer",
    )(*([pc] * 10), lb, gain128, bdb, bdf)


def _outproj_kernel(x_ref, ma_ref, mb_ref, mc_ref, w_ref, gate_ref, g2_ref, shift_ref, scale_ref,
                    wr_ref, br_ref, xo_ref, h2_ref, lg_ref, *, wa, wb):
    w = w_ref[...]
    y = jnp.dot(ma_ref[...], w[:wa], preferred_element_type=F32)
    y = y + jnp.dot(mb_ref[...], w[wa:wa + wb], preferred_element_type=F32)
    y = y + jnp.dot(mc_ref[...], w[wa + wb:], preferred_element_type=F32)
    x = x_ref[...] + gate_ref[0] * y
    xo_ref[...] = x
    ms = jnp.mean(x * x, axis=-1, keepdims=True)
    h = x * lax.rsqrt(ms + EPS) * g2_ref[0]
    h = h * (1.0 + scale_ref[0]) + shift_ref[0]
    h2_ref[...] = h
    h_hi = h.astype(BF16)
    h_lo = (h - h_hi.astype(F32)).astype(BF16)
    logits = (jnp.dot(h_hi, wr_ref[0], preferred_element_type=F32)
              + jnp.dot(h_lo, wr_ref[0], preferred_element_type=F32)
              + jnp.dot(h_hi, wr_ref[1], preferred_element_type=F32)) + br_ref[...]
    lg_ref[...] = _route_meta(logits)


def out_projection(xall, mix_a, mix_b, mix_c, w_out_bf, mod3, layer_g2, w_route, b_route, *, n_rows, seq, nseg, tm):
    r, d = n_rows, xall.shape[1]
    w_hi = w_route.astype(BF16)
    w_route = jnp.stack([w_hi, (w_route - w_hi.astype(F32)).astype(BF16)])
    wa, wb, wc = mix_a.shape[1], mix_b.shape[1], mix_c.shape[1]
    tiles_per_seq = seq // tm

    def seg(i):
        return jnp.minimum(i // tiles_per_seq, nseg - 1)

    def modspec(m):
        return pl.BlockSpec((1, 1, d), lambda i: (seg(i) * N_MOD + m, 0, 0))

    row = lambda wdt: pl.BlockSpec((tm, wdt), lambda i: (i, 0))
    kern = functools.partial(_outproj_kernel, wa=wa, wb=wb)
    return pl.pallas_call(
        kern,
        grid=(r // tm,),
        in_specs=[row(d), row(wa), row(wb), row(wc),
                  pl.BlockSpec((wa + wb + wc, d), lambda i: (0, 0)),
                  modspec(2),
                  pl.BlockSpec((1, d), lambda i: (0, 0)),
                  modspec(3), modspec(4),
                  pl.BlockSpec((2, d, LANES), lambda i: (0, 0, 0)),
                  pl.BlockSpec((1, LANES), lambda i: (0, 0))],
        out_specs=[row(d), row(d), row(LANES)],
        out_shape=[jax.ShapeDtypeStruct((r, d), F32),
                   jax.ShapeDtypeStruct((r, d), F32),
                   jax.ShapeDtypeStruct((r, LANES), F32)],
        compiler_params=_cparams("parallel"),
        name="out_projection",
    )(xall, mix_a, mix_b, mix_c, w_out_bf, mod3, layer_g2, mod3, mod3, w_route, b_route)


PAIRS_PER_GROUP = EXPERTS_PER_GROUP * (EXPERTS_PER_GROUP - 1) // 2
N_BUCKETS = N_GROUPS * PAIRS_PER_GROUP
ROW_TILE = 256
META_BUCKET, META_WA, META_WB = 0, 1, 2


def _bucket_experts():
    ea = np.zeros((LANES,), np.int32)
    eb = np.zeros((LANES,), np.int32)
    for g in range(N_GROUPS):
        k = g * PAIRS_PER_GROUP
        for a in range(EXPERTS_PER_GROUP):
            for b in range(a + 1, EXPERTS_PER_GROUP):
                ea[k], eb[k] = g * EXPERTS_PER_GROUP + a, g * EXPERTS_PER_GROUP + b
                k += 1
    return ea, eb


def _route_meta(logits):
    lane = lax.broadcasted_iota(jnp.int32, logits.shape, 1).astype(F32)
    is_g = lane < N_GROUPS
    gl = jnp.where(is_g, logits, -jnp.inf)
    gmax = jnp.max(gl, axis=-1, keepdims=True)
    g_idx = jnp.min(jnp.where(gl == gmax, lane, LANES), axis=-1, keepdims=True)
    gsum = jnp.sum(jnp.where(is_g, jnp.exp(gl - gmax), 0.0), axis=-1, keepdims=True)
    g_top = 1.0 / gsum
    lo = N_GROUPS + g_idx * EXPERTS_PER_GROUP
    in_grp = (lane >= lo) & (lane < lo + EXPERTS_PER_GROUP)
    el = jnp.where(in_grp, logits, -jnp.inf)
    m1 = jnp.max(el, axis=-1, keepdims=True)
    i1 = jnp.min(jnp.where(el == m1, lane, LANES), axis=-1, keepdims=True)
    el2 = jnp.where(lane == i1, -jnp.inf, el)
    m2 = jnp.max(el2, axis=-1, keepdims=True)
    i2 = jnp.min(jnp.where(el2 == m2, lane, LANES), axis=-1, keepdims=True)
    e21 = jnp.exp(m2 - m1)
    w1 = g_top / (1.0 + e21)
    w2 = e21 * w1
    first_low = i1 < i2
    la = jnp.minimum(i1, i2) - lo
    lb = jnp.maximum(i1, i2) - lo
    pair = la * (2 * EXPERTS_PER_GROUP - 1 - la) * 0.5 + (lb - la - 1.0)
    bucket = g_idx * PAIRS_PER_GROUP + pair
    wa = jnp.where(first_low, w1, w2)
    wb = jnp.where(first_low, w2, w1)
    return jnp.where(lane == META_BUCKET, bucket,
                     jnp.where(lane == META_WA, wa, jnp.where(lane == META_WB, wb, 0.0)))


def _plan_kernel(meta_ref, tri_ref, ids_ref, rank_ref, cnt_ref, carry):
    @pl.when(pl.program_id(0) == 0)
    def _():
        carry[...] = jnp.zeros_like(carry)

    ids = meta_ref[...].T[META_BUCKET:META_BUCKET + 1, :]
    sub = lax.broadcasted_iota(jnp.int32, (LANES, ids.shape[1]), 0).astype(F32)
    onehot = (sub == ids).astype(F32)
    before = jnp.dot(onehot.astype(BF16), tri_ref[...], preferred_element_type=F32)
    rank = jnp.sum(onehot * (before + carry[...]), axis=0, keepdims=True)
    ids_ref[0] = ids.astype(jnp.int32)
    rank_ref[0] = rank.astype(jnp.int32)
    total = carry[...] + jnp.sum(onehot, axis=1, keepdims=True)
    carry[...] = total
    cnt_ref[...] = total.astype(jnp.int32)


def route_plan(meta, tm):
    r = meta.shape[0]
    nt = r // tm
    i = np.arange(tm)
    tri = jnp.asarray(i[:, None] < i[None, :], BF16)
    ids, rank, cnt = pl.pallas_call(
        _plan_kernel,
        grid=(nt,),
        in_specs=[pl.BlockSpec((tm, LANES), lambda i: (i, 0)),
                  pl.BlockSpec((tm, tm), lambda i: (0, 0))],
        out_specs=[pl.BlockSpec((1, 1, tm), lambda i: (i, 0, 0)),
                   pl.BlockSpec((1, 1, tm), lambda i: (i, 0, 0)),
                   pl.BlockSpec((LANES, 1), lambda i: (0, 0))],
        out_shape=[jax.ShapeDtypeStruct((nt, 1, tm), jnp.int32),
                   jax.ShapeDtypeStruct((nt, 1, tm), jnp.int32),
                   jax.ShapeDtypeStruct((LANES, 1), jnp.int32)],
        scratch_shapes=[pltpu.VMEM((LANES, 1), F32)],
        compiler_params=_cparams("arbitrary"),
        name="route_plan",
    )(meta, tri)
    return ids.reshape(r), rank.reshape(r), cnt.reshape(LANES)


def _row_copy(src, dst, i, j, sem):
    return pltpu.make_async_copy(src.at[pl.ds(i, 1)], dst.at[pl.ds(j, 1)], sem)


def _dispatch_kernel(pos_ref, h_ref, init_hbm, o_hbm, sem):
    del init_hbm
    ch = h_ref.shape[0]
    base = pl.program_id(0) * ch

    def issue(i, c):
        _row_copy(h_ref, o_hbm, i, pos_ref[base + i], sem).start()
        return c

    def drain(i, c):
        _row_copy(h_ref, o_hbm, i, 0, sem).wait()
        return c

    lax.fori_loop(0, ch, issue, 0, unroll=8)
    lax.fori_loop(0, ch, drain, 0, unroll=8)


def dispatch_rows(pos, h2, n_rows, ch):
    r, d = h2.shape
    return pl.pallas_call(
        _dispatch_kernel,
        grid_spec=pltpu.PrefetchScalarGridSpec(
            num_scalar_prefetch=1,
            grid=(r // ch,),
            in_specs=[pl.BlockSpec((ch, d), lambda i, p: (i, 0)), pl.BlockSpec(memory_space=pl.ANY)],
            out_specs=pl.BlockSpec(memory_space=pl.ANY),
            scratch_shapes=[pltpu.SemaphoreType.DMA],
        ),
        out_shape=jax.ShapeDtypeStruct((n_rows, d), h2.dtype),
        input_output_aliases={2: 0},
        compiler_params=_cparams("arbitrary"),
        name="dispatch_rows",
    )(pos, h2, jnp.zeros((n_rows, d), h2.dtype))


def _expert_kernel(ta_ref, tb_ref, nu_ref, hs_ref, wg_ref, wu_ref, wd_ref, y_ref, gu_s, dn_s):
    j = pl.program_id(0)
    d = hs_ref.shape[1]
    ff = wg_ref.shape[3]
    epg = wg_ref.shape[1]
    prev = jnp.maximum(j - 1, 0)

    @pl.when((j == 0) | (ta_ref[j] // epg != ta_ref[prev] // epg))
    def _():
        for e in range(epg):
            gu_s[e, :, :ff] = wg_ref[0, e].astype(BF16)
            gu_s[e, :, ff:] = wu_ref[0, e].astype(BF16)
            dn_s[e] = wd_ref[0, e].astype(BF16)

    @pl.when(j < nu_ref[0])
    def _():
        h = hs_ref[...].astype(BF16)
        for n, t_ref in enumerate((ta_ref, tb_ref)):
            e = t_ref[j] % epg
            hgu = jnp.dot(h, gu_s[e], preferred_element_type=F32)
            hg = hgu[:, :ff]
            hid = (hg * jax.nn.sigmoid(hg)) * hgu[:, ff:]
            y_ref[:, n * d:(n + 1) * d] = jnp.dot(hid.astype(BF16), dn_s[e], preferred_element_type=F32)

    @pl.when(j >= nu_ref[0])
    def _():
        y_ref[...] = jnp.zeros_like(y_ref)


def expert_pairs(tile_a, tile_b, n_used, hs, w_gate, w_up, w_down, layer):
    rows, d = hs.shape
    depth, ne, _, ff = w_gate.shape
    epg = EXPERTS_PER_GROUP
    ng = ne // epg
    nt = rows // ROW_TILE
    blk = lambda i, ta, tb, nu: (jnp.minimum(i, nu[0]), 0)
    grp = lambda shape: pl.BlockSpec((1, epg) + shape, lambda i, ta, tb, nu: (layer * ng + ta[i] // epg, 0, 0, 0),
                                     pipeline_mode=pl.Buffered(1))
    by_group = lambda w: w.reshape((depth * ng, epg) + w.shape[2:])
    return pl.pallas_call(
        _expert_kernel,
        grid_spec=pltpu.PrefetchScalarGridSpec(
            num_scalar_prefetch=3,
            grid=(nt,),
            in_specs=[pl.BlockSpec((ROW_TILE, d), blk), grp((d, ff)), grp((d, ff)), grp((ff, d))],
            out_specs=pl.BlockSpec((ROW_TILE, 2 * d), lambda i, ta, tb, nu: (i, 0)),
            scratch_shapes=[pltpu.VMEM((epg, d, 2 * ff), BF16), pltpu.VMEM((epg, ff, d), BF16)],
        ),
        out_shape=jax.ShapeDtypeStruct((rows, 2 * d), F32),
        compiler_params=_cparams("arbitrary"),
        name="expert_pairs",
    )(tile_a, tile_b, n_used, hs, by_group(w_gate), by_group(w_up), by_group(w_down))


def _combine_kernel(pos_ref, x_ref, meta_ref, gate_ref, y_hbm, o_ref, ybuf, sems):
    tm = x_ref.shape[0]
    d = x_ref.shape[1]
    i = pl.program_id(0)
    slot = i % 2

    def gather(tile, into):
        def issue(r, c):
            _row_copy(y_hbm, ybuf.at[into], pos_ref[tile * tm + r], r, sems.at[into]).start()
            return c
        lax.fori_loop(0, tm, issue, 0, unroll=8)

    @pl.when(i == 0)
    def _():
        gather(0, 0)

    @pl.when(i + 1 < pl.num_programs(0))
    def _():
        gather(i + 1, 1 - slot)

    def drain(r, c):
        _row_copy(y_hbm, ybuf.at[slot], 0, r, sems.at[slot]).wait()
        return c

    lax.fori_loop(0, tm, drain, 0, unroll=8)
    meta = meta_ref[...]
    yb = ybuf[slot]
    ff = meta[:, META_WA:META_WA + 1] * yb[:, :d] + meta[:, META_WB:META_WB + 1] * yb[:, d:]
    o_ref[...] = x_ref[...] + gate_ref[0] * ff


def combine_rows(pos, x_mid, meta, mod3, y, *, seq, nseg, tm):
    r, d = x_mid.shape
    tiles_per_seq = seq // tm
    seg = lambda i: jnp.minimum(i // tiles_per_seq, nseg - 1)
    return pl.pallas_call(
        _combine_kernel,
        grid_spec=pltpu.PrefetchScalarGridSpec(
            num_scalar_prefetch=1,
            grid=(r // tm,),
            in_specs=[pl.BlockSpec((tm, d), lambda i, p: (i, 0)),
                      pl.BlockSpec((tm, LANES), lambda i, p: (i, 0)),
                      pl.BlockSpec((1, 1, d), lambda i, p: (seg(i) * N_MOD + 5, 0, 0)),
                      pl.BlockSpec(memory_space=pl.ANY)],
            out_specs=pl.BlockSpec((tm, d), lambda i, p: (i, 0)),
            scratch_shapes=[pltpu.VMEM((2, tm, 2 * d), F32), pltpu.SemaphoreType.DMA((2,))],
        ),
        out_shape=jax.ShapeDtypeStruct((r, d), F32),
        compiler_params=_cparams("arbitrary"),
        name="combine_rows",
    )(pos, x_mid, meta, mod3, y)


def moe_block(x_mid, h2, meta, mod3, w_gate, w_up, w_down, *, layer, seq, nseg):
    r, d = x_mid.shape
    ids, rank, cnt = route_plan(meta, 512)
    tiles = (cnt + (ROW_TILE - 1)) // ROW_TILE
    incl = jnp.cumsum(tiles)
    n_used = incl[-1]
    lookup = lambda table, idx: jnp.sum(jnp.where(idx[:, None] == jnp.arange(LANES)[None, :], table[None, :], 0), axis=1)
    pos = lookup(incl - tiles, ids) * ROW_TILE + rank
    nt = r // ROW_TILE + N_BUCKETS
    last = jnp.minimum(jnp.arange(nt), n_used - 1)
    tile_bucket = jnp.sum((incl[None, :] <= last[:, None]).astype(jnp.int32), axis=1)
    ea, eb = _bucket_experts()
    tile_a = lookup(jnp.asarray(ea), tile_bucket)
    tile_b = lookup(jnp.asarray(eb), tile_bucket)
    hs = dispatch_rows(pos, h2, nt * ROW_TILE, 512)
    y = expert_pairs(tile_a, tile_b, n_used.reshape(1).astype(jnp.int32), hs, w_gate, w_up, w_down, layer)
    return combine_rows(pos, x_mid, meta, mod3, y, seq=seq, nseg=nseg, tm=256)


def _rope_tables(seq, tm):
    t = np.arange(seq)
    row = (t // GRID_W).astype(np.float32)
    col = (t % GRID_W).astype(np.float32)
    half = HEAD_DIM // 2
    inv = jnp.asarray(ROPE_THETA, F32) ** (-jnp.arange(0, half, 2, dtype=F32) / half)
    ang = jnp.concatenate([jnp.asarray(row)[:, None] * inv, jnp.asarray(col)[:, None] * inv], axis=-1)
    cos = jnp.repeat(jnp.cos(ang), 2, axis=-1)
    sin = jnp.repeat(jnp.sin(ang), 2, axis=-1) * jnp.asarray(np.tile([-1.0, 1.0], half), F32)
    cos = jnp.tile(cos, (1, LANES // HEAD_DIM))
    sin = jnp.tile(sin, (1, LANES // HEAD_DIM))
    cos = jnp.concatenate([cos, jnp.ones((tm, LANES), F32)], axis=0)
    sin = jnp.concatenate([sin, jnp.zeros((tm, LANES), F32)], axis=0)
    return cos, sin


def kernel(x, c, ctx, c_ctx, w_ada, b_ada, norm1_g, w_in, na_q_norm, na_k_norm, na_rpb, gqa_q_norm, gqa_k_norm, hgrn_lb, hgrn_o_norm, w_out, norm2_g, w_route_group, b_route_group, w_route_expert, b_route_expert, w_exp_gate, w_exp_up, w_exp_down):
    b, s, d = x.shape
    l = ctx.shape[1]
    depth = w_ada.shape[0]
    assert s % 512 == 0 and (b * l) % 512 == 0 and s // GRID_W >= NA_WIN_R
    assert s % l == 0 and l % HG_BLOCK == 0
    nseg = b + 1
    n_lat = b * s
    n_ctx = b * l
    na_w, gq_qw = d // 4, d // 2
    gq_kw = gq_qw // 4
    hg_w = d // 4
    tm = 512

    c_all = jnp.zeros((16, d), F32).at[:b].set(c).at[b].set(c_ctx)
    mod = ada_mod(c_all, w_ada, b_ada)
    cos_t, sin_t = _rope_tables(s, tm)
    bd_f = _block_diag_ones(LANES, HEAD_DIM, F32)
    bd_b = _block_diag_ones(LANES, HEAD_DIM, BF16)
    p_lb = jax.nn.softmax(hgrn_lb.astype(F32), axis=0)
    lb_all = jnp.cumsum(p_lb, axis=0) - p_lb[0]
    tile2 = lambda g: jnp.tile(g, LANES // HEAD_DIM)

    xall = jnp.concatenate([x.reshape(n_lat, d), ctx.reshape(n_ctx, d)], axis=0)
    for layer in range(depth):
        ctx_out = layer < depth - 1
        mod3 = mod[layer].reshape(16 * N_MOD, 1, d)
        gains = jnp.zeros((8, LANES), F32)
        gains = gains.at[0].set(tile2(na_q_norm[layer])).at[1].set(tile2(na_k_norm[layer]))
        gains = gains.at[2].set(tile2(gqa_q_norm[layer])).at[3].set(tile2(gqa_k_norm[layer]))
        pa, pb, pc = in_projection(xall, mod3, norm1_g[layer][None], w_in[layer].astype(BF16), cos_t, sin_t,
                                   gains, bd_b, n_lat_rows=n_lat, seq=s, nseg=nseg, tm=tm)
        bias_tab = na_bias_table(na_rpb[layer], s // GRID_W)
        o_a = na_attention(pa, bias_tab, b=b, s=s, l=l, na_w=na_w)
        o_b = gqa_attention(pb, b=b, s=s, l=l, qw=gq_qw, kw=gq_kw, tq=256)
        lb4 = lb_all[layer].reshape(2, hg_w // LANES, 1, LANES)
        y_lat, y_ctx = hgrn_mixer(pc, lb4, tile2(hgrn_o_norm[layer])[None], bd_b, bd_f, b=b, s=s, l=l)

        w_route = jnp.zeros((d, LANES), F32).at[:, :N_GROUPS].set(w_route_group[layer])
        w_route = w_route.at[:, N_GROUPS:N_GROUPS + N_EXPERTS].set(w_route_expert[layer])
        b_route = jnp.zeros((1, LANES), F32).at[0, :N_GROUPS].set(b_route_group[layer])
        b_route = b_route.at[0, N_GROUPS:N_GROUPS + N_EXPERTS].set(b_route_expert[layer])
        if ctx_out:
            o_ac, o_bc = ctx_attention(pa, pb, b=b, s=s, l=l, na_w=na_w, qw=gq_qw, kw=gq_kw)
            mix_a = jnp.concatenate([o_a, o_ac], axis=0)
            mix_b = jnp.concatenate([o_b, o_bc], axis=0)
            y_c = jnp.concatenate([y_lat, y_ctx], axis=0)
            n_rows = n_lat + n_ctx
        else:
            mix_a, mix_b, y_c, n_rows = o_a, o_b, y_lat, n_lat
        x_mid, h2, meta = out_projection(xall, mix_a, mix_b, y_c, w_out[layer].astype(BF16), mod3,
                                          norm2_g[layer][None], w_route, b_route, n_rows=n_rows, seq=s,
                                          nseg=nseg, tm=tm)
        xall = moe_block(x_mid, h2, meta, mod3, w_exp_gate, w_exp_up, w_exp_down, layer=layer, seq=s, nseg=nseg)
    return xall[:n_lat].reshape(b, s, d)
```

```python
import functools

import jax
import jax.numpy as jnp
import numpy as np
from jax import lax
from jax.experimental import pallas as pl
from jax.experimental.pallas import tpu as pltpu

F32 = jnp.float32
BF16 = jnp.bfloat16
HIGHEST = lax.Precision.HIGHEST

HEAD_DIM = 64
GRID_W = 64
NA_WIN_R = 8
NA_WIN_C = 16
ROPE_THETA = 10000.0
HGRN_CHUNK = 16
N_GROUPS = 4
EXPERTS_PER_GROUP = 8
N_EXPERTS = N_GROUPS * EXPERTS_PER_GROUP
N_MOD = 6
EPS = 1e-6
NEG_INF = -1e30
LB_FLOOR = 1e-20
LANES = 128
VMEM_LIMIT = 56 * 1024 * 1024


def _cparams(*sem):
    return pltpu.CompilerParams(dimension_semantics=sem, vmem_limit_bytes=VMEM_LIMIT)


def _block_diag_ones(n, blk, dtype):
    i = np.arange(n)
    return jnp.asarray((i[:, None] // blk) == (i[None, :] // blk), dtype=dtype)


def _ada_kernel(c_ref, w_ref, b_ref, o_ref):
    c = c_ref[...]
    s = c * jax.nn.sigmoid(c)
    o_ref[0] = jnp.dot(s, w_ref[0], precision=HIGHEST, preferred_element_type=F32) + b_ref[0]


def ada_mod(c_all, w_ada, b_ada):
    depth, d, n = w_ada.shape
    tn = 1536
    return pl.pallas_call(
        _ada_kernel,
        grid=(depth, n // tn),
        in_specs=[
            pl.BlockSpec((16, d), lambda l, j: (0, 0)),
            pl.BlockSpec((1, d, tn), lambda l, j: (l, 0, j)),
            pl.BlockSpec((1, 1, tn), lambda l, j: (l, 0, j)),
        ],
        out_specs=pl.BlockSpec((1, 16, tn), lambda l, j: (l, 0, j)),
        out_shape=jax.ShapeDtypeStruct((depth, 16, n), F32),
        compiler_params=_cparams("parallel", "parallel"),
        name="ada_mod",
    )(c_all, w_ada, b_ada.reshape(depth, 1, n))


def _seg_inv_rms(x, bd):
    xs = x * x
    hi = xs.astype(BF16)
    lo = (xs - hi.astype(F32)).astype(BF16)
    ss = jnp.dot(hi, bd, preferred_element_type=F32) + jnp.dot(lo, bd, preferred_element_type=F32)
    return lax.rsqrt(ss * (1.0 / HEAD_DIM) + EPS)


def _pair_swap(x):
    lane = lax.broadcasted_iota(jnp.int32, x.shape, 1)
    return jnp.where((lane & 1) == 0, pltpu.roll(x, LANES - 1, 1), pltpu.roll(x, 1, 1))


def _inproj_kernel(x_ref, g1_ref, shift_ref, scale_ref, w_ref, cos_ref, sin_ref, gains_ref, bd_ref,
                   oa_ref, ob_ref, oc_ref, *, na_w, gq_qw, gq_kw):
    x = x_ref[...]
    ms = jnp.mean(x * x, axis=-1, keepdims=True)
    h = x * lax.rsqrt(ms + EPS) * g1_ref[0]
    h = h * (1.0 + scale_ref[0]) + shift_ref[0]
    p = jnp.dot(h.astype(BF16), w_ref[...], preferred_element_type=F32)
    bd = bd_ref[...]
    cos = cos_ref[...]
    sin = sin_ref[...]
    qscale = HEAD_DIM ** -0.5

    def normed(col, gain_row):
        xb = p[:, col:col + LANES]
        return xb * _seg_inv_rms(xb, bd) * gains_ref[gain_row:gain_row + 1, :]

    def rope(xn):
        return xn * cos + _pair_swap(xn) * sin

    for j in range(na_w // LANES):
        c = j * LANES
        oa_ref[:, c:c + LANES] = (normed(c, 0) * qscale).astype(BF16)
        oa_ref[:, na_w + c:na_w + c + LANES] = normed(na_w + c, 1).astype(BF16)
    oa_ref[:, 2 * na_w:3 * na_w] = p[:, 2 * na_w:3 * na_w].astype(BF16)
    b0 = 3 * na_w
    for j in range(gq_qw // LANES):
        c = j * LANES
        ob_ref[:, c:c + LANES] = (rope(normed(b0 + c, 2)) * qscale).astype(BF16)
    for j in range(gq_kw // LANES):
        c = gq_qw + j * LANES
        ob_ref[:, c:c + LANES] = rope(normed(b0 + c, 3)).astype(BF16)
    ob_ref[:, gq_qw + gq_kw:] = p[:, b0 + gq_qw + gq_kw:b0 + gq_qw + 2 * gq_kw].astype(BF16)
    oc_ref[...] = p[:, b0 + gq_qw + 2 * gq_kw:]


def in_projection(xall, mod3, layer_g1, w_in_bf, cos_t, sin_t, gains, bd, *, n_lat_rows, seq, nseg, tm):
    r, d = xall.shape
    d_in = w_in_bf.shape[1]
    na_w = d // 4
    gq_qw = d // 2
    gq_kw = gq_qw // 4
    c_w = d_in - 3 * na_w - gq_qw - 2 * gq_kw
    lat_tiles = n_lat_rows // tm
    tiles_per_seq = seq // tm

    def seg(i):
        return jnp.minimum(i // tiles_per_seq, nseg - 1)

    def rope_blk(i):
        return jnp.where(i < lat_tiles, i % tiles_per_seq, tiles_per_seq)

    kern = functools.partial(_inproj_kernel, na_w=na_w, gq_qw=gq_qw, gq_kw=gq_kw)
    return pl.pallas_call(
        kern,
        grid=(r // tm,),
        in_specs=[
            pl.BlockSpec((tm, d), lambda i: (i, 0)),
            pl.BlockSpec((1, d), lambda i: (0, 0)),
            pl.BlockSpec((1, 1, d), lambda i: (seg(i) * N_MOD + 0, 0, 0)),
            pl.BlockSpec((1, 1, d), lambda i: (seg(i) * N_MOD + 1, 0, 0)),
            pl.BlockSpec((d, d_in), lambda i: (0, 0)),
            pl.BlockSpec((tm, LANES), lambda i: (rope_blk(i), 0)),
            pl.BlockSpec((tm, LANES), lambda i: (rope_blk(i), 0)),
            pl.BlockSpec((8, LANES), lambda i: (0, 0)),
            pl.BlockSpec((LANES, LANES), lambda i: (0, 0)),
        ],
        out_specs=[
            pl.BlockSpec((tm, 3 * na_w), lambda i: (i, 0)),
            pl.BlockSpec((tm, gq_qw + 2 * gq_kw), lambda i: (i, 0)),
            pl.BlockSpec((tm, c_w), lambda i: (i, 0)),
        ],
        out_shape=[
            jax.ShapeDtypeStruct((r, 3 * na_w), BF16),
            jax.ShapeDtypeStruct((r, gq_qw + 2 * gq_kw), BF16),
            jax.ShapeDtypeStruct((r, c_w), F32),
        ],
        compiler_params=_cparams("parallel"),
        name="in_projection",
    )(xall, layer_g1, mod3, mod3, w_in_bf, cos_t, sin_t, gains, bd)


def _head(j):
    return slice(j * HEAD_DIM, (j + 1) * HEAD_DIM)


def _with_ones(v):
    return jnp.concatenate([v, jnp.ones_like(v)], axis=1)


def _scores(q, k):
    return lax.dot_general(q, k, (((1,), (1,)), ((), ())), preferred_element_type=F32)


def _normalise(o):
    return o[:, :HEAD_DIM] / o[:, HEAD_DIM:HEAD_DIM + 1]


def _softmax_attend_all(qs, ks, v1s):
    ss = [_scores(q, k) for q, k in zip(qs, ks)]
    outs = []
    for s, v1 in zip(ss, v1s):
        p = jnp.exp((s - jnp.max(s, axis=-1, keepdims=True)).astype(BF16))
        outs.append(_normalise(jnp.dot(p, v1, preferred_element_type=F32)))
    return outs


def _grouped_attend(q_ref, k_of, v_of, n_kv, grp, splits=1):
    t = q_ref.shape[0] // splits
    qs, ks, vs = [], [], []
    for j in range(n_kv):
        for h in range(splits):
            qs.append(jnp.concatenate([q_ref[h * t:(h + 1) * t, _head(j * grp + g)] for g in range(grp)], axis=0))
            ks.append(k_of(j))
            vs.append(v_of(j))
    os_ = _softmax_attend_all(qs, ks, vs)
    rows = []
    for h in range(splits):
        cols = []
        for j in range(n_kv):
            o = os_[j * splits + h]
            cols += [o[g * t:(g + 1) * t] for g in range(grp)]
        rows.append(jnp.concatenate(cols, axis=1))
    return jnp.concatenate(rows, axis=0) if splits > 1 else rows[0]


def _gqa_kernel(q_ref, kl_ref, vl_ref, kc_ref, vc_ref, o_ref, k_s, v_s, *, n_kv, grp):
    s_len = kl_ref.shape[0]

    @pl.when(pl.program_id(1) == 0)
    def _():
        for j in range(n_kv):
            k_s[j, :s_len, :] = kl_ref[:, _head(j)]
            k_s[j, s_len:, :] = kc_ref[:, _head(j)]
            v_s[j, :s_len, :] = _with_ones(vl_ref[:, _head(j)])
            v_s[j, s_len:, :] = _with_ones(vc_ref[:, _head(j)])

    o = _grouped_attend(q_ref, lambda j: k_s[j], lambda j: v_s[j], n_kv, grp, splits=2)
    o_ref[...] = o.astype(o_ref.dtype)


def gqa_attention(pb, *, b, s, l, qw, kw, tq):
    n_lat = b * s
    n_kv = kw // HEAD_DIM
    grp = qw // kw
    assert kw == LANES and qw % kw == 0
    kcol, vcol = qw // kw, qw // kw + 1
    kern = functools.partial(_gqa_kernel, n_kv=n_kv, grp=grp)
    return pl.pallas_call(
        kern,
        grid=(b, s // tq),
        in_specs=[
            pl.BlockSpec((tq, qw), lambda i, j: (i * (s // tq) + j, 0)),
            pl.BlockSpec((s, kw), lambda i, j: (i, kcol)),
            pl.BlockSpec((s, kw), lambda i, j: (i, vcol)),
            pl.BlockSpec((l, kw), lambda i, j: (n_lat // l + i, kcol)),
            pl.BlockSpec((l, kw), lambda i, j: (n_lat // l + i, vcol)),
        ],
        out_specs=pl.BlockSpec((tq, qw), lambda i, j: (i * (s // tq) + j, 0)),
        out_shape=jax.ShapeDtypeStruct((n_lat, qw), BF16),
        scratch_shapes=[pltpu.VMEM((n_kv, s + l, HEAD_DIM), BF16), pltpu.VMEM((n_kv, s + l, 2 * HEAD_DIM), BF16)],
        compiler_params=_cparams("parallel", "arbitrary"),
        name="gqa_attention",
    )(pb, pb, pb, pb, pb)


def _ctx_attn_kernel(qa_ref, ka_ref, va_ref, qb_ref, kb_ref, vb_ref, oa_ref, ob_ref, *, n_kv, grp):
    na_h = qa_ref.shape[1] // HEAD_DIM
    oa = _softmax_attend_all([qa_ref[:, _head(h)] for h in range(na_h)],
                             [ka_ref[:, _head(h)] for h in range(na_h)],
                             [_with_ones(va_ref[:, _head(h)]) for h in range(na_h)])
    oa_ref[...] = jnp.concatenate(oa, axis=1).astype(oa_ref.dtype)
    ob = _grouped_attend(qb_ref, lambda j: kb_ref[:, _head(j)], lambda j: _with_ones(vb_ref[:, _head(j)]),
                         n_kv, grp)
    ob_ref[...] = ob.astype(ob_ref.dtype)


def ctx_attention(pa, pb, *, b, s, l, na_w, qw, kw):
    r0 = (b * s) // l
    grp = qw // kw
    kern = functools.partial(_ctx_attn_kernel, n_kv=kw // HEAD_DIM, grp=grp)
    a_spec = lambda m: pl.BlockSpec((l, na_w), lambda i: (r0 + i, m))
    return pl.pallas_call(
        kern,
        grid=(b,),
        in_specs=[a_spec(0), a_spec(1), a_spec(2),
                  pl.BlockSpec((l, qw), lambda i: (r0 + i, 0)),
                  pl.BlockSpec((l, kw), lambda i: (r0 + i, grp)),
                  pl.BlockSpec((l, kw), lambda i: (r0 + i, grp + 1))],
        out_specs=[pl.BlockSpec((l, na_w), lambda i: (i, 0)), pl.BlockSpec((l, qw), lambda i: (i, 0))],
        out_shape=[jax.ShapeDtypeStruct((b * l, na_w), BF16), jax.ShapeDtypeStruct((b * l, qw), BF16)],
        compiler_params=_cparams("parallel"),
        name="ctx_attention",
    )(pa, pa, pa, pb, pb, pb)


NA_QROWS = 4
NA_UNION = NA_WIN_R + NA_QROWS


def _na_block_geometry(rows):
    wu = min(rows, NA_UNION)
    wr = min(NA_WIN_R, rows)
    nblk = rows // NA_QROWS
    sig, u0s = [], []
    for blk in range(nblk):
        r0 = blk * NA_QROWS
        u0 = int(np.clip(r0 - wr // 2, 0, rows - wu))
        u0s.append(u0)
        sig.append(tuple((r0 + j - u0, int(np.clip(r0 + j - wr // 2, 0, rows - wr)) - u0) for j in range(NA_QROWS)))
    cls = [int(blk > 0) + int(blk == nblk - 1) for blk in range(nblk)]
    reps = {}
    for blk in range(nblk):
        assert reps.setdefault(cls[blk], sig[blk]) == sig[blk]
    return wu, wr, [reps.get(c, reps[0]) for c in range(3)]


def _na_kernel(q_ref, k_ref, v_ref, kc_ref, vc_ref, bias_ref, o_ref, k_s, v_s, kc_s, vc_s, *, rows, wu, wr):
    nh = q_ref.shape[1] // HEAD_DIM
    nblk = rows // NA_QROWS
    nq = NA_QROWS * GRID_W
    for h in range(nh):
        k_s[h] = k_ref[:, _head(h)]
        v_s[h] = _with_ones(v_ref[:, _head(h)])
        kc_s[h] = kc_ref[:, _head(h)]
        vc_s[h] = _with_ones(vc_ref[:, _head(h)])

    def body(blk, carry):
        u0 = jnp.clip(blk * NA_QROWS - wr // 2, 0, rows - wu)
        cls = jnp.minimum(blk, 1) + jnp.maximum(blk - (nblk - 2), 0)
        q_rows = pl.ds(pl.multiple_of(blk * nq, nq), nq)
        k_rows = pl.ds(pl.multiple_of(u0 * GRID_W, GRID_W), wu * GRID_W)
        outs = []
        for h in range(nh):
            q = q_ref[q_rows, _head(h)]
            s_nb = _scores(q, k_s[h, k_rows, :]) + bias_ref[h, cls]
            s_cx = _scores(q, kc_s[h])
            m = jnp.maximum(jnp.max(s_nb, axis=-1, keepdims=True), jnp.max(s_cx, axis=-1, keepdims=True))
            p_nb = jnp.exp((s_nb - m).astype(BF16))
            p_cx = jnp.exp((s_cx - m).astype(BF16))
            outs.append(_normalise(jnp.dot(p_nb, v_s[h, k_rows, :], preferred_element_type=F32)
                                   + jnp.dot(p_cx, vc_s[h], preferred_element_type=F32)))
        o_ref[q_rows, :] = jnp.concatenate(outs, axis=1).astype(o_ref.dtype)
        return carry

    lax.fori_loop(0, nblk, body, 0)


def na_attention(pa, bias_tab, *, b, s, l, na_w):
    n_lat = b * s
    nh = na_w // HEAD_DIM
    rows = s // GRID_W
    wu, wr, _ = _na_block_geometry(rows)
    kern = functools.partial(_na_kernel, rows=rows, wu=wu, wr=wr)
    lat = lambda m: pl.BlockSpec((s, na_w), lambda i: (i, m))
    cx = lambda m: pl.BlockSpec((l, na_w), lambda i: (n_lat // l + i, m))
    return pl.pallas_call(
        kern,
        grid=(b,),
        in_specs=[lat(0), lat(1), lat(2), cx(1), cx(2),
                  pl.BlockSpec(bias_tab.shape, lambda i: (0, 0, 0, 0))],
        out_specs=pl.BlockSpec((s, na_w), lambda i: (i, 0)),
        out_shape=jax.ShapeDtypeStruct((n_lat, na_w), BF16),
        scratch_shapes=[pltpu.VMEM((nh, s, HEAD_DIM), BF16), pltpu.VMEM((nh, s, 2 * HEAD_DIM), BF16),
                        pltpu.VMEM((nh, l, HEAD_DIM), BF16), pltpu.VMEM((nh, l, 2 * HEAD_DIM), BF16)],
        compiler_params=_cparams("parallel"),
        name="na_attention",
    )(pa, pa, pa, pa, pa, bias_tab)


def na_bias_table(rpb, rows):
    wu, wr, reps = _na_block_geometry(rows)
    h = rpb.shape[0]
    r_off = np.array([[rj for rj, _ in rep] for rep in reps])
    s_off = np.array([[sj for _, sj in rep] for rep in reps])
    kr = np.arange(wu)
    row_ok = (kr >= s_off[..., None]) & (kr < s_off[..., None] + wr)
    dr = kr - r_off[..., None] + (NA_WIN_R - 1)
    cidx = np.arange(GRID_W)
    col_start = np.clip(cidx - NA_WIN_C // 2, 0, GRID_W - NA_WIN_C)
    col_ok = (cidx[None, :] >= col_start[:, None]) & (cidx[None, :] < col_start[:, None] + NA_WIN_C)
    dc = np.clip(cidx[None, :] - cidx[:, None] + (NA_WIN_C - 1), 0, 2 * NA_WIN_C - 2)
    sel_r = jnp.asarray((dr[..., None] == np.arange(2 * NA_WIN_R - 1)) & row_ok[..., None], F32)
    sel_c = jnp.asarray(dc[:, :, None] == np.arange(2 * NA_WIN_C - 1), F32)
    bias = jnp.einsum("cjki,hid->hcjkd", sel_r, rpb.astype(F32), precision=HIGHEST)
    bias = jnp.einsum("hcjkd,qxd->hcjqkx", bias, sel_c, precision=HIGHEST)
    ok = row_ok[None, :, :, None, :, None] & col_ok[None, None, None, :, None, :]
    bias = jnp.where(jnp.asarray(ok), bias, NEG_INF)
    return bias.reshape(h, len(reps), NA_QROWS * GRID_W, wu * GRID_W)


HG_BLOCK = 128


def _hgrn_block(q_ref, v_ref, z_ref, o_acc, lb, bdb, bdf, st, blk, *, reverse):
    c = HGRN_CHUNK
    hc = c // 2
    ncb = HG_BLOCK // c
    lbm = jnp.maximum(lb, LB_FLOOR)
    one_m_lb = 1.0 - lb
    scale = HEAD_DIM ** -0.5
    t_idx = lax.broadcasted_iota(jnp.int32, (2 * ncb, hc, LANES), 1)
    edge = 0 if reverse else c - 1
    early, late = (1, 0) if reverse else (0, 1)

    def bs(x, s):
        return jnp.broadcast_to(x[:, s:s + 1, :], x.shape)

    def seen(s):
        return (t_idx <= s) if reverse else (t_idx >= s)

    def halves(x):
        x4 = x.reshape(ncb, 2, hc, LANES)
        return x4[:, early], x4[:, late]

    def chunks(xe, xl):
        parts = [xl, xe] if reverse else [xe, xl]
        return jnp.stack(parts, axis=1).reshape(ncb, c, LANES)

    def seg_sum(w):
        rows = w.shape[0] * w.shape[1]
        return jnp.dot(w.reshape(rows, LANES).astype(BF16), bdb, preferred_element_type=F32).reshape(w.shape)

    if True:
        r0 = pl.multiple_of(blk * HG_BLOCK, HG_BLOCK)
        z = z_ref[pl.ds(r0, HG_BLOCK), :]
        q = q_ref[pl.ds(r0, HG_BLOCK), :] * scale
        v = v_ref[pl.ds(r0, HG_BLOCK), :]
        f = one_m_lb * jax.nn.sigmoid(z) + lbm
        k = one_m_lb * jax.nn.sigmoid(-z) - (lbm - lb)
        logf = jnp.log2(f)
        lf8 = logf.reshape(2 * ncb, hc, LANES)
        pre = jnp.zeros_like(lf8)
        for s in range(hc):
            pre = pre + jnp.where(seen(s), bs(lf8, s), 0.0)
        pre_e, pre_l = halves(pre.reshape(HG_BLOCK, LANES))
        cum_e = pre_e
        cum_l = pre_l + bs(pre_e, 0 if reverse else hc - 1)
        q8, k8, v8 = (a.reshape(2 * ncb, hc, LANES) for a in (q, k, v))
        o8 = jnp.zeros_like(pre)
        for s in range(hc):
            d = jnp.where(seen(s), pre - bs(pre, s), NEG_INF)
            o8 = o8 + seg_sum(q8 * bs(k8, s) * jnp.exp2(d)) * bs(v8, s)
        (q_e, q_l), (k_e, k_l), (v_e, v_l) = halves(q), halves(k), halves(v)
        o_e, o_l = halves(o8.reshape(HG_BLOCK, LANES))
        for s in range(hc):
            o_l = o_l + seg_sum(q_l * bs(k_e, s) * jnp.exp2(cum_l - bs(cum_e, s))) * bs(v_e, s)
        cum = chunks(cum_e, cum_l)
        o3 = chunks(o_e, o_l)
        q3 = q.reshape(ncb, c, LANES)
        k3 = k.reshape(ncb, c, LANES)
        v3 = v.reshape(ncb, c, LANES)
        cum_edge = bs(cum, edge)
        qe = (q3 * jnp.exp2(cum)).astype(BF16)
        kd = (k3 * jnp.exp2(cum_edge - cum)).astype(BF16)
        vb = v3.astype(BF16)
        chunk_decay = jnp.exp2(cum_edge)
        u_t = [lax.dot_general(vb[n], kd[n], (((0,), (0,)), ((), ())), preferred_element_type=F32) * bdf
               for n in range(ncb)]
        enter = [None] * ncb
        for n in (range(ncb - 1, -1, -1) if reverse else range(ncb)):
            enter[n] = st.astype(BF16)
            st = chunk_decay[n, 0:1, :] * st + u_t[n]
        outs = [o3[n] + lax.dot_general(qe[n], enter[n], (((1,), (1,)), ((), ())), preferred_element_type=F32)
                for n in range(ncb)]
        o_acc[pl.ds(r0, HG_BLOCK), :] = jnp.concatenate(outs, axis=0)
        return st


def _hgrn_segment(q_ref, v_ref, zf_ref, zb_ref, of_acc, ob_acc, lb_ref, bdb, bdf, st_f, st_b):
    nblk = q_ref.shape[0] // HG_BLOCK

    def body(i, carry):
        st_f, st_b = carry
        st_f = _hgrn_block(q_ref, v_ref, zf_ref, of_acc, lb_ref[0, 0], bdb, bdf, st_f, i, reverse=False)
        st_b = _hgrn_block(q_ref, v_ref, zb_ref, ob_acc, lb_ref[1, 0], bdb, bdf, st_b, nblk - 1 - i, reverse=True)
        return st_f, st_b

    return lax.fori_loop(0, nblk, body, (st_f, st_b))


def _hgrn_kernel(ql_ref, qc_ref, vl_ref, vc_ref, zfl_ref, zfc_ref, zbl_ref, zbc_ref, gl_ref, gc_ref,
                 lb_ref, gain_ref, bdb_ref, bdf_ref, yl_ref, yc_ref, olf_acc, olb_acc, ocf_acc, ocb_acc):
    bdb = bdb_ref[...]
    bdf = bdf_ref[...]
    zero = jnp.zeros((LANES, LANES), F32)
    st_f, st_b = _hgrn_segment(qc_ref, vc_ref, zfc_ref, zbc_ref, ocf_acc, ocb_acc, lb_ref, bdb, bdf, zero, zero)
    _hgrn_segment(ql_ref, vl_ref, zfl_ref, zbl_ref, olf_acc, olb_acc, lb_ref, bdb, bdf, st_f, st_b)
    for acc_f, acc_b, g_ref, y_ref in ((olf_acc, olb_acc, gl_ref, yl_ref), (ocf_acc, ocb_acc, gc_ref, yc_ref)):
        o = acc_f[...] + acc_b[...]
        g = g_ref[...]
        y = o * _seg_inv_rms(o, bdb) * gain_ref[...]
        y_ref[...] = (y * (g * jax.nn.sigmoid(g))).astype(y_ref.dtype)


def hgrn_mixer(pc, lb, gain128, bdb, bdf, *, b, s, l):
    w = pc.shape[1] // 5
    nj = w // LANES
    n_lat = b * s
    lat = lambda m: pl.BlockSpec((s, LANES), lambda i, j: (i, m * nj + j))
    cx = lambda m: pl.BlockSpec((l, LANES), lambda i, j: (n_lat // l + i, m * nj + j))
    in_specs = []
    for m in (0, 1, 2, 3, 4):
        in_specs += [lat(m), cx(m)]
    in_specs += [pl.BlockSpec((2, 1, 1, LANES), lambda i, j: (0, j, 0, 0)),
                 pl.BlockSpec((1, LANES), lambda i, j: (0, 0)),
                 pl.BlockSpec((LANES, LANES), lambda i, j: (0, 0)),
                 pl.BlockSpec((LANES, LANES), lambda i, j: (0, 0))]
    return pl.pallas_call(
        _hgrn_kernel,
        grid=(b, nj),
        in_specs=in_specs,
        out_specs=[pl.BlockSpec((s, LANES), lambda i, j: (i, j)),
                   pl.BlockSpec((l, LANES), lambda i, j: (i, j))],
        out_shape=[jax.ShapeDtypeStruct((n_lat, w), BF16), jax.ShapeDtypeStruct((b * l, w), BF16)],
        scratch_shapes=[pltpu.VMEM((s, LANES), F32), pltpu.VMEM((s, LANES), F32),
                        pltpu.VMEM((l, LANES), F32), pltpu.VMEM((l, LANES), F32)],
        compiler_params=_cparams("parallel", "parallel"),
        name="hgrn_mixer",
    )(*([pc] * 10), lb, gain128, bdb, bdf)


def _outproj_kernel(x_ref, ma_ref, mb_ref, mc_ref, w_ref, gate_ref, g2_ref, shift_ref, scale_ref,
                    wr_ref, br_ref, xo_ref, h2_ref, lg_ref, *, wa, wb):
    w = w_ref[...]
    y = jnp.dot(ma_ref[...], w[:wa], preferred_element_type=F32)
    y = y + jnp.dot(mb_ref[...], w[wa:wa + wb], preferred_element_type=F32)
    y = y + jnp.dot(mc_ref[...], w[wa + wb:], preferred_element_type=F32)
    x = x_ref[...] + gate_ref[0] * y
    xo_ref[...] = x
    ms = jnp.mean(x * x, axis=-1, keepdims=True)
    h = x * lax.rsqrt(ms + EPS) * g2_ref[0]
    h = h * (1.0 + scale_ref[0]) + shift_ref[0]
    h2_ref[...] = h
    h_hi = h.astype(BF16)
    h_lo = (h - h_hi.astype(F32)).astype(BF16)
    logits = (jnp.dot(h_hi, wr_ref[0], preferred_element_type=F32)
              + jnp.dot(h_lo, wr_ref[0], preferred_element_type=F32)
              + jnp.dot(h_hi, wr_ref[1], preferred_element_type=F32)) + br_ref[...]
    lg_ref[...] = _route_meta(logits)


def out_projection(xall, mix_a, mix_b, mix_c, w_out_bf, mod3, layer_g2, w_route, b_route, *, n_rows, seq, nseg, tm):
    r, d = n_rows, xall.shape[1]
    w_hi = w_route.astype(BF16)
    w_route = jnp.stack([w_hi, (w_route - w_hi.astype(F32)).astype(BF16)])
    wa, wb, wc = mix_a.shape[1], mix_b.shape[1], mix_c.shape[1]
    tiles_per_seq = seq // tm

    def seg(i):
        return jnp.minimum(i // tiles_per_seq, nseg - 1)

    def modspec(m):
        return pl.BlockSpec((1, 1, d), lambda i: (seg(i) * N_MOD + m, 0, 0))

    row = lambda wdt: pl.BlockSpec((tm, wdt), lambda i: (i, 0))
    kern = functools.partial(_outproj_kernel, wa=wa, wb=wb)
    return pl.pallas_call(
        kern,
        grid=(r // tm,),
        in_specs=[row(d), row(wa), row(wb), row(wc),
                  pl.BlockSpec((wa + wb + wc, d), lambda i: (0, 0)),
                  modspec(2),
                  pl.BlockSpec((1, d), lambda i: (0, 0)),
                  modspec(3), modspec(4),
                  pl.BlockSpec((2, d, LANES), lambda i: (0, 0, 0)),
                  pl.BlockSpec((1, LANES), lambda i: (0, 0))],
        out_specs=[row(d), row(d), row(LANES)],
        out_shape=[jax.ShapeDtypeStruct((r, d), F32),
                   jax.ShapeDtypeStruct((r, d), F32),
                   jax.ShapeDtypeStruct((r, LANES), F32)],
        compiler_params=_cparams("parallel"),
        name="out_projection",
    )(xall, mix_a, mix_b, mix_c, w_out_bf, mod3, layer_g2, mod3, mod3, w_route, b_route)


PAIRS_PER_GROUP = EXPERTS_PER_GROUP * (EXPERTS_PER_GROUP - 1) // 2
N_BUCKETS = N_GROUPS * PAIRS_PER_GROUP
ROW_TILE = 256
META_BUCKET, META_WA, META_WB = 0, 1, 2


def _bucket_experts():
    ea = np.zeros((LANES,), np.int32)
    eb = np.zeros((LANES,), np.int32)
    for g in range(N_GROUPS):
        k = g * PAIRS_PER_GROUP
        for a in range(EXPERTS_PER_GROUP):
            for b in range(a + 1, EXPERTS_PER_GROUP):
                ea[k], eb[k] = g * EXPERTS_PER_GROUP + a, g * EXPERTS_PER_GROUP + b
                k += 1
    return ea, eb


def _route_meta(logits):
    lane = lax.broadcasted_iota(jnp.int32, logits.shape, 1).astype(F32)
    is_g = lane < N_GROUPS
    gl = jnp.where(is_g, logits, -jnp.inf)
    gmax = jnp.max(gl, axis=-1, keepdims=True)
    g_idx = jnp.min(jnp.where(gl == gmax, lane, LANES), axis=-1, keepdims=True)
    gsum = jnp.sum(jnp.where(is_g, jnp.exp(gl - gmax), 0.0), axis=-1, keepdims=True)
    g_top = 1.0 / gsum
    lo = N_GROUPS + g_idx * EXPERTS_PER_GROUP
    in_grp = (lane >= lo) & (lane < lo + EXPERTS_PER_GROUP)
    el = jnp.where(in_grp, logits, -jnp.inf)
    m1 = jnp.max(el, axis=-1, keepdims=True)
    i1 = jnp.min(jnp.where(el == m1, lane, LANES), axis=-1, keepdims=True)
    el2 = jnp.where(lane == i1, -jnp.inf, el)
    m2 = jnp.max(el2, axis=-1, keepdims=True)
    i2 = jnp.min(jnp.where(el2 == m2, lane, LANES), axis=-1, keepdims=True)
    e21 = jnp.exp(m2 - m1)
    w1 = g_top / (1.0 + e21)
    w2 = e21 * w1
    first_low = i1 < i2
    la = jnp.minimum(i1, i2) - lo
    lb = jnp.maximum(i1, i2) - lo
    pair = la * (2 * EXPERTS_PER_GROUP - 1 - la) * 0.5 + (lb - la - 1.0)
    bucket = g_idx * PAIRS_PER_GROUP + pair
    wa = jnp.where(first_low, w1, w2)
    wb = jnp.where(first_low, w2, w1)
    return jnp.where(lane == META_BUCKET, bucket,
                     jnp.where(lane == META_WA, wa, jnp.where(lane == META_WB, wb, 0.0)))


def _plan_kernel(meta_ref, tri_ref, ids_ref, rank_ref, cnt_ref, carry):
    @pl.when(pl.program_id(0) == 0)
    def _():
        carry[...] = jnp.zeros_like(carry)

    ids = meta_ref[...].T[META_BUCKET:META_BUCKET + 1, :]
    sub = lax.broadcasted_iota(jnp.int32, (LANES, ids.shape[1]), 0).astype(F32)
    onehot = (sub == ids).astype(F32)
    before = jnp.dot(onehot.astype(BF16), tri_ref[...], preferred_element_type=F32)
    rank = jnp.sum(onehot * (before + carry[...]), axis=0, keepdims=True)
    ids_ref[0] = ids.astype(jnp.int32)
    rank_ref[0] = rank.astype(jnp.int32)
    total = carry[...] + jnp.sum(onehot, axis=1, keepdims=True)
    carry[...] = total
    cnt_ref[...] = total.astype(jnp.int32)


def route_plan(meta, tm):
    r = meta.shape[0]
    nt = r // tm
    i = np.arange(tm)
    tri = jnp.asarray(i[:, None] < i[None, :], BF16)
    ids, rank, cnt = pl.pallas_call(
        _plan_kernel,
        grid=(nt,),
        in_specs=[pl.BlockSpec((tm, LANES), lambda i: (i, 0)),
                  pl.BlockSpec((tm, tm), lambda i: (0, 0))],
        out_specs=[pl.BlockSpec((1, 1, tm), lambda i: (i, 0, 0)),
                   pl.BlockSpec((1, 1, tm), lambda i: (i, 0, 0)),
                   pl.BlockSpec((LANES, 1), lambda i: (0, 0))],
        out_shape=[jax.ShapeDtypeStruct((nt, 1, tm), jnp.int32),
                   jax.ShapeDtypeStruct((nt, 1, tm), jnp.int32),
                   jax.ShapeDtypeStruct((LANES, 1), jnp.int32)],
        scratch_shapes=[pltpu.VMEM((LANES, 1), F32)],
        compiler_params=_cparams("arbitrary"),
        name="route_plan",
    )(meta, tri)
    return ids.reshape(r), rank.reshape(r), cnt.reshape(LANES)


def _row_copy(src, dst, i, j, sem):
    return pltpu.make_async_copy(src.at[pl.ds(i, 1)], dst.at[pl.ds(j, 1)], sem)


def _dispatch_kernel(pos_ref, h_ref, init_hbm, o_hbm, sem):
    del init_hbm
    ch = h_ref.shape[0]
    base = pl.program_id(0) * ch

    def issue(i, c):
        _row_copy(h_ref, o_hbm, i, pos_ref[base + i], sem).start()
        return c

    def drain(i, c):
        _row_copy(h_ref, o_hbm, i, 0, sem).wait()
        return c

    lax.fori_loop(0, ch, issue, 0, unroll=8)
    lax.fori_loop(0, ch, drain, 0, unroll=8)


def dispatch_rows(pos, h2, n_rows, ch):
    r, d = h2.shape
    return pl.pallas_call(
        _dispatch_kernel,
        grid_spec=pltpu.PrefetchScalarGridSpec(
            num_scalar_prefetch=1,
            grid=(r // ch,),
            in_specs=[pl.BlockSpec((ch, d), lambda i, p: (i, 0)), pl.BlockSpec(memory_space=pl.ANY)],
            out_specs=pl.BlockSpec(memory_space=pl.ANY),
            scratch_shapes=[pltpu.SemaphoreType.DMA],
        ),
        out_shape=jax.ShapeDtypeStruct((n_rows, d), h2.dtype),
        input_output_aliases={2: 0},
        compiler_params=_cparams("arbitrary"),
        name="dispatch_rows",
    )(pos, h2, jnp.zeros((n_rows, d), h2.dtype))


def _expert_kernel(ta_ref, tb_ref, nu_ref, hs_ref, wg_ref, wu_ref, wd_ref, y_ref, gu_s, dn_s):
    j = pl.program_id(0)
    d = hs_ref.shape[1]
    ff = wg_ref.shape[3]
    epg = wg_ref.shape[1]
    prev = jnp.maximum(j - 1, 0)

    @pl.when((j == 0) | (ta_ref[j] // epg != ta_ref[prev] // epg))
    def _():
        for e in range(epg):
            gu_s[e, :, :ff] = wg_ref[0, e].astype(BF16)
            gu_s[e, :, ff:] = wu_ref[0, e].astype(BF16)
            dn_s[e] = wd_ref[0, e].astype(BF16)

    @pl.when(j < nu_ref[0])
    def _():
        h = hs_ref[...].astype(BF16)
        es = [t_ref[j] % epg for t_ref in (ta_ref, tb_ref)]
        hgus = [jnp.dot(h, gu_s[e], preferred_element_type=F32) for e in es]
        for n, (e, hgu) in enumerate(zip(es, hgus)):
            hg = hgu[:, :ff]
            hid = (hg * jax.nn.sigmoid(hg)) * hgu[:, ff:]
            y_ref[:, n * d:(n + 1) * d] = jnp.dot(hid.astype(BF16), dn_s[e], preferred_element_type=F32)

    @pl.when(j >= nu_ref[0])
    def _():
        y_ref[...] = jnp.zeros_like(y_ref)


def expert_pairs(tile_a, tile_b, n_used, hs, w_gate, w_up, w_down, layer):
    rows, d = hs.shape
    depth, ne, _, ff = w_gate.shape
    epg = EXPERTS_PER_GROUP
    ng = ne // epg
    nt = rows // ROW_TILE
    blk = lambda i, ta, tb, nu: (jnp.minimum(i, nu[0]), 0)
    grp = lambda shape: pl.BlockSpec((1, epg) + shape, lambda i, ta, tb, nu: (layer * ng + ta[i] // epg, 0, 0, 0),
                                     pipeline_mode=pl.Buffered(1))
    by_group = lambda w: w.reshape((depth * ng, epg) + w.shape[2:])
    return pl.pallas_call(
        _expert_kernel,
        grid_spec=pltpu.PrefetchScalarGridSpec(
            num_scalar_prefetch=3,
            grid=(nt,),
            in_specs=[pl.BlockSpec((ROW_TILE, d), blk), grp((d, ff)), grp((d, ff)), grp((ff, d))],
            out_specs=pl.BlockSpec((ROW_TILE, 2 * d), lambda i, ta, tb, nu: (i, 0)),
            scratch_shapes=[pltpu.VMEM((epg, d, 2 * ff), BF16), pltpu.VMEM((epg, ff, d), BF16)],
        ),
        out_shape=jax.ShapeDtypeStruct((rows, 2 * d), F32),
        compiler_params=_cparams("arbitrary"),
        name="expert_pairs",
    )(tile_a, tile_b, n_used, hs, by_group(w_gate), by_group(w_up), by_group(w_down))


def _combine_kernel(pos_ref, x_ref, meta_ref, gate_ref, y_hbm, o_ref, ybuf, sems):
    tm = x_ref.shape[0]
    d = x_ref.shape[1]
    i = pl.program_id(0)
    slot = i % 2

    def gather(tile, into):
        def issue(r, c):
            _row_copy(y_hbm, ybuf.at[into], pos_ref[tile * tm + r], r, sems.at[into]).start()
            return c
        lax.fori_loop(0, tm, issue, 0, unroll=8)

    @pl.when(i == 0)
    def _():
        gather(0, 0)

    @pl.when(i + 1 < pl.num_programs(0))
    def _():
        gather(i + 1, 1 - slot)

    def drain(r, c):
        _row_copy(y_hbm, ybuf.at[slot], 0, r, sems.at[slot]).wait()
        return c

    lax.fori_loop(0, tm, drain, 0, unroll=8)
    meta = meta_ref[...]
    yb = ybuf[slot]
    ff = meta[:, META_WA:META_WA + 1] * yb[:, :d] + meta[:, META_WB:META_WB + 1] * yb[:, d:]
    o_ref[...] = x_ref[...] + gate_ref[0] * ff


def combine_rows(pos, x_mid, meta, mod3, y, *, seq, nseg, tm):
    r, d = x_mid.shape
    tiles_per_seq = seq // tm
    seg = lambda i: jnp.minimum(i // tiles_per_seq, nseg - 1)
    return pl.pallas_call(
        _combine_kernel,
        grid_spec=pltpu.PrefetchScalarGridSpec(
            num_scalar_prefetch=1,
            grid=(r // tm,),
            in_specs=[pl.BlockSpec((tm, d), lambda i, p: (i, 0)),
                      pl.BlockSpec((tm, LANES), lambda i, p: (i, 0)),
                      pl.BlockSpec((1, 1, d), lambda i, p: (seg(i) * N_MOD + 5, 0, 0)),
                      pl.BlockSpec(memory_space=pl.ANY)],
            out_specs=pl.BlockSpec((tm, d), lambda i, p: (i, 0)),
            scratch_shapes=[pltpu.VMEM((2, tm, 2 * d), F32), pltpu.SemaphoreType.DMA((2,))],
        ),
        out_shape=jax.ShapeDtypeStruct((r, d), F32),
        compiler_params=_cparams("arbitrary"),
        name="combine_rows",
    )(pos, x_mid, meta, mod3, y)


def moe_block(x_mid, h2, meta, mod3, w_gate, w_up, w_down, *, layer, seq, nseg):
    r, d = x_mid.shape
    ids, rank, cnt = route_plan(meta, 512)
    tiles = (cnt + (ROW_TILE - 1)) // ROW_TILE
    incl = jnp.cumsum(tiles)
    n_used = incl[-1]
    lookup = lambda table, idx: jnp.sum(jnp.where(idx[:, None] == jnp.arange(LANES)[None, :], table[None, :], 0), axis=1)
    pos = lookup(incl - tiles, ids) * ROW_TILE + rank
    nt = r // ROW_TILE + N_BUCKETS
    last = jnp.minimum(jnp.arange(nt), n_used - 1)
    tile_bucket = jnp.sum((incl[None, :] <= last[:, None]).astype(jnp.int32), axis=1)
    ea, eb = _bucket_experts()
    tile_a = lookup(jnp.asarray(ea), tile_bucket)
    tile_b = lookup(jnp.asarray(eb), tile_bucket)
    hs = dispatch_rows(pos, h2, nt * ROW_TILE, 512)
    y = expert_pairs(tile_a, tile_b, n_used.reshape(1).astype(jnp.int32), hs, w_gate, w_up, w_down, layer)
    return combine_rows(pos, x_mid, meta, mod3, y, seq=seq, nseg=nseg, tm=256)


def _rope_tables(seq, tm):
    t = np.arange(seq)
    row = (t // GRID_W).astype(np.float32)
    col = (t % GRID_W).astype(np.float32)
    half = HEAD_DIM // 2
    inv = jnp.asarray(ROPE_THETA, F32) ** (-jnp.arange(0, half, 2, dtype=F32) / half)
    ang = jnp.concatenate([jnp.asarray(row)[:, None] * inv, jnp.asarray(col)[:, None] * inv], axis=-1)
    cos = jnp.repeat(jnp.cos(ang), 2, axis=-1)
    sin = jnp.repeat(jnp.sin(ang), 2, axis=-1) * jnp.asarray(np.tile([-1.0, 1.0], half), F32)
    cos = jnp.tile(cos, (1, LANES // HEAD_DIM))
    sin = jnp.tile(sin, (1, LANES // HEAD_DIM))
    cos = jnp.concatenate([cos, jnp.ones((tm, LANES), F32)], axis=0)
    sin = jnp.concatenate([sin, jnp.zeros((tm, LANES), F32)], axis=0)
    return cos, sin


def kernel(x, c, ctx, c_ctx, w_ada, b_ada, norm1_g, w_in, na_q_norm, na_k_norm, na_rpb, gqa_q_norm, gqa_k_norm, hgrn_lb, hgrn_o_norm, w_out, norm2_g, w_route_group, b_route_group, w_route_expert, b_route_expert, w_exp_gate, w_exp_up, w_exp_down):
    b, s, d = x.shape
    l = ctx.shape[1]
    depth = w_ada.shape[0]
    assert s % 512 == 0 and (b * l) % 512 == 0 and s // GRID_W >= NA_WIN_R
    assert s % l == 0 and l % HG_BLOCK == 0
    nseg = b + 1
    n_lat = b * s
    n_ctx = b * l
    na_w, gq_qw = d // 4, d // 2
    gq_kw = gq_qw // 4
    hg_w = d // 4
    tm = 512

    c_all = jnp.zeros((16, d), F32).at[:b].set(c).at[b].set(c_ctx)
    mod = ada_mod(c_all, w_ada, b_ada)
    cos_t, sin_t = _rope_tables(s, tm)
    bd_f = _block_diag_ones(LANES, HEAD_DIM, F32)
    bd_b = _block_diag_ones(LANES, HEAD_DIM, BF16)
    p_lb = jax.nn.softmax(hgrn_lb.astype(F32), axis=0)
    lb_all = jnp.cumsum(p_lb, axis=0) - p_lb[0]
    tile2 = lambda g: jnp.tile(g, LANES // HEAD_DIM)

    xall = jnp.concatenate([x.reshape(n_lat, d), ctx.reshape(n_ctx, d)], axis=0)
    for layer in range(depth):
        ctx_out = layer < depth - 1
        mod3 = mod[layer].reshape(16 * N_MOD, 1, d)
        gains = jnp.zeros((8, LANES), F32)
        gains = gains.at[0].set(tile2(na_q_norm[layer])).at[1].set(tile2(na_k_norm[layer]))
        gains = gains.at[2].set(tile2(gqa_q_norm[layer])).at[3].set(tile2(gqa_k_norm[layer]))
        pa, pb, pc = in_projection(xall, mod3, norm1_g[layer][None], w_in[layer].astype(BF16), cos_t, sin_t,
                                   gains, bd_b, n_lat_rows=n_lat, seq=s, nseg=nseg, tm=tm)
        bias_tab = na_bias_table(na_rpb[layer], s // GRID_W)
        o_a = na_attention(pa, bias_tab, b=b, s=s, l=l, na_w=na_w)
        o_b = gqa_attention(pb, b=b, s=s, l=l, qw=gq_qw, kw=gq_kw, tq=256)
        lb4 = lb_all[layer].reshape(2, hg_w // LANES, 1, LANES)
        y_lat, y_ctx = hgrn_mixer(pc, lb4, tile2(hgrn_o_norm[layer])[None], bd_b, bd_f, b=b, s=s, l=l)

        w_route = jnp.zeros((d, LANES), F32).at[:, :N_GROUPS].set(w_route_group[layer])
        w_route = w_route.at[:, N_GROUPS:N_GROUPS + N_EXPERTS].set(w_route_expert[layer])
        b_route = jnp.zeros((1, LANES), F32).at[0, :N_GROUPS].set(b_route_group[layer])
        b_route = b_route.at[0, N_GROUPS:N_GROUPS + N_EXPERTS].set(b_route_expert[layer])
        if ctx_out:
            o_ac, o_bc = ctx_attention(pa, pb, b=b, s=s, l=l, na_w=na_w, qw=gq_qw, kw=gq_kw)
            mix_a = jnp.concatenate([o_a, o_ac], axis=0)
            mix_b = jnp.concatenate([o_b, o_bc], axis=0)
            y_c = jnp.concatenate([y_lat, y_ctx], axis=0)
            n_rows = n_lat + n_ctx
        else:
            mix_a, mix_b, y_c, n_rows = o_a, o_b, y_lat, n_lat
        x_mid, h2, meta = out_projection(xall, mix_a, mix_b, y_c, w_out[layer].astype(BF16), mod3,
                                          norm2_g[layer][None], w_route, b_route, n_rows=n_rows, seq=s,
                                          nseg=nseg, tm=tm)
        xall = moe_block(x_mid, h2, meta, mod3, w_exp_gate, w_exp_up, w_exp_down, layer=layer, seq=s, nseg=nseg)
    return xall[:n_lat].reshape(b, s, d)
```

```python
import functools

import jax
import jax.numpy as jnp
import numpy as np
from jax import lax
from jax.experimental import pallas as pl
from jax.experimental.pallas import tpu as pltpu

F32 = jnp.float32
BF16 = jnp.bfloat16
HIGHEST = lax.Precision.HIGHEST

HEAD_DIM = 64
GRID_W = 64
NA_WIN_R = 8
NA_WIN_C = 16
ROPE_THETA = 10000.0
HGRN_CHUNK = 16
N_GROUPS = 4
EXPERTS_PER_GROUP = 8
N_EXPERTS = N_GROUPS * EXPERTS_PER_GROUP
N_MOD = 6
EPS = 1e-6
NEG_INF = -1e30
LB_FLOOR = 1e-20
LANES = 128
VMEM_LIMIT = 56 * 1024 * 1024


def _cparams(*sem):
    return pltpu.CompilerParams(dimension_semantics=sem, vmem_limit_bytes=VMEM_LIMIT)


def _block_diag_ones(n, blk, dtype):
    i = np.arange(n)
    return jnp.asarray((i[:, None] // blk) == (i[None, :] // blk), dtype=dtype)


def _ada_kernel(c_ref, w_ref, b_ref, o_ref):
    c = c_ref[...]
    s = c * jax.nn.sigmoid(c)
    o_ref[0] = jnp.dot(s, w_ref[0], precision=HIGHEST, preferred_element_type=F32) + b_ref[0]


def ada_mod(c_all, w_ada, b_ada):
    depth, d, n = w_ada.shape
    tn = 1536
    return pl.pallas_call(
        _ada_kernel,
        grid=(depth, n // tn),
        in_specs=[
            pl.BlockSpec((16, d), lambda l, j: (0, 0)),
            pl.BlockSpec((1, d, tn), lambda l, j: (l, 0, j)),
            pl.BlockSpec((1, 1, tn), lambda l, j: (l, 0, j)),
        ],
        out_specs=pl.BlockSpec((1, 16, tn), lambda l, j: (l, 0, j)),
        out_shape=jax.ShapeDtypeStruct((depth, 16, n), F32),
        compiler_params=_cparams("parallel", "parallel"),
        name="ada_mod",
    )(c_all, w_ada, b_ada.reshape(depth, 1, n))


def _seg_inv_rms(x, bd):
    xs = x * x
    hi = xs.astype(BF16)
    lo = (xs - hi.astype(F32)).astype(BF16)
    ss = jnp.dot(hi, bd, preferred_element_type=F32) + jnp.dot(lo, bd, preferred_element_type=F32)
    return lax.rsqrt(ss * (1.0 / HEAD_DIM) + EPS)


def _pair_swap(x):
    lane = lax.broadcasted_iota(jnp.int32, x.shape, 1)
    return jnp.where((lane & 1) == 0, pltpu.roll(x, LANES - 1, 1), pltpu.roll(x, 1, 1))


def _as_parts(a):
    return tuple(a) if isinstance(a, (tuple, list)) else (a,)


def _row_specs(parts, tm, lat_tiles):
    w = parts[0].shape[1]
    if len(parts) == 1:
        return [pl.BlockSpec((tm, w), lambda i: (i, 0))]
    return [pl.BlockSpec((tm, w), lambda i: (jnp.minimum(i, lat_tiles - 1), 0)),
            pl.BlockSpec((tm, w), lambda i: (jnp.maximum(i - lat_tiles, 0), 0))]


def _row_tile(refs, lat_tiles):
    if len(refs) == 1:
        return refs[0][...]
    return jnp.where(pl.program_id(0) < lat_tiles, refs[0][...], refs[1][...])


def _inproj_kernel(*refs, nx, lat_tiles, na_w, gq_qw, gq_kw):
    x_refs, refs = refs[:nx], refs[nx:]
    g1_ref, shift_ref, scale_ref, w_ref, cos_ref, sin_ref, gains_ref, bd_ref, oa_ref, ob_ref, oc_ref = refs
    x = _row_tile(x_refs, lat_tiles)
    ms = jnp.mean(x * x, axis=-1, keepdims=True)
    h = x * lax.rsqrt(ms + EPS) * g1_ref[0]
    h = h * (1.0 + scale_ref[0]) + shift_ref[0]
    p = jnp.dot(h.astype(BF16), w_ref[...], preferred_element_type=F32)
    bd = bd_ref[...]
    cos = cos_ref[...]
    sin = sin_ref[...]
    qscale = HEAD_DIM ** -0.5

    def normed(col, gain_row):
        xb = p[:, col:col + LANES]
        return xb * _seg_inv_rms(xb, bd) * gains_ref[gain_row:gain_row + 1, :]

    def rope(xn):
        return xn * cos + _pair_swap(xn) * sin

    for j in range(na_w // LANES):
        c = j * LANES
        oa_ref[:, c:c + LANES] = (normed(c, 0) * qscale).astype(BF16)
        oa_ref[:, na_w + c:na_w + c + LANES] = normed(na_w + c, 1).astype(BF16)
    oa_ref[:, 2 * na_w:3 * na_w] = p[:, 2 * na_w:3 * na_w].astype(BF16)
    b0 = 3 * na_w
    for j in range(gq_qw // LANES):
        c = j * LANES
        ob_ref[:, c:c + LANES] = (rope(normed(b0 + c, 2)) * qscale).astype(BF16)
    for j in range(gq_kw // LANES):
        c = gq_qw + j * LANES
        ob_ref[:, c:c + LANES] = rope(normed(b0 + c, 3)).astype(BF16)
    ob_ref[:, gq_qw + gq_kw:] = p[:, b0 + gq_qw + gq_kw:b0 + gq_qw + 2 * gq_kw].astype(BF16)
    oc_ref[...] = p[:, b0 + gq_qw + 2 * gq_kw:]


def in_projection(xall, mod3, layer_g1, w_in_bf, cos_t, sin_t, gains, bd, *, n_lat_rows, seq, nseg, tm):
    x_parts = _as_parts(xall)
    r, d = sum(a.shape[0] for a in x_parts), x_parts[0].shape[1]
    d_in = w_in_bf.shape[1]
    na_w = d // 4
    gq_qw = d // 2
    gq_kw = gq_qw // 4
    c_w = d_in - 3 * na_w - gq_qw - 2 * gq_kw
    lat_tiles = n_lat_rows // tm
    tiles_per_seq = seq // tm

    def seg(i):
        return jnp.minimum(i // tiles_per_seq, nseg - 1)

    def rope_blk(i):
        return jnp.where(i < lat_tiles, i % tiles_per_seq, tiles_per_seq)

    kern = functools.partial(_inproj_kernel, nx=len(x_parts), lat_tiles=lat_tiles, na_w=na_w, gq_qw=gq_qw,
                             gq_kw=gq_kw)
    return pl.pallas_call(
        kern,
        grid=(r // tm,),
        in_specs=_row_specs(x_parts, tm, lat_tiles) + [
            pl.BlockSpec((1, d), lambda i: (0, 0)),
            pl.BlockSpec((1, 1, d), lambda i: (seg(i) * N_MOD + 0, 0, 0)),
            pl.BlockSpec((1, 1, d), lambda i: (seg(i) * N_MOD + 1, 0, 0)),
            pl.BlockSpec((d, d_in), lambda i: (0, 0)),
            pl.BlockSpec((tm, LANES), lambda i: (rope_blk(i), 0)),
            pl.BlockSpec((tm, LANES), lambda i: (rope_blk(i), 0)),
            pl.BlockSpec((8, LANES), lambda i: (0, 0)),
            pl.BlockSpec((LANES, LANES), lambda i: (0, 0)),
        ],
        out_specs=[
            pl.BlockSpec((tm, 3 * na_w), lambda i: (i, 0)),
            pl.BlockSpec((tm, gq_qw + 2 * gq_kw), lambda i: (i, 0)),
            pl.BlockSpec((tm, c_w), lambda i: (i, 0)),
        ],
        out_shape=[
            jax.ShapeDtypeStruct((r, 3 * na_w), BF16),
            jax.ShapeDtypeStruct((r, gq_qw + 2 * gq_kw), BF16),
            jax.ShapeDtypeStruct((r, c_w), F32),
        ],
        compiler_params=_cparams("parallel"),
        name="in_projection",
    )(*x_parts, layer_g1, mod3, mod3, w_in_bf, cos_t, sin_t, gains, bd)


def _head(j):
    return slice(j * HEAD_DIM, (j + 1) * HEAD_DIM)


def _with_ones(v):
    return jnp.concatenate([v, jnp.ones_like(v)], axis=1)


def _scores(q, k):
    return lax.dot_general(q, k, (((1,), (1,)), ((), ())), preferred_element_type=F32)


def _normalise(o):
    return o[:, :HEAD_DIM] / o[:, HEAD_DIM:HEAD_DIM + 1]


def _softmax_attend_all(qs, ks, v1s):
    ss = [_scores(q, k) for q, k in zip(qs, ks)]
    outs = []
    for s, v1 in zip(ss, v1s):
        p = jnp.exp((s - jnp.max(s, axis=-1, keepdims=True)).astype(BF16))
        outs.append(_normalise(jnp.dot(p, v1, preferred_element_type=F32)))
    return outs


def _grouped_attend(q_ref, k_of, v_of, n_kv, grp, splits=1):
    t = q_ref.shape[0] // splits
    qs, ks, vs = [], [], []
    for j in range(n_kv):
        for h in range(splits):
            qs.append(jnp.concatenate([q_ref[h * t:(h + 1) * t, _head(j * grp + g)] for g in range(grp)], axis=0))
            ks.append(k_of(j))
            vs.append(v_of(j))
    os_ = _softmax_attend_all(qs, ks, vs)
    rows = []
    for h in range(splits):
        cols = []
        for j in range(n_kv):
            o = os_[j * splits + h]
            cols += [o[g * t:(g + 1) * t] for g in range(grp)]
        rows.append(jnp.concatenate(cols, axis=1))
    return jnp.concatenate(rows, axis=0) if splits > 1 else rows[0]


def _gqa_kernel(q_ref, kl_ref, vl_ref, kc_ref, vc_ref, o_ref, k_s, v_s, *, n_kv, grp):
    s_len = kl_ref.shape[0]

    @pl.when(pl.program_id(1) == 0)
    def _():
        for j in range(n_kv):
            k_s[j, :s_len, :] = kl_ref[:, _head(j)]
            k_s[j, s_len:, :] = kc_ref[:, _head(j)]
            v_s[j, :s_len, :] = _with_ones(vl_ref[:, _head(j)])
            v_s[j, s_len:, :] = _with_ones(vc_ref[:, _head(j)])

    o = _grouped_attend(q_ref, lambda j: k_s[j], lambda j: v_s[j], n_kv, grp, splits=2)
    o_ref[...] = o.astype(o_ref.dtype)


def gqa_attention(pb, *, b, s, l, qw, kw, tq):
    n_lat = b * s
    n_kv = kw // HEAD_DIM
    grp = qw // kw
    assert kw == LANES and qw % kw == 0
    kcol, vcol = qw // kw, qw // kw + 1
    kern = functools.partial(_gqa_kernel, n_kv=n_kv, grp=grp)
    return pl.pallas_call(
        kern,
        grid=(b, s // tq),
        in_specs=[
            pl.BlockSpec((tq, qw), lambda i, j: (i * (s // tq) + j, 0)),
            pl.BlockSpec((s, kw), lambda i, j: (i, kcol)),
            pl.BlockSpec((s, kw), lambda i, j: (i, vcol)),
            pl.BlockSpec((l, kw), lambda i, j: (n_lat // l + i, kcol)),
            pl.BlockSpec((l, kw), lambda i, j: (n_lat // l + i, vcol)),
        ],
        out_specs=pl.BlockSpec((tq, qw), lambda i, j: (i * (s // tq) + j, 0)),
        out_shape=jax.ShapeDtypeStruct((n_lat, qw), BF16),
        scratch_shapes=[pltpu.VMEM((n_kv, s + l, HEAD_DIM), BF16), pltpu.VMEM((n_kv, s + l, 2 * HEAD_DIM), BF16)],
        compiler_params=_cparams("parallel", "arbitrary"),
        name="gqa_attention",
    )(pb, pb, pb, pb, pb)


def _ctx_attn_kernel(qa_ref, ka_ref, va_ref, qb_ref, kb_ref, vb_ref, oa_ref, ob_ref, *, n_kv, grp):
    na_h = qa_ref.shape[1] // HEAD_DIM
    oa = _softmax_attend_all([qa_ref[:, _head(h)] for h in range(na_h)],
                             [ka_ref[:, _head(h)] for h in range(na_h)],
                             [_with_ones(va_ref[:, _head(h)]) for h in range(na_h)])
    oa_ref[...] = jnp.concatenate(oa, axis=1).astype(oa_ref.dtype)
    ob = _grouped_attend(qb_ref, lambda j: kb_ref[:, _head(j)], lambda j: _with_ones(vb_ref[:, _head(j)]),
                         n_kv, grp)
    ob_ref[...] = ob.astype(ob_ref.dtype)


def ctx_attention(pa, pb, *, b, s, l, na_w, qw, kw):
    r0 = (b * s) // l
    grp = qw // kw
    kern = functools.partial(_ctx_attn_kernel, n_kv=kw // HEAD_DIM, grp=grp)
    a_spec = lambda m: pl.BlockSpec((l, na_w), lambda i: (r0 + i, m))
    return pl.pallas_call(
        kern,
        grid=(b,),
        in_specs=[a_spec(0), a_spec(1), a_spec(2),
                  pl.BlockSpec((l, qw), lambda i: (r0 + i, 0)),
                  pl.BlockSpec((l, kw), lambda i: (r0 + i, grp)),
                  pl.BlockSpec((l, kw), lambda i: (r0 + i, grp + 1))],
        out_specs=[pl.BlockSpec((l, na_w), lambda i: (i, 0)), pl.BlockSpec((l, qw), lambda i: (i, 0))],
        out_shape=[jax.ShapeDtypeStruct((b * l, na_w), BF16), jax.ShapeDtypeStruct((b * l, qw), BF16)],
        compiler_params=_cparams("parallel"),
        name="ctx_attention",
    )(pa, pa, pa, pb, pb, pb)


NA_QROWS = 4
NA_UNION = NA_WIN_R + NA_QROWS


def _na_block_geometry(rows):
    wu = min(rows, NA_UNION)
    wr = min(NA_WIN_R, rows)
    nblk = rows // NA_QROWS
    sig, u0s = [], []
    for blk in range(nblk):
        r0 = blk * NA_QROWS
        u0 = int(np.clip(r0 - wr // 2, 0, rows - wu))
        u0s.append(u0)
        sig.append(tuple((r0 + j - u0, int(np.clip(r0 + j - wr // 2, 0, rows - wr)) - u0) for j in range(NA_QROWS)))
    cls = [int(blk > 0) + int(blk == nblk - 1) for blk in range(nblk)]
    reps = {}
    for blk in range(nblk):
        assert reps.setdefault(cls[blk], sig[blk]) == sig[blk]
    return wu, wr, [reps.get(c, reps[0]) for c in range(3)]


def _na_kernel(q_ref, k_ref, v_ref, kc_ref, vc_ref, bias_ref, o_ref, k_s, v_s, kc_s, vc_s, *, rows, wu, wr):
    nh = q_ref.shape[1] // HEAD_DIM
    nblk = rows // NA_QROWS
    nq = NA_QROWS * GRID_W
    for h in range(nh):
        k_s[h] = k_ref[:, _head(h)]
        v_s[h] = _with_ones(v_ref[:, _head(h)])
        kc_s[h] = kc_ref[:, _head(h)]
        vc_s[h] = _with_ones(vc_ref[:, _head(h)])

    def body(blk, carry):
        u0 = jnp.clip(blk * NA_QROWS - wr // 2, 0, rows - wu)
        cls = jnp.minimum(blk, 1) + jnp.maximum(blk - (nblk - 2), 0)
        q_rows = pl.ds(pl.multiple_of(blk * nq, nq), nq)
        k_rows = pl.ds(pl.multiple_of(u0 * GRID_W, GRID_W), wu * GRID_W)
        outs = []
        for h in range(nh):
            q = q_ref[q_rows, _head(h)]
            s_nb = _scores(q, k_s[h, k_rows, :]) + bias_ref[h, cls]
            s_cx = _scores(q, kc_s[h])
            m = jnp.maximum(jnp.max(s_nb, axis=-1, keepdims=True), jnp.max(s_cx, axis=-1, keepdims=True))
            p_nb = jnp.exp((s_nb - m).astype(BF16))
            p_cx = jnp.exp((s_cx - m).astype(BF16))
            outs.append(_normalise(jnp.dot(p_nb, v_s[h, k_rows, :], preferred_element_type=F32)
                                   + jnp.dot(p_cx, vc_s[h], preferred_element_type=F32)))
        o_ref[q_rows, :] = jnp.concatenate(outs, axis=1).astype(o_ref.dtype)
        return carry

    lax.fori_loop(0, nblk, body, 0)


def na_attention(pa, bias_tab, *, b, s, l, na_w):
    n_lat = b * s
    nh = na_w // HEAD_DIM
    rows = s // GRID_W
    wu, wr, _ = _na_block_geometry(rows)
    kern = functools.partial(_na_kernel, rows=rows, wu=wu, wr=wr)
    lat = lambda m: pl.BlockSpec((s, na_w), lambda i: (i, m))
    cx = lambda m: pl.BlockSpec((l, na_w), lambda i: (n_lat // l + i, m))
    return pl.pallas_call(
        kern,
        grid=(b,),
        in_specs=[lat(0), lat(1), lat(2), cx(1), cx(2),
                  pl.BlockSpec(bias_tab.shape, lambda i: (0, 0, 0, 0))],
        out_specs=pl.BlockSpec((s, na_w), lambda i: (i, 0)),
        out_shape=jax.ShapeDtypeStruct((n_lat, na_w), BF16),
        scratch_shapes=[pltpu.VMEM((nh, s, HEAD_DIM), BF16), pltpu.VMEM((nh, s, 2 * HEAD_DIM), BF16),
                        pltpu.VMEM((nh, l, HEAD_DIM), BF16), pltpu.VMEM((nh, l, 2 * HEAD_DIM), BF16)],
        compiler_params=_cparams("parallel"),
        name="na_attention",
    )(pa, pa, pa, pa, pa, bias_tab)


def na_bias_table(rpb, rows):
    wu, wr, reps = _na_block_geometry(rows)
    h = rpb.shape[0]
    r_off = np.array([[rj for rj, _ in rep] for rep in reps])
    s_off = np.array([[sj for _, sj in rep] for rep in reps])
    kr = np.arange(wu)
    row_ok = (kr >= s_off[..., None]) & (kr < s_off[..., None] + wr)
    dr = kr - r_off[..., None] + (NA_WIN_R - 1)
    cidx = np.arange(GRID_W)
    col_start = np.clip(cidx - NA_WIN_C // 2, 0, GRID_W - NA_WIN_C)
    col_ok = (cidx[None, :] >= col_start[:, None]) & (cidx[None, :] < col_start[:, None] + NA_WIN_C)
    dc = np.clip(cidx[None, :] - cidx[:, None] + (NA_WIN_C - 1), 0, 2 * NA_WIN_C - 2)
    sel_r = jnp.asarray((dr[..., None] == np.arange(2 * NA_WIN_R - 1)) & row_ok[..., None], F32)
    sel_c = jnp.asarray(dc[:, :, None] == np.arange(2 * NA_WIN_C - 1), F32)
    bias = jnp.einsum("cjki,hid->hcjkd", sel_r, rpb.astype(F32), precision=HIGHEST)
    bias = jnp.einsum("hcjkd,qxd->hcjqkx", bias, sel_c, precision=HIGHEST)
    ok = row_ok[None, :, :, None, :, None] & col_ok[None, None, None, :, None, :]
    bias = jnp.where(jnp.asarray(ok), bias, NEG_INF)
    return bias.reshape(h, len(reps), NA_QROWS * GRID_W, wu * GRID_W)


HG_BLOCK = 128


def _hgrn_block(q_ref, v_ref, z_ref, o_acc, lb, bdb, bdf, st, blk, *, reverse):
    c = HGRN_CHUNK
    hc = c // 2
    ncb = HG_BLOCK // c
    lbm = jnp.maximum(lb, LB_FLOOR)
    one_m_lb = 1.0 - lb
    scale = HEAD_DIM ** -0.5
    t_idx = lax.broadcasted_iota(jnp.int32, (2 * ncb, hc, LANES), 1)
    edge = 0 if reverse else c - 1
    early, late = (1, 0) if reverse else (0, 1)

    def bs(x, s):
        return jnp.broadcast_to(x[:, s:s + 1, :], x.shape)

    def seen(s):
        return (t_idx <= s) if reverse else (t_idx >= s)

    def halves(x):
        x4 = x.reshape(ncb, 2, hc, LANES)
        return x4[:, early], x4[:, late]

    def chunks(xe, xl):
        parts = [xl, xe] if reverse else [xe, xl]
        return jnp.stack(parts, axis=1).reshape(ncb, c, LANES)

    def seg_sum(w):
        rows = w.shape[0] * w.shape[1]
        return jnp.dot(w.reshape(rows, LANES).astype(BF16), bdb, preferred_element_type=F32).reshape(w.shape)

    if True:
        r0 = pl.multiple_of(blk * HG_BLOCK, HG_BLOCK)
        z = z_ref[pl.ds(r0, HG_BLOCK), :]
        q = q_ref[pl.ds(r0, HG_BLOCK), :] * scale
        v = v_ref[pl.ds(r0, HG_BLOCK), :]
        f = one_m_lb * jax.nn.sigmoid(z) + lbm
        k = one_m_lb * jax.nn.sigmoid(-z) - (lbm - lb)
        logf = jnp.log2(f)
        lf8 = logf.reshape(2 * ncb, hc, LANES)
        pre = jnp.zeros_like(lf8)
        for s in range(hc):
            pre = pre + jnp.where(seen(s), bs(lf8, s), 0.0)
        pre_e, pre_l = halves(pre.reshape(HG_BLOCK, LANES))
        cum_e = pre_e
        cum_l = pre_l + bs(pre_e, 0 if reverse else hc - 1)
        q8, k8, v8 = (a.reshape(2 * ncb, hc, LANES) for a in (q, k, v))
        o8 = jnp.zeros_like(pre)
        for s in range(hc):
            d = jnp.where(seen(s), pre - bs(pre, s), NEG_INF)
            o8 = o8 + seg_sum(q8 * bs(k8, s) * jnp.exp2(d)) * bs(v8, s)
        (q_e, q_l), (k_e, k_l), (v_e, v_l) = halves(q), halves(k), halves(v)
        o_e, o_l = halves(o8.reshape(HG_BLOCK, LANES))
        for s in range(hc):
            o_l = o_l + seg_sum(q_l * bs(k_e, s) * jnp.exp2(cum_l - bs(cum_e, s))) * bs(v_e, s)
        cum = chunks(cum_e, cum_l)
        o3 = chunks(o_e, o_l)
        q3 = q.reshape(ncb, c, LANES)
        k3 = k.reshape(ncb, c, LANES)
        v3 = v.reshape(ncb, c, LANES)
        cum_edge = bs(cum, edge)
        qe = (q3 * jnp.exp2(cum)).astype(BF16)
        kd = (k3 * jnp.exp2(cum_edge - cum)).astype(BF16)
        vb = v3.astype(BF16)
        chunk_decay = jnp.exp2(cum_edge)
        yield None
        u_t = [lax.dot_general(vb[n], kd[n], (((0,), (0,)), ((), ())), preferred_element_type=F32) * bdf
               for n in range(ncb)]
        yield None
        enter = [None] * ncb
        for n in (range(ncb - 1, -1, -1) if reverse else range(ncb)):
            enter[n] = st.astype(BF16)
            st = chunk_decay[n, 0:1, :] * st + u_t[n]
        yield None
        outs = [o3[n] + lax.dot_general(qe[n], enter[n], (((1,), (1,)), ((), ())), preferred_element_type=F32)
                for n in range(ncb)]
        o_acc[pl.ds(r0, HG_BLOCK), :] = jnp.concatenate(outs, axis=0)
        yield st


def _hgrn_segment(q_ref, v_ref, zf_ref, zb_ref, of_acc, ob_acc, lb_ref, bdb, bdf, st_f, st_b):
    nblk = q_ref.shape[0] // HG_BLOCK

    def body(i, carry):
        st_f, st_b = carry
        scans = [_hgrn_block(q_ref, v_ref, zf_ref, of_acc, lb_ref[0, 0], bdb, bdf, st_f, i, reverse=False),
                 _hgrn_block(q_ref, v_ref, zb_ref, ob_acc, lb_ref[1, 0], bdb, bdf, st_b, nblk - 1 - i, reverse=True)]
        for _ in range(3):
            for g in scans:
                next(g)
        return tuple(next(g) for g in scans)

    return lax.fori_loop(0, nblk, body, (st_f, st_b))


def _hgrn_kernel(ql_ref, qc_ref, vl_ref, vc_ref, zfl_ref, zfc_ref, zbl_ref, zbc_ref, gl_ref, gc_ref,
                 lb_ref, gain_ref, bdb_ref, bdf_ref, yl_ref, yc_ref, olf_acc, olb_acc, ocf_acc, ocb_acc):
    bdb = bdb_ref[...]
    bdf = bdf_ref[...]
    zero = jnp.zeros((LANES, LANES), F32)
    st_f, st_b = _hgrn_segment(qc_ref, vc_ref, zfc_ref, zbc_ref, ocf_acc, ocb_acc, lb_ref, bdb, bdf, zero, zero)
    _hgrn_segment(ql_ref, vl_ref, zfl_ref, zbl_ref, olf_acc, olb_acc, lb_ref, bdb, bdf, st_f, st_b)
    for acc_f, acc_b, g_ref, y_ref in ((olf_acc, olb_acc, gl_ref, yl_ref), (ocf_acc, ocb_acc, gc_ref, yc_ref)):
        o = acc_f[...] + acc_b[...]
        g = g_ref[...]
        y = o * _seg_inv_rms(o, bdb) * gain_ref[...]
        y_ref[...] = (y * (g * jax.nn.sigmoid(g))).astype(y_ref.dtype)


def hgrn_mixer(pc, lb, gain128, bdb, bdf, *, b, s, l):
    w = pc.shape[1] // 5
    nj = w // LANES
    n_lat = b * s
    lat = lambda m: pl.BlockSpec((s, LANES), lambda i, j: (i, m * nj + j))
    cx = lambda m: pl.BlockSpec((l, LANES), lambda i, j: (n_lat // l + i, m * nj + j))
    in_specs = []
    for m in (0, 1, 2, 3, 4):
        in_specs += [lat(m), cx(m)]
    in_specs += [pl.BlockSpec((2, 1, 1, LANES), lambda i, j: (0, j, 0, 0)),
                 pl.BlockSpec((1, LANES), lambda i, j: (0, 0)),
                 pl.BlockSpec((LANES, LANES), lambda i, j: (0, 0)),
                 pl.BlockSpec((LANES, LANES), lambda i, j: (0, 0))]
    return pl.pallas_call(
        _hgrn_kernel,
        grid=(b, nj),
        in_specs=in_specs,
        out_specs=[pl.BlockSpec((s, LANES), lambda i, j: (i, j)),
                   pl.BlockSpec((l, LANES), lambda i, j: (i, j))],
        out_shape=[jax.ShapeDtypeStruct((n_lat, w), BF16), jax.ShapeDtypeStruct((b * l, w), BF16)],
        scratch_shapes=[pltpu.VMEM((s, LANES), F32), pltpu.VMEM((s, LANES), F32),
                        pltpu.VMEM((l, LANES), F32), pltpu.VMEM((l, LANES), F32)],
        compiler_params=_cparams("parallel", "parallel"),
        name="hgrn_mixer",
    )(*([pc] * 10), lb, gain128, bdb, bdf)


def _outproj_kernel(*refs, counts, lat_tiles, wa, wb):
    rows = []
    for n in counts:
        rows.append(_row_tile(refs[:n], lat_tiles))
        refs = refs[n:]
    x_in, ma, mb, mc = rows
    w_ref, gate_ref, g2_ref, shift_ref, scale_ref, wr_ref, br_ref, xo_ref, h2_ref, lg_ref = refs
    w = w_ref[...]
    y = jnp.dot(ma, w[:wa], preferred_element_type=F32)
    y = y + jnp.dot(mb, w[wa:wa + wb], preferred_element_type=F32)
    y = y + jnp.dot(mc, w[wa + wb:], preferred_element_type=F32)
    x = x_in + gate_ref[0] * y
    xo_ref[...] = x
    ms = jnp.mean(x * x, axis=-1, keepdims=True)
    h = x * lax.rsqrt(ms + EPS) * g2_ref[0]
    h = h * (1.0 + scale_ref[0]) + shift_ref[0]
    h2_ref[...] = h
    h_hi = h.astype(BF16)
    h_lo = (h - h_hi.astype(F32)).astype(BF16)
    logits = (jnp.dot(h_hi, wr_ref[0], preferred_element_type=F32)
              + jnp.dot(h_lo, wr_ref[0], preferred_element_type=F32)
              + jnp.dot(h_hi, wr_ref[1], preferred_element_type=F32)) + br_ref[...]
    lg_ref[...] = _route_meta(logits)


def out_projection(xall, mix_a, mix_b, mix_c, w_out_bf, mod3, layer_g2, w_route, b_route, *, n_rows, seq, nseg, tm):
    ops = [_as_parts(a) for a in (xall, mix_a, mix_b, mix_c)]
    r, d = n_rows, ops[0][0].shape[1]
    w_hi = w_route.astype(BF16)
    w_route = jnp.stack([w_hi, (w_route - w_hi.astype(F32)).astype(BF16)])
    wa, wb, wc = (p[0].shape[1] for p in ops[1:])
    tiles_per_seq = seq // tm
    lat_tiles = ops[1][0].shape[0] // tm

    def seg(i):
        return jnp.minimum(i // tiles_per_seq, nseg - 1)

    def modspec(m):
        return pl.BlockSpec((1, 1, d), lambda i: (seg(i) * N_MOD + m, 0, 0))

    row = lambda wdt: pl.BlockSpec((tm, wdt), lambda i: (i, 0))
    kern = functools.partial(_outproj_kernel, counts=tuple(len(p) for p in ops), lat_tiles=lat_tiles, wa=wa, wb=wb)
    return pl.pallas_call(
        kern,
        grid=(r // tm,),
        in_specs=[s for p in ops for s in _row_specs(p, tm, lat_tiles)] + [
                  pl.BlockSpec((wa + wb + wc, d), lambda i: (0, 0)),
                  modspec(2),
                  pl.BlockSpec((1, d), lambda i: (0, 0)),
                  modspec(3), modspec(4),
                  pl.BlockSpec((2, d, LANES), lambda i: (0, 0, 0)),
                  pl.BlockSpec((1, LANES), lambda i: (0, 0))],
        out_specs=[row(d), row(d), row(LANES)],
        out_shape=[jax.ShapeDtypeStruct((r, d), F32),
                   jax.ShapeDtypeStruct((r, d), F32),
                   jax.ShapeDtypeStruct((r, LANES), F32)],
        compiler_params=_cparams("parallel"),
        name="out_projection",
    )(*[a for p in ops for a in p], w_out_bf, mod3, layer_g2, mod3, mod3, w_route, b_route)


PAIRS_PER_GROUP = EXPERTS_PER_GROUP * (EXPERTS_PER_GROUP - 1) // 2
N_BUCKETS = N_GROUPS * PAIRS_PER_GROUP
ROW_TILE = 192
META_BUCKET, META_WA, META_WB = 0, 1, 2


def _bucket_experts():
    ea = np.zeros((LANES,), np.int32)
    eb = np.zeros((LANES,), np.int32)
    for g in range(N_GROUPS):
        k = g * PAIRS_PER_GROUP
        for a in range(EXPERTS_PER_GROUP):
            for b in range(a + 1, EXPERTS_PER_GROUP):
                ea[k], eb[k] = g * EXPERTS_PER_GROUP + a, g * EXPERTS_PER_GROUP + b
                k += 1
    return ea, eb


def _route_meta(logits):
    lane = lax.broadcasted_iota(jnp.int32, logits.shape, 1).astype(F32)
    is_g = lane < N_GROUPS
    gl = jnp.where(is_g, logits, -jnp.inf)
    gmax = jnp.max(gl, axis=-1, keepdims=True)
    g_idx = jnp.min(jnp.where(gl == gmax, lane, LANES), axis=-1, keepdims=True)
    gsum = jnp.sum(jnp.where(is_g, jnp.exp(gl - gmax), 0.0), axis=-1, keepdims=True)
    g_top = 1.0 / gsum
    lo = N_GROUPS + g_idx * EXPERTS_PER_GROUP
    in_grp = (lane >= lo) & (lane < lo + EXPERTS_PER_GROUP)
    el = jnp.where(in_grp, logits, -jnp.inf)
    m1 = jnp.max(el, axis=-1, keepdims=True)
    i1 = jnp.min(jnp.where(el == m1, lane, LANES), axis=-1, keepdims=True)
    el2 = jnp.where(lane == i1, -jnp.inf, el)
    m2 = jnp.max(el2, axis=-1, keepdims=True)
    i2 = jnp.min(jnp.where(el2 == m2, lane, LANES), axis=-1, keepdims=True)
    e21 = jnp.exp(m2 - m1)
    w1 = g_top / (1.0 + e21)
    w2 = e21 * w1
    first_low = i1 < i2
    la = jnp.minimum(i1, i2) - lo
    lb = jnp.maximum(i1, i2) - lo
    pair = la * (2 * EXPERTS_PER_GROUP - 1 - la) * 0.5 + (lb - la - 1.0)
    bucket = g_idx * PAIRS_PER_GROUP + pair
    wa = jnp.where(first_low, w1, w2)
    wb = jnp.where(first_low, w2, w1)
    return jnp.where(lane == META_BUCKET, bucket,
                     jnp.where(lane == META_WA, wa, jnp.where(lane == META_WB, wb, 0.0)))


def _plan_kernel(meta_ref, tri_ref, ids_ref, rank_ref, cnt_ref, carry):
    @pl.when(pl.program_id(0) == 0)
    def _():
        carry[...] = jnp.zeros_like(carry)

    ids = meta_ref[...].T[META_BUCKET:META_BUCKET + 1, :]
    sub = lax.broadcasted_iota(jnp.int32, (LANES, ids.shape[1]), 0).astype(F32)
    onehot = (sub == ids).astype(F32)
    before = jnp.dot(onehot.astype(BF16), tri_ref[...], preferred_element_type=F32)
    rank = jnp.sum(onehot * (before + carry[...]), axis=0, keepdims=True)
    ids_ref[0] = ids.astype(jnp.int32)
    rank_ref[0] = rank.astype(jnp.int32)
    total = carry[...] + jnp.sum(onehot, axis=1, keepdims=True)
    carry[...] = total
    cnt_ref[...] = total.astype(jnp.int32)


def route_plan(meta, tm):
    r = meta.shape[0]
    nt = r // tm
    i = np.arange(tm)
    tri = jnp.asarray(i[:, None] < i[None, :], BF16)
    ids, rank, cnt = pl.pallas_call(
        _plan_kernel,
        grid=(nt,),
        in_specs=[pl.BlockSpec((tm, LANES), lambda i: (i, 0)),
                  pl.BlockSpec((tm, tm), lambda i: (0, 0))],
        out_specs=[pl.BlockSpec((1, 1, tm), lambda i: (i, 0, 0)),
                   pl.BlockSpec((1, 1, tm), lambda i: (i, 0, 0)),
                   pl.BlockSpec((LANES, 1), lambda i: (0, 0))],
        out_shape=[jax.ShapeDtypeStruct((nt, 1, tm), jnp.int32),
                   jax.ShapeDtypeStruct((nt, 1, tm), jnp.int32),
                   jax.ShapeDtypeStruct((LANES, 1), jnp.int32)],
        scratch_shapes=[pltpu.VMEM((LANES, 1), F32)],
        compiler_params=_cparams("arbitrary"),
        name="route_plan",
    )(meta, tri)
    return ids.reshape(r), rank.reshape(r), cnt.reshape(LANES)


def _row_copy(src, dst, i, j, sem):
    return pltpu.make_async_copy(src.at[pl.ds(i, 1)], dst.at[pl.ds(j, 1)], sem)


def _dispatch_kernel(pos_ref, h_ref, init_hbm, o_hbm, sem):
    del init_hbm
    ch = h_ref.shape[0]
    base = pl.program_id(0) * ch

    def issue(i, c):
        _row_copy(h_ref, o_hbm, i, pos_ref[base + i], sem).start()
        return c

    def drain(i, c):
        _row_copy(h_ref, o_hbm, i, 0, sem).wait()
        return c

    lax.fori_loop(0, ch, issue, 0, unroll=8)
    lax.fori_loop(0, ch, drain, 0, unroll=8)


def dispatch_rows(pos, h2, n_rows, ch):
    r, d = h2.shape
    return pl.pallas_call(
        _dispatch_kernel,
        grid_spec=pltpu.PrefetchScalarGridSpec(
            num_scalar_prefetch=1,
            grid=(r // ch,),
            in_specs=[pl.BlockSpec((ch, d), lambda i, p: (i, 0)), pl.BlockSpec(memory_space=pl.ANY)],
            out_specs=pl.BlockSpec(memory_space=pl.ANY),
            scratch_shapes=[pltpu.SemaphoreType.DMA],
        ),
        out_shape=jax.ShapeDtypeStruct((n_rows, d), h2.dtype),
        input_output_aliases={2: 0},
        compiler_params=_cparams("arbitrary"),
        name="dispatch_rows",
    )(pos, h2, jnp.zeros((n_rows, d), h2.dtype))


def _expert_kernel(ta_ref, tb_ref, nu_ref, hs_ref, wg_ref, wu_ref, wd_ref, y_ref, gu_s, dn_s):
    j = pl.program_id(0)
    d = hs_ref.shape[1]
    ff = wg_ref.shape[3]
    epg = wg_ref.shape[1]
    prev = jnp.maximum(j - 1, 0)

    @pl.when((j == 0) | (ta_ref[j] // epg != ta_ref[prev] // epg))
    def _():
        for e in range(epg):
            gu_s[e, :, :ff] = wg_ref[0, e].astype(BF16)
            gu_s[e, :, ff:] = wu_ref[0, e].astype(BF16)
            dn_s[e] = wd_ref[0, e].astype(BF16)

    @pl.when(j < nu_ref[0])
    def _():
        h = hs_ref[...].astype(BF16)
        es = [t_ref[j] % epg for t_ref in (ta_ref, tb_ref)]
        hgus = [jnp.dot(h, gu_s[e], preferred_element_type=F32) for e in es]
        for n, (e, hgu) in enumerate(zip(es, hgus)):
            hg = hgu[:, :ff]
            hid = (hg * jax.nn.sigmoid(hg)) * hgu[:, ff:]
            y_ref[:, n * d:(n + 1) * d] = jnp.dot(hid.astype(BF16), dn_s[e], preferred_element_type=F32)

    @pl.when(j >= nu_ref[0])
    def _():
        y_ref[...] = jnp.zeros_like(y_ref)


def expert_pairs(tile_a, tile_b, n_used, hs, w_gate, w_up, w_down, layer):
    rows, d = hs.shape
    depth, ne, _, ff = w_gate.shape
    epg = EXPERTS_PER_GROUP
    ng = ne // epg
    nt = rows // ROW_TILE
    blk = lambda i, ta, tb, nu: (jnp.minimum(i, nu[0]), 0)
    grp = lambda shape: pl.BlockSpec((1, epg) + shape, lambda i, ta, tb, nu: (layer * ng + ta[i] // epg, 0, 0, 0),
                                     pipeline_mode=pl.Buffered(1))
    by_group = lambda w: w.reshape((depth * ng, epg) + w.shape[2:])
    return pl.pallas_call(
        _expert_kernel,
        grid_spec=pltpu.PrefetchScalarGridSpec(
            num_scalar_prefetch=3,
            grid=(nt,),
            in_specs=[pl.BlockSpec((ROW_TILE, d), blk), grp((d, ff)), grp((d, ff)), grp((ff, d))],
            out_specs=pl.BlockSpec((ROW_TILE, 2 * d), lambda i, ta, tb, nu: (i, 0)),
            scratch_shapes=[pltpu.VMEM((epg, d, 2 * ff), BF16), pltpu.VMEM((epg, ff, d), BF16)],
        ),
        out_shape=jax.ShapeDtypeStruct((rows, 2 * d), F32),
        compiler_params=_cparams("arbitrary"),
        name="expert_pairs",
    )(tile_a, tile_b, n_used, hs, by_group(w_gate), by_group(w_up), by_group(w_down))


def _combine_kernel(pos_ref, x_ref, meta_ref, gate_ref, y_hbm, o_ref, ybuf, sems):
    tm = x_ref.shape[0]
    d = x_ref.shape[1]
    i = pl.program_id(0)
    slot = i % 2

    def gather(tile, into):
        def issue(r, c):
            _row_copy(y_hbm, ybuf.at[into], pos_ref[tile * tm + r], r, sems.at[into]).start()
            return c
        lax.fori_loop(0, tm, issue, 0, unroll=8)

    @pl.when(i == 0)
    def _():
        gather(0, 0)

    @pl.when(i + 1 < pl.num_programs(0))
    def _():
        gather(i + 1, 1 - slot)

    def drain(r, c):
        _row_copy(y_hbm, ybuf.at[slot], 0, r, sems.at[slot]).wait()
        return c

    lax.fori_loop(0, tm, drain, 0, unroll=8)
    meta = meta_ref[...]
    yb = ybuf[slot]
    ff = meta[:, META_WA:META_WA + 1] * yb[:, :d] + meta[:, META_WB:META_WB + 1] * yb[:, d:]
    o_ref[...] = x_ref[...] + gate_ref[0] * ff


def combine_rows(pos, x_mid, meta, mod3, y, *, seq, nseg, tm):
    r, d = x_mid.shape
    tiles_per_seq = seq // tm
    seg = lambda i: jnp.minimum(i // tiles_per_seq, nseg - 1)
    return pl.pallas_call(
        _combine_kernel,
        grid_spec=pltpu.PrefetchScalarGridSpec(
            num_scalar_prefetch=1,
            grid=(r // tm,),
            in_specs=[pl.BlockSpec((tm, d), lambda i, p: (i, 0)),
                      pl.BlockSpec((tm, LANES), lambda i, p: (i, 0)),
                      pl.BlockSpec((1, 1, d), lambda i, p: (seg(i) * N_MOD + 5, 0, 0)),
                      pl.BlockSpec(memory_space=pl.ANY)],
            out_specs=pl.BlockSpec((tm, d), lambda i, p: (i, 0)),
            scratch_shapes=[pltpu.VMEM((2, tm, 2 * d), F32), pltpu.SemaphoreType.DMA((2,))],
        ),
        out_shape=jax.ShapeDtypeStruct((r, d), F32),
        compiler_params=_cparams("arbitrary"),
        name="combine_rows",
    )(pos, x_mid, meta, mod3, y)


def moe_block(x_mid, h2, meta, mod3, w_gate, w_up, w_down, *, layer, seq, nseg):
    r, d = x_mid.shape
    ids, rank, cnt = route_plan(meta, 512)
    tiles = (cnt + (ROW_TILE - 1)) // ROW_TILE
    incl = jnp.cumsum(tiles)
    n_used = incl[-1]
    lookup = lambda table, idx: jnp.sum(jnp.where(idx[:, None] == jnp.arange(LANES)[None, :], table[None, :], 0), axis=1)
    pos = lookup(incl - tiles, ids) * ROW_TILE + rank
    nt = r // ROW_TILE + N_BUCKETS
    last = jnp.minimum(jnp.arange(nt), n_used - 1)
    tile_bucket = jnp.sum((incl[None, :] <= last[:, None]).astype(jnp.int32), axis=1)
    ea, eb = _bucket_experts()
    tile_a = lookup(jnp.asarray(ea), tile_bucket)
    tile_b = lookup(jnp.asarray(eb), tile_bucket)
    hs = dispatch_rows(pos, h2, nt * ROW_TILE, 512)
    y = expert_pairs(tile_a, tile_b, n_used.reshape(1).astype(jnp.int32), hs, w_gate, w_up, w_down, layer)
    return combine_rows(pos, x_mid, meta, mod3, y, seq=seq, nseg=nseg, tm=256)


def _rope_tables(seq, tm):
    t = np.arange(seq)
    row = (t // GRID_W).astype(np.float32)
    col = (t % GRID_W).astype(np.float32)
    half = HEAD_DIM // 2
    inv = jnp.asarray(ROPE_THETA, F32) ** (-jnp.arange(0, half, 2, dtype=F32) / half)
    ang = jnp.concatenate([jnp.asarray(row)[:, None] * inv, jnp.asarray(col)[:, None] * inv], axis=-1)
    cos = jnp.repeat(jnp.cos(ang), 2, axis=-1)
    sin = jnp.repeat(jnp.sin(ang), 2, axis=-1) * jnp.asarray(np.tile([-1.0, 1.0], half), F32)
    cos = jnp.tile(cos, (1, LANES // HEAD_DIM))
    sin = jnp.tile(sin, (1, LANES // HEAD_DIM))
    cos = jnp.concatenate([cos, jnp.ones((tm, LANES), F32)], axis=0)
    sin = jnp.concatenate([sin, jnp.zeros((tm, LANES), F32)], axis=0)
    return cos, sin


def kernel(x, c, ctx, c_ctx, w_ada, b_ada, norm1_g, w_in, na_q_norm, na_k_norm, na_rpb, gqa_q_norm, gqa_k_norm, hgrn_lb, hgrn_o_norm, w_out, norm2_g, w_route_group, b_route_group, w_route_expert, b_route_expert, w_exp_gate, w_exp_up, w_exp_down):
    b, s, d = x.shape
    l = ctx.shape[1]
    depth = w_ada.shape[0]
    assert s % 512 == 0 and (b * l) % 512 == 0 and s // GRID_W >= NA_WIN_R
    assert s % l == 0 and l % HG_BLOCK == 0
    nseg = b + 1
    n_lat = b * s
    n_ctx = b * l
    na_w, gq_qw = d // 4, d // 2
    gq_kw = gq_qw // 4
    hg_w = d // 4
    tm = 512

    c_all = jnp.zeros((16, d), F32).at[:b].set(c).at[b].set(c_ctx)
    mod = ada_mod(c_all, w_ada, b_ada)
    cos_t, sin_t = _rope_tables(s, tm)
    bd_f = _block_diag_ones(LANES, HEAD_DIM, F32)
    bd_b = _block_diag_ones(LANES, HEAD_DIM, BF16)
    p_lb = jax.nn.softmax(hgrn_lb.astype(F32), axis=0)
    lb_all = jnp.cumsum(p_lb, axis=0) - p_lb[0]
    tile2 = lambda g: jnp.tile(g, LANES // HEAD_DIM)

    xall = (x.reshape(n_lat, d), ctx.reshape(n_ctx, d))
    for layer in range(depth):
        ctx_out = layer < depth - 1
        mod3 = mod[layer].reshape(16 * N_MOD, 1, d)
        gains = jnp.zeros((8, LANES), F32)
        gains = gains.at[0].set(tile2(na_q_norm[layer])).at[1].set(tile2(na_k_norm[layer]))
        gains = gains.at[2].set(tile2(gqa_q_norm[layer])).at[3].set(tile2(gqa_k_norm[layer]))
        pa, pb, pc = in_projection(xall, mod3, norm1_g[layer][None], w_in[layer].astype(BF16), cos_t, sin_t,
                                   gains, bd_b, n_lat_rows=n_lat, seq=s, nseg=nseg, tm=tm)
        bias_tab = na_bias_table(na_rpb[layer], s // GRID_W)
        o_a = na_attention(pa, bias_tab, b=b, s=s, l=l, na_w=na_w)
        o_b = gqa_attention(pb, b=b, s=s, l=l, qw=gq_qw, kw=gq_kw, tq=256)
        lb4 = lb_all[layer].reshape(2, hg_w // LANES, 1, LANES)
        y_lat, y_ctx = hgrn_mixer(pc, lb4, tile2(hgrn_o_norm[layer])[None], bd_b, bd_f, b=b, s=s, l=l)

        w_route = jnp.zeros((d, LANES), F32).at[:, :N_GROUPS].set(w_route_group[layer])
        w_route = w_route.at[:, N_GROUPS:N_GROUPS + N_EXPERTS].set(w_route_expert[layer])
        b_route = jnp.zeros((1, LANES), F32).at[0, :N_GROUPS].set(b_route_group[layer])
        b_route = b_route.at[0, N_GROUPS:N_GROUPS + N_EXPERTS].set(b_route_expert[layer])
        if ctx_out:
            o_ac, o_bc = ctx_attention(pa, pb, b=b, s=s, l=l, na_w=na_w, qw=gq_qw, kw=gq_kw)
            mix_a, mix_b, y_c = (o_a, o_ac), (o_b, o_bc), (y_lat, y_ctx)
            n_rows = n_lat + n_ctx
        else:
            mix_a, mix_b, y_c, n_rows = o_a, o_b, y_lat, n_lat
        x_mid, h2, meta = out_projection(xall, mix_a, mix_b, y_c, w_out[layer].astype(BF16), mod3,
                                          norm2_g[layer][None], w_route, b_route, n_rows=n_rows, seq=s,
                                          nseg=nseg, tm=tm)
        xall = moe_block(x_mid, h2, meta, mod3, w_exp_gate, w_exp_up, w_exp_down, layer=layer, seq=s, nseg=nseg)
    return xall[:n_lat].reshape(b, s, d)
```

```python
import functools

import jax
import jax.numpy as jnp
import numpy as np
from jax import lax
from jax.experimental import pallas as pl
from jax.experimental.pallas import tpu as pltpu

F32 = jnp.float32
BF16 = jnp.bfloat16
HIGHEST = lax.Precision.HIGHEST

HEAD_DIM = 64
GRID_W = 64
NA_WIN_R = 8
NA_WIN_C = 16
ROPE_THETA = 10000.0
HGRN_CHUNK = 16
N_GROUPS = 4
EXPERTS_PER_GROUP = 8
N_EXPERTS = N_GROUPS * EXPERTS_PER_GROUP
N_MOD = 6
EPS = 1e-6
NEG_INF = -1e30
LB_FLOOR = 1e-20
LANES = 128
VMEM_LIMIT = 56 * 1024 * 1024


def _cparams(*sem):
    return pltpu.CompilerParams(dimension_semantics=sem, vmem_limit_bytes=VMEM_LIMIT)


def _block_diag_ones(n, blk, dtype):
    i = np.arange(n)
    return jnp.asarray((i[:, None] // blk) == (i[None, :] // blk), dtype=dtype)


def _ada_kernel(c_ref, w_ref, b_ref, o_ref):
    c = c_ref[...]
    s = c * jax.nn.sigmoid(c)
    o_ref[0] = jnp.dot(s, w_ref[0], precision=HIGHEST, preferred_element_type=F32) + b_ref[0]


def ada_mod(c_all, w_ada, b_ada):
    depth, d, n = w_ada.shape
    tn = 1536
    return pl.pallas_call(
        _ada_kernel,
        grid=(depth, n // tn),
        in_specs=[
            pl.BlockSpec((16, d), lambda l, j: (0, 0)),
            pl.BlockSpec((1, d, tn), lambda l, j: (l, 0, j)),
            pl.BlockSpec((1, 1, tn), lambda l, j: (l, 0, j)),
        ],
        out_specs=pl.BlockSpec((1, 16, tn), lambda l, j: (l, 0, j)),
        out_shape=jax.ShapeDtypeStruct((depth, 16, n), F32),
        compiler_params=_cparams("parallel", "parallel"),
        name="ada_mod",
    )(c_all, w_ada, b_ada.reshape(depth, 1, n))


def _seg_inv_rms(x, bd):
    xs = x * x
    hi = xs.astype(BF16)
    lo = (xs - hi.astype(F32)).astype(BF16)
    ss = jnp.dot(hi, bd, preferred_element_type=F32) + jnp.dot(lo, bd, preferred_element_type=F32)
    return lax.rsqrt(ss * (1.0 / HEAD_DIM) + EPS)


def _pair_swap(x):
    lane = lax.broadcasted_iota(jnp.int32, x.shape, 1)
    return jnp.where((lane & 1) == 0, pltpu.roll(x, LANES - 1, 1), pltpu.roll(x, 1, 1))


def _as_parts(a):
    return tuple(a) if isinstance(a, (tuple, list)) else (a,)


def _row_specs(parts, tm, lat_tiles):
    w = parts[0].shape[1]
    if len(parts) == 1:
        return [pl.BlockSpec((tm, w), lambda i: (i, 0))]
    return [pl.BlockSpec((tm, w), lambda i: (jnp.minimum(i, lat_tiles - 1), 0)),
            pl.BlockSpec((tm, w), lambda i: (jnp.maximum(i - lat_tiles, 0), 0))]


def _row_tile(refs, lat_tiles):
    if len(refs) == 1:
        return refs[0][...]
    return jnp.where(pl.program_id(0) < lat_tiles, refs[0][...], refs[1][...])


def _inproj_kernel(*refs, nx, lat_tiles, na_w, gq_qw, gq_kw):
    x_refs, refs = refs[:nx], refs[nx:]
    g1_ref, shift_ref, scale_ref, w_ref, cos_ref, sin_ref, gains_ref, bd_ref, oa_ref, ob_ref, oc_ref = refs
    x = _row_tile(x_refs, lat_tiles)
    tm = x.shape[0]
    bd = bd_ref[...]
    qscale = HEAD_DIM ** -0.5
    b0 = 3 * na_w
    halves = [slice(0, tm // 2), slice(tm // 2, tm)]
    hs = []
    for rows in halves:
        xh = x[rows]
        ms = jnp.mean(xh * xh, axis=-1, keepdims=True)
        h = xh * lax.rsqrt(ms + EPS) * g1_ref[0]
        hs.append((h * (1.0 + scale_ref[0]) + shift_ref[0]).astype(BF16))
    ps = [jnp.dot(h, w_ref[...], preferred_element_type=F32) for h in hs]
    for rows, p in zip(halves, ps):
        cos = cos_ref[rows, :]
        sin = sin_ref[rows, :]

        def normed(col, gain_row):
            xb = p[:, col:col + LANES]
            return xb * _seg_inv_rms(xb, bd) * gains_ref[gain_row:gain_row + 1, :]

        def rope(xn):
            return xn * cos + _pair_swap(xn) * sin

        for j in range(na_w // LANES):
            c = j * LANES
            oa_ref[rows, c:c + LANES] = (normed(c, 0) * qscale).astype(BF16)
            oa_ref[rows, na_w + c:na_w + c + LANES] = normed(na_w + c, 1).astype(BF16)
        oa_ref[rows, 2 * na_w:3 * na_w] = p[:, 2 * na_w:3 * na_w].astype(BF16)
        for j in range(gq_qw // LANES):
            c = j * LANES
            ob_ref[rows, c:c + LANES] = (rope(normed(b0 + c, 2)) * qscale).astype(BF16)
        for j in range(gq_kw // LANES):
            c = gq_qw + j * LANES
            ob_ref[rows, c:c + LANES] = rope(normed(b0 + c, 3)).astype(BF16)
        ob_ref[rows, gq_qw + gq_kw:] = p[:, b0 + gq_qw + gq_kw:b0 + gq_qw + 2 * gq_kw].astype(BF16)
        oc_ref[rows, :] = p[:, b0 + gq_qw + 2 * gq_kw:]


def in_projection(xall, mod3, layer_g1, w_in_bf, cos_t, sin_t, gains, bd, *, n_lat_rows, seq, nseg, tm):
    x_parts = _as_parts(xall)
    r, d = sum(a.shape[0] for a in x_parts), x_parts[0].shape[1]
    d_in = w_in_bf.shape[1]
    na_w = d // 4
    gq_qw = d // 2
    gq_kw = gq_qw // 4
    c_w = d_in - 3 * na_w - gq_qw - 2 * gq_kw
    lat_tiles = n_lat_rows // tm
    tiles_per_seq = seq // tm

    def seg(i):
        return jnp.minimum(i // tiles_per_seq, nseg - 1)

    def rope_blk(i):
        return jnp.where(i < lat_tiles, i % tiles_per_seq, tiles_per_seq)

    kern = functools.partial(_inproj_kernel, nx=len(x_parts), lat_tiles=lat_tiles, na_w=na_w, gq_qw=gq_qw,
                             gq_kw=gq_kw)
    return pl.pallas_call(
        kern,
        grid=(r // tm,),
        in_specs=_row_specs(x_parts, tm, lat_tiles) + [
            pl.BlockSpec((1, d), lambda i: (0, 0)),
            pl.BlockSpec((1, 1, d), lambda i: (seg(i) * N_MOD + 0, 0, 0)),
            pl.BlockSpec((1, 1, d), lambda i: (seg(i) * N_MOD + 1, 0, 0)),
            pl.BlockSpec((d, d_in), lambda i: (0, 0)),
            pl.BlockSpec((tm, LANES), lambda i: (rope_blk(i), 0)),
            pl.BlockSpec((tm, LANES), lambda i: (rope_blk(i), 0)),
            pl.BlockSpec((8, LANES), lambda i: (0, 0)),
            pl.BlockSpec((LANES, LANES), lambda i: (0, 0)),
        ],
        out_specs=[
            pl.BlockSpec((tm, 3 * na_w), lambda i: (i, 0)),
            pl.BlockSpec((tm, gq_qw + 2 * gq_kw), lambda i: (i, 0)),
            pl.BlockSpec((tm, c_w), lambda i: (i, 0)),
        ],
        out_shape=[
            jax.ShapeDtypeStruct((r, 3 * na_w), BF16),
            jax.ShapeDtypeStruct((r, gq_qw + 2 * gq_kw), BF16),
            jax.ShapeDtypeStruct((r, c_w), F32),
        ],
        compiler_params=_cparams("parallel"),
        name="in_projection",
    )(*x_parts, layer_g1, mod3, mod3, w_in_bf, cos_t, sin_t, gains, bd)


def _head(j):
    return slice(j * HEAD_DIM, (j + 1) * HEAD_DIM)


def _with_ones(v):
    return jnp.concatenate([v, jnp.ones_like(v)], axis=1)


def _scores(q, k):
    return lax.dot_general(q, k, (((1,), (1,)), ((), ())), preferred_element_type=F32)


def _normalise(o):
    return o[:, :HEAD_DIM] / o[:, HEAD_DIM:HEAD_DIM + 1]


def _softmax_attend_all(qs, ks, v1s):
    ss = [_scores(q, k) for q, k in zip(qs, ks)]
    outs = []
    for s, v1 in zip(ss, v1s):
        p = jnp.exp((s - jnp.max(s, axis=-1, keepdims=True)).astype(BF16))
        outs.append(_normalise(jnp.dot(p, v1, preferred_element_type=F32)))
    return outs


def _grouped_attend(q_ref, k_of, v_of, n_kv, grp, splits=1):
    t = q_ref.shape[0] // splits
    qs, ks, vs = [], [], []
    for j in range(n_kv):
        for h in range(splits):
            qs.append(jnp.concatenate([q_ref[h * t:(h + 1) * t, _head(j * grp + g)] for g in range(grp)], axis=0))
            ks.append(k_of(j))
            vs.append(v_of(j))
    os_ = _softmax_attend_all(qs, ks, vs)
    rows = []
    for h in range(splits):
        cols = []
        for j in range(n_kv):
            o = os_[j * splits + h]
            cols += [o[g * t:(g + 1) * t] for g in range(grp)]
        rows.append(jnp.concatenate(cols, axis=1))
    return jnp.concatenate(rows, axis=0) if splits > 1 else rows[0]


def _gqa_kernel(q_ref, kl_ref, vl_ref, kc_ref, vc_ref, o_ref, k_s, v_s, *, n_kv, grp):
    s_len = kl_ref.shape[0]

    @pl.when(pl.program_id(1) == 0)
    def _():
        for j in range(n_kv):
            k_s[j, :s_len, :] = kl_ref[:, _head(j)]
            k_s[j, s_len:, :] = kc_ref[:, _head(j)]
            v_s[j, :s_len, :] = _with_ones(vl_ref[:, _head(j)])
            v_s[j, s_len:, :] = _with_ones(vc_ref[:, _head(j)])

    o = _grouped_attend(q_ref, lambda j: k_s[j], lambda j: v_s[j], n_kv, grp, splits=2)
    o_ref[...] = o.astype(o_ref.dtype)


def gqa_attention(pb, *, b, s, l, qw, kw, tq):
    n_lat = b * s
    n_kv = kw // HEAD_DIM
    grp = qw // kw
    assert kw == LANES and qw % kw == 0
    kcol, vcol = qw // kw, qw // kw + 1
    kern = functools.partial(_gqa_kernel, n_kv=n_kv, grp=grp)
    return pl.pallas_call(
        kern,
        grid=(b, s // tq),
        in_specs=[
            pl.BlockSpec((tq, qw), lambda i, j: (i * (s // tq) + j, 0)),
            pl.BlockSpec((s, kw), lambda i, j: (i, kcol)),
            pl.BlockSpec((s, kw), lambda i, j: (i, vcol)),
            pl.BlockSpec((l, kw), lambda i, j: (n_lat // l + i, kcol)),
            pl.BlockSpec((l, kw), lambda i, j: (n_lat // l + i, vcol)),
        ],
        out_specs=pl.BlockSpec((tq, qw), lambda i, j: (i * (s // tq) + j, 0)),
        out_shape=jax.ShapeDtypeStruct((n_lat, qw), BF16),
        scratch_shapes=[pltpu.VMEM((n_kv, s + l, HEAD_DIM), BF16), pltpu.VMEM((n_kv, s + l, 2 * HEAD_DIM), BF16)],
        compiler_params=_cparams("parallel", "arbitrary"),
        name="gqa_attention",
    )(pb, pb, pb, pb, pb)


def _ctx_attn_kernel(qa_ref, ka_ref, va_ref, qb_ref, kb_ref, vb_ref, oa_ref, ob_ref, *, n_kv, grp):
    na_h = qa_ref.shape[1] // HEAD_DIM
    oa = _softmax_attend_all([qa_ref[:, _head(h)] for h in range(na_h)],
                             [ka_ref[:, _head(h)] for h in range(na_h)],
                             [_with_ones(va_ref[:, _head(h)]) for h in range(na_h)])
    oa_ref[...] = jnp.concatenate(oa, axis=1).astype(oa_ref.dtype)
    ob = _grouped_attend(qb_ref, lambda j: kb_ref[:, _head(j)], lambda j: _with_ones(vb_ref[:, _head(j)]),
                         n_kv, grp)
    ob_ref[...] = ob.astype(ob_ref.dtype)


def ctx_attention(pa, pb, *, b, s, l, na_w, qw, kw):
    r0 = (b * s) // l
    grp = qw // kw
    kern = functools.partial(_ctx_attn_kernel, n_kv=kw // HEAD_DIM, grp=grp)
    a_spec = lambda m: pl.BlockSpec((l, na_w), lambda i: (r0 + i, m))
    return pl.pallas_call(
        kern,
        grid=(b,),
        in_specs=[a_spec(0), a_spec(1), a_spec(2),
                  pl.BlockSpec((l, qw), lambda i: (r0 + i, 0)),
                  pl.BlockSpec((l, kw), lambda i: (r0 + i, grp)),
                  pl.BlockSpec((l, kw), lambda i: (r0 + i, grp + 1))],
        out_specs=[pl.BlockSpec((l, na_w), lambda i: (i, 0)), pl.BlockSpec((l, qw), lambda i: (i, 0))],
        out_shape=[jax.ShapeDtypeStruct((b * l, na_w), BF16), jax.ShapeDtypeStruct((b * l, qw), BF16)],
        compiler_params=_cparams("parallel"),
        name="ctx_attention",
    )(pa, pa, pa, pb, pb, pb)


NA_QROWS = 4
NA_UNION = NA_WIN_R + NA_QROWS


def _na_block_geometry(rows):
    wu = min(rows, NA_UNION)
    wr = min(NA_WIN_R, rows)
    nblk = rows // NA_QROWS
    sig, u0s = [], []
    for blk in range(nblk):
        r0 = blk * NA_QROWS
        u0 = int(np.clip(r0 - wr // 2, 0, rows - wu))
        u0s.append(u0)
        sig.append(tuple((r0 + j - u0, int(np.clip(r0 + j - wr // 2, 0, rows - wr)) - u0) for j in range(NA_QROWS)))
    cls = [int(blk > 0) + int(blk == nblk - 1) for blk in range(nblk)]
    reps = {}
    for blk in range(nblk):
        assert reps.setdefault(cls[blk], sig[blk]) == sig[blk]
    return wu, wr, [reps.get(c, reps[0]) for c in range(3)]


def _na_kernel(q_ref, k_ref, v_ref, kc_ref, vc_ref, bias_ref, o_ref, k_s, v_s, kc_s, vc_s, *, rows, wu, wr):
    nh = q_ref.shape[1] // HEAD_DIM
    nblk = rows // NA_QROWS
    nq = NA_QROWS * GRID_W
    for h in range(nh):
        k_s[h] = k_ref[:, _head(h)]
        v_s[h] = _with_ones(v_ref[:, _head(h)])
        kc_s[h] = kc_ref[:, _head(h)]
        vc_s[h] = _with_ones(vc_ref[:, _head(h)])

    def body(blk, carry):
        u0 = jnp.clip(blk * NA_QROWS - wr // 2, 0, rows - wu)
        cls = jnp.minimum(blk, 1) + jnp.maximum(blk - (nblk - 2), 0)
        q_rows = pl.ds(pl.multiple_of(blk * nq, nq), nq)
        k_rows = pl.ds(pl.multiple_of(u0 * GRID_W, GRID_W), wu * GRID_W)
        outs = []
        for h in range(nh):
            q = q_ref[q_rows, _head(h)]
            s_nb = _scores(q, k_s[h, k_rows, :]) + bias_ref[h, cls]
            s_cx = _scores(q, kc_s[h])
            m = jnp.maximum(jnp.max(s_nb, axis=-1, keepdims=True), jnp.max(s_cx, axis=-1, keepdims=True))
            p_nb = jnp.exp((s_nb - m).astype(BF16))
            p_cx = jnp.exp((s_cx - m).astype(BF16))
            outs.append(_normalise(jnp.dot(p_nb, v_s[h, k_rows, :], preferred_element_type=F32)
                                   + jnp.dot(p_cx, vc_s[h], preferred_element_type=F32)))
        o_ref[q_rows, :] = jnp.concatenate(outs, axis=1).astype(o_ref.dtype)
        return carry

    lax.fori_loop(0, nblk, body, 0)


def na_attention(pa, bias_tab, *, b, s, l, na_w):
    n_lat = b * s
    nh = na_w // HEAD_DIM
    rows = s // GRID_W
    wu, wr, _ = _na_block_geometry(rows)
    kern = functools.partial(_na_kernel, rows=rows, wu=wu, wr=wr)
    lat = lambda m: pl.BlockSpec((s, na_w), lambda i: (i, m))
    cx = lambda m: pl.BlockSpec((l, na_w), lambda i: (n_lat // l + i, m))
    return pl.pallas_call(
        kern,
        grid=(b,),
        in_specs=[lat(0), lat(1), lat(2), cx(1), cx(2),
                  pl.BlockSpec(bias_tab.shape, lambda i: (0, 0, 0, 0))],
        out_specs=pl.BlockSpec((s, na_w), lambda i: (i, 0)),
        out_shape=jax.ShapeDtypeStruct((n_lat, na_w), BF16),
        scratch_shapes=[pltpu.VMEM((nh, s, HEAD_DIM), BF16), pltpu.VMEM((nh, s, 2 * HEAD_DIM), BF16),
                        pltpu.VMEM((nh, l, HEAD_DIM), BF16), pltpu.VMEM((nh, l, 2 * HEAD_DIM), BF16)],
        compiler_params=_cparams("parallel"),
        name="na_attention",
    )(pa, pa, pa, pa, pa, bias_tab)


def na_bias_table(rpb, rows):
    wu, wr, reps = _na_block_geometry(rows)
    h = rpb.shape[0]
    r_off = np.array([[rj for rj, _ in rep] for rep in reps])
    s_off = np.array([[sj for _, sj in rep] for rep in reps])
    kr = np.arange(wu)
    row_ok = (kr >= s_off[..., None]) & (kr < s_off[..., None] + wr)
    dr = kr - r_off[..., None] + (NA_WIN_R - 1)
    cidx = np.arange(GRID_W)
    col_start = np.clip(cidx - NA_WIN_C // 2, 0, GRID_W - NA_WIN_C)
    col_ok = (cidx[None, :] >= col_start[:, None]) & (cidx[None, :] < col_start[:, None] + NA_WIN_C)
    dc = np.clip(cidx[None, :] - cidx[:, None] + (NA_WIN_C - 1), 0, 2 * NA_WIN_C - 2)
    sel_r = jnp.asarray((dr[..., None] == np.arange(2 * NA_WIN_R - 1)) & row_ok[..., None], F32)
    sel_c = jnp.asarray(dc[:, :, None] == np.arange(2 * NA_WIN_C - 1), F32)
    bias = jnp.einsum("cjki,hid->hcjkd", sel_r, rpb.astype(F32), precision=HIGHEST)
    bias = jnp.einsum("hcjkd,qxd->hcjqkx", bias, sel_c, precision=HIGHEST)
    ok = row_ok[None, :, :, None, :, None] & col_ok[None, None, None, :, None, :]
    bias = jnp.where(jnp.asarray(ok), bias, NEG_INF)
    return bias.reshape(h, len(reps), NA_QROWS * GRID_W, wu * GRID_W)


HG_BLOCK = 128


def _hgrn_block(q_ref, v_ref, z_ref, o_acc, lb, bdb, bdf, st, blk, *, reverse):
    c = HGRN_CHUNK
    hc = c // 2
    ncb = HG_BLOCK // c
    lbm = jnp.maximum(lb, LB_FLOOR)
    one_m_lb = 1.0 - lb
    scale = HEAD_DIM ** -0.5
    t_idx = lax.broadcasted_iota(jnp.int32, (2 * ncb, hc, LANES), 1)
    edge = 0 if reverse else c - 1
    early, late = (1, 0) if reverse else (0, 1)

    def bs(x, s):
        return jnp.broadcast_to(x[:, s:s + 1, :], x.shape)

    def seen(s):
        return (t_idx <= s) if reverse else (t_idx >= s)

    def halves(x):
        x4 = x.reshape(ncb, 2, hc, LANES)
        return x4[:, early], x4[:, late]

    def chunks(xe, xl):
        parts = [xl, xe] if reverse else [xe, xl]
        return jnp.stack(parts, axis=1).reshape(ncb, c, LANES)

    def seg_sum(w):
        rows = w.shape[0] * w.shape[1]
        return jnp.dot(w.reshape(rows, LANES).astype(BF16), bdb, preferred_element_type=F32).reshape(w.shape)

    if True:
        r0 = pl.multiple_of(blk * HG_BLOCK, HG_BLOCK)
        z = z_ref[pl.ds(r0, HG_BLOCK), :]
        q = q_ref[pl.ds(r0, HG_BLOCK), :] * scale
        v = v_ref[pl.ds(r0, HG_BLOCK), :]
        f = one_m_lb * jax.nn.sigmoid(z) + lbm
        k = one_m_lb * jax.nn.sigmoid(-z) - (lbm - lb)
        logf = jnp.log2(f)
        lf8 = logf.reshape(2 * ncb, hc, LANES)
        pre = jnp.zeros_like(lf8)
        for s in range(hc):
            pre = pre + jnp.where(seen(s), bs(lf8, s), 0.0)
        pre_e, pre_l = halves(pre.reshape(HG_BLOCK, LANES))
        cum_e = pre_e
        cum_l = pre_l + bs(pre_e, 0 if reverse else hc - 1)
        q8, k8, v8 = (a.reshape(2 * ncb, hc, LANES) for a in (q, k, v))
        o8 = jnp.zeros_like(pre)
        for s in range(hc):
            d = jnp.where(seen(s), pre - bs(pre, s), NEG_INF)
            o8 = o8 + seg_sum(q8 * bs(k8, s) * jnp.exp2(d)) * bs(v8, s)
        (q_e, q_l), (k_e, k_l), (v_e, v_l) = halves(q), halves(k), halves(v)
        o_e, o_l = halves(o8.reshape(HG_BLOCK, LANES))
        for s in range(hc):
            o_l = o_l + seg_sum(q_l * bs(k_e, s) * jnp.exp2(cum_l - bs(cum_e, s))) * bs(v_e, s)
        cum = chunks(cum_e, cum_l)
        o3 = chunks(o_e, o_l)
        q3 = q.reshape(ncb, c, LANES)
        k3 = k.reshape(ncb, c, LANES)
        v3 = v.reshape(ncb, c, LANES)
        cum_edge = bs(cum, edge)
        qe = (q3 * jnp.exp2(cum)).astype(BF16)
        kd = (k3 * jnp.exp2(cum_edge - cum)).astype(BF16)
        vb = v3.astype(BF16)
        chunk_decay = jnp.exp2(cum_edge)
        yield None
        u_t = [lax.dot_general(vb[n], kd[n], (((0,), (0,)), ((), ())), preferred_element_type=F32) * bdf
               for n in range(ncb)]
        yield None
        enter = [None] * ncb
        for n in (range(ncb - 1, -1, -1) if reverse else range(ncb)):
            enter[n] = st.astype(BF16)
            st = chunk_decay[n, 0:1, :] * st + u_t[n]
        yield None
        outs = [o3[n] + lax.dot_general(qe[n], enter[n], (((1,), (1,)), ((), ())), preferred_element_type=F32)
                for n in range(ncb)]
        o_acc[pl.ds(r0, HG_BLOCK), :] = jnp.concatenate(outs, axis=0)
        yield st


def _hgrn_segment(q_ref, v_ref, zf_ref, zb_ref, of_acc, ob_acc, lb_ref, bdb, bdf, st_f, st_b):
    nblk = q_ref.shape[0] // HG_BLOCK

    def body(i, carry):
        st_f, st_b = carry
        scans = [_hgrn_block(q_ref, v_ref, zf_ref, of_acc, lb_ref[0, 0], bdb, bdf, st_f, i, reverse=False),
                 _hgrn_block(q_ref, v_ref, zb_ref, ob_acc, lb_ref[1, 0], bdb, bdf, st_b, nblk - 1 - i, reverse=True)]
        for _ in range(3):
            for g in scans:
                next(g)
        return tuple(next(g) for g in scans)

    return lax.fori_loop(0, nblk, body, (st_f, st_b))


def _hgrn_kernel(ql_ref, qc_ref, vl_ref, vc_ref, zfl_ref, zfc_ref, zbl_ref, zbc_ref, gl_ref, gc_ref,
                 lb_ref, gain_ref, bdb_ref, bdf_ref, yl_ref, yc_ref, olf_acc, olb_acc, ocf_acc, ocb_acc):
    bdb = bdb_ref[...]
    bdf = bdf_ref[...]
    zero = jnp.zeros((LANES, LANES), F32)
    st_f, st_b = _hgrn_segment(qc_ref, vc_ref, zfc_ref, zbc_ref, ocf_acc, ocb_acc, lb_ref, bdb, bdf, zero, zero)
    _hgrn_segment(ql_ref, vl_ref, zfl_ref, zbl_ref, olf_acc, olb_acc, lb_ref, bdb, bdf, st_f, st_b)
    for acc_f, acc_b, g_ref, y_ref in ((olf_acc, olb_acc, gl_ref, yl_ref), (ocf_acc, ocb_acc, gc_ref, yc_ref)):
        o = acc_f[...] + acc_b[...]
        g = g_ref[...]
        y = o * _seg_inv_rms(o, bdb) * gain_ref[...]
        y_ref[...] = (y * (g * jax.nn.sigmoid(g))).astype(y_ref.dtype)


def hgrn_mixer(pc, lb, gain128, bdb, bdf, *, b, s, l):
    w = pc.shape[1] // 5
    nj = w // LANES
    n_lat = b * s
    lat = lambda m: pl.BlockSpec((s, LANES), lambda i, j: (i, m * nj + j))
    cx = lambda m: pl.BlockSpec((l, LANES), lambda i, j: (n_lat // l + i, m * nj + j))
    in_specs = []
    for m in (0, 1, 2, 3, 4):
        in_specs += [lat(m), cx(m)]
    in_specs += [pl.BlockSpec((2, 1, 1, LANES), lambda i, j: (0, j, 0, 0)),
                 pl.BlockSpec((1, LANES), lambda i, j: (0, 0)),
                 pl.BlockSpec((LANES, LANES), lambda i, j: (0, 0)),
                 pl.BlockSpec((LANES, LANES), lambda i, j: (0, 0))]
    return pl.pallas_call(
        _hgrn_kernel,
        grid=(b, nj),
        in_specs=in_specs,
        out_specs=[pl.BlockSpec((s, LANES), lambda i, j: (i, j)),
                   pl.BlockSpec((l, LANES), lambda i, j: (i, j))],
        out_shape=[jax.ShapeDtypeStruct((n_lat, w), BF16), jax.ShapeDtypeStruct((b * l, w), BF16)],
        scratch_shapes=[pltpu.VMEM((s, LANES), F32), pltpu.VMEM((s, LANES), F32),
                        pltpu.VMEM((l, LANES), F32), pltpu.VMEM((l, LANES), F32)],
        compiler_params=_cparams("parallel", "parallel"),
        name="hgrn_mixer",
    )(*([pc] * 10), lb, gain128, bdb, bdf)


def _outproj_kernel(*refs, counts, lat_tiles, wa, wb):
    rows = []
    for n in counts:
        rows.append(_row_tile(refs[:n], lat_tiles))
        refs = refs[n:]
    x_in, ma, mb, mc = rows
    w_ref, gate_ref, g2_ref, shift_ref, scale_ref, wr_ref, br_ref, xo_ref, h2_ref = refs
    d = xo_ref.shape[1]
    w = w_ref[...]
    y = jnp.dot(ma, w[:wa], preferred_element_type=F32)
    y = y + jnp.dot(mb, w[wa:wa + wb], preferred_element_type=F32)
    y = y + jnp.dot(mc, w[wa + wb:], preferred_element_type=F32)
    x = x_in + gate_ref[0] * y
    xo_ref[...] = x
    ms = jnp.mean(x * x, axis=-1, keepdims=True)
    h = x * lax.rsqrt(ms + EPS) * g2_ref[0]
    h = h * (1.0 + scale_ref[0]) + shift_ref[0]
    h2_ref[:, :d] = h
    h_hi = h.astype(BF16)
    h_lo = (h - h_hi.astype(F32)).astype(BF16)
    logits = (jnp.dot(h_hi, wr_ref[0], preferred_element_type=F32)
              + jnp.dot(h_lo, wr_ref[0], preferred_element_type=F32)
              + jnp.dot(h_hi, wr_ref[1], preferred_element_type=F32)) + br_ref[...]
    h2_ref[:, d:] = _route_meta(logits)


def out_projection(xall, mix_a, mix_b, mix_c, w_out_bf, mod3, layer_g2, w_route, b_route, *, n_rows, seq, nseg, tm):
    ops = [_as_parts(a) for a in (xall, mix_a, mix_b, mix_c)]
    r, d = n_rows, ops[0][0].shape[1]
    w_hi = w_route.astype(BF16)
    w_route = jnp.stack([w_hi, (w_route - w_hi.astype(F32)).astype(BF16)])
    wa, wb, wc = (p[0].shape[1] for p in ops[1:])
    tiles_per_seq = seq // tm
    lat_tiles = ops[1][0].shape[0] // tm

    def seg(i):
        return jnp.minimum(i // tiles_per_seq, nseg - 1)

    def modspec(m):
        return pl.BlockSpec((1, 1, d), lambda i: (seg(i) * N_MOD + m, 0, 0))

    row = lambda wdt: pl.BlockSpec((tm, wdt), lambda i: (i, 0))
    kern = functools.partial(_outproj_kernel, counts=tuple(len(p) for p in ops), lat_tiles=lat_tiles, wa=wa, wb=wb)
    return pl.pallas_call(
        kern,
        grid=(r // tm,),
        in_specs=[s for p in ops for s in _row_specs(p, tm, lat_tiles)] + [
                  pl.BlockSpec((wa + wb + wc, d), lambda i: (0, 0)),
                  modspec(2),
                  pl.BlockSpec((1, d), lambda i: (0, 0)),
                  modspec(3), modspec(4),
                  pl.BlockSpec((2, d, LANES), lambda i: (0, 0, 0)),
                  pl.BlockSpec((1, LANES), lambda i: (0, 0))],
        out_specs=[row(d), row(d + LANES)],
        out_shape=[jax.ShapeDtypeStruct((r, d), F32),
                   jax.ShapeDtypeStruct((r, d + LANES), F32)],
        compiler_params=_cparams("parallel"),
        name="out_projection",
    )(*[a for p in ops for a in p], w_out_bf, mod3, layer_g2, mod3, mod3, w_route, b_route)


PAIRS_PER_GROUP = EXPERTS_PER_GROUP * (EXPERTS_PER_GROUP - 1) // 2
N_BUCKETS = N_GROUPS * PAIRS_PER_GROUP
ROW_TILE = 192
META_BUCKET, META_WA, META_WB = 0, 1, 2


def _bucket_experts():
    ea = np.zeros((LANES,), np.int32)
    eb = np.zeros((LANES,), np.int32)
    for g in range(N_GROUPS):
        k = g * PAIRS_PER_GROUP
        for a in range(EXPERTS_PER_GROUP):
            for b in range(a + 1, EXPERTS_PER_GROUP):
                ea[k], eb[k] = g * EXPERTS_PER_GROUP + a, g * EXPERTS_PER_GROUP + b
                k += 1
    return ea, eb


def _route_meta(logits):
    lane = lax.broadcasted_iota(jnp.int32, logits.shape, 1).astype(F32)
    is_g = lane < N_GROUPS
    gl = jnp.where(is_g, logits, -jnp.inf)
    gmax = jnp.max(gl, axis=-1, keepdims=True)
    g_idx = jnp.min(jnp.where(gl == gmax, lane, LANES), axis=-1, keepdims=True)
    gsum = jnp.sum(jnp.where(is_g, jnp.exp(gl - gmax), 0.0), axis=-1, keepdims=True)
    g_top = 1.0 / gsum
    lo = N_GROUPS + g_idx * EXPERTS_PER_GROUP
    in_grp = (lane >= lo) & (lane < lo + EXPERTS_PER_GROUP)
    el = jnp.where(in_grp, logits, -jnp.inf)
    m1 = jnp.max(el, axis=-1, keepdims=True)
    i1 = jnp.min(jnp.where(el == m1, lane, LANES), axis=-1, keepdims=True)
    el2 = jnp.where(lane == i1, -jnp.inf, el)
    m2 = jnp.max(el2, axis=-1, keepdims=True)
    i2 = jnp.min(jnp.where(el2 == m2, lane, LANES), axis=-1, keepdims=True)
    e21 = jnp.exp(m2 - m1)
    w1 = g_top / (1.0 + e21)
    w2 = e21 * w1
    first_low = i1 < i2
    la = jnp.minimum(i1, i2) - lo
    lb = jnp.maximum(i1, i2) - lo
    pair = la * (2 * EXPERTS_PER_GROUP - 1 - la) * 0.5 + (lb - la - 1.0)
    bucket = g_idx * PAIRS_PER_GROUP + pair
    wa = jnp.where(first_low, w1, w2)
    wb = jnp.where(first_low, w2, w1)
    return jnp.where(lane == META_BUCKET, bucket,
                     jnp.where(lane == META_WA, wa, jnp.where(lane == META_WB, wb, 0.0)))


def _plan_kernel(meta_ref, tri_ref, ids_ref, rank_ref, cnt_ref, carry):
    @pl.when(pl.program_id(0) == 0)
    def _():
        carry[...] = jnp.zeros_like(carry)

    ids = meta_ref[...].T[META_BUCKET:META_BUCKET + 1, :]
    sub = lax.broadcasted_iota(jnp.int32, (LANES, ids.shape[1]), 0).astype(F32)
    onehot = (sub == ids).astype(F32)
    before = jnp.dot(onehot.astype(BF16), tri_ref[...], preferred_element_type=F32)
    rank = jnp.sum(onehot * (before + carry[...]), axis=0, keepdims=True)
    ids_ref[0] = ids.astype(jnp.int32)
    rank_ref[0] = rank.astype(jnp.int32)
    total = carry[...] + jnp.sum(onehot, axis=1, keepdims=True)
    carry[...] = total
    cnt_ref[...] = total.astype(jnp.int32)


def route_plan(h2x, tm):
    r = h2x.shape[0]
    meta_blk = h2x.shape[1] // LANES - 1
    nt = r // tm
    i = np.arange(tm)
    tri = jnp.asarray(i[:, None] < i[None, :], BF16)
    ids, rank, cnt = pl.pallas_call(
        _plan_kernel,
        grid=(nt,),
        in_specs=[pl.BlockSpec((tm, LANES), lambda i: (i, meta_blk)),
                  pl.BlockSpec((tm, tm), lambda i: (0, 0))],
        out_specs=[pl.BlockSpec((1, 1, tm), lambda i: (i, 0, 0)),
                   pl.BlockSpec((1, 1, tm), lambda i: (i, 0, 0)),
                   pl.BlockSpec((LANES, 1), lambda i: (0, 0))],
        out_shape=[jax.ShapeDtypeStruct((nt, 1, tm), jnp.int32),
                   jax.ShapeDtypeStruct((nt, 1, tm), jnp.int32),
                   jax.ShapeDtypeStruct((LANES, 1), jnp.int32)],
        scratch_shapes=[pltpu.VMEM((LANES, 1), F32)],
        compiler_params=_cparams("arbitrary"),
        name="route_plan",
    )(h2x, tri)
    return ids.reshape(r), rank.reshape(r), cnt.reshape(LANES)


def _row_copy(src, dst, i, j, sem):
    return pltpu.make_async_copy(src.at[pl.ds(i, 1)], dst.at[pl.ds(j, 1)], sem)


def _dispatch_kernel(pos_ref, h_ref, init_hbm, o_hbm, sem):
    del init_hbm
    ch = h_ref.shape[0]
    base = pl.program_id(0) * ch

    def issue(i, c):
        for k in range(2):
            _row_copy(h_ref, o_hbm, 2 * i + k, pos_ref[base + 2 * i + k], sem).start(priority=k)
        return c

    def drain(i, c):
        _row_copy(h_ref, o_hbm, i, 0, sem).wait()
        return c

    lax.fori_loop(0, ch // 2, issue, 0, unroll=4)
    lax.fori_loop(0, ch, drain, 0, unroll=8)


def dispatch_rows(pos, h2, n_rows, ch):
    r, d = h2.shape
    return pl.pallas_call(
        _dispatch_kernel,
        grid_spec=pltpu.PrefetchScalarGridSpec(
            num_scalar_prefetch=1,
            grid=(r // ch,),
            in_specs=[pl.BlockSpec((ch, d), lambda i, p: (i, 0)), pl.BlockSpec(memory_space=pl.ANY)],
            out_specs=pl.BlockSpec(memory_space=pl.ANY),
            scratch_shapes=[pltpu.SemaphoreType.DMA],
        ),
        out_shape=jax.ShapeDtypeStruct((n_rows, d), h2.dtype),
        input_output_aliases={2: 0},
        compiler_params=_cparams("arbitrary"),
        name="dispatch_rows",
    )(pos, h2, jnp.zeros((n_rows, d), h2.dtype))


def _expert_kernel(ta_ref, tb_ref, nu_ref, hs_ref, wg_ref, wu_ref, wd_ref, y_ref, gu_s, dn_s):
    j = pl.program_id(0)
    d = wg_ref.shape[2]
    ff = wg_ref.shape[3]
    epg = wg_ref.shape[1]
    prev = jnp.maximum(j - 1, 0)

    @pl.when((j == 0) | (ta_ref[j] // epg != ta_ref[prev] // epg))
    def _():
        for e in range(epg):
            gu_s[e, :, :ff] = wg_ref[0, e].astype(BF16)
            gu_s[e, :, ff:] = wu_ref[0, e].astype(BF16)
            dn_s[e] = wd_ref[0, e].astype(BF16)

    @pl.when(j < nu_ref[0])
    def _():
        h = hs_ref[:, :d].astype(BF16)
        meta = hs_ref[:, d:]
        es = [t_ref[j] % epg for t_ref in (ta_ref, tb_ref)]
        hgus = [jnp.dot(h, gu_s[e], preferred_element_type=F32) for e in es]
        y = None
        for e, hgu, lane in zip(es, hgus, (META_WA, META_WB)):
            hg = hgu[:, :ff]
            hid = (hg * jax.nn.sigmoid(hg)) * hgu[:, ff:]
            part = meta[:, lane:lane + 1] * jnp.dot(hid.astype(BF16), dn_s[e], preferred_element_type=F32)
            y = part if y is None else y + part
        y_ref[:, :d] = y
        y_ref[:, d:] = meta


def expert_pairs(tile_a, tile_b, n_used, hs, w_gate, w_up, w_down, layer):
    rows, dx = hs.shape
    depth, ne, d, ff = w_gate.shape
    epg = EXPERTS_PER_GROUP
    ng = ne // epg
    nt = rows // ROW_TILE
    blk = lambda i, ta, tb, nu: (jnp.minimum(i, nu[0] - 1), 0)
    grp = lambda shape: pl.BlockSpec((1, epg) + shape, lambda i, ta, tb, nu: (layer * ng + ta[i] // epg, 0, 0, 0),
                                     pipeline_mode=pl.Buffered(1))
    by_group = lambda w: w.reshape((depth * ng, epg) + w.shape[2:])
    return pl.pallas_call(
        _expert_kernel,
        grid_spec=pltpu.PrefetchScalarGridSpec(
            num_scalar_prefetch=3,
            grid=(nt,),
            in_specs=[pl.BlockSpec((ROW_TILE, dx), blk), grp((d, ff)), grp((d, ff)), grp((ff, d))],
            out_specs=pl.BlockSpec((ROW_TILE, dx), blk),
            scratch_shapes=[pltpu.VMEM((epg, d, 2 * ff), BF16), pltpu.VMEM((epg, ff, d), BF16)],
        ),
        out_shape=jax.ShapeDtypeStruct((rows, dx), F32),
        input_output_aliases={3: 0},
        compiler_params=_cparams("arbitrary"),
        name="expert_pairs",
    )(tile_a, tile_b, n_used, hs, by_group(w_gate), by_group(w_up), by_group(w_down))


def _combine_kernel(pos_ref, x_ref, gate_ref, y_hbm, o_ref, ybuf, sems):
    tm = x_ref.shape[0]
    d = x_ref.shape[1]
    i = pl.program_id(0)
    slot = i % 2

    def gather(tile, into):
        def issue(r, c):
            for k in range(2):
                row = 2 * r + k
                _row_copy(y_hbm, ybuf.at[into], pos_ref[tile * tm + row], row, sems.at[into]).start(priority=k)
            return c
        lax.fori_loop(0, tm // 2, issue, 0, unroll=4)

    @pl.when(i == 0)
    def _():
        gather(0, 0)

    @pl.when(i + 1 < pl.num_programs(0))
    def _():
        gather(i + 1, 1 - slot)

    def drain(r, c):
        _row_copy(y_hbm, ybuf.at[slot], 0, r, sems.at[slot]).wait()
        return c

    lax.fori_loop(0, tm, drain, 0, unroll=8)
    o_ref[...] = x_ref[...] + gate_ref[0] * ybuf[slot, :, :d]


def combine_rows(pos, x_mid, mod3, y, *, seq, nseg, tm):
    r, d = x_mid.shape
    tiles_per_seq = seq // tm
    seg = lambda i: jnp.minimum(i // tiles_per_seq, nseg - 1)
    return pl.pallas_call(
        _combine_kernel,
        grid_spec=pltpu.PrefetchScalarGridSpec(
            num_scalar_prefetch=1,
            grid=(r // tm,),
            in_specs=[pl.BlockSpec((tm, d), lambda i, p: (i, 0)),
                      pl.BlockSpec((1, 1, d), lambda i, p: (seg(i) * N_MOD + 5, 0, 0)),
                      pl.BlockSpec(memory_space=pl.ANY)],
            out_specs=pl.BlockSpec((tm, d), lambda i, p: (i, 0)),
            scratch_shapes=[pltpu.VMEM((2, tm, y.shape[1]), F32), pltpu.SemaphoreType.DMA((2,))],
        ),
        out_shape=jax.ShapeDtypeStruct((r, d), F32),
        compiler_params=_cparams("arbitrary"),
        name="combine_rows",
    )(pos, x_mid, mod3, y)


def moe_block(x_mid, h2, mod3, w_gate, w_up, w_down, *, layer, seq, nseg):
    r, d = x_mid.shape
    ids, rank, cnt = route_plan(h2, 512)
    tiles = (cnt + (ROW_TILE - 1)) // ROW_TILE
    incl = jnp.cumsum(tiles)
    n_used = incl[-1]
    lookup = lambda table, idx: jnp.sum(jnp.where(idx[:, None] == jnp.arange(LANES)[None, :], table[None, :], 0), axis=1)
    pos = lookup(incl - tiles, ids) * ROW_TILE + rank
    nt = r // ROW_TILE + N_BUCKETS
    last = jnp.minimum(jnp.arange(nt), n_used - 1)
    tile_bucket = jnp.sum((incl[None, :] <= last[:, None]).astype(jnp.int32), axis=1)
    ea, eb = _bucket_experts()
    tile_a = lookup(jnp.asarray(ea), tile_bucket)
    tile_b = lookup(jnp.asarray(eb), tile_bucket)
    hs = dispatch_rows(pos, h2, nt * ROW_TILE, 512)
    y = expert_pairs(tile_a, tile_b, n_used.reshape(1).astype(jnp.int32), hs, w_gate, w_up, w_down, layer)
    return combine_rows(pos, x_mid, mod3, y, seq=seq, nseg=nseg, tm=256)


def _rope_tables(seq, tm):
    t = np.arange(seq)
    row = (t // GRID_W).astype(np.float32)
    col = (t % GRID_W).astype(np.float32)
    half = HEAD_DIM // 2
    inv = jnp.asarray(ROPE_THETA, F32) ** (-jnp.arange(0, half, 2, dtype=F32) / half)
    ang = jnp.concatenate([jnp.asarray(row)[:, None] * inv, jnp.asarray(col)[:, None] * inv], axis=-1)
    cos = jnp.repeat(jnp.cos(ang), 2, axis=-1)
    sin = jnp.repeat(jnp.sin(ang), 2, axis=-1) * jnp.asarray(np.tile([-1.0, 1.0], half), F32)
    cos = jnp.tile(cos, (1, LANES // HEAD_DIM))
    sin = jnp.tile(sin, (1, LANES // HEAD_DIM))
    cos = jnp.concatenate([cos, jnp.ones((tm, LANES), F32)], axis=0)
    sin = jnp.concatenate([sin, jnp.zeros((tm, LANES), F32)], axis=0)
    return cos, sin


def kernel(x, c, ctx, c_ctx, w_ada, b_ada, norm1_g, w_in, na_q_norm, na_k_norm, na_rpb, gqa_q_norm, gqa_k_norm, hgrn_lb, hgrn_o_norm, w_out, norm2_g, w_route_group, b_route_group, w_route_expert, b_route_expert, w_exp_gate, w_exp_up, w_exp_down):
    b, s, d = x.shape
    l = ctx.shape[1]
    depth = w_ada.shape[0]
    assert s % 512 == 0 and (b * l) % 512 == 0 and s // GRID_W >= NA_WIN_R
    assert s % l == 0 and l % HG_BLOCK == 0
    nseg = b + 1
    n_lat = b * s
    n_ctx = b * l
    na_w, gq_qw = d // 4, d // 2
    gq_kw = gq_qw // 4
    hg_w = d // 4
    tm = 512

    c_all = jnp.zeros((16, d), F32).at[:b].set(c).at[b].set(c_ctx)
    mod = ada_mod(c_all, w_ada, b_ada)
    cos_t, sin_t = _rope_tables(s, tm)
    bd_f = _block_diag_ones(LANES, HEAD_DIM, F32)
    bd_b = _block_diag_ones(LANES, HEAD_DIM, BF16)
    p_lb = jax.nn.softmax(hgrn_lb.astype(F32), axis=0)
    lb_all = jnp.cumsum(p_lb, axis=0) - p_lb[0]
    tile2 = lambda g: jnp.tile(g, LANES // HEAD_DIM)

    xall = (x.reshape(n_lat, d), ctx.reshape(n_ctx, d))
    for layer in range(depth):
        ctx_out = layer < depth - 1
        mod3 = mod[layer].reshape(16 * N_MOD, 1, d)
        gains = jnp.zeros((8, LANES), F32)
        gains = gains.at[0].set(tile2(na_q_norm[layer])).at[1].set(tile2(na_k_norm[layer]))
        gains = gains.at[2].set(tile2(gqa_q_norm[layer])).at[3].set(tile2(gqa_k_norm[layer]))
        pa, pb, pc = in_projection(xall, mod3, norm1_g[layer][None], w_in[layer].astype(BF16), cos_t, sin_t,
                                   gains, bd_b, n_lat_rows=n_lat, seq=s, nseg=nseg, tm=tm)
        bias_tab = na_bias_table(na_rpb[layer], s // GRID_W)
        o_a = na_attention(pa, bias_tab, b=b, s=s, l=l, na_w=na_w)
        o_b = gqa_attention(pb, b=b, s=s, l=l, qw=gq_qw, kw=gq_kw, tq=256)
        lb4 = lb_all[layer].reshape(2, hg_w // LANES, 1, LANES)
        y_lat, y_ctx = hgrn_mixer(pc, lb4, tile2(hgrn_o_norm[layer])[None], bd_b, bd_f, b=b, s=s, l=l)

        w_route = jnp.zeros((d, LANES), F32).at[:, :N_GROUPS].set(w_route_group[layer])
        w_route = w_route.at[:, N_GROUPS:N_GROUPS + N_EXPERTS].set(w_route_expert[layer])
        b_route = jnp.zeros((1, LANES), F32).at[0, :N_GROUPS].set(b_route_group[layer])
        b_route = b_route.at[0, N_GROUPS:N_GROUPS + N_EXPERTS].set(b_route_expert[layer])
        if ctx_out:
            o_ac, o_bc = ctx_attention(pa, pb, b=b, s=s, l=l, na_w=na_w, qw=gq_qw, kw=gq_kw)
            mix_a, mix_b, y_c = (o_a, o_ac), (o_b, o_bc), (y_lat, y_ctx)
            n_rows = n_lat + n_ctx
        else:
            mix_a, mix_b, y_c, n_rows = o_a, o_b, y_lat, n_lat
        x_mid, h2 = out_projection(xall, mix_a, mix_b, y_c, w_out[layer].astype(BF16), mod3,
                                    norm2_g[layer][None], w_route, b_route, n_rows=n_rows, seq=s,
                                    nseg=nseg, tm=tm)
        xall = moe_block(x_mid, h2, mod3, w_exp_gate, w_exp_up, w_exp_down, layer=layer, seq=s, nseg=nseg)
    return xall[:n_lat].reshape(b, s, d)
```

```python
import functools

import jax
import jax.numpy as jnp
import numpy as np
from jax import lax
from jax.experimental import pallas as pl
from jax.experimental.pallas import tpu as pltpu

F32 = jnp.float32
BF16 = jnp.bfloat16
HIGHEST = lax.Precision.HIGHEST

HEAD_DIM = 64
GRID_W = 64
NA_WIN_R = 8
NA_WIN_C = 16
ROPE_THETA = 10000.0
HGRN_CHUNK = 16
N_GROUPS = 4
EXPERTS_PER_GROUP = 8
N_EXPERTS = N_GROUPS * EXPERTS_PER_GROUP
N_MOD = 6
EPS = 1e-6
NEG_INF = -1e30
LB_FLOOR = 1e-20
LANES = 128
VMEM_LIMIT = 56 * 1024 * 1024


def _cparams(*sem):
    return pltpu.CompilerParams(dimension_semantics=sem, vmem_limit_bytes=VMEM_LIMIT)


def _block_diag_ones(n, blk, dtype):
    i = np.arange(n)
    return jnp.asarray((i[:, None] // blk) == (i[None, :] // blk), dtype=dtype)


def _ada_kernel(c_ref, w_ref, b_ref, o_ref):
    c = c_ref[...]
    s = c * jax.nn.sigmoid(c)
    o_ref[0] = jnp.dot(s, w_ref[0], precision=HIGHEST, preferred_element_type=F32) + b_ref[0]


def ada_mod(c_all, w_ada, b_ada):
    depth, d, n = w_ada.shape
    tn = 1536
    return pl.pallas_call(
        _ada_kernel,
        grid=(depth, n // tn),
        in_specs=[
            pl.BlockSpec((16, d), lambda l, j: (0, 0)),
            pl.BlockSpec((1, d, tn), lambda l, j: (l, 0, j)),
            pl.BlockSpec((1, 1, tn), lambda l, j: (l, 0, j)),
        ],
        out_specs=pl.BlockSpec((1, 16, tn), lambda l, j: (l, 0, j)),
        out_shape=jax.ShapeDtypeStruct((depth, 16, n), F32),
        compiler_params=_cparams("parallel", "parallel"),
        name="ada_mod",
    )(c_all, w_ada, b_ada.reshape(depth, 1, n))


def _seg_inv_rms(x, bd):
    xs = x * x
    hi = xs.astype(BF16)
    lo = (xs - hi.astype(F32)).astype(BF16)
    ss = jnp.dot(hi, bd, preferred_element_type=F32) + jnp.dot(lo, bd, preferred_element_type=F32)
    return lax.rsqrt(ss * (1.0 / HEAD_DIM) + EPS)


def _pair_swap(x):
    lane = lax.broadcasted_iota(jnp.int32, x.shape, 1)
    return jnp.where((lane & 1) == 0, pltpu.roll(x, LANES - 1, 1), pltpu.roll(x, 1, 1))


def _as_parts(a):
    return tuple(a) if isinstance(a, (tuple, list)) else (a,)


def _row_specs(parts, tm, lat_tiles):
    w = parts[0].shape[1]
    if len(parts) == 1:
        return [pl.BlockSpec((tm, w), lambda i: (i, 0))]
    return [pl.BlockSpec((tm, w), lambda i: (jnp.minimum(i, lat_tiles - 1), 0)),
            pl.BlockSpec((tm, w), lambda i: (jnp.maximum(i - lat_tiles, 0), 0))]


def _row_tile(refs, lat_tiles):
    if len(refs) == 1:
        return refs[0][...]
    return jnp.where(pl.program_id(0) < lat_tiles, refs[0][...], refs[1][...])


def _inproj_kernel(*refs, nx, lat_tiles, na_w, gq_qw, gq_kw):
    x_refs, refs = refs[:nx], refs[nx:]
    g1_ref, shift_ref, scale_ref, w_ref, cos_ref, sin_ref, gains_ref, bd_ref, oa_ref, ob_ref, oc_ref = refs
    x = _row_tile(x_refs, lat_tiles)
    tm = x.shape[0]
    bd = bd_ref[...]
    qscale = HEAD_DIM ** -0.5
    b0 = 3 * na_w
    halves = [slice(0, tm // 2), slice(tm // 2, tm)]
    hs = []
    for rows in halves:
        xh = x[rows]
        ms = jnp.mean(xh * xh, axis=-1, keepdims=True)
        h = xh * lax.rsqrt(ms + EPS) * g1_ref[0]
        hs.append((h * (1.0 + scale_ref[0]) + shift_ref[0]).astype(BF16))
    ps = [jnp.dot(h, w_ref[...], preferred_element_type=F32) for h in hs]
    for rows, p in zip(halves, ps):
        cos = cos_ref[rows, :]
        sin = sin_ref[rows, :]

        def normed(col, gain_row):
            xb = p[:, col:col + LANES]
            return xb * _seg_inv_rms(xb, bd) * gains_ref[gain_row:gain_row + 1, :]

        def rope(xn):
            return xn * cos + _pair_swap(xn) * sin

        for j in range(na_w // LANES):
            c = j * LANES
            oa_ref[rows, c:c + LANES] = (normed(c, 0) * qscale).astype(BF16)
            oa_ref[rows, na_w + c:na_w + c + LANES] = normed(na_w + c, 1).astype(BF16)
        oa_ref[rows, 2 * na_w:3 * na_w] = p[:, 2 * na_w:3 * na_w].astype(BF16)
        for j in range(gq_qw // LANES):
            c = j * LANES
            ob_ref[rows, c:c + LANES] = (rope(normed(b0 + c, 2)) * qscale).astype(BF16)
        for j in range(gq_kw // LANES):
            c = gq_qw + j * LANES
            ob_ref[rows, c:c + LANES] = rope(normed(b0 + c, 3)).astype(BF16)
        ob_ref[rows, gq_qw + gq_kw:] = p[:, b0 + gq_qw + gq_kw:b0 + gq_qw + 2 * gq_kw].astype(BF16)
        oc_ref[rows, :] = p[:, b0 + gq_qw + 2 * gq_kw:]


def in_projection(xall, mod3, layer_g1, w_in_bf, cos_t, sin_t, gains, bd, *, n_lat_rows, seq, nseg, tm):
    x_parts = _as_parts(xall)
    r, d = sum(a.shape[0] for a in x_parts), x_parts[0].shape[1]
    d_in = w_in_bf.shape[1]
    na_w = d // 4
    gq_qw = d // 2
    gq_kw = gq_qw // 4
    c_w = d_in - 3 * na_w - gq_qw - 2 * gq_kw
    lat_tiles = n_lat_rows // tm
    tiles_per_seq = seq // tm

    def seg(i):
        return jnp.minimum(i // tiles_per_seq, nseg - 1)

    def rope_blk(i):
        return jnp.where(i < lat_tiles, i % tiles_per_seq, tiles_per_seq)

    kern = functools.partial(_inproj_kernel, nx=len(x_parts), lat_tiles=lat_tiles, na_w=na_w, gq_qw=gq_qw,
                             gq_kw=gq_kw)
    return pl.pallas_call(
        kern,
        grid=(r // tm,),
        in_specs=_row_specs(x_parts, tm, lat_tiles) + [
            pl.BlockSpec((1, d), lambda i: (0, 0)),
            pl.BlockSpec((1, 1, d), lambda i: (seg(i) * N_MOD + 0, 0, 0)),
            pl.BlockSpec((1, 1, d), lambda i: (seg(i) * N_MOD + 1, 0, 0)),
            pl.BlockSpec((d, d_in), lambda i: (0, 0)),
            pl.BlockSpec((tm, LANES), lambda i: (rope_blk(i), 0)),
            pl.BlockSpec((tm, LANES), lambda i: (rope_blk(i), 0)),
            pl.BlockSpec((8, LANES), lambda i: (0, 0)),
            pl.BlockSpec((LANES, LANES), lambda i: (0, 0)),
        ],
        out_specs=[
            pl.BlockSpec((tm, 3 * na_w), lambda i: (i, 0)),
            pl.BlockSpec((tm, gq_qw + 2 * gq_kw), lambda i: (i, 0)),
            pl.BlockSpec((tm, c_w), lambda i: (i, 0)),
        ],
        out_shape=[
            jax.ShapeDtypeStruct((r, 3 * na_w), BF16),
            jax.ShapeDtypeStruct((r, gq_qw + 2 * gq_kw), BF16),
            jax.ShapeDtypeStruct((r, c_w), F32),
        ],
        compiler_params=_cparams("parallel"),
        name="in_projection",
    )(*x_parts, layer_g1, mod3, mod3, w_in_bf, cos_t, sin_t, gains, bd)


def _head(j):
    return slice(j * HEAD_DIM, (j + 1) * HEAD_DIM)


def _with_ones(v):
    return jnp.concatenate([v, jnp.ones_like(v)], axis=1)


def _scores(q, k):
    return lax.dot_general(q, k, (((1,), (1,)), ((), ())), preferred_element_type=F32)


def _normalise(o):
    return o[:, :HEAD_DIM] / o[:, HEAD_DIM:HEAD_DIM + 1]


def _softmax_attend_all(qs, ks, v1s):
    ss = [_scores(q, k) for q, k in zip(qs, ks)]
    outs = []
    for s, v1 in zip(ss, v1s):
        p = jnp.exp((s - jnp.max(s, axis=-1, keepdims=True)).astype(BF16))
        outs.append(_normalise(jnp.dot(p, v1, preferred_element_type=F32)))
    return outs


def _grouped_attend(q_ref, k_of, v_of, n_kv, grp, splits=1):
    t = q_ref.shape[0] // splits
    qs, ks, vs = [], [], []
    for j in range(n_kv):
        for h in range(splits):
            qs.append(jnp.concatenate([q_ref[h * t:(h + 1) * t, _head(j * grp + g)] for g in range(grp)], axis=0))
            ks.append(k_of(j))
            vs.append(v_of(j))
    os_ = _softmax_attend_all(qs, ks, vs)
    rows = []
    for h in range(splits):
        cols = []
        for j in range(n_kv):
            o = os_[j * splits + h]
            cols += [o[g * t:(g + 1) * t] for g in range(grp)]
        rows.append(jnp.concatenate(cols, axis=1))
    return jnp.concatenate(rows, axis=0) if splits > 1 else rows[0]


def _gqa_kernel(q_ref, kl_ref, vl_ref, kc_ref, vc_ref, o_ref, k_s, v_s, *, n_kv, grp):
    s_len = kl_ref.shape[0]

    @pl.when(pl.program_id(1) == 0)
    def _():
        for j in range(n_kv):
            k_s[j, :s_len, :] = kl_ref[:, _head(j)]
            k_s[j, s_len:, :] = kc_ref[:, _head(j)]
            v_s[j, :s_len, :] = _with_ones(vl_ref[:, _head(j)])
            v_s[j, s_len:, :] = _with_ones(vc_ref[:, _head(j)])

    o = _grouped_attend(q_ref, lambda j: k_s[j], lambda j: v_s[j], n_kv, grp, splits=2)
    o_ref[...] = o.astype(o_ref.dtype)


def gqa_attention(pb, *, b, s, l, qw, kw, tq):
    n_lat = b * s
    n_kv = kw // HEAD_DIM
    grp = qw // kw
    assert kw == LANES and qw % kw == 0
    kcol, vcol = qw // kw, qw // kw + 1
    kern = functools.partial(_gqa_kernel, n_kv=n_kv, grp=grp)
    return pl.pallas_call(
        kern,
        grid=(b, s // tq),
        in_specs=[
            pl.BlockSpec((tq, qw), lambda i, j: (i * (s // tq) + j, 0)),
            pl.BlockSpec((s, kw), lambda i, j: (i, kcol)),
            pl.BlockSpec((s, kw), lambda i, j: (i, vcol)),
            pl.BlockSpec((l, kw), lambda i, j: (n_lat // l + i, kcol)),
            pl.BlockSpec((l, kw), lambda i, j: (n_lat // l + i, vcol)),
        ],
        out_specs=pl.BlockSpec((tq, qw), lambda i, j: (i * (s // tq) + j, 0)),
        out_shape=jax.ShapeDtypeStruct((n_lat, qw), BF16),
        scratch_shapes=[pltpu.VMEM((n_kv, s + l, HEAD_DIM), BF16), pltpu.VMEM((n_kv, s + l, 2 * HEAD_DIM), BF16)],
        compiler_params=_cparams("parallel", "arbitrary"),
        name="gqa_attention",
    )(pb, pb, pb, pb, pb)


def _ctx_attn_kernel(qa_ref, ka_ref, va_ref, qb_ref, kb_ref, vb_ref, oa_ref, ob_ref, *, n_kv, grp):
    na_h = qa_ref.shape[1] // HEAD_DIM
    oa = _softmax_attend_all([qa_ref[:, _head(h)] for h in range(na_h)],
                             [ka_ref[:, _head(h)] for h in range(na_h)],
                             [_with_ones(va_ref[:, _head(h)]) for h in range(na_h)])
    oa_ref[...] = jnp.concatenate(oa, axis=1).astype(oa_ref.dtype)
    ob = _grouped_attend(qb_ref, lambda j: kb_ref[:, _head(j)], lambda j: _with_ones(vb_ref[:, _head(j)]),
                         n_kv, grp)
    ob_ref[...] = ob.astype(ob_ref.dtype)


def ctx_attention(pa, pb, *, b, s, l, na_w, qw, kw):
    r0 = (b * s) // l
    grp = qw // kw
    kern = functools.partial(_ctx_attn_kernel, n_kv=kw // HEAD_DIM, grp=grp)
    a_spec = lambda m: pl.BlockSpec((l, na_w), lambda i: (r0 + i, m))
    return pl.pallas_call(
        kern,
        grid=(b,),
        in_specs=[a_spec(0), a_spec(1), a_spec(2),
                  pl.BlockSpec((l, qw), lambda i: (r0 + i, 0)),
                  pl.BlockSpec((l, kw), lambda i: (r0 + i, grp)),
                  pl.BlockSpec((l, kw), lambda i: (r0 + i, grp + 1))],
        out_specs=[pl.BlockSpec((l, na_w), lambda i: (i, 0)), pl.BlockSpec((l, qw), lambda i: (i, 0))],
        out_shape=[jax.ShapeDtypeStruct((b * l, na_w), BF16), jax.ShapeDtypeStruct((b * l, qw), BF16)],
        compiler_params=_cparams("parallel"),
        name="ctx_attention",
    )(pa, pa, pa, pb, pb, pb)


NA_QROWS = 4
NA_UNION = NA_WIN_R + NA_QROWS


def _na_block_geometry(rows):
    wu = min(rows, NA_UNION)
    wr = min(NA_WIN_R, rows)
    nblk = rows // NA_QROWS
    sig, u0s = [], []
    for blk in range(nblk):
        r0 = blk * NA_QROWS
        u0 = int(np.clip(r0 - wr // 2, 0, rows - wu))
        u0s.append(u0)
        sig.append(tuple((r0 + j - u0, int(np.clip(r0 + j - wr // 2, 0, rows - wr)) - u0) for j in range(NA_QROWS)))
    cls = [int(blk > 0) + int(blk == nblk - 1) for blk in range(nblk)]
    reps = {}
    for blk in range(nblk):
        assert reps.setdefault(cls[blk], sig[blk]) == sig[blk]
    return wu, wr, [reps.get(c, reps[0]) for c in range(3)]


def _na_kernel(q_ref, k_ref, v_ref, kc_ref, vc_ref, bias_ref, o_ref, k_s, v_s, kc_s, vc_s, *, rows, wu, wr):
    nh = q_ref.shape[1] // HEAD_DIM
    nblk = rows // NA_QROWS
    nq = NA_QROWS * GRID_W
    for h in range(nh):
        k_s[h] = k_ref[:, _head(h)]
        v_s[h] = _with_ones(v_ref[:, _head(h)])
        kc_s[h] = kc_ref[:, _head(h)]
        vc_s[h] = _with_ones(vc_ref[:, _head(h)])

    def body(blk, carry):
        u0 = jnp.clip(blk * NA_QROWS - wr // 2, 0, rows - wu)
        cls = jnp.minimum(blk, 1) + jnp.maximum(blk - (nblk - 2), 0)
        q_rows = pl.ds(pl.multiple_of(blk * nq, nq), nq)
        k_rows = pl.ds(pl.multiple_of(u0 * GRID_W, GRID_W), wu * GRID_W)
        outs = []
        for h in range(nh):
            q = q_ref[q_rows, _head(h)]
            s_nb = _scores(q, k_s[h, k_rows, :]) + bias_ref[h, cls]
            s_cx = _scores(q, kc_s[h])
            m = jnp.maximum(jnp.max(s_nb, axis=-1, keepdims=True), jnp.max(s_cx, axis=-1, keepdims=True))
            p_nb = jnp.exp((s_nb - m).astype(BF16))
            p_cx = jnp.exp((s_cx - m).astype(BF16))
            outs.append(_normalise(jnp.dot(p_nb, v_s[h, k_rows, :], preferred_element_type=F32)
                                   + jnp.dot(p_cx, vc_s[h], preferred_element_type=F32)))
        o_ref[q_rows, :] = jnp.concatenate(outs, axis=1).astype(o_ref.dtype)
        return carry

    lax.fori_loop(0, nblk, body, 0)


def na_attention(pa, bias_tab, *, b, s, l, na_w):
    n_lat = b * s
    nh = na_w // HEAD_DIM
    rows = s // GRID_W
    wu, wr, _ = _na_block_geometry(rows)
    kern = functools.partial(_na_kernel, rows=rows, wu=wu, wr=wr)
    lat = lambda m: pl.BlockSpec((s, na_w), lambda i: (i, m))
    cx = lambda m: pl.BlockSpec((l, na_w), lambda i: (n_lat // l + i, m))
    return pl.pallas_call(
        kern,
        grid=(b,),
        in_specs=[lat(0), lat(1), lat(2), cx(1), cx(2),
                  pl.BlockSpec(bias_tab.shape, lambda i: (0, 0, 0, 0))],
        out_specs=pl.BlockSpec((s, na_w), lambda i: (i, 0)),
        out_shape=jax.ShapeDtypeStruct((n_lat, na_w), BF16),
        scratch_shapes=[pltpu.VMEM((nh, s, HEAD_DIM), BF16), pltpu.VMEM((nh, s, 2 * HEAD_DIM), BF16),
                        pltpu.VMEM((nh, l, HEAD_DIM), BF16), pltpu.VMEM((nh, l, 2 * HEAD_DIM), BF16)],
        compiler_params=_cparams("parallel"),
        name="na_attention",
    )(pa, pa, pa, pa, pa, bias_tab)


def na_bias_table(rpb, rows):
    wu, wr, reps = _na_block_geometry(rows)
    h = rpb.shape[0]
    r_off = np.array([[rj for rj, _ in rep] for rep in reps])
    s_off = np.array([[sj for _, sj in rep] for rep in reps])
    kr = np.arange(wu)
    row_ok = (kr >= s_off[..., None]) & (kr < s_off[..., None] + wr)
    dr = kr - r_off[..., None] + (NA_WIN_R - 1)
    cidx = np.arange(GRID_W)
    col_start = np.clip(cidx - NA_WIN_C // 2, 0, GRID_W - NA_WIN_C)
    col_ok = (cidx[None, :] >= col_start[:, None]) & (cidx[None, :] < col_start[:, None] + NA_WIN_C)
    dc = np.clip(cidx[None, :] - cidx[:, None] + (NA_WIN_C - 1), 0, 2 * NA_WIN_C - 2)
    sel_r = jnp.asarray((dr[..., None] == np.arange(2 * NA_WIN_R - 1)) & row_ok[..., None], F32)
    sel_c = jnp.asarray(dc[:, :, None] == np.arange(2 * NA_WIN_C - 1), F32)
    bias = jnp.einsum("cjki,hid->hcjkd", sel_r, rpb.astype(F32), precision=HIGHEST)
    bias = jnp.einsum("hcjkd,qxd->hcjqkx", bias, sel_c, precision=HIGHEST)
    ok = row_ok[None, :, :, None, :, None] & col_ok[None, None, None, :, None, :]
    bias = jnp.where(jnp.asarray(ok), bias, NEG_INF)
    return bias.reshape(h, len(reps), NA_QROWS * GRID_W, wu * GRID_W)


HG_BLOCK = 128


def _hgrn_block(q_ref, v_ref, z_ref, o_acc, bc_ref, lb, bdb, bdf, st, blk, *, reverse):
    c = HGRN_CHUNK
    hc = c // 2
    ncb = HG_BLOCK // c
    lbm = jnp.maximum(lb, LB_FLOOR)
    one_m_lb = 1.0 - lb
    scale = HEAD_DIM ** -0.5
    t_idx = lax.broadcasted_iota(jnp.int32, (2 * ncb, hc, LANES), 1)
    edge = 0 if reverse else c - 1
    early, late = (1, 0) if reverse else (0, 1)

    def bs(x, s):
        return jnp.broadcast_to(x[:, s:s + 1, :], x.shape)

    def seen(s):
        return (t_idx <= s) if reverse else (t_idx >= s)

    def halves(x):
        x4 = x.reshape(ncb, 2, hc, LANES)
        return x4[:, early], x4[:, late]

    def chunks(xe, xl):
        parts = [xl, xe] if reverse else [xe, xl]
        return jnp.stack(parts, axis=1).reshape(ncb, c, LANES)

    def seg_sum(w):
        rows = w.shape[0] * w.shape[1]
        return jnp.dot(w.reshape(rows, LANES).astype(BF16), bdb, preferred_element_type=F32).reshape(w.shape)

    if True:
        r0 = pl.multiple_of(blk * HG_BLOCK, HG_BLOCK)
        z = z_ref[pl.ds(r0, HG_BLOCK), :]
        q = q_ref[pl.ds(r0, HG_BLOCK), :] * scale
        v = v_ref[pl.ds(r0, HG_BLOCK), :]
        sig = jax.nn.sigmoid(z)
        f = one_m_lb * sig + lbm
        k = one_m_lb * (1.0 - sig) - (lbm - lb)
        logf = jnp.log2(f)
        pre = logf.reshape(2 * ncb, hc, LANES)
        for sh in (1, 2, 4):
            if reverse:
                pre = pre + jnp.where(t_idx < hc - sh, pltpu.roll(pre, hc - sh, 1), 0.0)
            else:
                pre = pre + jnp.where(t_idx >= sh, pltpu.roll(pre, sh, 1), 0.0)
        pre_e, pre_l = halves(pre.reshape(HG_BLOCK, LANES))
        cum_e = pre_e
        cum_l = pre_l + bs(pre_e, 0 if reverse else hc - 1)
        q8, k8, v8 = (a.reshape(2 * ncb, hc, LANES) for a in (q, k, v))
        for slot, a in enumerate((pre, k8, v8)):
            bc_ref[slot] = a

        def row(slot, s, early_only=False):
            lead = pl.ds(early, ncb, stride=2) if early_only else slice(None)
            return jnp.broadcast_to(bc_ref[slot, lead, s:s + 1, :], (ncb if early_only else 2 * ncb, hc, LANES))

        o8 = jnp.zeros_like(pre)
        for s in range(hc):
            d = jnp.where(seen(s), pre - row(0, s), NEG_INF)
            o8 = o8 + seg_sum(q8 * row(1, s) * jnp.exp2(d)) * row(2, s)
        q_l = halves(q)[1]
        o_e, o_l = halves(o8.reshape(HG_BLOCK, LANES))
        for s in range(hc):
            o_l = o_l + seg_sum(q_l * row(1, s, True) * jnp.exp2(cum_l - row(0, s, True))) * row(2, s, True)
        cum = chunks(cum_e, cum_l)
        o3 = chunks(o_e, o_l)
        q3 = q.reshape(ncb, c, LANES)
        k3 = k.reshape(ncb, c, LANES)
        v3 = v.reshape(ncb, c, LANES)
        cum_edge = bs(cum, edge)
        qe = (q3 * jnp.exp2(cum)).astype(BF16)
        kd = (k3 * jnp.exp2(cum_edge - cum)).astype(BF16)
        vb = v3.astype(BF16)
        chunk_decay = jnp.exp2(cum_edge)
        yield None
        u_t = [lax.dot_general(vb[n], kd[n], (((0,), (0,)), ((), ())), preferred_element_type=F32) * bdf
               for n in range(ncb)]
        yield None
        enter = [None] * ncb
        for n in (range(ncb - 1, -1, -1) if reverse else range(ncb)):
            enter[n] = st.astype(BF16)
            st = chunk_decay[n, 0:1, :] * st + u_t[n]
        yield None
        outs = [o3[n] + lax.dot_general(qe[n], enter[n], (((1,), (1,)), ((), ())), preferred_element_type=F32)
                for n in range(ncb)]
        o_acc[pl.ds(r0, HG_BLOCK), :] = jnp.concatenate(outs, axis=0)
        yield st


def _hgrn_segment(q_ref, v_ref, zf_ref, zb_ref, of_acc, ob_acc, bc_ref, lb_ref, bdb, bdf, st_f, st_b):
    nblk = q_ref.shape[0] // HG_BLOCK

    def body(i, carry):
        st_f, st_b = carry
        scans = [_hgrn_block(q_ref, v_ref, zf_ref, of_acc, bc_ref.at[0], lb_ref[0, 0], bdb, bdf, st_f, i,
                             reverse=False),
                 _hgrn_block(q_ref, v_ref, zb_ref, ob_acc, bc_ref.at[1], lb_ref[1, 0], bdb, bdf, st_b,
                             nblk - 1 - i, reverse=True)]
        for _ in range(3):
            for g in scans:
                next(g)
        return tuple(next(g) for g in scans)

    return lax.fori_loop(0, nblk, body, (st_f, st_b))


def _hgrn_kernel(ql_ref, qc_ref, vl_ref, vc_ref, zfl_ref, zfc_ref, zbl_ref, zbc_ref, gl_ref, gc_ref,
                 lb_ref, gain_ref, bdb_ref, bdf_ref, yl_ref, yc_ref, olf_acc, olb_acc, ocf_acc, ocb_acc, bc_ref):
    bdb = bdb_ref[...]
    bdf = bdf_ref[...]
    zero = jnp.zeros((LANES, LANES), F32)
    st_f, st_b = _hgrn_segment(qc_ref, vc_ref, zfc_ref, zbc_ref, ocf_acc, ocb_acc, bc_ref, lb_ref, bdb, bdf,
                               zero, zero)
    _hgrn_segment(ql_ref, vl_ref, zfl_ref, zbl_ref, olf_acc, olb_acc, bc_ref, lb_ref, bdb, bdf, st_f, st_b)
    for acc_f, acc_b, g_ref, y_ref in ((olf_acc, olb_acc, gl_ref, yl_ref), (ocf_acc, ocb_acc, gc_ref, yc_ref)):
        o = acc_f[...] + acc_b[...]
        g = g_ref[...]
        y = o * _seg_inv_rms(o, bdb) * gain_ref[...]
        y_ref[...] = (y * (g * jax.nn.sigmoid(g))).astype(y_ref.dtype)


def hgrn_mixer(pc, lb, gain128, bdb, bdf, *, b, s, l):
    w = pc.shape[1] // 5
    nj = w // LANES
    n_lat = b * s
    lat = lambda m: pl.BlockSpec((s, LANES), lambda i, j: (i, m * nj + j))
    cx = lambda m: pl.BlockSpec((l, LANES), lambda i, j: (n_lat // l + i, m * nj + j))
    in_specs = []
    for m in (0, 1, 2, 3, 4):
        in_specs += [lat(m), cx(m)]
    in_specs += [pl.BlockSpec((2, 1, 1, LANES), lambda i, j: (0, j, 0, 0)),
                 pl.BlockSpec((1, LANES), lambda i, j: (0, 0)),
                 pl.BlockSpec((LANES, LANES), lambda i, j: (0, 0)),
                 pl.BlockSpec((LANES, LANES), lambda i, j: (0, 0))]
    return pl.pallas_call(
        _hgrn_kernel,
        grid=(b, nj),
        in_specs=in_specs,
        out_specs=[pl.BlockSpec((s, LANES), lambda i, j: (i, j)),
                   pl.BlockSpec((l, LANES), lambda i, j: (i, j))],
        out_shape=[jax.ShapeDtypeStruct((n_lat, w), BF16), jax.ShapeDtypeStruct((b * l, w), BF16)],
        scratch_shapes=[pltpu.VMEM((s, LANES), F32), pltpu.VMEM((s, LANES), F32),
                        pltpu.VMEM((l, LANES), F32), pltpu.VMEM((l, LANES), F32),
                        pltpu.VMEM((2, 3, HG_BLOCK // 8, 8, LANES), F32)],
        compiler_params=_cparams("parallel", "parallel"),
        name="hgrn_mixer",
    )(*([pc] * 10), lb, gain128, bdb, bdf)


def _outproj_kernel(*refs, counts, lat_tiles, wa, wb):
    rows = []
    for n in counts:
        rows.append(_row_tile(refs[:n], lat_tiles))
        refs = refs[n:]
    x_in, ma, mb, mc = rows
    w_ref, gate_ref, g2_ref, shift_ref, scale_ref, wr_ref, br_ref, xo_ref, h2_ref = refs
    d = xo_ref.shape[1]
    w = w_ref[...]
    y = jnp.dot(ma, w[:wa], preferred_element_type=F32)
    y = y + jnp.dot(mb, w[wa:wa + wb], preferred_element_type=F32)
    y = y + jnp.dot(mc, w[wa + wb:], preferred_element_type=F32)
    x = x_in + gate_ref[0] * y
    xo_ref[...] = x
    ms = jnp.mean(x * x, axis=-1, keepdims=True)
    h = x * lax.rsqrt(ms + EPS) * g2_ref[0]
    h = h * (1.0 + scale_ref[0]) + shift_ref[0]
    h2_ref[:, :d] = h
    h_hi = h.astype(BF16)
    h_lo = (h - h_hi.astype(F32)).astype(BF16)
    logits = (jnp.dot(h_hi, wr_ref[0], preferred_element_type=F32)
              + jnp.dot(h_lo, wr_ref[0], preferred_element_type=F32)
              + jnp.dot(h_hi, wr_ref[1], preferred_element_type=F32)) + br_ref[...]
    h2_ref[:, d:] = _route_meta(logits)


def out_projection(xall, mix_a, mix_b, mix_c, w_out_bf, mod3, layer_g2, w_route, b_route, *, n_rows, seq, nseg, tm):
    ops = [_as_parts(a) for a in (xall, mix_a, mix_b, mix_c)]
    r, d = n_rows, ops[0][0].shape[1]
    w_hi = w_route.astype(BF16)
    w_route = jnp.stack([w_hi, (w_route - w_hi.astype(F32)).astype(BF16)])
    wa, wb, wc = (p[0].shape[1] for p in ops[1:])
    tiles_per_seq = seq // tm
    lat_tiles = ops[1][0].shape[0] // tm

    def seg(i):
        return jnp.minimum(i // tiles_per_seq, nseg - 1)

    def modspec(m):
        return pl.BlockSpec((1, 1, d), lambda i: (seg(i) * N_MOD + m, 0, 0))

    row = lambda wdt: pl.BlockSpec((tm, wdt), lambda i: (i, 0))
    kern = functools.partial(_outproj_kernel, counts=tuple(len(p) for p in ops), lat_tiles=lat_tiles, wa=wa, wb=wb)
    return pl.pallas_call(
        kern,
        grid=(r // tm,),
        in_specs=[s for p in ops for s in _row_specs(p, tm, lat_tiles)] + [
                  pl.BlockSpec((wa + wb + wc, d), lambda i: (0, 0)),
                  modspec(2),
                  pl.BlockSpec((1, d), lambda i: (0, 0)),
                  modspec(3), modspec(4),
                  pl.BlockSpec((2, d, LANES), lambda i: (0, 0, 0)),
                  pl.BlockSpec((1, LANES), lambda i: (0, 0))],
        out_specs=[row(d), row(d + LANES)],
        out_shape=[jax.ShapeDtypeStruct((r, d), F32),
                   jax.ShapeDtypeStruct((r, d + LANES), F32)],
        compiler_params=_cparams("parallel"),
        name="out_projection",
    )(*[a for p in ops for a in p], w_out_bf, mod3, layer_g2, mod3, mod3, w_route, b_route)


PAIRS_PER_GROUP = EXPERTS_PER_GROUP * (EXPERTS_PER_GROUP - 1) // 2
N_BUCKETS = N_GROUPS * PAIRS_PER_GROUP
ROW_TILE = 192
META_BUCKET, META_WA, META_WB = 0, 1, 2


def _bucket_experts():
    ea = np.zeros((LANES,), np.int32)
    eb = np.zeros((LANES,), np.int32)
    for g in range(N_GROUPS):
        k = g * PAIRS_PER_GROUP
        for a in range(EXPERTS_PER_GROUP):
            for b in range(a + 1, EXPERTS_PER_GROUP):
                ea[k], eb[k] = g * EXPERTS_PER_GROUP + a, g * EXPERTS_PER_GROUP + b
                k += 1
    return ea, eb


def _route_meta(logits):
    lane = lax.broadcasted_iota(jnp.int32, logits.shape, 1).astype(F32)
    is_g = lane < N_GROUPS
    gl = jnp.where(is_g, logits, -jnp.inf)
    gmax = jnp.max(gl, axis=-1, keepdims=True)
    g_idx = jnp.min(jnp.where(gl == gmax, lane, LANES), axis=-1, keepdims=True)
    gsum = jnp.sum(jnp.where(is_g, jnp.exp(gl - gmax), 0.0), axis=-1, keepdims=True)
    g_top = 1.0 / gsum
    lo = N_GROUPS + g_idx * EXPERTS_PER_GROUP
    in_grp = (lane >= lo) & (lane < lo + EXPERTS_PER_GROUP)
    el = jnp.where(in_grp, logits, -jnp.inf)
    m1 = jnp.max(el, axis=-1, keepdims=True)
    i1 = jnp.min(jnp.where(el == m1, lane, LANES), axis=-1, keepdims=True)
    el2 = jnp.where(lane == i1, -jnp.inf, el)
    m2 = jnp.max(el2, axis=-1, keepdims=True)
    i2 = jnp.min(jnp.where(el2 == m2, lane, LANES), axis=-1, keepdims=True)
    e21 = jnp.exp(m2 - m1)
    w1 = g_top / (1.0 + e21)
    w2 = e21 * w1
    first_low = i1 < i2
    la = jnp.minimum(i1, i2) - lo
    lb = jnp.maximum(i1, i2) - lo
    pair = la * (2 * EXPERTS_PER_GROUP - 1 - la) * 0.5 + (lb - la - 1.0)
    bucket = g_idx * PAIRS_PER_GROUP + pair
    wa = jnp.where(first_low, w1, w2)
    wb = jnp.where(first_low, w2, w1)
    return jnp.where(lane == META_BUCKET, bucket,
                     jnp.where(lane == META_WA, wa, jnp.where(lane == META_WB, wb, 0.0)))


def _plan_kernel(meta_ref, tri_ref, ids_ref, rank_ref, cnt_ref, carry):
    @pl.when(pl.program_id(0) == 0)
    def _():
        carry[...] = jnp.zeros_like(carry)

    ids = meta_ref[...].T[META_BUCKET:META_BUCKET + 1, :]
    sub = lax.broadcasted_iota(jnp.int32, (LANES, ids.shape[1]), 0).astype(F32)
    onehot = (sub == ids).astype(F32)
    before = jnp.dot(onehot.astype(BF16), tri_ref[...], preferred_element_type=F32)
    rank = jnp.sum(onehot * (before + carry[...]), axis=0, keepdims=True)
    ids_ref[0] = ids.astype(jnp.int32)
    rank_ref[0] = rank.astype(jnp.int32)
    total = carry[...] + jnp.sum(onehot, axis=1, keepdims=True)
    carry[...] = total
    cnt_ref[...] = total.astype(jnp.int32)


def route_plan(h2x, tm):
    r = h2x.shape[0]
    meta_blk = h2x.shape[1] // LANES - 1
    nt = r // tm
    i = np.arange(tm)
    tri = jnp.asarray(i[:, None] < i[None, :], BF16)
    ids, rank, cnt = pl.pallas_call(
        _plan_kernel,
        grid=(nt,),
        in_specs=[pl.BlockSpec((tm, LANES), lambda i: (i, meta_blk)),
                  pl.BlockSpec((tm, tm), lambda i: (0, 0))],
        out_specs=[pl.BlockSpec((1, 1, tm), lambda i: (i, 0, 0)),
                   pl.BlockSpec((1, 1, tm), lambda i: (i, 0, 0)),
                   pl.BlockSpec((LANES, 1), lambda i: (0, 0))],
        out_shape=[jax.ShapeDtypeStruct((nt, 1, tm), jnp.int32),
                   jax.ShapeDtypeStruct((nt, 1, tm), jnp.int32),
                   jax.ShapeDtypeStruct((LANES, 1), jnp.int32)],
        scratch_shapes=[pltpu.VMEM((LANES, 1), F32)],
        compiler_params=_cparams("arbitrary"),
        name="route_plan",
    )(h2x, tri)
    return ids.reshape(r), rank.reshape(r), cnt.reshape(LANES)


def _row_copy(src, dst, i, j, sem):
    return pltpu.make_async_copy(src.at[pl.ds(i, 1)], dst.at[pl.ds(j, 1)], sem)


def _dispatch_kernel(pos_ref, h_ref, init_hbm, o_hbm, sem):
    del init_hbm
    ch = h_ref.shape[0]
    base = pl.program_id(0) * ch

    def issue(i, c):
        _row_copy(h_ref, o_hbm, i, pos_ref[base + i], sem).start()
        return c

    def drain(i, c):
        _row_copy(h_ref, o_hbm, i, 0, sem).wait()
        return c

    lax.fori_loop(0, ch, issue, 0, unroll=8)
    lax.fori_loop(0, ch, drain, 0, unroll=8)


def dispatch_rows(pos, h2, n_rows, ch):
    r, d = h2.shape
    return pl.pallas_call(
        _dispatch_kernel,
        grid_spec=pltpu.PrefetchScalarGridSpec(
            num_scalar_prefetch=1,
            grid=(r // ch,),
            in_specs=[pl.BlockSpec((ch, d), lambda i, p: (i, 0)), pl.BlockSpec(memory_space=pl.ANY)],
            out_specs=pl.BlockSpec(memory_space=pl.ANY),
            scratch_shapes=[pltpu.SemaphoreType.DMA],
        ),
        out_shape=jax.ShapeDtypeStruct((n_rows, d), h2.dtype),
        input_output_aliases={2: 0},
        compiler_params=_cparams("arbitrary"),
        name="dispatch_rows",
    )(pos, h2, jnp.zeros((n_rows, d), h2.dtype))


def _expert_kernel(ta_ref, tb_ref, nu_ref, hs_ref, wg_ref, wu_ref, wd_ref, y_ref, gu_s, dn_s):
    j = pl.program_id(0)
    d = wg_ref.shape[2]
    ff = wg_ref.shape[3]
    epg = wg_ref.shape[1]
    prev = jnp.maximum(j - 1, 0)

    @pl.when((j == 0) | (ta_ref[j] // epg != ta_ref[prev] // epg))
    def _():
        for e in range(epg):
            gu_s[e, :, :ff] = wg_ref[0, e].astype(BF16)
            gu_s[e, :, ff:] = wu_ref[0, e].astype(BF16)
            dn_s[e] = wd_ref[0, e].astype(BF16)

    @pl.when(j < nu_ref[0])
    def _():
        h = hs_ref[:, :d].astype(BF16)
        meta = hs_ref[:, d:]
        es = [t_ref[j] % epg for t_ref in (ta_ref, tb_ref)]
        hgus = [jnp.dot(h, gu_s[e], preferred_element_type=F32) for e in es]
        y = None
        for e, hgu, lane in zip(es, hgus, (META_WA, META_WB)):
            hg = hgu[:, :ff]
            hid = (hg * jax.nn.sigmoid(hg)) * hgu[:, ff:]
            part = meta[:, lane:lane + 1] * jnp.dot(hid.astype(BF16), dn_s[e], preferred_element_type=F32)
            y = part if y is None else y + part
        y_ref[:, :d] = y
        y_ref[:, d:] = meta


def expert_pairs(tile_a, tile_b, n_used, hs, w_gate, w_up, w_down, layer):
    rows, dx = hs.shape
    depth, ne, d, ff = w_gate.shape
    epg = EXPERTS_PER_GROUP
    ng = ne // epg
    nt = rows // ROW_TILE
    blk = lambda i, ta, tb, nu: (jnp.minimum(i, nu[0] - 1), 0)
    grp = lambda shape: pl.BlockSpec((1, epg) + shape, lambda i, ta, tb, nu: (layer * ng + ta[i] // epg, 0, 0, 0),
                                     pipeline_mode=pl.Buffered(1))
    by_group = lambda w: w.reshape((depth * ng, epg) + w.shape[2:])
    return pl.pallas_call(
        _expert_kernel,
        grid_spec=pltpu.PrefetchScalarGridSpec(
            num_scalar_prefetch=3,
            grid=(nt,),
            in_specs=[pl.BlockSpec((ROW_TILE, dx), blk), grp((d, ff)), grp((d, ff)), grp((ff, d))],
            out_specs=pl.BlockSpec((ROW_TILE, dx), blk),
            scratch_shapes=[pltpu.VMEM((epg, d, 2 * ff), BF16), pltpu.VMEM((epg, ff, d), BF16)],
        ),
        out_shape=jax.ShapeDtypeStruct((rows, dx), F32),
        input_output_aliases={3: 0},
        compiler_params=_cparams("arbitrary"),
        name="expert_pairs",
    )(tile_a, tile_b, n_used, hs, by_group(w_gate), by_group(w_up), by_group(w_down))


def _combine_kernel(pos_ref, x_ref, gate_ref, y_hbm, o_ref, ybuf, sems):
    tm = x_ref.shape[0]
    d = x_ref.shape[1]
    i = pl.program_id(0)
    slot = i % 2

    def gather(tile, into):
        def issue(r, c):
            _row_copy(y_hbm, ybuf.at[into], pos_ref[tile * tm + r], r, sems.at[into]).start()
            return c
        lax.fori_loop(0, tm, issue, 0, unroll=8)

    @pl.when(i == 0)
    def _():
        gather(0, 0)

    @pl.when(i + 1 < pl.num_programs(0))
    def _():
        gather(i + 1, 1 - slot)

    def drain(r, c):
        _row_copy(y_hbm, ybuf.at[slot], 0, r, sems.at[slot]).wait()
        return c

    lax.fori_loop(0, tm, drain, 0, unroll=8)
    o_ref[...] = x_ref[...] + gate_ref[0] * ybuf[slot, :, :d]


def combine_rows(pos, x_mid, mod3, y, *, seq, nseg, tm):
    r, d = x_mid.shape
    tiles_per_seq = seq // tm
    seg = lambda i: jnp.minimum(i // tiles_per_seq, nseg - 1)
    return pl.pallas_call(
        _combine_kernel,
        grid_spec=pltpu.PrefetchScalarGridSpec(
            num_scalar_prefetch=1,
            grid=(r // tm,),
            in_specs=[pl.BlockSpec((tm, d), lambda i, p: (i, 0)),
                      pl.BlockSpec((1, 1, d), lambda i, p: (seg(i) * N_MOD + 5, 0, 0)),
                      pl.BlockSpec(memory_space=pl.ANY)],
            out_specs=pl.BlockSpec((tm, d), lambda i, p: (i, 0)),
            scratch_shapes=[pltpu.VMEM((2, tm, y.shape[1]), F32), pltpu.SemaphoreType.DMA((2,))],
        ),
        out_shape=jax.ShapeDtypeStruct((r, d), F32),
        compiler_params=_cparams("arbitrary"),
        name="combine_rows",
    )(pos, x_mid, mod3, y)


def moe_block(x_mid, h2, mod3, w_gate, w_up, w_down, *, layer, seq, nseg):
    r, d = x_mid.shape
    ids, rank, cnt = route_plan(h2, 512)
    tiles = (cnt + (ROW_TILE - 1)) // ROW_TILE
    incl = jnp.cumsum(tiles)
    n_used = incl[-1]
    lookup = lambda table, idx: jnp.sum(jnp.where(idx[:, None] == jnp.arange(LANES)[None, :], table[None, :], 0), axis=1)
    pos = lookup(incl - tiles, ids) * ROW_TILE + rank
    nt = r // ROW_TILE + N_BUCKETS
    last = jnp.minimum(jnp.arange(nt), n_used - 1)
    tile_bucket = jnp.sum((incl[None, :] <= last[:, None]).astype(jnp.int32), axis=1)
    ea, eb = _bucket_experts()
    tile_a = lookup(jnp.asarray(ea), tile_bucket)
    tile_b = lookup(jnp.asarray(eb), tile_bucket)
    hs = dispatch_rows(pos, h2, nt * ROW_TILE, 512)
    y = expert_pairs(tile_a, tile_b, n_used.reshape(1).astype(jnp.int32), hs, w_gate, w_up, w_down, layer)
    return combine_rows(pos, x_mid, mod3, y, seq=seq, nseg=nseg, tm=256)


def _rope_tables(seq, tm):
    t = np.arange(seq)
    row = (t // GRID_W).astype(np.float32)
    col = (t % GRID_W).astype(np.float32)
    half = HEAD_DIM // 2
    inv = jnp.asarray(ROPE_THETA, F32) ** (-jnp.arange(0, half, 2, dtype=F32) / half)
    ang = jnp.concatenate([jnp.asarray(row)[:, None] * inv, jnp.asarray(col)[:, None] * inv], axis=-1)
    cos = jnp.repeat(jnp.cos(ang), 2, axis=-1)
    sin = jnp.repeat(jnp.sin(ang), 2, axis=-1) * jnp.asarray(np.tile([-1.0, 1.0], half), F32)
    cos = jnp.tile(cos, (1, LANES // HEAD_DIM))
    sin = jnp.tile(sin, (1, LANES // HEAD_DIM))
    cos = jnp.concatenate([cos, jnp.ones((tm, LANES), F32)], axis=0)
    sin = jnp.concatenate([sin, jnp.zeros((tm, LANES), F32)], axis=0)
    return cos, sin


def kernel(x, c, ctx, c_ctx, w_ada, b_ada, norm1_g, w_in, na_q_norm, na_k_norm, na_rpb, gqa_q_norm, gqa_k_norm, hgrn_lb, hgrn_o_norm, w_out, norm2_g, w_route_group, b_route_group, w_route_expert, b_route_expert, w_exp_gate, w_exp_up, w_exp_down):
    b, s, d = x.shape
    l = ctx.shape[1]
    depth = w_ada.shape[0]
    assert s % 512 == 0 and (b * l) % 512 == 0 and s // GRID_W >= NA_WIN_R
    assert s % l == 0 and l % HG_BLOCK == 0
    nseg = b + 1
    n_lat = b * s
    n_ctx = b * l
    na_w, gq_qw = d // 4, d // 2
    gq_kw = gq_qw // 4
    hg_w = d // 4
    tm = 512

    c_all = jnp.zeros((16, d), F32).at[:b].set(c).at[b].set(c_ctx)
    mod = ada_mod(c_all, w_ada, b_ada)
    cos_t, sin_t = _rope_tables(s, tm)
    bd_f = _block_diag_ones(LANES, HEAD_DIM, F32)
    bd_b = _block_diag_ones(LANES, HEAD_DIM, BF16)
    p_lb = jax.nn.softmax(hgrn_lb.astype(F32), axis=0)
    lb_all = jnp.cumsum(p_lb, axis=0) - p_lb[0]
    tile2 = lambda g: jnp.tile(g, LANES // HEAD_DIM)

    xall = (x.reshape(n_lat, d), ctx.reshape(n_ctx, d))
    for layer in range(depth):
        ctx_out = layer < depth - 1
        mod3 = mod[layer].reshape(16 * N_MOD, 1, d)
        gains = jnp.zeros((8, LANES), F32)
        gains = gains.at[0].set(tile2(na_q_norm[layer])).at[1].set(tile2(na_k_norm[layer]))
        gains = gains.at[2].set(tile2(gqa_q_norm[layer])).at[3].set(tile2(gqa_k_norm[layer]))
        pa, pb, pc = in_projection(xall, mod3, norm1_g[layer][None], w_in[layer].astype(BF16), cos_t, sin_t,
                                   gains, bd_b, n_lat_rows=n_lat, seq=s, nseg=nseg, tm=tm)
        bias_tab = na_bias_table(na_rpb[layer], s // GRID_W)
        o_a = na_attention(pa, bias_tab, b=b, s=s, l=l, na_w=na_w)
        o_b = gqa_attention(pb, b=b, s=s, l=l, qw=gq_qw, kw=gq_kw, tq=256)
        lb4 = lb_all[layer].reshape(2, hg_w // LANES, 1, LANES)
        y_lat, y_ctx = hgrn_mixer(pc, lb4, tile2(hgrn_o_norm[layer])[None], bd_b, bd_f, b=b, s=s, l=l)

        w_route = jnp.zeros((d, LANES), F32).at[:, :N_GROUPS].set(w_route_group[layer])
        w_route = w_route.at[:, N_GROUPS:N_GROUPS + N_EXPERTS].set(w_route_expert[layer])
        b_route = jnp.zeros((1, LANES), F32).at[0, :N_GROUPS].set(b_route_group[layer])
        b_route = b_route.at[0, N_GROUPS:N_GROUPS + N_EXPERTS].set(b_route_expert[layer])
        if ctx_out:
            o_ac, o_bc = ctx_attention(pa, pb, b=b, s=s, l=l, na_w=na_w, qw=gq_qw, kw=gq_kw)
            mix_a, mix_b, y_c = (o_a, o_ac), (o_b, o_bc), (y_lat, y_ctx)
            n_rows = n_lat + n_ctx
        else:
            mix_a, mix_b, y_c, n_rows = o_a, o_b, y_lat, n_lat
        x_mid, h2 = out_projection(xall, mix_a, mix_b, y_c, w_out[layer].astype(BF16), mod3,
                                    norm2_g[layer][None], w_route, b_route, n_rows=n_rows, seq=s,
                                    nseg=nseg, tm=tm)
        xall = moe_block(x_mid, h2, mod3, w_exp_gate, w_exp_up, w_exp_down, layer=layer, seq=s, nseg=nseg)
    return xall[:n_lat].reshape(b, s, d)
```

```python
import functools

import jax
import jax.numpy as jnp
import numpy as np
from jax import lax
from jax.experimental import pallas as pl
from jax.experimental.pallas import tpu as pltpu

F32 = jnp.float32
BF16 = jnp.bfloat16
HIGHEST = lax.Precision.HIGHEST

HEAD_DIM = 64
GRID_W = 64
NA_WIN_R = 8
NA_WIN_C = 16
ROPE_THETA = 10000.0
HGRN_CHUNK = 16
N_GROUPS = 4
EXPERTS_PER_GROUP = 8
N_EXPERTS = N_GROUPS * EXPERTS_PER_GROUP
N_MOD = 6
EPS = 1e-6
NEG_INF = -1e30
LB_FLOOR = 1e-20
LANES = 128
VMEM_LIMIT = 56 * 1024 * 1024


def _cparams(*sem):
    return pltpu.CompilerParams(dimension_semantics=sem, vmem_limit_bytes=VMEM_LIMIT)


def _block_diag_ones(n, blk, dtype):
    i = np.arange(n)
    return jnp.asarray((i[:, None] // blk) == (i[None, :] // blk), dtype=dtype)


def _ada_kernel(c_ref, w_ref, b_ref, o_ref):
    c = c_ref[...]
    s = c * jax.nn.sigmoid(c)
    o_ref[0] = jnp.dot(s, w_ref[0], precision=HIGHEST, preferred_element_type=F32) + b_ref[0]


def ada_mod(c_all, w_ada, b_ada):
    depth, d, n = w_ada.shape
    tn = 1536
    return pl.pallas_call(
        _ada_kernel,
        grid=(depth, n // tn),
        in_specs=[
            pl.BlockSpec((16, d), lambda l, j: (0, 0)),
            pl.BlockSpec((1, d, tn), lambda l, j: (l, 0, j)),
            pl.BlockSpec((1, 1, tn), lambda l, j: (l, 0, j)),
        ],
        out_specs=pl.BlockSpec((1, 16, tn), lambda l, j: (l, 0, j)),
        out_shape=jax.ShapeDtypeStruct((depth, 16, n), F32),
        compiler_params=_cparams("parallel", "parallel"),
        name="ada_mod",
    )(c_all, w_ada, b_ada.reshape(depth, 1, n))


def _seg_inv_rms(x, bd):
    xs = x * x
    hi = xs.astype(BF16)
    lo = (xs - hi.astype(F32)).astype(BF16)
    ss = jnp.dot(hi, bd, preferred_element_type=F32) + jnp.dot(lo, bd, preferred_element_type=F32)
    return lax.rsqrt(ss * (1.0 / HEAD_DIM) + EPS)


def _pair_swap(x):
    lane = lax.broadcasted_iota(jnp.int32, x.shape, 1)
    return jnp.where((lane & 1) == 0, pltpu.roll(x, LANES - 1, 1), pltpu.roll(x, 1, 1))


def _as_parts(a):
    return tuple(a) if isinstance(a, (tuple, list)) else (a,)


def _row_specs(parts, tm, lat_tiles):
    w = parts[0].shape[1]
    if len(parts) == 1:
        return [pl.BlockSpec((tm, w), lambda i: (i, 0))]
    return [pl.BlockSpec((tm, w), lambda i: (jnp.minimum(i, lat_tiles - 1), 0)),
            pl.BlockSpec((tm, w), lambda i: (jnp.maximum(i - lat_tiles, 0), 0))]


def _row_tile(refs, lat_tiles):
    if len(refs) == 1:
        return refs[0][...]
    return jnp.where(pl.program_id(0) < lat_tiles, refs[0][...], refs[1][...])


def _inproj_kernel(*refs, nx, lat_tiles, na_w, gq_qw, gq_kw):
    x_refs, refs = refs[:nx], refs[nx:]
    g1_ref, shift_ref, scale_ref, w_ref, cos_ref, sin_ref, gains_ref, bd_ref, oa_ref, ob_ref, oc_ref = refs
    x = _row_tile(x_refs, lat_tiles)
    tm = x.shape[0]
    bd = bd_ref[...]
    qscale = HEAD_DIM ** -0.5
    b0 = 3 * na_w
    halves = [slice(0, tm // 2), slice(tm // 2, tm)]
    hs = []
    for rows in halves:
        xh = x[rows]
        ms = jnp.mean(xh * xh, axis=-1, keepdims=True)
        h = xh * lax.rsqrt(ms + EPS) * g1_ref[0]
        hs.append((h * (1.0 + scale_ref[0]) + shift_ref[0]).astype(BF16))
    ps = [jnp.dot(h, w_ref[...], preferred_element_type=F32) for h in hs]
    for rows, p in zip(halves, ps):
        cos = cos_ref[rows, :]
        sin = sin_ref[rows, :]

        def normed(col, gain_row):
            xb = p[:, col:col + LANES]
            return xb * _seg_inv_rms(xb, bd) * gains_ref[gain_row:gain_row + 1, :]

        def rope(xn):
            return xn * cos + _pair_swap(xn) * sin

        for j in range(na_w // LANES):
            c = j * LANES
            oa_ref[rows, c:c + LANES] = (normed(c, 0) * qscale).astype(BF16)
            oa_ref[rows, na_w + c:na_w + c + LANES] = normed(na_w + c, 1).astype(BF16)
        oa_ref[rows, 2 * na_w:3 * na_w] = p[:, 2 * na_w:3 * na_w].astype(BF16)
        for j in range(gq_qw // LANES):
            c = j * LANES
            ob_ref[rows, c:c + LANES] = (rope(normed(b0 + c, 2)) * qscale).astype(BF16)
        for j in range(gq_kw // LANES):
            c = gq_qw + j * LANES
            ob_ref[rows, c:c + LANES] = rope(normed(b0 + c, 3)).astype(BF16)
        ob_ref[rows, gq_qw + gq_kw:] = p[:, b0 + gq_qw + gq_kw:b0 + gq_qw + 2 * gq_kw].astype(BF16)
        oc_ref[rows, :] = p[:, b0 + gq_qw + 2 * gq_kw:]


def in_projection(xall, mod3, layer_g1, w_in_bf, cos_t, sin_t, gains, bd, *, n_lat_rows, seq, nseg, tm):
    x_parts = _as_parts(xall)
    r, d = sum(a.shape[0] for a in x_parts), x_parts[0].shape[1]
    d_in = w_in_bf.shape[1]
    na_w = d // 4
    gq_qw = d // 2
    gq_kw = gq_qw // 4
    c_w = d_in - 3 * na_w - gq_qw - 2 * gq_kw
    lat_tiles = n_lat_rows // tm
    tiles_per_seq = seq // tm

    def seg(i):
        return jnp.minimum(i // tiles_per_seq, nseg - 1)

    def rope_blk(i):
        return jnp.where(i < lat_tiles, i % tiles_per_seq, tiles_per_seq)

    kern = functools.partial(_inproj_kernel, nx=len(x_parts), lat_tiles=lat_tiles, na_w=na_w, gq_qw=gq_qw,
                             gq_kw=gq_kw)
    return pl.pallas_call(
        kern,
        grid=(r // tm,),
        in_specs=_row_specs(x_parts, tm, lat_tiles) + [
            pl.BlockSpec((1, d), lambda i: (0, 0)),
            pl.BlockSpec((1, 1, d), lambda i: (seg(i) * N_MOD + 0, 0, 0)),
            pl.BlockSpec((1, 1, d), lambda i: (seg(i) * N_MOD + 1, 0, 0)),
            pl.BlockSpec((d, d_in), lambda i: (0, 0)),
            pl.BlockSpec((tm, LANES), lambda i: (rope_blk(i), 0)),
            pl.BlockSpec((tm, LANES), lambda i: (rope_blk(i), 0)),
            pl.BlockSpec((8, LANES), lambda i: (0, 0)),
            pl.BlockSpec((LANES, LANES), lambda i: (0, 0)),
        ],
        out_specs=[
            pl.BlockSpec((tm, 3 * na_w), lambda i: (i, 0)),
            pl.BlockSpec((tm, gq_qw + 2 * gq_kw), lambda i: (i, 0)),
            pl.BlockSpec((tm, c_w), lambda i: (i, 0)),
        ],
        out_shape=[
            jax.ShapeDtypeStruct((r, 3 * na_w), BF16),
            jax.ShapeDtypeStruct((r, gq_qw + 2 * gq_kw), BF16),
            jax.ShapeDtypeStruct((r, c_w), F32),
        ],
        compiler_params=_cparams("parallel"),
        name="in_projection",
    )(*x_parts, layer_g1, mod3, mod3, w_in_bf, cos_t, sin_t, gains, bd)


def _head(j):
    return slice(j * HEAD_DIM, (j + 1) * HEAD_DIM)


def _with_ones(v):
    return jnp.concatenate([v, jnp.ones_like(v)], axis=1)


def _scores(q, k):
    return lax.dot_general(q, k, (((1,), (1,)), ((), ())), preferred_element_type=F32)


def _normalise(o):
    return o[:, :HEAD_DIM] / o[:, HEAD_DIM:HEAD_DIM + 1]


def _softmax_attend_all(qs, ks, v1s):
    ss = [_scores(q, k) for q, k in zip(qs, ks)]
    outs = []
    for s, v1 in zip(ss, v1s):
        p = jnp.exp((s - jnp.max(s, axis=-1, keepdims=True)).astype(BF16))
        outs.append(_normalise(jnp.dot(p, v1, preferred_element_type=F32)))
    return outs


def _grouped_attend(q_ref, k_of, v_of, n_kv, grp, splits=1):
    t = q_ref.shape[0] // splits
    qs, ks, vs = [], [], []
    for j in range(n_kv):
        for h in range(splits):
            qs.append(jnp.concatenate([q_ref[h * t:(h + 1) * t, _head(j * grp + g)] for g in range(grp)], axis=0))
            ks.append(k_of(j))
            vs.append(v_of(j))
    os_ = _softmax_attend_all(qs, ks, vs)
    rows = []
    for h in range(splits):
        cols = []
        for j in range(n_kv):
            o = os_[j * splits + h]
            cols += [o[g * t:(g + 1) * t] for g in range(grp)]
        rows.append(jnp.concatenate(cols, axis=1))
    return jnp.concatenate(rows, axis=0) if splits > 1 else rows[0]


def _gqa_kernel(q_ref, kl_ref, vl_ref, kc_ref, vc_ref, o_ref, k_s, v_s, *, n_kv, grp):
    s_len = kl_ref.shape[0]

    @pl.when(pl.program_id(1) == 0)
    def _():
        for j in range(n_kv):
            k_s[j, :s_len, :] = kl_ref[:, _head(j)]
            k_s[j, s_len:, :] = kc_ref[:, _head(j)]
            v_s[j, :s_len, :] = _with_ones(vl_ref[:, _head(j)])
            v_s[j, s_len:, :] = _with_ones(vc_ref[:, _head(j)])

    o = _grouped_attend(q_ref, lambda j: k_s[j], lambda j: v_s[j], n_kv, grp, splits=2)
    o_ref[...] = o.astype(o_ref.dtype)


def gqa_attention(pb, *, b, s, l, qw, kw, tq):
    n_lat = b * s
    n_kv = kw // HEAD_DIM
    grp = qw // kw
    assert kw == LANES and qw % kw == 0
    kcol, vcol = qw // kw, qw // kw + 1
    kern = functools.partial(_gqa_kernel, n_kv=n_kv, grp=grp)
    return pl.pallas_call(
        kern,
        grid=(b, s // tq),
        in_specs=[
            pl.BlockSpec((tq, qw), lambda i, j: (i * (s // tq) + j, 0)),
            pl.BlockSpec((s, kw), lambda i, j: (i, kcol)),
            pl.BlockSpec((s, kw), lambda i, j: (i, vcol)),
            pl.BlockSpec((l, kw), lambda i, j: (n_lat // l + i, kcol)),
            pl.BlockSpec((l, kw), lambda i, j: (n_lat // l + i, vcol)),
        ],
        out_specs=pl.BlockSpec((tq, qw), lambda i, j: (i * (s // tq) + j, 0)),
        out_shape=jax.ShapeDtypeStruct((n_lat, qw), BF16),
        scratch_shapes=[pltpu.VMEM((n_kv, s + l, HEAD_DIM), BF16), pltpu.VMEM((n_kv, s + l, 2 * HEAD_DIM), BF16)],
        compiler_params=_cparams("parallel", "arbitrary"),
        name="gqa_attention",
    )(pb, pb, pb, pb, pb)


def _ctx_attn_kernel(qa_ref, ka_ref, va_ref, qb_ref, kb_ref, vb_ref, oa_ref, ob_ref, *, n_kv, grp):
    na_h = qa_ref.shape[1] // HEAD_DIM
    oa = _softmax_attend_all([qa_ref[:, _head(h)] for h in range(na_h)],
                             [ka_ref[:, _head(h)] for h in range(na_h)],
                             [_with_ones(va_ref[:, _head(h)]) for h in range(na_h)])
    oa_ref[...] = jnp.concatenate(oa, axis=1).astype(oa_ref.dtype)
    ob = _grouped_attend(qb_ref, lambda j: kb_ref[:, _head(j)], lambda j: _with_ones(vb_ref[:, _head(j)]),
                         n_kv, grp)
    ob_ref[...] = ob.astype(ob_ref.dtype)


def ctx_attention(pa, pb, *, b, s, l, na_w, qw, kw):
    r0 = (b * s) // l
    grp = qw // kw
    kern = functools.partial(_ctx_attn_kernel, n_kv=kw // HEAD_DIM, grp=grp)
    a_spec = lambda m: pl.BlockSpec((l, na_w), lambda i: (r0 + i, m))
    return pl.pallas_call(
        kern,
        grid=(b,),
        in_specs=[a_spec(0), a_spec(1), a_spec(2),
                  pl.BlockSpec((l, qw), lambda i: (r0 + i, 0)),
                  pl.BlockSpec((l, kw), lambda i: (r0 + i, grp)),
                  pl.BlockSpec((l, kw), lambda i: (r0 + i, grp + 1))],
        out_specs=[pl.BlockSpec((l, na_w), lambda i: (i, 0)), pl.BlockSpec((l, qw), lambda i: (i, 0))],
        out_shape=[jax.ShapeDtypeStruct((b * l, na_w), BF16), jax.ShapeDtypeStruct((b * l, qw), BF16)],
        compiler_params=_cparams("parallel"),
        name="ctx_attention",
    )(pa, pa, pa, pb, pb, pb)


NA_QROWS = 4
NA_UNION = NA_WIN_R + NA_QROWS


def _na_block_geometry(rows):
    wu = min(rows, NA_UNION)
    wr = min(NA_WIN_R, rows)
    nblk = rows // NA_QROWS
    sig, u0s = [], []
    for blk in range(nblk):
        r0 = blk * NA_QROWS
        u0 = int(np.clip(r0 - wr // 2, 0, rows - wu))
        u0s.append(u0)
        sig.append(tuple((r0 + j - u0, int(np.clip(r0 + j - wr // 2, 0, rows - wr)) - u0) for j in range(NA_QROWS)))
    cls = [int(blk > 0) + int(blk == nblk - 1) for blk in range(nblk)]
    reps = {}
    for blk in range(nblk):
        assert reps.setdefault(cls[blk], sig[blk]) == sig[blk]
    return wu, wr, [reps.get(c, reps[0]) for c in range(3)]


def _na_kernel(q_ref, k_ref, v_ref, kc_ref, vc_ref, bias_ref, o_ref, k_s, v_s, kc_s, vc_s, *, rows, wu, wr):
    nh = q_ref.shape[1] // HEAD_DIM
    nblk = rows // NA_QROWS
    nq = NA_QROWS * GRID_W
    for h in range(nh):
        k_s[h] = k_ref[:, _head(h)]
        v_s[h] = _with_ones(v_ref[:, _head(h)])
        kc_s[h] = kc_ref[:, _head(h)]
        vc_s[h] = _with_ones(vc_ref[:, _head(h)])

    def body(blk, carry):
        u0 = jnp.clip(blk * NA_QROWS - wr // 2, 0, rows - wu)
        cls = jnp.minimum(blk, 1) + jnp.maximum(blk - (nblk - 2), 0)
        q_rows = pl.ds(pl.multiple_of(blk * nq, nq), nq)
        k_rows = pl.ds(pl.multiple_of(u0 * GRID_W, GRID_W), wu * GRID_W)
        outs = []
        for h in range(nh):
            q = q_ref[q_rows, _head(h)]
            s_nb = _scores(q, k_s[h, k_rows, :]) + bias_ref[h, cls]
            s_cx = _scores(q, kc_s[h])
            m = jnp.maximum(jnp.max(s_nb, axis=-1, keepdims=True), jnp.max(s_cx, axis=-1, keepdims=True))
            p_nb = jnp.exp((s_nb - m).astype(BF16))
            p_cx = jnp.exp((s_cx - m).astype(BF16))
            outs.append(_normalise(jnp.dot(p_nb, v_s[h, k_rows, :], preferred_element_type=F32)
                                   + jnp.dot(p_cx, vc_s[h], preferred_element_type=F32)))
        o_ref[q_rows, :] = jnp.concatenate(outs, axis=1).astype(o_ref.dtype)
        return carry

    lax.fori_loop(0, nblk, body, 0, unroll=2)


def na_attention(pa, bias_tab, *, b, s, l, na_w):
    n_lat = b * s
    nh = na_w // HEAD_DIM
    rows = s // GRID_W
    wu, wr, _ = _na_block_geometry(rows)
    kern = functools.partial(_na_kernel, rows=rows, wu=wu, wr=wr)
    lat = lambda m: pl.BlockSpec((s, na_w), lambda i: (i, m))
    cx = lambda m: pl.BlockSpec((l, na_w), lambda i: (n_lat // l + i, m))
    return pl.pallas_call(
        kern,
        grid=(b,),
        in_specs=[lat(0), lat(1), lat(2), cx(1), cx(2),
                  pl.BlockSpec(bias_tab.shape, lambda i: (0, 0, 0, 0))],
        out_specs=pl.BlockSpec((s, na_w), lambda i: (i, 0)),
        out_shape=jax.ShapeDtypeStruct((n_lat, na_w), BF16),
        scratch_shapes=[pltpu.VMEM((nh, s, HEAD_DIM), BF16), pltpu.VMEM((nh, s, 2 * HEAD_DIM), BF16),
                        pltpu.VMEM((nh, l, HEAD_DIM), BF16), pltpu.VMEM((nh, l, 2 * HEAD_DIM), BF16)],
        compiler_params=_cparams("parallel"),
        name="na_attention",
    )(pa, pa, pa, pa, pa, bias_tab)


def na_bias_table(rpb, rows):
    wu, wr, reps = _na_block_geometry(rows)
    h = rpb.shape[0]
    r_off = np.array([[rj for rj, _ in rep] for rep in reps])
    s_off = np.array([[sj for _, sj in rep] for rep in reps])
    kr = np.arange(wu)
    row_ok = (kr >= s_off[..., None]) & (kr < s_off[..., None] + wr)
    dr = kr - r_off[..., None] + (NA_WIN_R - 1)
    cidx = np.arange(GRID_W)
    col_start = np.clip(cidx - NA_WIN_C // 2, 0, GRID_W - NA_WIN_C)
    col_ok = (cidx[None, :] >= col_start[:, None]) & (cidx[None, :] < col_start[:, None] + NA_WIN_C)
    dc = np.clip(cidx[None, :] - cidx[:, None] + (NA_WIN_C - 1), 0, 2 * NA_WIN_C - 2)
    n_dr, n_dc = 2 * NA_WIN_R - 1, 2 * NA_WIN_C - 1
    sel_r = jnp.asarray(np.where(row_ok, dr, n_dr)[..., None] == np.arange(n_dr + 1), F32)
    sel_c = jnp.asarray(np.where(col_ok, dc, n_dc)[..., None] == np.arange(n_dc + 1), F32)
    rpb_x = jnp.pad(rpb.astype(F32), ((0, 0), (0, 1), (0, 1)), constant_values=NEG_INF)
    bias = jnp.einsum("cjki,hid->hcjkd", sel_r, rpb_x, precision=HIGHEST)
    bias = jnp.einsum("hcjkd,qxd->hcjqkx", bias, sel_c, precision=HIGHEST)
    return bias.reshape(h, len(reps), NA_QROWS * GRID_W, wu * GRID_W)


HG_BLOCK = 128


def _hgrn_block(q_ref, v_ref, z_ref, o_acc, bc_ref, lb, bdb, bdf, st, blk, *, reverse):
    c = HGRN_CHUNK
    hc = c // 2
    ncb = HG_BLOCK // c
    lbm = jnp.maximum(lb, LB_FLOOR)
    one_m_lb = 1.0 - lb
    scale = HEAD_DIM ** -0.5
    t_idx = lax.broadcasted_iota(jnp.int32, (2 * ncb, hc, LANES), 1)
    edge = 0 if reverse else c - 1
    early, late = (1, 0) if reverse else (0, 1)

    def bs(x, s):
        return jnp.broadcast_to(x[:, s:s + 1, :], x.shape)

    def seen(s):
        return (t_idx <= s) if reverse else (t_idx >= s)

    def halves(x):
        x4 = x.reshape(ncb, 2, hc, LANES)
        return x4[:, early], x4[:, late]

    def chunks(xe, xl):
        parts = [xl, xe] if reverse else [xe, xl]
        return jnp.stack(parts, axis=1).reshape(ncb, c, LANES)

    def seg_sum(w):
        rows = w.shape[0] * w.shape[1]
        return jnp.dot(w.reshape(rows, LANES).astype(BF16), bdb, preferred_element_type=F32).reshape(w.shape)

    if True:
        r0 = pl.multiple_of(blk * HG_BLOCK, HG_BLOCK)
        z = z_ref[pl.ds(r0, HG_BLOCK), :]
        q = q_ref[pl.ds(r0, HG_BLOCK), :] * scale
        v = v_ref[pl.ds(r0, HG_BLOCK), :]
        sig = jax.nn.sigmoid(z)
        f = one_m_lb * sig + lbm
        k = one_m_lb * (1.0 - sig) - (lbm - lb)
        logf = jnp.log2(f)
        pre = logf.reshape(2 * ncb, hc, LANES)
        for sh in (1, 2, 4):
            if reverse:
                pre = pre + jnp.where(t_idx < hc - sh, pltpu.roll(pre, hc - sh, 1), 0.0)
            else:
                pre = pre + jnp.where(t_idx >= sh, pltpu.roll(pre, sh, 1), 0.0)
        pre_e, pre_l = halves(pre.reshape(HG_BLOCK, LANES))
        cum_e = pre_e
        cum_l = pre_l + bs(pre_e, 0 if reverse else hc - 1)
        q8, k8, v8 = (a.reshape(2 * ncb, hc, LANES) for a in (q, k, v))
        for slot, a in enumerate((pre, k8, v8)):
            bc_ref[slot] = a

        def row(slot, s, early_only=False):
            lead = pl.ds(early, ncb, stride=2) if early_only else slice(None)
            return jnp.broadcast_to(bc_ref[slot, lead, s:s + 1, :], (ncb if early_only else 2 * ncb, hc, LANES))

        o8 = jnp.zeros_like(pre)
        for s in range(hc):
            d = jnp.where(seen(s), pre - row(0, s), NEG_INF)
            o8 = o8 + seg_sum(q8 * row(1, s) * jnp.exp2(d)) * row(2, s)
        q_l = halves(q)[1]
        o_e, o_l = halves(o8.reshape(HG_BLOCK, LANES))
        for s in range(hc):
            o_l = o_l + seg_sum(q_l * row(1, s, True) * jnp.exp2(cum_l - row(0, s, True))) * row(2, s, True)
        cum = chunks(cum_e, cum_l)
        o3 = chunks(o_e, o_l)
        q3 = q.reshape(ncb, c, LANES)
        k3 = k.reshape(ncb, c, LANES)
        v3 = v.reshape(ncb, c, LANES)
        cum_edge = bs(cum, edge)
        qe = (q3 * jnp.exp2(cum)).astype(BF16)
        kd = (k3 * jnp.exp2(cum_edge - cum)).astype(BF16)
        vb = v3.astype(BF16)
        chunk_decay = jnp.exp2(cum_edge)
        yield None
        u_t = [lax.dot_general(vb[n], kd[n], (((0,), (0,)), ((), ())), preferred_element_type=F32) * bdf
               for n in range(ncb)]
        yield None
        enter = [None] * ncb
        for n in (range(ncb - 1, -1, -1) if reverse else range(ncb)):
            enter[n] = st.astype(BF16)
            st = chunk_decay[n, 0:1, :] * st + u_t[n]
        yield None
        outs = [o3[n] + lax.dot_general(qe[n], enter[n], (((1,), (1,)), ((), ())), preferred_element_type=F32)
                for n in range(ncb)]
        o_acc[pl.ds(r0, HG_BLOCK), :] = jnp.concatenate(outs, axis=0)
        yield st


def _hgrn_segment(q_ref, v_ref, zf_ref, zb_ref, of_acc, ob_acc, bc_ref, lb_ref, bdb, bdf, st_f, st_b):
    nblk = q_ref.shape[0] // HG_BLOCK

    def body(i, carry):
        st_f, st_b = carry
        scans = [_hgrn_block(q_ref, v_ref, zf_ref, of_acc, bc_ref.at[0], lb_ref[0, 0], bdb, bdf, st_f, i,
                             reverse=False),
                 _hgrn_block(q_ref, v_ref, zb_ref, ob_acc, bc_ref.at[1], lb_ref[1, 0], bdb, bdf, st_b,
                             nblk - 1 - i, reverse=True)]
        for _ in range(3):
            for g in scans:
                next(g)
        return tuple(next(g) for g in scans)

    return lax.fori_loop(0, nblk, body, (st_f, st_b))


def _hgrn_kernel(ql_ref, qc_ref, vl_ref, vc_ref, zfl_ref, zfc_ref, zbl_ref, zbc_ref, gl_ref, gc_ref,
                 lb_ref, gain_ref, bdb_ref, bdf_ref, yl_ref, yc_ref, olf_acc, olb_acc, ocf_acc, ocb_acc, bc_ref):
    bdb = bdb_ref[...]
    bdf = bdf_ref[...]
    zero = jnp.zeros((LANES, LANES), F32)
    st_f, st_b = _hgrn_segment(qc_ref, vc_ref, zfc_ref, zbc_ref, ocf_acc, ocb_acc, bc_ref, lb_ref, bdb, bdf,
                               zero, zero)
    _hgrn_segment(ql_ref, vl_ref, zfl_ref, zbl_ref, olf_acc, olb_acc, bc_ref, lb_ref, bdb, bdf, st_f, st_b)
    for acc_f, acc_b, g_ref, y_ref in ((olf_acc, olb_acc, gl_ref, yl_ref), (ocf_acc, ocb_acc, gc_ref, yc_ref)):
        o = acc_f[...] + acc_b[...]
        g = g_ref[...]
        y = o * _seg_inv_rms(o, bdb) * gain_ref[...]
        y_ref[...] = (y * (g * jax.nn.sigmoid(g))).astype(y_ref.dtype)


def hgrn_mixer(pc, lb, gain128, bdb, bdf, *, b, s, l):
    w = pc.shape[1] // 5
    nj = w // LANES
    n_lat = b * s
    lat = lambda m: pl.BlockSpec((s, LANES), lambda i, j: (i, m * nj + j))
    cx = lambda m: pl.BlockSpec((l, LANES), lambda i, j: (n_lat // l + i, m * nj + j))
    in_specs = []
    for m in (0, 1, 2, 3, 4):
        in_specs += [lat(m), cx(m)]
    in_specs += [pl.BlockSpec((2, 1, 1, LANES), lambda i, j: (0, j, 0, 0)),
                 pl.BlockSpec((1, LANES), lambda i, j: (0, 0)),
                 pl.BlockSpec((LANES, LANES), lambda i, j: (0, 0)),
                 pl.BlockSpec((LANES, LANES), lambda i, j: (0, 0))]
    return pl.pallas_call(
        _hgrn_kernel,
        grid=(b, nj),
        in_specs=in_specs,
        out_specs=[pl.BlockSpec((s, LANES), lambda i, j: (i, j)),
                   pl.BlockSpec((l, LANES), lambda i, j: (i, j))],
        out_shape=[jax.ShapeDtypeStruct((n_lat, w), BF16), jax.ShapeDtypeStruct((b * l, w), BF16)],
        scratch_shapes=[pltpu.VMEM((s, LANES), F32), pltpu.VMEM((s, LANES), F32),
                        pltpu.VMEM((l, LANES), F32), pltpu.VMEM((l, LANES), F32),
                        pltpu.VMEM((2, 3, HG_BLOCK // 8, 8, LANES), F32)],
        compiler_params=_cparams("parallel", "parallel"),
        name="hgrn_mixer",
    )(*([pc] * 10), lb, gain128, bdb, bdf)


def _outproj_kernel(*refs, counts, lat_tiles, wa, wb):
    rows = []
    for n in counts:
        rows.append(_row_tile(refs[:n], lat_tiles))
        refs = refs[n:]
    x_in, ma, mb, mc = rows
    w_ref, gate_ref, g2_ref, shift_ref, scale_ref, wr_ref, br_ref, xo_ref, h2_ref = refs
    d = xo_ref.shape[1]
    w = w_ref[...]
    y = jnp.dot(ma, w[:wa], preferred_element_type=F32)
    y = y + jnp.dot(mb, w[wa:wa + wb], preferred_element_type=F32)
    y = y + jnp.dot(mc, w[wa + wb:], preferred_element_type=F32)
    x = x_in + gate_ref[0] * y
    xo_ref[...] = x
    ms = jnp.mean(x * x, axis=-1, keepdims=True)
    h = x * lax.rsqrt(ms + EPS) * g2_ref[0]
    h = h * (1.0 + scale_ref[0]) + shift_ref[0]
    h2_ref[:, :d] = h
    h_hi = h.astype(BF16)
    h_lo = (h - h_hi.astype(F32)).astype(BF16)
    logits = (jnp.dot(h_hi, wr_ref[0], preferred_element_type=F32)
              + jnp.dot(h_lo, wr_ref[0], preferred_element_type=F32)
              + jnp.dot(h_hi, wr_ref[1], preferred_element_type=F32)) + br_ref[...]
    h2_ref[:, d:] = _route_meta(logits)


def out_projection(xall, mix_a, mix_b, mix_c, w_out_bf, mod3, layer_g2, w_route, b_route, *, n_rows, seq, nseg, tm):
    ops = [_as_parts(a) for a in (xall, mix_a, mix_b, mix_c)]
    r, d = n_rows, ops[0][0].shape[1]
    w_hi = w_route.astype(BF16)
    w_route = jnp.stack([w_hi, (w_route - w_hi.astype(F32)).astype(BF16)])
    wa, wb, wc = (p[0].shape[1] for p in ops[1:])
    tiles_per_seq = seq // tm
    lat_tiles = ops[1][0].shape[0] // tm

    def seg(i):
        return jnp.minimum(i // tiles_per_seq, nseg - 1)

    def modspec(m):
        return pl.BlockSpec((1, 1, d), lambda i: (seg(i) * N_MOD + m, 0, 0))

    row = lambda wdt: pl.BlockSpec((tm, wdt), lambda i: (i, 0))
    kern = functools.partial(_outproj_kernel, counts=tuple(len(p) for p in ops), lat_tiles=lat_tiles, wa=wa, wb=wb)
    return pl.pallas_call(
        kern,
        grid=(r // tm,),
        in_specs=[s for p in ops for s in _row_specs(p, tm, lat_tiles)] + [
                  pl.BlockSpec((wa + wb + wc, d), lambda i: (0, 0)),
                  modspec(2),
                  pl.BlockSpec((1, d), lambda i: (0, 0)),
                  modspec(3), modspec(4),
                  pl.BlockSpec((2, d, LANES), lambda i: (0, 0, 0)),
                  pl.BlockSpec((1, LANES), lambda i: (0, 0))],
        out_specs=[row(d), row(d + LANES)],
        out_shape=[jax.ShapeDtypeStruct((r, d), F32),
                   jax.ShapeDtypeStruct((r, d + LANES), F32)],
        compiler_params=_cparams("parallel"),
        name="out_projection",
    )(*[a for p in ops for a in p], w_out_bf, mod3, layer_g2, mod3, mod3, w_route, b_route)


PAIRS_PER_GROUP = EXPERTS_PER_GROUP * (EXPERTS_PER_GROUP - 1) // 2
N_BUCKETS = N_GROUPS * PAIRS_PER_GROUP
ROW_TILE = 192
META_BUCKET, META_WA, META_WB = 0, 1, 2


def _bucket_experts():
    ea = np.zeros((LANES,), np.int32)
    eb = np.zeros((LANES,), np.int32)
    for g in range(N_GROUPS):
        k = g * PAIRS_PER_GROUP
        for a in range(EXPERTS_PER_GROUP):
            for b in range(a + 1, EXPERTS_PER_GROUP):
                ea[k], eb[k] = g * EXPERTS_PER_GROUP + a, g * EXPERTS_PER_GROUP + b
                k += 1
    return ea, eb


def _route_meta(logits):
    lane = lax.broadcasted_iota(jnp.int32, logits.shape, 1).astype(F32)
    is_g = lane < N_GROUPS
    gl = jnp.where(is_g, logits, -jnp.inf)
    gmax = jnp.max(gl, axis=-1, keepdims=True)
    g_idx = jnp.min(jnp.where(gl == gmax, lane, LANES), axis=-1, keepdims=True)
    gsum = jnp.sum(jnp.where(is_g, jnp.exp(gl - gmax), 0.0), axis=-1, keepdims=True)
    g_top = 1.0 / gsum
    lo = N_GROUPS + g_idx * EXPERTS_PER_GROUP
    in_grp = (lane >= lo) & (lane < lo + EXPERTS_PER_GROUP)
    el = jnp.where(in_grp, logits, -jnp.inf)
    m1 = jnp.max(el, axis=-1, keepdims=True)
    i1 = jnp.min(jnp.where(el == m1, lane, LANES), axis=-1, keepdims=True)
    el2 = jnp.where(lane == i1, -jnp.inf, el)
    m2 = jnp.max(el2, axis=-1, keepdims=True)
    i2 = jnp.min(jnp.where(el2 == m2, lane, LANES), axis=-1, keepdims=True)
    e21 = jnp.exp(m2 - m1)
    w1 = g_top / (1.0 + e21)
    w2 = e21 * w1
    first_low = i1 < i2
    la = jnp.minimum(i1, i2) - lo
    lb = jnp.maximum(i1, i2) - lo
    pair = la * (2 * EXPERTS_PER_GROUP - 1 - la) * 0.5 + (lb - la - 1.0)
    bucket = g_idx * PAIRS_PER_GROUP + pair
    wa = jnp.where(first_low, w1, w2)
    wb = jnp.where(first_low, w2, w1)
    return jnp.where(lane == META_BUCKET, bucket,
                     jnp.where(lane == META_WA, wa, jnp.where(lane == META_WB, wb, 0.0)))


def _plan_kernel(meta_ref, tri_ref, ids_ref, rank_ref, cnt_ref, carry):
    @pl.when(pl.program_id(0) == 0)
    def _():
        carry[...] = jnp.zeros_like(carry)

    ids = meta_ref[...].T[META_BUCKET:META_BUCKET + 1, :]
    sub = lax.broadcasted_iota(jnp.int32, (LANES, ids.shape[1]), 0).astype(F32)
    onehot = (sub == ids).astype(F32)
    before = jnp.dot(onehot.astype(BF16), tri_ref[...], preferred_element_type=F32)
    rank = jnp.sum(onehot * (before + carry[...]), axis=0, keepdims=True)
    ids_ref[0] = ids.astype(jnp.int32)
    rank_ref[0] = rank.astype(jnp.int32)
    total = carry[...] + jnp.sum(onehot, axis=1, keepdims=True)
    carry[...] = total
    cnt_ref[...] = total.astype(jnp.int32)


def route_plan(h2x, tm):
    r = h2x.shape[0]
    meta_blk = h2x.shape[1] // LANES - 1
    nt = r // tm
    i = np.arange(tm)
    tri = jnp.asarray(i[:, None] < i[None, :], BF16)
    ids, rank, cnt = pl.pallas_call(
        _plan_kernel,
        grid=(nt,),
        in_specs=[pl.BlockSpec((tm, LANES), lambda i: (i, meta_blk)),
                  pl.BlockSpec((tm, tm), lambda i: (0, 0))],
        out_specs=[pl.BlockSpec((1, 1, tm), lambda i: (i, 0, 0)),
                   pl.BlockSpec((1, 1, tm), lambda i: (i, 0, 0)),
                   pl.BlockSpec((LANES, 1), lambda i: (0, 0))],
        out_shape=[jax.ShapeDtypeStruct((nt, 1, tm), jnp.int32),
                   jax.ShapeDtypeStruct((nt, 1, tm), jnp.int32),
                   jax.ShapeDtypeStruct((LANES, 1), jnp.int32)],
        scratch_shapes=[pltpu.VMEM((LANES, 1), F32)],
        compiler_params=_cparams("arbitrary"),
        name="route_plan",
    )(h2x, tri)
    return ids.reshape(r), rank.reshape(r), cnt.reshape(LANES)


def _row_copy(src, dst, i, j, sem):
    return pltpu.make_async_copy(src.at[pl.ds(i, 1)], dst.at[pl.ds(j, 1)], sem)


def _dispatch_kernel(pos_ref, h_ref, init_hbm, o_hbm, sem):
    del init_hbm
    ch = h_ref.shape[0]
    base = pl.program_id(0) * ch

    def issue(i, c):
        _row_copy(h_ref, o_hbm, i, pos_ref[base + i], sem).start()
        return c

    def drain(i, c):
        _row_copy(h_ref, o_hbm, i, 0, sem).wait()
        return c

    lax.fori_loop(0, ch, issue, 0, unroll=8)
    lax.fori_loop(0, ch, drain, 0, unroll=8)


def dispatch_rows(pos, h2, n_rows, ch):
    r, d = h2.shape
    return pl.pallas_call(
        _dispatch_kernel,
        grid_spec=pltpu.PrefetchScalarGridSpec(
            num_scalar_prefetch=1,
            grid=(r // ch,),
            in_specs=[pl.BlockSpec((ch, d), lambda i, p: (i, 0)), pl.BlockSpec(memory_space=pl.ANY)],
            out_specs=pl.BlockSpec(memory_space=pl.ANY),
            scratch_shapes=[pltpu.SemaphoreType.DMA],
        ),
        out_shape=jax.ShapeDtypeStruct((n_rows, d), h2.dtype),
        input_output_aliases={2: 0},
        compiler_params=_cparams("arbitrary"),
        name="dispatch_rows",
    )(pos, h2, jnp.zeros((n_rows, d), h2.dtype))


def _expert_kernel(ta_ref, tb_ref, nu_ref, hs_ref, wg_ref, wu_ref, wd_ref, y_ref, gu_s, dn_s):
    j = pl.program_id(0)
    d = wg_ref.shape[2]
    ff = wg_ref.shape[3]
    epg = wg_ref.shape[1]
    prev = jnp.maximum(j - 1, 0)

    @pl.when((j == 0) | (ta_ref[j] // epg != ta_ref[prev] // epg))
    def _():
        for e in range(epg):
            gu_s[e, :, :ff] = wg_ref[0, e].astype(BF16)
            gu_s[e, :, ff:] = wu_ref[0, e].astype(BF16)
            dn_s[e] = wd_ref[0, e].astype(BF16)

    @pl.when(j < nu_ref[0])
    def _():
        h = hs_ref[:, :d].astype(BF16)
        meta = hs_ref[:, d:]
        es = [t_ref[j] % epg for t_ref in (ta_ref, tb_ref)]
        hgus = [jnp.dot(h, gu_s[e], preferred_element_type=F32) for e in es]
        y = None
        for e, hgu, lane in zip(es, hgus, (META_WA, META_WB)):
            hg = hgu[:, :ff]
            hid = (hg * jax.nn.sigmoid(hg)) * hgu[:, ff:]
            part = meta[:, lane:lane + 1] * jnp.dot(hid.astype(BF16), dn_s[e], preferred_element_type=F32)
            y = part if y is None else y + part
        y_ref[:, :d] = y
        y_ref[:, d:] = meta


def expert_pairs(tile_a, tile_b, n_used, hs, w_gate, w_up, w_down, layer):
    rows, dx = hs.shape
    depth, ne, d, ff = w_gate.shape
    epg = EXPERTS_PER_GROUP
    ng = ne // epg
    nt = rows // ROW_TILE
    blk = lambda i, ta, tb, nu: (jnp.minimum(i, nu[0] - 1), 0)
    grp = lambda shape: pl.BlockSpec((1, epg) + shape, lambda i, ta, tb, nu: (layer * ng + ta[i] // epg, 0, 0, 0),
                                     pipeline_mode=pl.Buffered(1))
    by_group = lambda w: w.reshape((depth * ng, epg) + w.shape[2:])
    return pl.pallas_call(
        _expert_kernel,
        grid_spec=pltpu.PrefetchScalarGridSpec(
            num_scalar_prefetch=3,
            grid=(nt,),
            in_specs=[pl.BlockSpec((ROW_TILE, dx), blk), grp((d, ff)), grp((d, ff)), grp((ff, d))],
            out_specs=pl.BlockSpec((ROW_TILE, dx), blk),
            scratch_shapes=[pltpu.VMEM((epg, d, 2 * ff), BF16), pltpu.VMEM((epg, ff, d), BF16)],
        ),
        out_shape=jax.ShapeDtypeStruct((rows, dx), F32),
        input_output_aliases={3: 0},
        compiler_params=_cparams("arbitrary"),
        name="expert_pairs",
    )(tile_a, tile_b, n_used, hs, by_group(w_gate), by_group(w_up), by_group(w_down))


def _combine_kernel(pos_ref, x_ref, gate_ref, y_hbm, o_ref, ybuf, sems):
    tm = x_ref.shape[0]
    d = x_ref.shape[1]
    i = pl.program_id(0)
    slot = i % 2

    def gather(tile, into):
        def issue(r, c):
            _row_copy(y_hbm, ybuf.at[into], pos_ref[tile * tm + r], r, sems.at[into]).start()
            return c
        lax.fori_loop(0, tm, issue, 0, unroll=8)

    @pl.when(i == 0)
    def _():
        gather(0, 0)

    @pl.when(i + 1 < pl.num_programs(0))
    def _():
        gather(i + 1, 1 - slot)

    def drain(r, c):
        _row_copy(y_hbm, ybuf.at[slot], 0, r, sems.at[slot]).wait()
        return c

    lax.fori_loop(0, tm, drain, 0, unroll=8)
    o_ref[...] = x_ref[...] + gate_ref[0] * ybuf[slot, :, :d]


def combine_rows(pos, x_mid, mod3, y, *, seq, nseg, tm):
    r, d = x_mid.shape
    tiles_per_seq = seq // tm
    seg = lambda i: jnp.minimum(i // tiles_per_seq, nseg - 1)
    return pl.pallas_call(
        _combine_kernel,
        grid_spec=pltpu.PrefetchScalarGridSpec(
            num_scalar_prefetch=1,
            grid=(r // tm,),
            in_specs=[pl.BlockSpec((tm, d), lambda i, p: (i, 0)),
                      pl.BlockSpec((1, 1, d), lambda i, p: (seg(i) * N_MOD + 5, 0, 0)),
                      pl.BlockSpec(memory_space=pl.ANY)],
            out_specs=pl.BlockSpec((tm, d), lambda i, p: (i, 0)),
            scratch_shapes=[pltpu.VMEM((2, tm, y.shape[1]), F32), pltpu.SemaphoreType.DMA((2,))],
        ),
        out_shape=jax.ShapeDtypeStruct((r, d), F32),
        compiler_params=_cparams("arbitrary"),
        name="combine_rows",
    )(pos, x_mid, mod3, y)


def moe_block(x_mid, h2, mod3, w_gate, w_up, w_down, *, layer, seq, nseg):
    r, d = x_mid.shape
    ids, rank, cnt = route_plan(h2, 512)
    tiles = (cnt + (ROW_TILE - 1)) // ROW_TILE
    incl = jnp.cumsum(tiles)
    n_used = incl[-1]
    lookup = lambda table, idx: jnp.sum(jnp.where(idx[:, None] == jnp.arange(LANES)[None, :], table[None, :], 0), axis=1)
    pos = lookup(incl - tiles, ids) * ROW_TILE + rank
    nt = r // ROW_TILE + N_BUCKETS
    last = jnp.minimum(jnp.arange(nt), n_used - 1)
    tile_bucket = jnp.sum((incl[None, :] <= last[:, None]).astype(jnp.int32), axis=1)
    ea, eb = _bucket_experts()
    tile_a = lookup(jnp.asarray(ea), tile_bucket)
    tile_b = lookup(jnp.asarray(eb), tile_bucket)
    hs = dispatch_rows(pos, h2, nt * ROW_TILE, 512)
    y = expert_pairs(tile_a, tile_b, n_used.reshape(1).astype(jnp.int32), hs, w_gate, w_up, w_down, layer)
    return combine_rows(pos, x_mid, mod3, y, seq=seq, nseg=nseg, tm=256)


def _rope_tables(seq, tm):
    t = np.arange(seq)
    row = (t // GRID_W).astype(np.float32)
    col = (t % GRID_W).astype(np.float32)
    half = HEAD_DIM // 2
    inv = jnp.asarray(ROPE_THETA, F32) ** (-jnp.arange(0, half, 2, dtype=F32) / half)
    ang = jnp.concatenate([jnp.asarray(row)[:, None] * inv, jnp.asarray(col)[:, None] * inv], axis=-1)
    cos = jnp.repeat(jnp.cos(ang), 2, axis=-1)
    sin = jnp.repeat(jnp.sin(ang), 2, axis=-1) * jnp.asarray(np.tile([-1.0, 1.0], half), F32)
    cos = jnp.tile(cos, (1, LANES // HEAD_DIM))
    sin = jnp.tile(sin, (1, LANES // HEAD_DIM))
    cos = jnp.concatenate([cos, jnp.ones((tm, LANES), F32)], axis=0)
    sin = jnp.concatenate([sin, jnp.zeros((tm, LANES), F32)], axis=0)
    return cos, sin


def kernel(x, c, ctx, c_ctx, w_ada, b_ada, norm1_g, w_in, na_q_norm, na_k_norm, na_rpb, gqa_q_norm, gqa_k_norm, hgrn_lb, hgrn_o_norm, w_out, norm2_g, w_route_group, b_route_group, w_route_expert, b_route_expert, w_exp_gate, w_exp_up, w_exp_down):
    b, s, d = x.shape
    l = ctx.shape[1]
    depth = w_ada.shape[0]
    assert s % 512 == 0 and (b * l) % 512 == 0 and s // GRID_W >= NA_WIN_R
    assert s % l == 0 and l % HG_BLOCK == 0
    nseg = b + 1
    n_lat = b * s
    n_ctx = b * l
    na_w, gq_qw = d // 4, d // 2
    gq_kw = gq_qw // 4
    hg_w = d // 4
    tm = 512

    c_all = jnp.zeros((16, d), F32).at[:b].set(c).at[b].set(c_ctx)
    mod = ada_mod(c_all, w_ada, b_ada)
    cos_t, sin_t = _rope_tables(s, tm)
    bd_f = _block_diag_ones(LANES, HEAD_DIM, F32)
    bd_b = _block_diag_ones(LANES, HEAD_DIM, BF16)
    p_lb = jax.nn.softmax(hgrn_lb.astype(F32), axis=0)
    lb_all = jnp.cumsum(p_lb, axis=0) - p_lb[0]
    tile2 = lambda g: jnp.tile(g, LANES // HEAD_DIM)

    xall = (x.reshape(n_lat, d), ctx.reshape(n_ctx, d))
    for layer in range(depth):
        ctx_out = layer < depth - 1
        mod3 = mod[layer].reshape(16 * N_MOD, 1, d)
        gains = jnp.zeros((8, LANES), F32)
        gains = gains.at[0].set(tile2(na_q_norm[layer])).at[1].set(tile2(na_k_norm[layer]))
        gains = gains.at[2].set(tile2(gqa_q_norm[layer])).at[3].set(tile2(gqa_k_norm[layer]))
        pa, pb, pc = in_projection(xall, mod3, norm1_g[layer][None], w_in[layer].astype(BF16), cos_t, sin_t,
                                   gains, bd_b, n_lat_rows=n_lat, seq=s, nseg=nseg, tm=tm)
        bias_tab = na_bias_table(na_rpb[layer], s // GRID_W)
        o_a = na_attention(pa, bias_tab, b=b, s=s, l=l, na_w=na_w)
        o_b = gqa_attention(pb, b=b, s=s, l=l, qw=gq_qw, kw=gq_kw, tq=256)
        lb4 = lb_all[layer].reshape(2, hg_w // LANES, 1, LANES)
        y_lat, y_ctx = hgrn_mixer(pc, lb4, tile2(hgrn_o_norm[layer])[None], bd_b, bd_f, b=b, s=s, l=l)

        w_route = jnp.zeros((d, LANES), F32).at[:, :N_GROUPS].set(w_route_group[layer])
        w_route = w_route.at[:, N_GROUPS:N_GROUPS + N_EXPERTS].set(w_route_expert[layer])
        b_route = jnp.zeros((1, LANES), F32).at[0, :N_GROUPS].set(b_route_group[layer])
        b_route = b_route.at[0, N_GROUPS:N_GROUPS + N_EXPERTS].set(b_route_expert[layer])
        if ctx_out:
            o_ac, o_bc = ctx_attention(pa, pb, b=b, s=s, l=l, na_w=na_w, qw=gq_qw, kw=gq_kw)
            mix_a, mix_b, y_c = (o_a, o_ac), (o_b, o_bc), (y_lat, y_ctx)
            n_rows = n_lat + n_ctx
        else:
            mix_a, mix_b, y_c, n_rows = o_a, o_b, y_lat, n_lat
        x_mid, h2 = out_projection(xall, mix_a, mix_b, y_c, w_out[layer].astype(BF16), mod3,
                                    norm2_g[layer][None], w_route, b_route, n_rows=n_rows, seq=s,
                                    nseg=nseg, tm=tm)
        xall = moe_block(x_mid, h2, mod3, w_exp_gate, w_exp_up, w_exp_down, layer=layer, seq=s, nseg=nseg)
    return xall[:n_lat].reshape(b, s, d)
```

```python
import functools

import jax
import jax.numpy as jnp
import numpy as np
from jax import lax
from jax.experimental import pallas as pl
from jax.experimental.pallas import tpu as pltpu

F32 = jnp.float32
BF16 = jnp.bfloat16
HIGHEST = lax.Precision.HIGHEST

HEAD_DIM = 64
GRID_W = 64
NA_WIN_R = 8
NA_WIN_C = 16
ROPE_THETA = 10000.0
HGRN_CHUNK = 16
N_GROUPS = 4
EXPERTS_PER_GROUP = 8
N_EXPERTS = N_GROUPS * EXPERTS_PER_GROUP
N_MOD = 6
EPS = 1e-6
NEG_INF = -1e30
LB_FLOOR = 1e-20
LANES = 128
VMEM_LIMIT = 56 * 1024 * 1024
ADA_COLS = 1536
PROJ_ROWS = 512
GQA_QROWS = 256
PLAN_ROWS = 512
DISPATCH_ROWS = 512
COMBINE_ROWS = 256


def _cparams(*sem):
    return pltpu.CompilerParams(dimension_semantics=sem, vmem_limit_bytes=VMEM_LIMIT)


def _block_diag_ones(n, blk, dtype):
    i = np.arange(n)
    return jnp.asarray((i[:, None] // blk) == (i[None, :] // blk), dtype=dtype)


def _ada_kernel(c_ref, w_ref, b_ref, o_ref):
    c = c_ref[...]
    s = c * jax.nn.sigmoid(c)
    o_ref[0] = jnp.dot(s, w_ref[0], precision=HIGHEST, preferred_element_type=F32) + b_ref[0]


def ada_mod(c_all, w_ada, b_ada):
    depth, d, n = w_ada.shape
    tn = ADA_COLS
    return pl.pallas_call(
        _ada_kernel,
        grid=(depth, n // tn),
        in_specs=[
            pl.BlockSpec((16, d), lambda l, j: (0, 0)),
            pl.BlockSpec((1, d, tn), lambda l, j: (l, 0, j)),
            pl.BlockSpec((1, 1, tn), lambda l, j: (l, 0, j)),
        ],
        out_specs=pl.BlockSpec((1, 16, tn), lambda l, j: (l, 0, j)),
        out_shape=jax.ShapeDtypeStruct((depth, 16, n), F32),
        compiler_params=_cparams("parallel", "parallel"),
        name="ada_mod",
    )(c_all, w_ada, b_ada.reshape(depth, 1, n))


def _seg_inv_rms(x, bd):
    xs = x * x
    hi = xs.astype(BF16)
    lo = (xs - hi.astype(F32)).astype(BF16)
    ss = jnp.dot(hi, bd, preferred_element_type=F32) + jnp.dot(lo, bd, preferred_element_type=F32)
    return lax.rsqrt(ss * (1.0 / HEAD_DIM) + EPS)


def _pair_swap(x):
    lane = lax.broadcasted_iota(jnp.int32, x.shape, 1)
    return jnp.where((lane & 1) == 0, pltpu.roll(x, LANES - 1, 1), pltpu.roll(x, 1, 1))


def _as_parts(a):
    return tuple(a) if isinstance(a, (tuple, list)) else (a,)


def _row_specs(parts, tm, lat_tiles):
    w = parts[0].shape[1]
    if len(parts) == 1:
        return [pl.BlockSpec((tm, w), lambda i: (i, 0))]
    return [pl.BlockSpec((tm, w), lambda i: (jnp.minimum(i, lat_tiles - 1), 0)),
            pl.BlockSpec((tm, w), lambda i: (jnp.maximum(i - lat_tiles, 0), 0))]


def _row_tile(refs, lat_tiles):
    if len(refs) == 1:
        return refs[0][...]
    return jnp.where(pl.program_id(0) < lat_tiles, refs[0][...], refs[1][...])


def _inproj_kernel(*refs, nx, lat_tiles, na_w, gq_qw, gq_kw):
    x_refs, refs = refs[:nx], refs[nx:]
    g1_ref, shift_ref, scale_ref, w_ref, cos_ref, sin_ref, gains_ref, bd_ref, oa_ref, ob_ref, oc_ref = refs
    x = _row_tile(x_refs, lat_tiles)
    tm = x.shape[0]
    bd = bd_ref[...]
    qscale = HEAD_DIM ** -0.5
    b0 = 3 * na_w
    halves = [slice(0, tm // 2), slice(tm // 2, tm)]
    hs = []
    for rows in halves:
        xh = x[rows]
        ms = jnp.mean(xh * xh, axis=-1, keepdims=True)
        h = xh * lax.rsqrt(ms + EPS) * g1_ref[0]
        hs.append((h * (1.0 + scale_ref[0]) + shift_ref[0]).astype(BF16))
    ps = [jnp.dot(h, w_ref[...], preferred_element_type=F32) for h in hs]
    for rows, p in zip(halves, ps):
        cos = cos_ref[rows, :]
        sin = sin_ref[rows, :]

        def normed(col, gain_row):
            xb = p[:, col:col + LANES]
            return xb * _seg_inv_rms(xb, bd) * gains_ref[gain_row:gain_row + 1, :]

        def rope(xn):
            return xn * cos + _pair_swap(xn) * sin

        for j in range(na_w // LANES):
            c = j * LANES
            oa_ref[rows, c:c + LANES] = (normed(c, 0) * qscale).astype(BF16)
            oa_ref[rows, na_w + c:na_w + c + LANES] = normed(na_w + c, 1).astype(BF16)
        oa_ref[rows, 2 * na_w:3 * na_w] = p[:, 2 * na_w:3 * na_w].astype(BF16)
        for j in range(gq_qw // LANES):
            c = j * LANES
            ob_ref[rows, c:c + LANES] = (rope(normed(b0 + c, 2)) * qscale).astype(BF16)
        for j in range(gq_kw // LANES):
            c = gq_qw + j * LANES
            ob_ref[rows, c:c + LANES] = rope(normed(b0 + c, 3)).astype(BF16)
        ob_ref[rows, gq_qw + gq_kw:] = p[:, b0 + gq_qw + gq_kw:b0 + gq_qw + 2 * gq_kw].astype(BF16)
        oc_ref[rows, :] = p[:, b0 + gq_qw + 2 * gq_kw:]


def in_projection(xall, mod3, layer_g1, w_in_bf, cos_t, sin_t, gains, bd, *, n_lat_rows, seq, nseg, tm):
    x_parts = _as_parts(xall)
    r, d = sum(a.shape[0] for a in x_parts), x_parts[0].shape[1]
    d_in = w_in_bf.shape[1]
    na_w = d // 4
    gq_qw = d // 2
    gq_kw = gq_qw // 4
    c_w = d_in - 3 * na_w - gq_qw - 2 * gq_kw
    lat_tiles = n_lat_rows // tm
    tiles_per_seq = seq // tm

    def seg(i):
        return jnp.minimum(i // tiles_per_seq, nseg - 1)

    def rope_blk(i):
        return jnp.where(i < lat_tiles, i % tiles_per_seq, tiles_per_seq)

    kern = functools.partial(_inproj_kernel, nx=len(x_parts), lat_tiles=lat_tiles, na_w=na_w, gq_qw=gq_qw,
                             gq_kw=gq_kw)
    return pl.pallas_call(
        kern,
        grid=(r // tm,),
        in_specs=_row_specs(x_parts, tm, lat_tiles) + [
            pl.BlockSpec((1, d), lambda i: (0, 0)),
            pl.BlockSpec((1, 1, d), lambda i: (seg(i) * N_MOD + 0, 0, 0)),
            pl.BlockSpec((1, 1, d), lambda i: (seg(i) * N_MOD + 1, 0, 0)),
            pl.BlockSpec((d, d_in), lambda i: (0, 0)),
            pl.BlockSpec((tm, LANES), lambda i: (rope_blk(i), 0)),
            pl.BlockSpec((tm, LANES), lambda i: (rope_blk(i), 0)),
            pl.BlockSpec((8, LANES), lambda i: (0, 0)),
            pl.BlockSpec((LANES, LANES), lambda i: (0, 0)),
        ],
        out_specs=[
            pl.BlockSpec((tm, 3 * na_w), lambda i: (i, 0)),
            pl.BlockSpec((tm, gq_qw + 2 * gq_kw), lambda i: (i, 0)),
            pl.BlockSpec((tm, c_w), lambda i: (i, 0)),
        ],
        out_shape=[
            jax.ShapeDtypeStruct((r, 3 * na_w), BF16),
            jax.ShapeDtypeStruct((r, gq_qw + 2 * gq_kw), BF16),
            jax.ShapeDtypeStruct((r, c_w), F32),
        ],
        compiler_params=_cparams("parallel"),
        name="in_projection",
    )(*x_parts, layer_g1, mod3, mod3, w_in_bf, cos_t, sin_t, gains, bd)


def _head(j):
    return slice(j * HEAD_DIM, (j + 1) * HEAD_DIM)


def _with_ones(v):
    return jnp.concatenate([v, jnp.ones_like(v)], axis=1)


def _scores(q, k):
    return lax.dot_general(q, k, (((1,), (1,)), ((), ())), preferred_element_type=F32)


def _normalise(o):
    return o[:, :HEAD_DIM] / o[:, HEAD_DIM:HEAD_DIM + 1]


def _softmax_attend_all(qs, ks, v1s):
    ss = [_scores(q, k) for q, k in zip(qs, ks)]
    outs = []
    for s, v1 in zip(ss, v1s):
        p = jnp.exp((s - jnp.max(s, axis=-1, keepdims=True)).astype(BF16))
        outs.append(_normalise(jnp.dot(p, v1, preferred_element_type=F32)))
    return outs


def _grouped_attend(q_ref, k_of, v_of, n_kv, grp, splits=1):
    t = q_ref.shape[0] // splits
    qs, ks, vs = [], [], []
    for j in range(n_kv):
        for h in range(splits):
            qs.append(jnp.concatenate([q_ref[h * t:(h + 1) * t, _head(j * grp + g)] for g in range(grp)], axis=0))
            ks.append(k_of(j))
            vs.append(v_of(j))
    os_ = _softmax_attend_all(qs, ks, vs)
    rows = []
    for h in range(splits):
        cols = []
        for j in range(n_kv):
            o = os_[j * splits + h]
            cols += [o[g * t:(g + 1) * t] for g in range(grp)]
        rows.append(jnp.concatenate(cols, axis=1))
    return jnp.concatenate(rows, axis=0) if splits > 1 else rows[0]


def _gqa_kernel(q_ref, kl_ref, vl_ref, kc_ref, vc_ref, o_ref, k_s, v_s, *, n_kv, grp):
    s_len = kl_ref.shape[0]

    @pl.when(pl.program_id(1) == 0)
    def _():
        for j in range(n_kv):
            k_s[j, :s_len, :] = kl_ref[:, _head(j)]
            k_s[j, s_len:, :] = kc_ref[:, _head(j)]
            v_s[j, :s_len, :] = _with_ones(vl_ref[:, _head(j)])
            v_s[j, s_len:, :] = _with_ones(vc_ref[:, _head(j)])

    o = _grouped_attend(q_ref, lambda j: k_s[j], lambda j: v_s[j], n_kv, grp, splits=2)
    o_ref[...] = o.astype(o_ref.dtype)


def gqa_attention(pb, *, b, s, l, qw, kw, tq):
    n_lat = b * s
    n_kv = kw // HEAD_DIM
    grp = qw // kw
    assert kw == LANES and qw % kw == 0
    kcol, vcol = qw // kw, qw // kw + 1
    kern = functools.partial(_gqa_kernel, n_kv=n_kv, grp=grp)
    return pl.pallas_call(
        kern,
        grid=(b, s // tq),
        in_specs=[
            pl.BlockSpec((tq, qw), lambda i, j: (i * (s // tq) + j, 0)),
            pl.BlockSpec((s, kw), lambda i, j: (i, kcol)),
            pl.BlockSpec((s, kw), lambda i, j: (i, vcol)),
            pl.BlockSpec((l, kw), lambda i, j: (n_lat // l + i, kcol)),
            pl.BlockSpec((l, kw), lambda i, j: (n_lat // l + i, vcol)),
        ],
        out_specs=pl.BlockSpec((tq, qw), lambda i, j: (i * (s // tq) + j, 0)),
        out_shape=jax.ShapeDtypeStruct((n_lat, qw), BF16),
        scratch_shapes=[pltpu.VMEM((n_kv, s + l, HEAD_DIM), BF16), pltpu.VMEM((n_kv, s + l, 2 * HEAD_DIM), BF16)],
        compiler_params=_cparams("parallel", "arbitrary"),
        name="gqa_attention",
    )(pb, pb, pb, pb, pb)


def _ctx_attn_kernel(qa_ref, ka_ref, va_ref, qb_ref, kb_ref, vb_ref, oa_ref, ob_ref, *, n_kv, grp):
    na_h = qa_ref.shape[1] // HEAD_DIM
    oa = _softmax_attend_all([qa_ref[:, _head(h)] for h in range(na_h)],
                             [ka_ref[:, _head(h)] for h in range(na_h)],
                             [_with_ones(va_ref[:, _head(h)]) for h in range(na_h)])
    oa_ref[...] = jnp.concatenate(oa, axis=1).astype(oa_ref.dtype)
    ob = _grouped_attend(qb_ref, lambda j: kb_ref[:, _head(j)], lambda j: _with_ones(vb_ref[:, _head(j)]),
                         n_kv, grp)
    ob_ref[...] = ob.astype(ob_ref.dtype)


def ctx_attention(pa, pb, *, b, s, l, na_w, qw, kw):
    r0 = (b * s) // l
    grp = qw // kw
    kern = functools.partial(_ctx_attn_kernel, n_kv=kw // HEAD_DIM, grp=grp)
    a_spec = lambda m: pl.BlockSpec((l, na_w), lambda i: (r0 + i, m))
    return pl.pallas_call(
        kern,
        grid=(b,),
        in_specs=[a_spec(0), a_spec(1), a_spec(2),
                  pl.BlockSpec((l, qw), lambda i: (r0 + i, 0)),
                  pl.BlockSpec((l, kw), lambda i: (r0 + i, grp)),
                  pl.BlockSpec((l, kw), lambda i: (r0 + i, grp + 1))],
        out_specs=[pl.BlockSpec((l, na_w), lambda i: (i, 0)), pl.BlockSpec((l, qw), lambda i: (i, 0))],
        out_shape=[jax.ShapeDtypeStruct((b * l, na_w), BF16), jax.ShapeDtypeStruct((b * l, qw), BF16)],
        compiler_params=_cparams("parallel"),
        name="ctx_attention",
    )(pa, pa, pa, pb, pb, pb)


NA_QROWS = 4
NA_UNION = NA_WIN_R + NA_QROWS


def _na_block_geometry(rows):
    wu = min(rows, NA_UNION)
    wr = min(NA_WIN_R, rows)
    nblk = rows // NA_QROWS
    sig, u0s = [], []
    for blk in range(nblk):
        r0 = blk * NA_QROWS
        u0 = int(np.clip(r0 - wr // 2, 0, rows - wu))
        u0s.append(u0)
        sig.append(tuple((r0 + j - u0, int(np.clip(r0 + j - wr // 2, 0, rows - wr)) - u0) for j in range(NA_QROWS)))
    cls = [int(blk > 0) + int(blk == nblk - 1) for blk in range(nblk)]
    reps = {}
    for blk in range(nblk):
        assert reps.setdefault(cls[blk], sig[blk]) == sig[blk]
    return wu, wr, [reps.get(c, reps[0]) for c in range(3)]


def _na_kernel(q_ref, k_ref, v_ref, kc_ref, vc_ref, bias_ref, o_ref, k_s, v_s, kc_s, vc_s, *, rows, wu, wr):
    nh = q_ref.shape[1] // HEAD_DIM
    nblk = rows // NA_QROWS
    nq = NA_QROWS * GRID_W
    for h in range(nh):
        k_s[h] = k_ref[:, _head(h)]
        v_s[h] = _with_ones(v_ref[:, _head(h)])
        kc_s[h] = kc_ref[:, _head(h)]
        vc_s[h] = _with_ones(vc_ref[:, _head(h)])

    def body(blk, carry):
        u0 = jnp.clip(blk * NA_QROWS - wr // 2, 0, rows - wu)
        cls = jnp.minimum(blk, 1) + jnp.maximum(blk - (nblk - 2), 0)
        q_rows = pl.ds(pl.multiple_of(blk * nq, nq), nq)
        k_rows = pl.ds(pl.multiple_of(u0 * GRID_W, GRID_W), wu * GRID_W)
        outs = []
        for h in range(nh):
            q = q_ref[q_rows, _head(h)]
            s_nb = _scores(q, k_s[h, k_rows, :]) + bias_ref[h, cls]
            s_cx = _scores(q, kc_s[h])
            m = jnp.maximum(jnp.max(s_nb, axis=-1, keepdims=True), jnp.max(s_cx, axis=-1, keepdims=True))
            p_nb = jnp.exp((s_nb - m).astype(BF16))
            p_cx = jnp.exp((s_cx - m).astype(BF16))
            outs.append(_normalise(jnp.dot(p_nb, v_s[h, k_rows, :], preferred_element_type=F32)
                                   + jnp.dot(p_cx, vc_s[h], preferred_element_type=F32)))
        o_ref[q_rows, :] = jnp.concatenate(outs, axis=1).astype(o_ref.dtype)
        return carry

    lax.fori_loop(0, nblk, body, 0, unroll=2)


def na_attention(pa, bias_tab, layer, *, b, s, l, na_w):
    n_lat = b * s
    nh = na_w // HEAD_DIM
    rows = s // GRID_W
    wu, wr, _ = _na_block_geometry(rows)
    kern = functools.partial(_na_kernel, rows=rows, wu=wu, wr=wr)
    lat = lambda m: pl.BlockSpec((s, na_w), lambda i: (i, m))
    cx = lambda m: pl.BlockSpec((l, na_w), lambda i: (n_lat // l + i, m))
    return pl.pallas_call(
        kern,
        grid=(b,),
        in_specs=[lat(0), lat(1), lat(2), cx(1), cx(2),
                  pl.BlockSpec((nh,) + bias_tab.shape[1:], lambda i: (layer, 0, 0, 0))],
        out_specs=pl.BlockSpec((s, na_w), lambda i: (i, 0)),
        out_shape=jax.ShapeDtypeStruct((n_lat, na_w), BF16),
        scratch_shapes=[pltpu.VMEM((nh, s, HEAD_DIM), BF16), pltpu.VMEM((nh, s, 2 * HEAD_DIM), BF16),
                        pltpu.VMEM((nh, l, HEAD_DIM), BF16), pltpu.VMEM((nh, l, 2 * HEAD_DIM), BF16)],
        compiler_params=_cparams("parallel"),
        name="na_attention",
    )(pa, pa, pa, pa, pa, bias_tab)


def na_bias_table(rpb, rows):
    wu, wr, reps = _na_block_geometry(rows)
    h = rpb.shape[0]
    r_off = np.array([[rj for rj, _ in rep] for rep in reps])
    s_off = np.array([[sj for _, sj in rep] for rep in reps])
    kr = np.arange(wu)
    row_ok = (kr >= s_off[..., None]) & (kr < s_off[..., None] + wr)
    dr = kr - r_off[..., None] + (NA_WIN_R - 1)
    cidx = np.arange(GRID_W)
    col_start = np.clip(cidx - NA_WIN_C // 2, 0, GRID_W - NA_WIN_C)
    col_ok = (cidx[None, :] >= col_start[:, None]) & (cidx[None, :] < col_start[:, None] + NA_WIN_C)
    dc = np.clip(cidx[None, :] - cidx[:, None] + (NA_WIN_C - 1), 0, 2 * NA_WIN_C - 2)
    n_dr, n_dc = 2 * NA_WIN_R - 1, 2 * NA_WIN_C - 1
    sel_r = jnp.asarray(np.where(row_ok, dr, n_dr)[..., None] == np.arange(n_dr + 1), F32)
    sel_c = jnp.asarray(np.where(col_ok, dc, n_dc)[..., None] == np.arange(n_dc + 1), F32)
    rpb_x = jnp.pad(rpb.astype(F32), ((0, 0), (0, 1), (0, 1)), constant_values=NEG_INF)
    bias = jnp.einsum("cjki,hid->hcjkd", sel_r, rpb_x, precision=HIGHEST)
    bias = jnp.einsum("hcjkd,qxd->hcjqkx", bias, sel_c, precision=HIGHEST)
    return bias.reshape(h, len(reps), NA_QROWS * GRID_W, wu * GRID_W)


HG_BLOCK = 128


def _hgrn_block(q_ref, v_ref, z_ref, o_acc, bc_ref, lb, bdb, bdf, st, blk, *, reverse):
    c = HGRN_CHUNK
    hc = c // 2
    ncb = HG_BLOCK // c
    lbm = jnp.maximum(lb, LB_FLOOR)
    one_m_lb = 1.0 - lb
    scale = HEAD_DIM ** -0.5
    t_idx = lax.broadcasted_iota(jnp.int32, (2 * ncb, hc, LANES), 1)
    edge = 0 if reverse else c - 1
    early, late = (1, 0) if reverse else (0, 1)

    def bs(x, s):
        return jnp.broadcast_to(x[:, s:s + 1, :], x.shape)

    def seen(s):
        return (t_idx <= s) if reverse else (t_idx >= s)

    def halves(x):
        x4 = x.reshape(ncb, 2, hc, LANES)
        return x4[:, early], x4[:, late]

    def chunks(xe, xl):
        parts = [xl, xe] if reverse else [xe, xl]
        return jnp.stack(parts, axis=1).reshape(ncb, c, LANES)

    def seg_sum(w):
        rows = w.shape[0] * w.shape[1]
        return jnp.dot(w.reshape(rows, LANES).astype(BF16), bdb, preferred_element_type=F32).reshape(w.shape)

    if True:
        r0 = pl.multiple_of(blk * HG_BLOCK, HG_BLOCK)
        z = z_ref[pl.ds(r0, HG_BLOCK), :]
        q = q_ref[pl.ds(r0, HG_BLOCK), :] * scale
        v = v_ref[pl.ds(r0, HG_BLOCK), :]
        sig = jax.nn.sigmoid(z)
        f = one_m_lb * sig + lbm
        k = one_m_lb * (1.0 - sig) - (lbm - lb)
        logf = jnp.log2(f)
        pre = logf.reshape(2 * ncb, hc, LANES)
        for sh in (1, 2, 4):
            if reverse:
                pre = pre + jnp.where(t_idx < hc - sh, pltpu.roll(pre, hc - sh, 1), 0.0)
            else:
                pre = pre + jnp.where(t_idx >= sh, pltpu.roll(pre, sh, 1), 0.0)
        pre_e, pre_l = halves(pre.reshape(HG_BLOCK, LANES))
        cum_e = pre_e
        cum_l = pre_l + bs(pre_e, 0 if reverse else hc - 1)
        q8, k8, v8 = (a.reshape(2 * ncb, hc, LANES) for a in (q, k, v))
        for slot, a in enumerate((pre, k8, v8)):
            bc_ref[slot] = a

        def row(slot, s, early_only=False):
            lead = pl.ds(early, ncb, stride=2) if early_only else slice(None)
            return jnp.broadcast_to(bc_ref[slot, lead, s:s + 1, :], (ncb if early_only else 2 * ncb, hc, LANES))

        o8 = jnp.zeros_like(pre)
        for s in range(hc):
            d = jnp.where(seen(s), pre - row(0, s), NEG_INF)
            o8 = o8 + seg_sum(q8 * row(1, s) * jnp.exp2(d)) * row(2, s)
        q_l = halves(q)[1]
        o_e, o_l = halves(o8.reshape(HG_BLOCK, LANES))
        for s in range(hc):
            o_l = o_l + seg_sum(q_l * row(1, s, True) * jnp.exp2(cum_l - row(0, s, True))) * row(2, s, True)
        cum = chunks(cum_e, cum_l)
        o3 = chunks(o_e, o_l)
        q3 = q.reshape(ncb, c, LANES)
        k3 = k.reshape(ncb, c, LANES)
        v3 = v.reshape(ncb, c, LANES)
        cum_edge = bs(cum, edge)
        qe = (q3 * jnp.exp2(cum)).astype(BF16)
        kd = (k3 * jnp.exp2(cum_edge - cum)).astype(BF16)
        vb = v3.astype(BF16)
        chunk_decay = jnp.exp2(cum_edge)
        yield None
        u_t = [lax.dot_general(vb[n], kd[n], (((0,), (0,)), ((), ())), preferred_element_type=F32) * bdf
               for n in range(ncb)]
        yield None
        enter = [None] * ncb
        for n in (range(ncb - 1, -1, -1) if reverse else range(ncb)):
            enter[n] = st.astype(BF16)
            st = chunk_decay[n, 0:1, :] * st + u_t[n]
        yield None
        outs = [o3[n] + lax.dot_general(qe[n], enter[n], (((1,), (1,)), ((), ())), preferred_element_type=F32)
                for n in range(ncb)]
        o_acc[pl.ds(r0, HG_BLOCK), :] = jnp.concatenate(outs, axis=0)
        yield st


def _hgrn_segment(q_ref, v_ref, zf_ref, zb_ref, of_acc, ob_acc, bc_ref, lb_ref, bdb, bdf, st_f, st_b):
    nblk = q_ref.shape[0] // HG_BLOCK

    def body(i, carry):
        st_f, st_b = carry
        scans = [_hgrn_block(q_ref, v_ref, zf_ref, of_acc, bc_ref.at[0], lb_ref[0, 0], bdb, bdf, st_f, i,
                             reverse=False),
                 _hgrn_block(q_ref, v_ref, zb_ref, ob_acc, bc_ref.at[1], lb_ref[1, 0], bdb, bdf, st_b,
                             nblk - 1 - i, reverse=True)]
        for _ in range(3):
            for g in scans:
                next(g)
        return tuple(next(g) for g in scans)

    return lax.fori_loop(0, nblk, body, (st_f, st_b))


def _hgrn_kernel(ql_ref, qc_ref, vl_ref, vc_ref, zfl_ref, zfc_ref, zbl_ref, zbc_ref, gl_ref, gc_ref,
                 lb_ref, gain_ref, bdb_ref, bdf_ref, yl_ref, yc_ref, olf_acc, olb_acc, ocf_acc, ocb_acc, bc_ref):
    bdb = bdb_ref[...]
    bdf = bdf_ref[...]
    zero = jnp.zeros((LANES, LANES), F32)
    st_f, st_b = _hgrn_segment(qc_ref, vc_ref, zfc_ref, zbc_ref, ocf_acc, ocb_acc, bc_ref, lb_ref, bdb, bdf,
                               zero, zero)
    _hgrn_segment(ql_ref, vl_ref, zfl_ref, zbl_ref, olf_acc, olb_acc, bc_ref, lb_ref, bdb, bdf, st_f, st_b)
    for acc_f, acc_b, g_ref, y_ref in ((olf_acc, olb_acc, gl_ref, yl_ref), (ocf_acc, ocb_acc, gc_ref, yc_ref)):
        o = acc_f[...] + acc_b[...]
        g = g_ref[...]
        y = o * _seg_inv_rms(o, bdb) * gain_ref[...]
        y_ref[...] = (y * (g * jax.nn.sigmoid(g))).astype(y_ref.dtype)


def hgrn_mixer(pc, lb, gain128, bdb, bdf, *, b, s, l):
    w = pc.shape[1] // 5
    nj = w // LANES
    n_lat = b * s
    lat = lambda m: pl.BlockSpec((s, LANES), lambda i, j: (i, m * nj + j))
    cx = lambda m: pl.BlockSpec((l, LANES), lambda i, j: (n_lat // l + i, m * nj + j))
    in_specs = []
    for m in (0, 1, 2, 3, 4):
        in_specs += [lat(m), cx(m)]
    in_specs += [pl.BlockSpec((2, 1, 1, LANES), lambda i, j: (0, j, 0, 0)),
                 pl.BlockSpec((1, LANES), lambda i, j: (0, 0)),
                 pl.BlockSpec((LANES, LANES), lambda i, j: (0, 0)),
                 pl.BlockSpec((LANES, LANES), lambda i, j: (0, 0))]
    return pl.pallas_call(
        _hgrn_kernel,
        grid=(b, nj),
        in_specs=in_specs,
        out_specs=[pl.BlockSpec((s, LANES), lambda i, j: (i, j)),
                   pl.BlockSpec((l, LANES), lambda i, j: (i, j))],
        out_shape=[jax.ShapeDtypeStruct((n_lat, w), BF16), jax.ShapeDtypeStruct((b * l, w), BF16)],
        scratch_shapes=[pltpu.VMEM((s, LANES), F32), pltpu.VMEM((s, LANES), F32),
                        pltpu.VMEM((l, LANES), F32), pltpu.VMEM((l, LANES), F32),
                        pltpu.VMEM((2, 3, HG_BLOCK // 8, 8, LANES), F32)],
        compiler_params=_cparams("parallel", "parallel"),
        name="hgrn_mixer",
    )(*([pc] * 10), lb, gain128, bdb, bdf)


def _outproj_kernel(*refs, counts, lat_tiles, wa, wb):
    rows = []
    for n in counts:
        rows.append(_row_tile(refs[:n], lat_tiles))
        refs = refs[n:]
    x_in, ma, mb, mc = rows
    w_ref, gate_ref, g2_ref, shift_ref, scale_ref, wr_ref, br_ref, xo_ref, h2_ref = refs
    d = xo_ref.shape[1]
    w = w_ref[...]
    y = jnp.dot(ma, w[:wa], preferred_element_type=F32)
    y = y + jnp.dot(mb, w[wa:wa + wb], preferred_element_type=F32)
    y = y + jnp.dot(mc, w[wa + wb:], preferred_element_type=F32)
    x = x_in + gate_ref[0] * y
    xo_ref[...] = x
    ms = jnp.mean(x * x, axis=-1, keepdims=True)
    h = x * lax.rsqrt(ms + EPS) * g2_ref[0]
    h = h * (1.0 + scale_ref[0]) + shift_ref[0]
    h2_ref[:, :d] = h
    h_hi = h.astype(BF16)
    h_lo = (h - h_hi.astype(F32)).astype(BF16)
    logits = (jnp.dot(h_hi, wr_ref[0], preferred_element_type=F32)
              + jnp.dot(h_lo, wr_ref[0], preferred_element_type=F32)
              + jnp.dot(h_hi, wr_ref[1], preferred_element_type=F32)) + br_ref[...]
    h2_ref[:, d:] = _route_meta(logits)


def out_projection(xall, mix_a, mix_b, mix_c, w_out_bf, mod3, layer_g2, w_route, b_route, *, n_rows, seq, nseg, tm):
    ops = [_as_parts(a) for a in (xall, mix_a, mix_b, mix_c)]
    r, d = n_rows, ops[0][0].shape[1]
    w_hi = w_route.astype(BF16)
    w_route = jnp.stack([w_hi, (w_route - w_hi.astype(F32)).astype(BF16)])
    wa, wb, wc = (p[0].shape[1] for p in ops[1:])
    tiles_per_seq = seq // tm
    lat_tiles = ops[1][0].shape[0] // tm

    def seg(i):
        return jnp.minimum(i // tiles_per_seq, nseg - 1)

    def modspec(m):
        return pl.BlockSpec((1, 1, d), lambda i: (seg(i) * N_MOD + m, 0, 0))

    row = lambda wdt: pl.BlockSpec((tm, wdt), lambda i: (i, 0))
    kern = functools.partial(_outproj_kernel, counts=tuple(len(p) for p in ops), lat_tiles=lat_tiles, wa=wa, wb=wb)
    return pl.pallas_call(
        kern,
        grid=(r // tm,),
        in_specs=[s for p in ops for s in _row_specs(p, tm, lat_tiles)] + [
                  pl.BlockSpec((wa + wb + wc, d), lambda i: (0, 0)),
                  modspec(2),
                  pl.BlockSpec((1, d), lambda i: (0, 0)),
                  modspec(3), modspec(4),
                  pl.BlockSpec((2, d, LANES), lambda i: (0, 0, 0)),
                  pl.BlockSpec((1, LANES), lambda i: (0, 0))],
        out_specs=[row(d), row(d + LANES)],
        out_shape=[jax.ShapeDtypeStruct((r, d), F32),
                   jax.ShapeDtypeStruct((r, d + LANES), F32)],
        compiler_params=_cparams("parallel"),
        name="out_projection",
    )(*[a for p in ops for a in p], w_out_bf, mod3, layer_g2, mod3, mod3, w_route, b_route)


PAIRS_PER_GROUP = EXPERTS_PER_GROUP * (EXPERTS_PER_GROUP - 1) // 2
N_BUCKETS = N_GROUPS * PAIRS_PER_GROUP
ROW_TILE = 192
META_BUCKET, META_WA, META_WB = 0, 1, 2


def _bucket_experts():
    ea = np.zeros((LANES,), np.int32)
    eb = np.zeros((LANES,), np.int32)
    for g in range(N_GROUPS):
        k = g * PAIRS_PER_GROUP
        for a in range(EXPERTS_PER_GROUP):
            for b in range(a + 1, EXPERTS_PER_GROUP):
                ea[k], eb[k] = g * EXPERTS_PER_GROUP + a, g * EXPERTS_PER_GROUP + b
                k += 1
    return ea, eb


def _route_meta(logits):
    lane = lax.broadcasted_iota(jnp.int32, logits.shape, 1).astype(F32)
    is_g = lane < N_GROUPS
    gl = jnp.where(is_g, logits, -jnp.inf)
    gmax = jnp.max(gl, axis=-1, keepdims=True)
    g_idx = jnp.min(jnp.where(gl == gmax, lane, LANES), axis=-1, keepdims=True)
    gsum = jnp.sum(jnp.where(is_g, jnp.exp(gl - gmax), 0.0), axis=-1, keepdims=True)
    g_top = 1.0 / gsum
    lo = N_GROUPS + g_idx * EXPERTS_PER_GROUP
    in_grp = (lane >= lo) & (lane < lo + EXPERTS_PER_GROUP)
    el = jnp.where(in_grp, logits, -jnp.inf)
    m1 = jnp.max(el, axis=-1, keepdims=True)
    i1 = jnp.min(jnp.where(el == m1, lane, LANES), axis=-1, keepdims=True)
    el2 = jnp.where(lane == i1, -jnp.inf, el)
    m2 = jnp.max(el2, axis=-1, keepdims=True)
    i2 = jnp.min(jnp.where(el2 == m2, lane, LANES), axis=-1, keepdims=True)
    e21 = jnp.exp(m2 - m1)
    w1 = g_top / (1.0 + e21)
    w2 = e21 * w1
    first_low = i1 < i2
    la = jnp.minimum(i1, i2) - lo
    lb = jnp.maximum(i1, i2) - lo
    pair = la * (2 * EXPERTS_PER_GROUP - 1 - la) * 0.5 + (lb - la - 1.0)
    bucket = g_idx * PAIRS_PER_GROUP + pair
    wa = jnp.where(first_low, w1, w2)
    wb = jnp.where(first_low, w2, w1)
    return jnp.where(lane == META_BUCKET, bucket,
                     jnp.where(lane == META_WA, wa, jnp.where(lane == META_WB, wb, 0.0)))


def _plan_kernel(meta_ref, tri_ref, ids_ref, rank_ref, cnt_ref, carry):
    @pl.when(pl.program_id(0) == 0)
    def _():
        carry[...] = jnp.zeros_like(carry)

    ids = meta_ref[...].T[META_BUCKET:META_BUCKET + 1, :]
    sub = lax.broadcasted_iota(jnp.int32, (LANES, ids.shape[1]), 0).astype(F32)
    onehot = (sub == ids).astype(F32)
    before = jnp.dot(onehot.astype(BF16), tri_ref[...], preferred_element_type=F32)
    rank = jnp.sum(onehot * (before + carry[...]), axis=0, keepdims=True)
    ids_ref[0] = ids.astype(jnp.int32)
    rank_ref[0] = rank.astype(jnp.int32)
    total = carry[...] + jnp.sum(onehot, axis=1, keepdims=True)
    carry[...] = total
    cnt_ref[...] = total.astype(jnp.int32)


def route_plan(h2x, tm):
    r = h2x.shape[0]
    meta_blk = h2x.shape[1] // LANES - 1
    nt = r // tm
    i = np.arange(tm)
    tri = jnp.asarray(i[:, None] < i[None, :], BF16)
    ids, rank, cnt = pl.pallas_call(
        _plan_kernel,
        grid=(nt,),
        in_specs=[pl.BlockSpec((tm, LANES), lambda i: (i, meta_blk)),
                  pl.BlockSpec((tm, tm), lambda i: (0, 0))],
        out_specs=[pl.BlockSpec((1, 1, tm), lambda i: (i, 0, 0)),
                   pl.BlockSpec((1, 1, tm), lambda i: (i, 0, 0)),
                   pl.BlockSpec((LANES, 1), lambda i: (0, 0))],
        out_shape=[jax.ShapeDtypeStruct((nt, 1, tm), jnp.int32),
                   jax.ShapeDtypeStruct((nt, 1, tm), jnp.int32),
                   jax.ShapeDtypeStruct((LANES, 1), jnp.int32)],
        scratch_shapes=[pltpu.VMEM((LANES, 1), F32)],
        compiler_params=_cparams("arbitrary"),
        name="route_plan",
    )(h2x, tri)
    return ids.reshape(r), rank.reshape(r), cnt.reshape(LANES)


def _row_copy(src, dst, i, j, sem):
    return pltpu.make_async_copy(src.at[pl.ds(i, 1)], dst.at[pl.ds(j, 1)], sem)


def _dispatch_kernel(pos_ref, h_ref, init_hbm, o_hbm, sem):
    del init_hbm
    ch = h_ref.shape[0]
    base = pl.program_id(0) * ch

    def issue(i, c):
        _row_copy(h_ref, o_hbm, i, pos_ref[base + i], sem).start()
        return c

    def drain(i, c):
        _row_copy(h_ref, o_hbm, i, 0, sem).wait()
        return c

    lax.fori_loop(0, ch, issue, 0, unroll=8)
    lax.fori_loop(0, ch, drain, 0, unroll=8)


def dispatch_rows(pos, h2, n_rows, ch):
    r, d = h2.shape
    return pl.pallas_call(
        _dispatch_kernel,
        grid_spec=pltpu.PrefetchScalarGridSpec(
            num_scalar_prefetch=1,
            grid=(r // ch,),
            in_specs=[pl.BlockSpec((ch, d), lambda i, p: (i, 0)), pl.BlockSpec(memory_space=pl.ANY)],
            out_specs=pl.BlockSpec(memory_space=pl.ANY),
            scratch_shapes=[pltpu.SemaphoreType.DMA],
        ),
        out_shape=jax.ShapeDtypeStruct((n_rows, d), h2.dtype),
        input_output_aliases={2: 0},
        compiler_params=_cparams("arbitrary"),
        name="dispatch_rows",
    )(pos, h2, jnp.zeros((n_rows, d), h2.dtype))


def _expert_kernel(ta_ref, tb_ref, nu_ref, hs_ref, wg_ref, wu_ref, wd_ref, y_ref, gu_s, dn_s):
    del nu_ref
    j = pl.program_id(0)
    d = wg_ref.shape[2]
    ff = wg_ref.shape[3]
    epg = wg_ref.shape[1]
    prev = jnp.maximum(j - 1, 0)

    @pl.when((j == 0) | (ta_ref[j] // epg != ta_ref[prev] // epg))
    def _():
        for e in range(epg):
            gu_s[e, :, :ff] = wg_ref[0, e].astype(BF16)
            gu_s[e, :, ff:] = wu_ref[0, e].astype(BF16)
            dn_s[e] = wd_ref[0, e].astype(BF16)

    h = hs_ref[:, :d].astype(BF16)
    meta = hs_ref[:, d:]
    es = [t_ref[j] % epg for t_ref in (ta_ref, tb_ref)]
    hgus = [jnp.dot(h, gu_s[e], preferred_element_type=F32) for e in es]
    y = None
    for e, hgu, lane in zip(es, hgus, (META_WA, META_WB)):
        hg = hgu[:, :ff]
        hid = (hg * jax.nn.sigmoid(hg)) * hgu[:, ff:]
        part = meta[:, lane:lane + 1] * jnp.dot(hid.astype(BF16), dn_s[e], preferred_element_type=F32)
        y = part if y is None else y + part
    y_ref[:, :d] = y
    y_ref[:, d:] = meta


def expert_pairs(tile_a, tile_b, n_used, hs, w_gate, w_up, w_down, layer):
    rows, dx = hs.shape
    depth, ne, d, ff = w_gate.shape
    epg = EXPERTS_PER_GROUP
    ng = ne // epg
    blk = lambda i, ta, tb, nu: (i, 0)
    grp = lambda shape: pl.BlockSpec((1, epg) + shape, lambda i, ta, tb, nu: (layer * ng + ta[i] // epg, 0, 0, 0),
                                     pipeline_mode=pl.Buffered(1))
    by_group = lambda w: w.reshape((depth * ng, epg) + w.shape[2:])
    return pl.pallas_call(
        _expert_kernel,
        grid_spec=pltpu.PrefetchScalarGridSpec(
            num_scalar_prefetch=3,
            grid=(n_used[0],),
            in_specs=[pl.BlockSpec((ROW_TILE, dx), blk), grp((d, ff)), grp((d, ff)), grp((ff, d))],
            out_specs=pl.BlockSpec((ROW_TILE, dx), blk),
            scratch_shapes=[pltpu.VMEM((epg, d, 2 * ff), BF16), pltpu.VMEM((epg, ff, d), BF16)],
        ),
        out_shape=jax.ShapeDtypeStruct((rows, dx), F32),
        input_output_aliases={3: 0},
        compiler_params=_cparams("arbitrary"),
        name="expert_pairs",
    )(tile_a, tile_b, n_used, hs, by_group(w_gate), by_group(w_up), by_group(w_down))


def _combine_kernel(pos_ref, x_ref, gate_ref, y_hbm, o_ref, ybuf, sems):
    tm = x_ref.shape[0]
    d = x_ref.shape[1]
    i = pl.program_id(0)
    slot = i % 2

    def gather(tile, into):
        def issue(r, c):
            _row_copy(y_hbm, ybuf.at[into], pos_ref[tile * tm + r], r, sems.at[into]).start()
            return c
        lax.fori_loop(0, tm, issue, 0, unroll=8)

    @pl.when(i == 0)
    def _():
        gather(0, 0)

    @pl.when(i + 1 < pl.num_programs(0))
    def _():
        gather(i + 1, 1 - slot)

    def drain(r, c):
        _row_copy(y_hbm, ybuf.at[slot], 0, r, sems.at[slot]).wait()
        return c

    lax.fori_loop(0, tm, drain, 0, unroll=8)
    o_ref[...] = x_ref[...] + gate_ref[0] * ybuf[slot, :, :d]


def combine_rows(pos, x_mid, mod3, y, *, seq, nseg, tm):
    r, d = x_mid.shape
    tiles_per_seq = seq // tm
    seg = lambda i: jnp.minimum(i // tiles_per_seq, nseg - 1)
    return pl.pallas_call(
        _combine_kernel,
        grid_spec=pltpu.PrefetchScalarGridSpec(
            num_scalar_prefetch=1,
            grid=(r // tm,),
            in_specs=[pl.BlockSpec((tm, d), lambda i, p: (i, 0)),
                      pl.BlockSpec((1, 1, d), lambda i, p: (seg(i) * N_MOD + 5, 0, 0)),
                      pl.BlockSpec(memory_space=pl.ANY)],
            out_specs=pl.BlockSpec((tm, d), lambda i, p: (i, 0)),
            scratch_shapes=[pltpu.VMEM((2, tm, y.shape[1]), F32), pltpu.SemaphoreType.DMA((2,))],
        ),
        out_shape=jax.ShapeDtypeStruct((r, d), F32),
        compiler_params=_cparams("arbitrary"),
        name="combine_rows",
    )(pos, x_mid, mod3, y)


def moe_block(x_mid, h2, mod3, w_gate, w_up, w_down, *, layer, seq, nseg):
    r, d = x_mid.shape
    ids, rank, cnt = route_plan(h2, PLAN_ROWS)
    tiles = (cnt + (ROW_TILE - 1)) // ROW_TILE
    incl = jnp.cumsum(tiles)
    n_used = incl[-1]
    lookup = lambda table, idx: jnp.sum(jnp.where(idx[:, None] == jnp.arange(LANES)[None, :], table[None, :], 0), axis=1)
    pos = lookup(incl - tiles, ids) * ROW_TILE + rank
    nt = r // ROW_TILE + N_BUCKETS
    last = jnp.minimum(jnp.arange(nt), n_used - 1)
    tile_bucket = jnp.sum((incl[None, :] <= last[:, None]).astype(jnp.int32), axis=1)
    ea, eb = _bucket_experts()
    tile_a = lookup(jnp.asarray(ea), tile_bucket)
    tile_b = lookup(jnp.asarray(eb), tile_bucket)
    hs = dispatch_rows(pos, h2, nt * ROW_TILE, DISPATCH_ROWS)
    y = expert_pairs(tile_a, tile_b, n_used.reshape(1).astype(jnp.int32), hs, w_gate, w_up, w_down, layer)
    return combine_rows(pos, x_mid, mod3, y, seq=seq, nseg=nseg, tm=COMBINE_ROWS)


def _rope_tables(seq, tm):
    t = np.arange(seq)
    row = (t // GRID_W).astype(np.float32)
    col = (t % GRID_W).astype(np.float32)
    half = HEAD_DIM // 2
    inv = jnp.asarray(ROPE_THETA, F32) ** (-jnp.arange(0, half, 2, dtype=F32) / half)
    ang = jnp.concatenate([jnp.asarray(row)[:, None] * inv, jnp.asarray(col)[:, None] * inv], axis=-1)
    cos = jnp.repeat(jnp.cos(ang), 2, axis=-1)
    sin = jnp.repeat(jnp.sin(ang), 2, axis=-1) * jnp.asarray(np.tile([-1.0, 1.0], half), F32)
    cos = jnp.tile(cos, (1, LANES // HEAD_DIM))
    sin = jnp.tile(sin, (1, LANES // HEAD_DIM))
    cos = jnp.concatenate([cos, jnp.ones((tm, LANES), F32)], axis=0)
    sin = jnp.concatenate([sin, jnp.zeros((tm, LANES), F32)], axis=0)
    return cos, sin


def kernel(x, c, ctx, c_ctx, w_ada, b_ada, norm1_g, w_in, na_q_norm, na_k_norm, na_rpb, gqa_q_norm, gqa_k_norm, hgrn_lb, hgrn_o_norm, w_out, norm2_g, w_route_group, b_route_group, w_route_expert, b_route_expert, w_exp_gate, w_exp_up, w_exp_down):
    b, s, d = x.shape
    l = ctx.shape[1]
    depth = w_ada.shape[0]
    assert s % PROJ_ROWS == 0 and (b * l) % PROJ_ROWS == 0 and s // GRID_W >= NA_WIN_R
    assert s % l == 0 and l % HG_BLOCK == 0
    nseg = b + 1
    n_lat = b * s
    n_ctx = b * l
    na_w, gq_qw = d // 4, d // 2
    gq_kw = gq_qw // 4
    hg_w = d // 4
    tm = PROJ_ROWS

    c_all = jnp.zeros((16, d), F32).at[:b].set(c).at[b].set(c_ctx)
    mod = ada_mod(c_all, w_ada, b_ada)
    cos_t, sin_t = _rope_tables(s, tm)
    bd_f = _block_diag_ones(LANES, HEAD_DIM, F32)
    bd_b = _block_diag_ones(LANES, HEAD_DIM, BF16)
    p_lb = jax.nn.softmax(hgrn_lb.astype(F32), axis=0)
    lb_all = jnp.cumsum(p_lb, axis=0) - p_lb[0]
    tile2 = lambda g: jnp.tile(g, LANES // HEAD_DIM)
    bias_tab = na_bias_table(na_rpb.reshape((-1,) + na_rpb.shape[2:]), s // GRID_W)

    xall = (x.reshape(n_lat, d), ctx.reshape(n_ctx, d))
    for layer in range(depth):
        ctx_out = layer < depth - 1
        mod3 = mod[layer].reshape(16 * N_MOD, 1, d)
        gains = jnp.zeros((8, LANES), F32)
        gains = gains.at[0].set(tile2(na_q_norm[layer])).at[1].set(tile2(na_k_norm[layer]))
        gains = gains.at[2].set(tile2(gqa_q_norm[layer])).at[3].set(tile2(gqa_k_norm[layer]))
        pa, pb, pc = in_projection(xall, mod3, norm1_g[layer][None], w_in[layer].astype(BF16), cos_t, sin_t,
                                   gains, bd_b, n_lat_rows=n_lat, seq=s, nseg=nseg, tm=tm)
        o_a = na_attention(pa, bias_tab, layer, b=b, s=s, l=l, na_w=na_w)
        o_b = gqa_attention(pb, b=b, s=s, l=l, qw=gq_qw, kw=gq_kw, tq=GQA_QROWS)
        lb4 = lb_all[layer].reshape(2, hg_w // LANES, 1, LANES)
        y_lat, y_ctx = hgrn_mixer(pc, lb4, tile2(hgrn_o_norm[layer])[None], bd_b, bd_f, b=b, s=s, l=l)

        w_route = jnp.zeros((d, LANES), F32).at[:, :N_GROUPS].set(w_route_group[layer])
        w_route = w_route.at[:, N_GROUPS:N_GROUPS + N_EXPERTS].set(w_route_expert[layer])
        b_route = jnp.zeros((1, LANES), F32).at[0, :N_GROUPS].set(b_route_group[layer])
        b_route = b_route.at[0, N_GROUPS:N_GROUPS + N_EXPERTS].set(b_route_expert[layer])
        if ctx_out:
            o_ac, o_bc = ctx_attention(pa, pb, b=b, s=s, l=l, na_w=na_w, qw=gq_qw, kw=gq_kw)
            mix_a, mix_b, y_c = (o_a, o_ac), (o_b, o_bc), (y_lat, y_ctx)
            n_rows = n_lat + n_ctx
        else:
            mix_a, mix_b, y_c, n_rows = o_a, o_b, y_lat, n_lat
        x_mid, h2 = out_projection(xall, mix_a, mix_b, y_c, w_out[layer].astype(BF16), mod3,
                                    norm2_g[layer][None], w_route, b_route, n_rows=n_rows, seq=s,
                                    nseg=nseg, tm=tm)
        xall = moe_block(x_mid, h2, mod3, w_exp_gate, w_exp_up, w_exp_down, layer=layer, seq=s, nseg=nseg)
    return xall[:n_lat].reshape(b, s, d)
```

```python
import functools

import jax
import jax.numpy as jnp
import numpy as np
from jax import lax
from jax.experimental import pallas as pl
from jax.experimental.pallas import tpu as pltpu

F32 = jnp.float32
BF16 = jnp.bfloat16
HIGHEST = lax.Precision.HIGHEST

HEAD_DIM = 64
GRID_W = 64
NA_WIN_R = 8
NA_WIN_C = 16
ROPE_THETA = 10000.0
HGRN_CHUNK = 16
N_GROUPS = 4
EXPERTS_PER_GROUP = 8
N_EXPERTS = N_GROUPS * EXPERTS_PER_GROUP
N_MOD = 6
EPS = 1e-6
NEG_INF = -1e30
LB_FLOOR = 1e-20
LANES = 128
VMEM_LIMIT = 56 * 1024 * 1024
ADA_COLS = 1536
PROJ_ROWS = 512
GQA_QROWS = 256
PLAN_ROWS = 512
DISPATCH_ROWS = 1024
COMBINE_ROWS = 512


def _cparams(*sem):
    return pltpu.CompilerParams(dimension_semantics=sem, vmem_limit_bytes=VMEM_LIMIT)


def _block_diag_ones(n, blk, dtype):
    i = np.arange(n)
    return jnp.asarray((i[:, None] // blk) == (i[None, :] // blk), dtype=dtype)


def _ada_kernel(c_ref, w_ref, b_ref, o_ref):
    c = c_ref[...]
    s = c * jax.nn.sigmoid(c)
    o_ref[0] = jnp.dot(s, w_ref[0], precision=HIGHEST, preferred_element_type=F32) + b_ref[0]


def ada_mod(c_all, w_ada, b_ada):
    depth, d, n = w_ada.shape
    tn = ADA_COLS
    return pl.pallas_call(
        _ada_kernel,
        grid=(depth, n // tn),
        in_specs=[
            pl.BlockSpec((16, d), lambda l, j: (0, 0)),
            pl.BlockSpec((1, d, tn), lambda l, j: (l, 0, j)),
            pl.BlockSpec((1, 1, tn), lambda l, j: (l, 0, j)),
        ],
        out_specs=pl.BlockSpec((1, 16, tn), lambda l, j: (l, 0, j)),
        out_shape=jax.ShapeDtypeStruct((depth, 16, n), F32),
        compiler_params=_cparams("parallel", "parallel"),
        name="ada_mod",
    )(c_all, w_ada, b_ada.reshape(depth, 1, n))


def _seg_inv_rms(x, bd):
    xs = x * x
    hi = xs.astype(BF16)
    lo = (xs - hi.astype(F32)).astype(BF16)
    ss = jnp.dot(hi, bd, preferred_element_type=F32) + jnp.dot(lo, bd, preferred_element_type=F32)
    return lax.rsqrt(ss * (1.0 / HEAD_DIM) + EPS)


def _pair_swap(x):
    lane = lax.broadcasted_iota(jnp.int32, x.shape, 1)
    return jnp.where((lane & 1) == 0, pltpu.roll(x, LANES - 1, 1), pltpu.roll(x, 1, 1))


def _as_parts(a):
    return tuple(a) if isinstance(a, (tuple, list)) else (a,)


def _row_specs(parts, tm, lat_tiles):
    w = parts[0].shape[1]
    if len(parts) == 1:
        return [pl.BlockSpec((tm, w), lambda i: (i, 0))]
    return [pl.BlockSpec((tm, w), lambda i: (jnp.minimum(i, lat_tiles - 1), 0)),
            pl.BlockSpec((tm, w), lambda i: (jnp.maximum(i - lat_tiles, 0), 0))]


def _row_tile(refs, lat_tiles):
    if len(refs) == 1:
        return refs[0][...]
    return jnp.where(pl.program_id(0) < lat_tiles, refs[0][...], refs[1][...])


def _inproj_kernel(*refs, nx, lat_tiles, na_w, gq_qw, gq_kw):
    x_refs, refs = refs[:nx], refs[nx:]
    g1_ref, shift_ref, scale_ref, w_ref, cos_ref, sin_ref, gains_ref, bd_ref, oa_ref, ob_ref, oc_ref = refs
    x = _row_tile(x_refs, lat_tiles)
    tm = x.shape[0]
    bd = bd_ref[...]
    qscale = HEAD_DIM ** -0.5
    b0 = 3 * na_w
    halves = [slice(0, tm // 2), slice(tm // 2, tm)]
    hs = []
    for rows in halves:
        xh = x[rows]
        ms = jnp.mean(xh * xh, axis=-1, keepdims=True)
        h = xh * lax.rsqrt(ms + EPS) * g1_ref[0]
        hs.append((h * (1.0 + scale_ref[0]) + shift_ref[0]).astype(BF16))
    ps = [jnp.dot(h, w_ref[...], preferred_element_type=F32) for h in hs]
    for rows, p in zip(halves, ps):
        cos = cos_ref[rows, :]
        sin = sin_ref[rows, :]

        def normed(col, gain_row):
            xb = p[:, col:col + LANES]
            return xb * _seg_inv_rms(xb, bd) * gains_ref[gain_row:gain_row + 1, :]

        def rope(xn):
            return xn * cos + _pair_swap(xn) * sin

        for j in range(na_w // LANES):
            c = j * LANES
            oa_ref[rows, c:c + LANES] = (normed(c, 0) * qscale).astype(BF16)
            oa_ref[rows, na_w + c:na_w + c + LANES] = normed(na_w + c, 1).astype(BF16)
        oa_ref[rows, 2 * na_w:3 * na_w] = p[:, 2 * na_w:3 * na_w].astype(BF16)
        for j in range(gq_qw // LANES):
            c = j * LANES
            ob_ref[rows, c:c + LANES] = (rope(normed(b0 + c, 2)) * qscale).astype(BF16)
        for j in range(gq_kw // LANES):
            c = gq_qw + j * LANES
            ob_ref[rows, c:c + LANES] = rope(normed(b0 + c, 3)).astype(BF16)
        ob_ref[rows, gq_qw + gq_kw:] = p[:, b0 + gq_qw + gq_kw:b0 + gq_qw + 2 * gq_kw].astype(BF16)
        oc_ref[rows, :] = p[:, b0 + gq_qw + 2 * gq_kw:]


def in_projection(xall, mod3, layer_g1, w_in_bf, cos_t, sin_t, gains, bd, *, n_lat_rows, seq, nseg, tm):
    x_parts = _as_parts(xall)
    r, d = sum(a.shape[0] for a in x_parts), x_parts[0].shape[1]
    d_in = w_in_bf.shape[1]
    na_w = d // 4
    gq_qw = d // 2
    gq_kw = gq_qw // 4
    c_w = d_in - 3 * na_w - gq_qw - 2 * gq_kw
    lat_tiles = n_lat_rows // tm
    tiles_per_seq = seq // tm

    def seg(i):
        return jnp.minimum(i // tiles_per_seq, nseg - 1)

    def rope_blk(i):
        return jnp.where(i < lat_tiles, i % tiles_per_seq, tiles_per_seq)

    kern = functools.partial(_inproj_kernel, nx=len(x_parts), lat_tiles=lat_tiles, na_w=na_w, gq_qw=gq_qw,
                             gq_kw=gq_kw)
    return pl.pallas_call(
        kern,
        grid=(r // tm,),
        in_specs=_row_specs(x_parts, tm, lat_tiles) + [
            pl.BlockSpec((1, d), lambda i: (0, 0)),
            pl.BlockSpec((1, 1, d), lambda i: (seg(i) * N_MOD + 0, 0, 0)),
            pl.BlockSpec((1, 1, d), lambda i: (seg(i) * N_MOD + 1, 0, 0)),
            pl.BlockSpec((d, d_in), lambda i: (0, 0)),
            pl.BlockSpec((tm, LANES), lambda i: (rope_blk(i), 0)),
            pl.BlockSpec((tm, LANES), lambda i: (rope_blk(i), 0)),
            pl.BlockSpec((8, LANES), lambda i: (0, 0)),
            pl.BlockSpec((LANES, LANES), lambda i: (0, 0)),
        ],
        out_specs=[
            pl.BlockSpec((tm, 3 * na_w), lambda i: (i, 0)),
            pl.BlockSpec((tm, gq_qw + 2 * gq_kw), lambda i: (i, 0)),
            pl.BlockSpec((tm, c_w), lambda i: (i, 0)),
        ],
        out_shape=[
            jax.ShapeDtypeStruct((r, 3 * na_w), BF16),
            jax.ShapeDtypeStruct((r, gq_qw + 2 * gq_kw), BF16),
            jax.ShapeDtypeStruct((r, c_w), F32),
        ],
        compiler_params=_cparams("parallel"),
        name="in_projection",
    )(*x_parts, layer_g1, mod3, mod3, w_in_bf, cos_t, sin_t, gains, bd)


def _head(j):
    return slice(j * HEAD_DIM, (j + 1) * HEAD_DIM)


def _with_ones(v):
    return jnp.concatenate([v, jnp.ones_like(v)], axis=1)


def _scores(q, k):
    return lax.dot_general(q, k, (((1,), (1,)), ((), ())), preferred_element_type=F32)


def _normalise(o):
    return o[:, :HEAD_DIM] / o[:, HEAD_DIM:HEAD_DIM + 1]


def _softmax_attend_all(qs, ks, v1s):
    ss = [_scores(q, k) for q, k in zip(qs, ks)]
    outs = []
    for s, v1 in zip(ss, v1s):
        p = jnp.exp((s - jnp.max(s, axis=-1, keepdims=True)).astype(BF16))
        outs.append(_normalise(jnp.dot(p, v1, preferred_element_type=F32)))
    return outs


def _grouped_attend(q_ref, k_of, v_of, n_kv, grp, splits=1):
    t = q_ref.shape[0] // splits
    qs, ks, vs = [], [], []
    for j in range(n_kv):
        for h in range(splits):
            qs.append(jnp.concatenate([q_ref[h * t:(h + 1) * t, _head(j * grp + g)] for g in range(grp)], axis=0))
            ks.append(k_of(j))
            vs.append(v_of(j))
    os_ = _softmax_attend_all(qs, ks, vs)
    rows = []
    for h in range(splits):
        cols = []
        for j in range(n_kv):
            o = os_[j * splits + h]
            cols += [o[g * t:(g + 1) * t] for g in range(grp)]
        rows.append(jnp.concatenate(cols, axis=1))
    return jnp.concatenate(rows, axis=0) if splits > 1 else rows[0]


def _gqa_kernel(q_ref, kl_ref, vl_ref, kc_ref, vc_ref, o_ref, k_s, v_s, *, n_kv, grp):
    s_len = kl_ref.shape[0]

    @pl.when(pl.program_id(1) == 0)
    def _():
        for j in range(n_kv):
            k_s[j, :s_len, :] = kl_ref[:, _head(j)]
            k_s[j, s_len:, :] = kc_ref[:, _head(j)]
            v_s[j, :s_len, :] = _with_ones(vl_ref[:, _head(j)])
            v_s[j, s_len:, :] = _with_ones(vc_ref[:, _head(j)])

    o = _grouped_attend(q_ref, lambda j: k_s[j], lambda j: v_s[j], n_kv, grp, splits=2)
    o_ref[...] = o.astype(o_ref.dtype)


def gqa_attention(pb, *, b, s, l, qw, kw, tq):
    n_lat = b * s
    n_kv = kw // HEAD_DIM
    grp = qw // kw
    assert kw == LANES and qw % kw == 0
    kcol, vcol = qw // kw, qw // kw + 1
    kern = functools.partial(_gqa_kernel, n_kv=n_kv, grp=grp)
    return pl.pallas_call(
        kern,
        grid=(b, s // tq),
        in_specs=[
            pl.BlockSpec((tq, qw), lambda i, j: (i * (s // tq) + j, 0)),
            pl.BlockSpec((s, kw), lambda i, j: (i, kcol)),
            pl.BlockSpec((s, kw), lambda i, j: (i, vcol)),
            pl.BlockSpec((l, kw), lambda i, j: (n_lat // l + i, kcol)),
            pl.BlockSpec((l, kw), lambda i, j: (n_lat // l + i, vcol)),
        ],
        out_specs=pl.BlockSpec((tq, qw), lambda i, j: (i * (s // tq) + j, 0)),
        out_shape=jax.ShapeDtypeStruct((n_lat, qw), BF16),
        scratch_shapes=[pltpu.VMEM((n_kv, s + l, HEAD_DIM), BF16), pltpu.VMEM((n_kv, s + l, 2 * HEAD_DIM), BF16)],
        compiler_params=_cparams("parallel", "arbitrary"),
        name="gqa_attention",
    )(pb, pb, pb, pb, pb)


def _ctx_attn_kernel(qa_ref, ka_ref, va_ref, qb_ref, kb_ref, vb_ref, oa_ref, ob_ref, *, n_kv, grp):
    na_h = qa_ref.shape[1] // HEAD_DIM
    oa = _softmax_attend_all([qa_ref[:, _head(h)] for h in range(na_h)],
                             [ka_ref[:, _head(h)] for h in range(na_h)],
                             [_with_ones(va_ref[:, _head(h)]) for h in range(na_h)])
    oa_ref[...] = jnp.concatenate(oa, axis=1).astype(oa_ref.dtype)
    ob = _grouped_attend(qb_ref, lambda j: kb_ref[:, _head(j)], lambda j: _with_ones(vb_ref[:, _head(j)]),
                         n_kv, grp)
    ob_ref[...] = ob.astype(ob_ref.dtype)


def ctx_attention(pa, pb, *, b, s, l, na_w, qw, kw):
    r0 = (b * s) // l
    grp = qw // kw
    kern = functools.partial(_ctx_attn_kernel, n_kv=kw // HEAD_DIM, grp=grp)
    a_spec = lambda m: pl.BlockSpec((l, na_w), lambda i: (r0 + i, m))
    return pl.pallas_call(
        kern,
        grid=(b,),
        in_specs=[a_spec(0), a_spec(1), a_spec(2),
                  pl.BlockSpec((l, qw), lambda i: (r0 + i, 0)),
                  pl.BlockSpec((l, kw), lambda i: (r0 + i, grp)),
                  pl.BlockSpec((l, kw), lambda i: (r0 + i, grp + 1))],
        out_specs=[pl.BlockSpec((l, na_w), lambda i: (i, 0)), pl.BlockSpec((l, qw), lambda i: (i, 0))],
        out_shape=[jax.ShapeDtypeStruct((b * l, na_w), BF16), jax.ShapeDtypeStruct((b * l, qw), BF16)],
        compiler_params=_cparams("parallel"),
        name="ctx_attention",
    )(pa, pa, pa, pb, pb, pb)


NA_QROWS = 4
NA_UNION = NA_WIN_R + NA_QROWS


def _na_block_geometry(rows):
    wu = min(rows, NA_UNION)
    wr = min(NA_WIN_R, rows)
    nblk = rows // NA_QROWS
    sig, u0s = [], []
    for blk in range(nblk):
        r0 = blk * NA_QROWS
        u0 = int(np.clip(r0 - wr // 2, 0, rows - wu))
        u0s.append(u0)
        sig.append(tuple((r0 + j - u0, int(np.clip(r0 + j - wr // 2, 0, rows - wr)) - u0) for j in range(NA_QROWS)))
    cls = [int(blk > 0) + int(blk == nblk - 1) for blk in range(nblk)]
    reps = {}
    for blk in range(nblk):
        assert reps.setdefault(cls[blk], sig[blk]) == sig[blk]
    return wu, wr, [reps.get(c, reps[0]) for c in range(3)]


def _na_kernel(q_ref, k_ref, v_ref, kc_ref, vc_ref, bias_ref, o_ref, k_s, v_s, kc_s, vc_s, *, rows, wu, wr):
    nh = q_ref.shape[1] // HEAD_DIM
    nblk = rows // NA_QROWS
    nq = NA_QROWS * GRID_W
    for h in range(nh):
        k_s[h] = k_ref[:, _head(h)]
        v_s[h] = _with_ones(v_ref[:, _head(h)])
        kc_s[h] = kc_ref[:, _head(h)]
        vc_s[h] = _with_ones(vc_ref[:, _head(h)])

    def body(blk, carry):
        u0 = jnp.clip(blk * NA_QROWS - wr // 2, 0, rows - wu)
        cls = jnp.minimum(blk, 1) + jnp.maximum(blk - (nblk - 2), 0)
        q_rows = pl.ds(pl.multiple_of(blk * nq, nq), nq)
        k_rows = pl.ds(pl.multiple_of(u0 * GRID_W, GRID_W), wu * GRID_W)
        outs = []
        for h in range(nh):
            q = q_ref[q_rows, _head(h)]
            s_nb = _scores(q, k_s[h, k_rows, :]) + bias_ref[h, cls]
            s_cx = _scores(q, kc_s[h])
            m = jnp.maximum(jnp.max(s_nb, axis=-1, keepdims=True), jnp.max(s_cx, axis=-1, keepdims=True))
            p_nb = jnp.exp((s_nb - m).astype(BF16))
            p_cx = jnp.exp((s_cx - m).astype(BF16))
            outs.append(_normalise(jnp.dot(p_nb, v_s[h, k_rows, :], preferred_element_type=F32)
                                   + jnp.dot(p_cx, vc_s[h], preferred_element_type=F32)))
        o_ref[q_rows, :] = jnp.concatenate(outs, axis=1).astype(o_ref.dtype)
        return carry

    lax.fori_loop(0, nblk, body, 0, unroll=2)


def na_attention(pa, bias_tab, layer, *, b, s, l, na_w):
    n_lat = b * s
    nh = na_w // HEAD_DIM
    rows = s // GRID_W
    wu, wr, _ = _na_block_geometry(rows)
    kern = functools.partial(_na_kernel, rows=rows, wu=wu, wr=wr)
    lat = lambda m: pl.BlockSpec((s, na_w), lambda i: (i, m))
    cx = lambda m: pl.BlockSpec((l, na_w), lambda i: (n_lat // l + i, m))
    return pl.pallas_call(
        kern,
        grid=(b,),
        in_specs=[lat(0), lat(1), lat(2), cx(1), cx(2),
                  pl.BlockSpec((nh,) + bias_tab.shape[1:], lambda i: (layer, 0, 0, 0))],
        out_specs=pl.BlockSpec((s, na_w), lambda i: (i, 0)),
        out_shape=jax.ShapeDtypeStruct((n_lat, na_w), BF16),
        scratch_shapes=[pltpu.VMEM((nh, s, HEAD_DIM), BF16), pltpu.VMEM((nh, s, 2 * HEAD_DIM), BF16),
                        pltpu.VMEM((nh, l, HEAD_DIM), BF16), pltpu.VMEM((nh, l, 2 * HEAD_DIM), BF16)],
        compiler_params=_cparams("parallel"),
        name="na_attention",
    )(pa, pa, pa, pa, pa, bias_tab)


def na_bias_table(rpb, rows):
    wu, wr, reps = _na_block_geometry(rows)
    h = rpb.shape[0]
    r_off = np.array([[rj for rj, _ in rep] for rep in reps])
    s_off = np.array([[sj for _, sj in rep] for rep in reps])
    kr = np.arange(wu)
    row_ok = (kr >= s_off[..., None]) & (kr < s_off[..., None] + wr)
    dr = kr - r_off[..., None] + (NA_WIN_R - 1)
    cidx = np.arange(GRID_W)
    col_start = np.clip(cidx - NA_WIN_C // 2, 0, GRID_W - NA_WIN_C)
    col_ok = (cidx[None, :] >= col_start[:, None]) & (cidx[None, :] < col_start[:, None] + NA_WIN_C)
    dc = np.clip(cidx[None, :] - cidx[:, None] + (NA_WIN_C - 1), 0, 2 * NA_WIN_C - 2)
    n_dr, n_dc = 2 * NA_WIN_R - 1, 2 * NA_WIN_C - 1
    sel_r = jnp.asarray(np.where(row_ok, dr, n_dr)[..., None] == np.arange(n_dr + 1), F32)
    sel_c = jnp.asarray(np.where(col_ok, dc, n_dc)[..., None] == np.arange(n_dc + 1), F32)
    rpb_x = jnp.pad(rpb.astype(F32), ((0, 0), (0, 1), (0, 1)), constant_values=NEG_INF)
    bias = jnp.einsum("cjki,hid->hcjkd", sel_r, rpb_x, precision=HIGHEST)
    bias = jnp.einsum("hcjkd,qxd->hcjqkx", bias, sel_c, precision=HIGHEST)
    return bias.reshape(h, len(reps), NA_QROWS * GRID_W, wu * GRID_W)


HG_BLOCK = 128


def _hgrn_block(q_ref, v_ref, z_ref, o_acc, bc_ref, lb, bdb, bdf, st, blk, *, reverse):
    c = HGRN_CHUNK
    hc = c // 2
    ncb = HG_BLOCK // c
    lbm = jnp.maximum(lb, LB_FLOOR)
    one_m_lb = 1.0 - lb
    scale = HEAD_DIM ** -0.5
    t_idx = lax.broadcasted_iota(jnp.int32, (2 * ncb, hc, LANES), 1)
    edge = 0 if reverse else c - 1
    early, late = (1, 0) if reverse else (0, 1)

    def bs(x, s):
        return jnp.broadcast_to(x[:, s:s + 1, :], x.shape)

    def seen(s):
        return (t_idx <= s) if reverse else (t_idx >= s)

    def halves(x):
        x4 = x.reshape(ncb, 2, hc, LANES)
        return x4[:, early], x4[:, late]

    def chunks(xe, xl):
        parts = [xl, xe] if reverse else [xe, xl]
        return jnp.stack(parts, axis=1).reshape(ncb, c, LANES)

    def seg_sum(w):
        rows = w.shape[0] * w.shape[1]
        return jnp.dot(w.reshape(rows, LANES).astype(BF16), bdb, preferred_element_type=F32).reshape(w.shape)

    if True:
        r0 = pl.multiple_of(blk * HG_BLOCK, HG_BLOCK)
        z = z_ref[pl.ds(r0, HG_BLOCK), :]
        q = q_ref[pl.ds(r0, HG_BLOCK), :] * scale
        v = v_ref[pl.ds(r0, HG_BLOCK), :]
        sig = jax.nn.sigmoid(z)
        f = one_m_lb * sig + lbm
        k = one_m_lb * (1.0 - sig) - (lbm - lb)
        logf = jnp.log2(f)
        pre = logf.reshape(2 * ncb, hc, LANES)
        for sh in (1, 2, 4):
            if reverse:
                pre = pre + jnp.where(t_idx < hc - sh, pltpu.roll(pre, hc - sh, 1), 0.0)
            else:
                pre = pre + jnp.where(t_idx >= sh, pltpu.roll(pre, sh, 1), 0.0)
        pre_e, pre_l = halves(pre.reshape(HG_BLOCK, LANES))
        cum_e = pre_e
        cum_l = pre_l + bs(pre_e, 0 if reverse else hc - 1)
        q8, k8, v8 = (a.reshape(2 * ncb, hc, LANES) for a in (q, k, v))
        for slot, a in enumerate((pre, k8, v8)):
            bc_ref[slot] = a

        def row(slot, s, early_only=False):
            lead = pl.ds(early, ncb, stride=2) if early_only else slice(None)
            return jnp.broadcast_to(bc_ref[slot, lead, s:s + 1, :], (ncb if early_only else 2 * ncb, hc, LANES))

        o8 = jnp.zeros_like(pre)
        for s in range(hc):
            d = jnp.where(seen(s), pre - row(0, s), NEG_INF)
            o8 = o8 + seg_sum(q8 * row(1, s) * jnp.exp2(d)) * row(2, s)
        q_l = halves(q)[1]
        o_e, o_l = halves(o8.reshape(HG_BLOCK, LANES))
        for s in range(hc):
            o_l = o_l + seg_sum(q_l * row(1, s, True) * jnp.exp2(cum_l - row(0, s, True))) * row(2, s, True)
        cum = chunks(cum_e, cum_l)
        o3 = chunks(o_e, o_l)
        q3 = q.reshape(ncb, c, LANES)
        k3 = k.reshape(ncb, c, LANES)
        v3 = v.reshape(ncb, c, LANES)
        cum_edge = bs(cum, edge)
        qe = (q3 * jnp.exp2(cum)).astype(BF16)
        kd = (k3 * jnp.exp2(cum_edge - cum)).astype(BF16)
        vb = v3.astype(BF16)
        chunk_decay = jnp.exp2(cum_edge)
        yield None
        u_t = [lax.dot_general(vb[n], kd[n], (((0,), (0,)), ((), ())), preferred_element_type=F32) * bdf
               for n in range(ncb)]
        yield None
        enter = [None] * ncb
        for n in (range(ncb - 1, -1, -1) if reverse else range(ncb)):
            enter[n] = st.astype(BF16)
            st = chunk_decay[n, 0:1, :] * st + u_t[n]
        yield None
        outs = [o3[n] + lax.dot_general(qe[n], enter[n], (((1,), (1,)), ((), ())), preferred_element_type=F32)
                for n in range(ncb)]
        o_acc[pl.ds(r0, HG_BLOCK), :] = jnp.concatenate(outs, axis=0)
        yield st


def _hgrn_segment(q_ref, v_ref, zf_ref, zb_ref, of_acc, ob_acc, bc_ref, lb_ref, bdb, bdf, st_f, st_b):
    nblk = q_ref.shape[0] // HG_BLOCK

    def body(i, carry):
        st_f, st_b = carry
        scans = [_hgrn_block(q_ref, v_ref, zf_ref, of_acc, bc_ref.at[0], lb_ref[0, 0], bdb, bdf, st_f, i,
                             reverse=False),
                 _hgrn_block(q_ref, v_ref, zb_ref, ob_acc, bc_ref.at[1], lb_ref[1, 0], bdb, bdf, st_b,
                             nblk - 1 - i, reverse=True)]
        for _ in range(3):
            for g in scans:
                next(g)
        return tuple(next(g) for g in scans)

    return lax.fori_loop(0, nblk, body, (st_f, st_b))


def _hgrn_kernel(ql_ref, qc_ref, vl_ref, vc_ref, zfl_ref, zfc_ref, zbl_ref, zbc_ref, gl_ref, gc_ref,
                 lb_ref, gain_ref, bdb_ref, bdf_ref, yl_ref, yc_ref, olf_acc, olb_acc, ocf_acc, ocb_acc, bc_ref):
    bdb = bdb_ref[...]
    bdf = bdf_ref[...]
    zero = jnp.zeros((LANES, LANES), F32)
    st_f, st_b = _hgrn_segment(qc_ref, vc_ref, zfc_ref, zbc_ref, ocf_acc, ocb_acc, bc_ref, lb_ref, bdb, bdf,
                               zero, zero)
    _hgrn_segment(ql_ref, vl_ref, zfl_ref, zbl_ref, olf_acc, olb_acc, bc_ref, lb_ref, bdb, bdf, st_f, st_b)
    for acc_f, acc_b, g_ref, y_ref in ((olf_acc, olb_acc, gl_ref, yl_ref), (ocf_acc, ocb_acc, gc_ref, yc_ref)):
        o = acc_f[...] + acc_b[...]
        g = g_ref[...]
        y = o * _seg_inv_rms(o, bdb) * gain_ref[...]
        y_ref[...] = (y * (g * jax.nn.sigmoid(g))).astype(y_ref.dtype)


def hgrn_mixer(pc, lb, gain128, bdb, bdf, *, b, s, l):
    w = pc.shape[1] // 5
    nj = w // LANES
    n_lat = b * s
    lat = lambda m: pl.BlockSpec((s, LANES), lambda i, j: (i, m * nj + j))
    cx = lambda m: pl.BlockSpec((l, LANES), lambda i, j: (n_lat // l + i, m * nj + j))
    in_specs = []
    for m in (0, 1, 2, 3, 4):
        in_specs += [lat(m), cx(m)]
    in_specs += [pl.BlockSpec((2, 1, 1, LANES), lambda i, j: (0, j, 0, 0)),
                 pl.BlockSpec((1, LANES), lambda i, j: (0, 0)),
                 pl.BlockSpec((LANES, LANES), lambda i, j: (0, 0)),
                 pl.BlockSpec((LANES, LANES), lambda i, j: (0, 0))]
    return pl.pallas_call(
        _hgrn_kernel,
        grid=(b, nj),
        in_specs=in_specs,
        out_specs=[pl.BlockSpec((s, LANES), lambda i, j: (i, j)),
                   pl.BlockSpec((l, LANES), lambda i, j: (i, j))],
        out_shape=[jax.ShapeDtypeStruct((n_lat, w), BF16), jax.ShapeDtypeStruct((b * l, w), BF16)],
        scratch_shapes=[pltpu.VMEM((s, LANES), F32), pltpu.VMEM((s, LANES), F32),
                        pltpu.VMEM((l, LANES), F32), pltpu.VMEM((l, LANES), F32),
                        pltpu.VMEM((2, 3, HG_BLOCK // 8, 8, LANES), F32)],
        compiler_params=_cparams("parallel", "parallel"),
        name="hgrn_mixer",
    )(*([pc] * 10), lb, gain128, bdb, bdf)


def _outproj_kernel(*refs, counts, lat_tiles, wa, wb):
    rows = []
    for n in counts:
        rows.append(_row_tile(refs[:n], lat_tiles))
        refs = refs[n:]
    x_in, ma, mb, mc = rows
    w_ref, gate_ref, g2_ref, shift_ref, scale_ref, wr_ref, br_ref, xo_ref, h2_ref = refs
    d = xo_ref.shape[1]
    w = w_ref[...]
    y = jnp.dot(ma, w[:wa], preferred_element_type=F32)
    y = y + jnp.dot(mb, w[wa:wa + wb], preferred_element_type=F32)
    y = y + jnp.dot(mc, w[wa + wb:], preferred_element_type=F32)
    x = x_in + gate_ref[0] * y
    xo_ref[...] = x
    ms = jnp.mean(x * x, axis=-1, keepdims=True)
    h = x * lax.rsqrt(ms + EPS) * g2_ref[0]
    h = h * (1.0 + scale_ref[0]) + shift_ref[0]
    h2_ref[:, :d] = h
    h_hi = h.astype(BF16)
    h_lo = (h - h_hi.astype(F32)).astype(BF16)
    logits = (jnp.dot(h_hi, wr_ref[0], preferred_element_type=F32)
              + jnp.dot(h_lo, wr_ref[0], preferred_element_type=F32)
              + jnp.dot(h_hi, wr_ref[1], preferred_element_type=F32)) + br_ref[...]
    h2_ref[:, d:] = _route_meta(logits)


def out_projection(xall, mix_a, mix_b, mix_c, w_out_bf, mod3, layer_g2, w_route, b_route, *, n_rows, seq, nseg, tm):
    ops = [_as_parts(a) for a in (xall, mix_a, mix_b, mix_c)]
    r, d = n_rows, ops[0][0].shape[1]
    w_hi = w_route.astype(BF16)
    w_route = jnp.stack([w_hi, (w_route - w_hi.astype(F32)).astype(BF16)])
    wa, wb, wc = (p[0].shape[1] for p in ops[1:])
    tiles_per_seq = seq // tm
    lat_tiles = ops[1][0].shape[0] // tm

    def seg(i):
        return jnp.minimum(i // tiles_per_seq, nseg - 1)

    def modspec(m):
        return pl.BlockSpec((1, 1, d), lambda i: (seg(i) * N_MOD + m, 0, 0))

    row = lambda wdt: pl.BlockSpec((tm, wdt), lambda i: (i, 0))
    kern = functools.partial(_outproj_kernel, counts=tuple(len(p) for p in ops), lat_tiles=lat_tiles, wa=wa, wb=wb)
    return pl.pallas_call(
        kern,
        grid=(r // tm,),
        in_specs=[s for p in ops for s in _row_specs(p, tm, lat_tiles)] + [
                  pl.BlockSpec((wa + wb + wc, d), lambda i: (0, 0)),
                  modspec(2),
                  pl.BlockSpec((1, d), lambda i: (0, 0)),
                  modspec(3), modspec(4),
                  pl.BlockSpec((2, d, LANES), lambda i: (0, 0, 0)),
                  pl.BlockSpec((1, LANES), lambda i: (0, 0))],
        out_specs=[row(d), row(d + LANES)],
        out_shape=[jax.ShapeDtypeStruct((r, d), F32),
                   jax.ShapeDtypeStruct((r, d + LANES), F32)],
        compiler_params=_cparams("parallel"),
        name="out_projection",
    )(*[a for p in ops for a in p], w_out_bf, mod3, layer_g2, mod3, mod3, w_route, b_route)


PAIRS_PER_GROUP = EXPERTS_PER_GROUP * (EXPERTS_PER_GROUP - 1) // 2
N_BUCKETS = N_GROUPS * PAIRS_PER_GROUP
ROW_TILE = 192
META_BUCKET, META_WA, META_WB = 0, 1, 2


def _bucket_experts():
    ea = np.zeros((LANES,), np.int32)
    eb = np.zeros((LANES,), np.int32)
    for g in range(N_GROUPS):
        k = g * PAIRS_PER_GROUP
        for a in range(EXPERTS_PER_GROUP):
            for b in range(a + 1, EXPERTS_PER_GROUP):
                ea[k], eb[k] = g * EXPERTS_PER_GROUP + a, g * EXPERTS_PER_GROUP + b
                k += 1
    return ea, eb


def _route_meta(logits):
    lane = lax.broadcasted_iota(jnp.int32, logits.shape, 1).astype(F32)
    is_g = lane < N_GROUPS
    gl = jnp.where(is_g, logits, -jnp.inf)
    gmax = jnp.max(gl, axis=-1, keepdims=True)
    g_idx = jnp.min(jnp.where(gl == gmax, lane, LANES), axis=-1, keepdims=True)
    gsum = jnp.sum(jnp.where(is_g, jnp.exp(gl - gmax), 0.0), axis=-1, keepdims=True)
    g_top = 1.0 / gsum
    lo = N_GROUPS + g_idx * EXPERTS_PER_GROUP
    in_grp = (lane >= lo) & (lane < lo + EXPERTS_PER_GROUP)
    el = jnp.where(in_grp, logits, -jnp.inf)
    m1 = jnp.max(el, axis=-1, keepdims=True)
    i1 = jnp.min(jnp.where(el == m1, lane, LANES), axis=-1, keepdims=True)
    el2 = jnp.where(lane == i1, -jnp.inf, el)
    m2 = jnp.max(el2, axis=-1, keepdims=True)
    i2 = jnp.min(jnp.where(el2 == m2, lane, LANES), axis=-1, keepdims=True)
    e21 = jnp.exp(m2 - m1)
    w1 = g_top / (1.0 + e21)
    w2 = e21 * w1
    first_low = i1 < i2
    la = jnp.minimum(i1, i2) - lo
    lb = jnp.maximum(i1, i2) - lo
    pair = la * (2 * EXPERTS_PER_GROUP - 1 - la) * 0.5 + (lb - la - 1.0)
    bucket = g_idx * PAIRS_PER_GROUP + pair
    wa = jnp.where(first_low, w1, w2)
    wb = jnp.where(first_low, w2, w1)
    return jnp.where(lane == META_BUCKET, bucket,
                     jnp.where(lane == META_WA, wa, jnp.where(lane == META_WB, wb, 0.0)))


def _plan_kernel(meta_ref, tri_ref, ids_ref, rank_ref, cnt_ref, carry):
    @pl.when(pl.program_id(0) == 0)
    def _():
        carry[...] = jnp.zeros_like(carry)

    ids = meta_ref[...].T[META_BUCKET:META_BUCKET + 1, :]
    sub = lax.broadcasted_iota(jnp.int32, (LANES, ids.shape[1]), 0).astype(F32)
    onehot = (sub == ids).astype(F32)
    before = jnp.dot(onehot.astype(BF16), tri_ref[...], preferred_element_type=F32)
    rank = jnp.sum(onehot * (before + carry[...]), axis=0, keepdims=True)
    ids_ref[0] = ids.astype(jnp.int32)
    rank_ref[0] = rank.astype(jnp.int32)
    total = carry[...] + jnp.sum(onehot, axis=1, keepdims=True)
    carry[...] = total
    cnt_ref[...] = total.astype(jnp.int32)


def route_plan(h2x, tm):
    r = h2x.shape[0]
    meta_blk = h2x.shape[1] // LANES - 1
    nt = r // tm
    i = np.arange(tm)
    tri = jnp.asarray(i[:, None] < i[None, :], BF16)
    ids, rank, cnt = pl.pallas_call(
        _plan_kernel,
        grid=(nt,),
        in_specs=[pl.BlockSpec((tm, LANES), lambda i: (i, meta_blk)),
                  pl.BlockSpec((tm, tm), lambda i: (0, 0))],
        out_specs=[pl.BlockSpec((1, 1, tm), lambda i: (i, 0, 0)),
                   pl.BlockSpec((1, 1, tm), lambda i: (i, 0, 0)),
                   pl.BlockSpec((LANES, 1), lambda i: (0, 0))],
        out_shape=[jax.ShapeDtypeStruct((nt, 1, tm), jnp.int32),
                   jax.ShapeDtypeStruct((nt, 1, tm), jnp.int32),
                   jax.ShapeDtypeStruct((LANES, 1), jnp.int32)],
        scratch_shapes=[pltpu.VMEM((LANES, 1), F32)],
        compiler_params=_cparams("arbitrary"),
        name="route_plan",
    )(h2x, tri)
    return ids.reshape(r), rank.reshape(r), cnt.reshape(LANES)


def _row_copy(src, dst, i, j, sem):
    return pltpu.make_async_copy(src.at[pl.ds(i, 1)], dst.at[pl.ds(j, 1)], sem)


def _dispatch_kernel(pos_ref, h_ref, init_hbm, o_hbm, sem):
    del init_hbm
    ch = h_ref.shape[0]
    base = pl.program_id(0) * ch

    def issue(i, c):
        _row_copy(h_ref, o_hbm, i, pos_ref[base + i], sem).start()
        return c

    def drain(i, c):
        _row_copy(h_ref, o_hbm, i, 0, sem).wait()
        return c

    lax.fori_loop(0, ch, issue, 0, unroll=8)
    lax.fori_loop(0, ch, drain, 0, unroll=8)


def dispatch_rows(pos, h2, n_rows, ch):
    r, d = h2.shape
    return pl.pallas_call(
        _dispatch_kernel,
        grid_spec=pltpu.PrefetchScalarGridSpec(
            num_scalar_prefetch=1,
            grid=(r // ch,),
            in_specs=[pl.BlockSpec((ch, d), lambda i, p: (i, 0)), pl.BlockSpec(memory_space=pl.ANY)],
            out_specs=pl.BlockSpec(memory_space=pl.ANY),
            scratch_shapes=[pltpu.SemaphoreType.DMA],
        ),
        out_shape=jax.ShapeDtypeStruct((n_rows, d), h2.dtype),
        input_output_aliases={2: 0},
        compiler_params=_cparams("arbitrary"),
        name="dispatch_rows",
    )(pos, h2, jnp.zeros((n_rows, d), h2.dtype))


def _expert_kernel(ta_ref, tb_ref, nu_ref, hs_ref, wg_ref, wu_ref, wd_ref, y_ref, gu_s, dn_s):
    j = pl.program_id(0)
    d = wg_ref.shape[2]
    ff = wg_ref.shape[3]
    epg = wg_ref.shape[1]
    prev = jnp.maximum(j - 1, 0)

    @pl.when((j == 0) | (ta_ref[j] // epg != ta_ref[prev] // epg))
    def _():
        for e in range(epg):
            gu_s[e, :, :ff] = wg_ref[0, e].astype(BF16)
            gu_s[e, :, ff:] = wu_ref[0, e].astype(BF16)
            dn_s[e] = wd_ref[0, e].astype(BF16)

    @pl.when(j < nu_ref[0])
    def _():
        h = hs_ref[:, :d].astype(BF16)
        meta = hs_ref[:, d:]
        es = [t_ref[j] % epg for t_ref in (ta_ref, tb_ref)]
        hgus = [jnp.dot(h, gu_s[e], preferred_element_type=F32) for e in es]
        y = None
        for e, hgu, lane in zip(es, hgus, (META_WA, META_WB)):
            hg = hgu[:, :ff]
            hid = (hg * jax.nn.sigmoid(hg)) * hgu[:, ff:]
            part = meta[:, lane:lane + 1] * jnp.dot(hid.astype(BF16), dn_s[e], preferred_element_type=F32)
            y = part if y is None else y + part
        y_ref[:, :d] = y
        y_ref[:, d:] = meta


def expert_pairs(tile_a, tile_b, n_used, hs, w_gate, w_up, w_down, layer):
    rows, dx = hs.shape
    depth, ne, d, ff = w_gate.shape
    epg = EXPERTS_PER_GROUP
    ng = ne // epg
    nt = rows // ROW_TILE
    blk = lambda i, ta, tb, nu: (jnp.minimum(i, nu[0] - 1), 0)
    grp = lambda shape: pl.BlockSpec((1, epg) + shape, lambda i, ta, tb, nu: (layer * ng + ta[i] // epg, 0, 0, 0),
                                     pipeline_mode=pl.Buffered(1))
    by_group = lambda w: w.reshape((depth * ng, epg) + w.shape[2:])
    return pl.pallas_call(
        _expert_kernel,
        grid_spec=pltpu.PrefetchScalarGridSpec(
            num_scalar_prefetch=3,
            grid=(nt,),
            in_specs=[pl.BlockSpec((ROW_TILE, dx), blk), grp((d, ff)), grp((d, ff)), grp((ff, d))],
            out_specs=pl.BlockSpec((ROW_TILE, dx), blk),
            scratch_shapes=[pltpu.VMEM((epg, d, 2 * ff), BF16), pltpu.VMEM((epg, ff, d), BF16)],
        ),
        out_shape=jax.ShapeDtypeStruct((rows, dx), F32),
        input_output_aliases={3: 0},
        compiler_params=_cparams("arbitrary"),
        name="expert_pairs",
    )(tile_a, tile_b, n_used, hs, by_group(w_gate), by_group(w_up), by_group(w_down))


def _combine_kernel(pos_ref, x_ref, gate_ref, y_hbm, o_ref, ybuf, sems):
    tm = x_ref.shape[0]
    d = x_ref.shape[1]
    i = pl.program_id(0)
    slot = i % 2

    def gather(tile, into):
        def issue(r, c):
            _row_copy(y_hbm, ybuf.at[into], pos_ref[tile * tm + r], r, sems.at[into]).start()
            return c
        lax.fori_loop(0, tm, issue, 0, unroll=8)

    @pl.when(i == 0)
    def _():
        gather(0, 0)

    @pl.when(i + 1 < pl.num_programs(0))
    def _():
        gather(i + 1, 1 - slot)

    def drain(r, c):
        _row_copy(y_hbm, ybuf.at[slot], 0, r, sems.at[slot]).wait()
        return c

    lax.fori_loop(0, tm, drain, 0, unroll=8)
    o_ref[...] = x_ref[...] + gate_ref[0] * ybuf[slot, :, :d]


def combine_rows(pos, x_mid, mod3, y, *, seq, nseg, tm):
    r, d = x_mid.shape
    tiles_per_seq = seq // tm
    seg = lambda i: jnp.minimum(i // tiles_per_seq, nseg - 1)
    return pl.pallas_call(
        _combine_kernel,
        grid_spec=pltpu.PrefetchScalarGridSpec(
            num_scalar_prefetch=1,
            grid=(r // tm,),
            in_specs=[pl.BlockSpec((tm, d), lambda i, p: (i, 0)),
                      pl.BlockSpec((1, 1, d), lambda i, p: (seg(i) * N_MOD + 5, 0, 0)),
                      pl.BlockSpec(memory_space=pl.ANY)],
            out_specs=pl.BlockSpec((tm, d), lambda i, p: (i, 0)),
            scratch_shapes=[pltpu.VMEM((2, tm, y.shape[1]), F32), pltpu.SemaphoreType.DMA((2,))],
        ),
        out_shape=jax.ShapeDtypeStruct((r, d), F32),
        compiler_params=_cparams("arbitrary"),
        name="combine_rows",
    )(pos, x_mid, mod3, y)


def moe_block(x_mid, h2, mod3, w_gate, w_up, w_down, *, layer, seq, nseg):
    r, d = x_mid.shape
    ids, rank, cnt = route_plan(h2, PLAN_ROWS)
    tiles = (cnt + (ROW_TILE - 1)) // ROW_TILE
    incl = jnp.cumsum(tiles)
    n_used = incl[-1]
    lookup = lambda table, idx: jnp.sum(jnp.where(idx[:, None] == jnp.arange(LANES)[None, :], table[None, :], 0), axis=1)
    pos = lookup(incl - tiles, ids) * ROW_TILE + rank
    nt = r // ROW_TILE + N_BUCKETS
    last = jnp.minimum(jnp.arange(nt), n_used - 1)
    tile_bucket = jnp.sum((incl[None, :] <= last[:, None]).astype(jnp.int32), axis=1)
    ea, eb = _bucket_experts()
    tile_a = lookup(jnp.asarray(ea), tile_bucket)
    tile_b = lookup(jnp.asarray(eb), tile_bucket)
    hs = dispatch_rows(pos, h2, nt * ROW_TILE, DISPATCH_ROWS)
    y = expert_pairs(tile_a, tile_b, n_used.reshape(1).astype(jnp.int32), hs, w_gate, w_up, w_down, layer)
    return combine_rows(pos, x_mid, mod3, y, seq=seq, nseg=nseg, tm=COMBINE_ROWS)


def _rope_tables(seq, tm):
    t = np.arange(seq)
    row = (t // GRID_W).astype(np.float32)
    col = (t % GRID_W).astype(np.float32)
    half = HEAD_DIM // 2
    inv = jnp.asarray(ROPE_THETA, F32) ** (-jnp.arange(0, half, 2, dtype=F32) / half)
    ang = jnp.concatenate([jnp.asarray(row)[:, None] * inv, jnp.asarray(col)[:, None] * inv], axis=-1)
    cos = jnp.repeat(jnp.cos(ang), 2, axis=-1)
    sin = jnp.repeat(jnp.sin(ang), 2, axis=-1) * jnp.asarray(np.tile([-1.0, 1.0], half), F32)
    cos = jnp.tile(cos, (1, LANES // HEAD_DIM))
    sin = jnp.tile(sin, (1, LANES // HEAD_DIM))
    cos = jnp.concatenate([cos, jnp.ones((tm, LANES), F32)], axis=0)
    sin = jnp.concatenate([sin, jnp.zeros((tm, LANES), F32)], axis=0)
    return cos, sin


def kernel(x, c, ctx, c_ctx, w_ada, b_ada, norm1_g, w_in, na_q_norm, na_k_norm, na_rpb, gqa_q_norm, gqa_k_norm, hgrn_lb, hgrn_o_norm, w_out, norm2_g, w_route_group, b_route_group, w_route_expert, b_route_expert, w_exp_gate, w_exp_up, w_exp_down):
    b, s, d = x.shape
    l = ctx.shape[1]
    depth = w_ada.shape[0]
    assert s % PROJ_ROWS == 0 and (b * l) % PROJ_ROWS == 0 and s // GRID_W >= NA_WIN_R
    assert s % l == 0 and l % HG_BLOCK == 0
    nseg = b + 1
    n_lat = b * s
    n_ctx = b * l
    na_w, gq_qw = d // 4, d // 2
    gq_kw = gq_qw // 4
    hg_w = d // 4
    tm = PROJ_ROWS

    c_all = jnp.zeros((16, d), F32).at[:b].set(c).at[b].set(c_ctx)
    mod = ada_mod(c_all, w_ada, b_ada)
    cos_t, sin_t = _rope_tables(s, tm)
    bd_f = _block_diag_ones(LANES, HEAD_DIM, F32)
    bd_b = _block_diag_ones(LANES, HEAD_DIM, BF16)
    p_lb = jax.nn.softmax(hgrn_lb.astype(F32), axis=0)
    lb_all = jnp.cumsum(p_lb, axis=0) - p_lb[0]
    tile2 = lambda g: jnp.tile(g, LANES // HEAD_DIM)
    bias_tab = na_bias_table(na_rpb.reshape((-1,) + na_rpb.shape[2:]), s // GRID_W)

    xall = (x.reshape(n_lat, d), ctx.reshape(n_ctx, d))
    for layer in range(depth):
        ctx_out = layer < depth - 1
        mod3 = mod[layer].reshape(16 * N_MOD, 1, d)
        gains = jnp.zeros((8, LANES), F32)
        gains = gains.at[0].set(tile2(na_q_norm[layer])).at[1].set(tile2(na_k_norm[layer]))
        gains = gains.at[2].set(tile2(gqa_q_norm[layer])).at[3].set(tile2(gqa_k_norm[layer]))
        pa, pb, pc = in_projection(xall, mod3, norm1_g[layer][None], w_in[layer].astype(BF16), cos_t, sin_t,
                                   gains, bd_b, n_lat_rows=n_lat, seq=s, nseg=nseg, tm=tm)
        o_a = na_attention(pa, bias_tab, layer, b=b, s=s, l=l, na_w=na_w)
        o_b = gqa_attention(pb, b=b, s=s, l=l, qw=gq_qw, kw=gq_kw, tq=GQA_QROWS)
        lb4 = lb_all[layer].reshape(2, hg_w // LANES, 1, LANES)
        y_lat, y_ctx = hgrn_mixer(pc, lb4, tile2(hgrn_o_norm[layer])[None], bd_b, bd_f, b=b, s=s, l=l)

        w_route = jnp.zeros((d, LANES), F32).at[:, :N_GROUPS].set(w_route_group[layer])
        w_route = w_route.at[:, N_GROUPS:N_GROUPS + N_EXPERTS].set(w_route_expert[layer])
        b_route = jnp.zeros((1, LANES), F32).at[0, :N_GROUPS].set(b_route_group[layer])
        b_route = b_route.at[0, N_GROUPS:N_GROUPS + N_EXPERTS].set(b_route_expert[layer])
        if ctx_out:
            o_ac, o_bc = ctx_attention(pa, pb, b=b, s=s, l=l, na_w=na_w, qw=gq_qw, kw=gq_kw)
            mix_a, mix_b, y_c = (o_a, o_ac), (o_b, o_bc), (y_lat, y_ctx)
            n_rows = n_lat + n_ctx
        else:
            mix_a, mix_b, y_c, n_rows = o_a, o_b, y_lat, n_lat
        x_mid, h2 = out_projection(xall, mix_a, mix_b, y_c, w_out[layer].astype(BF16), mod3,
                                    norm2_g[layer][None], w_route, b_route, n_rows=n_rows, seq=s,
                                    nseg=nseg, tm=tm)
        xall = moe_block(x_mid, h2, mod3, w_exp_gate, w_exp_up, w_exp_down, layer=layer, seq=s, nseg=nseg)
    return xall[:n_lat].reshape(b, s, d)
```
